```python
import math
import jax, jax.numpy as jnp
from jax import lax
import numpy as np

D_MODEL = 1024
BATCH = 4
SEQ = 4096
DEPTH = 2

D_MIX = D_MODEL
D_GMLP = D_MIX // 2
GMLP_HEADS = 4
GMLP_HEAD_DIM = D_GMLP // GMLP_HEADS
CHUNK = 128
D_SSM = D_MIX - D_GMLP
SSM_GROUP = 16
SSM_GROUPS = D_SSM // SSM_GROUP
SSM_STATE = 64
DT_MIN = 0.001
DT_MAX = 0.1
D_IN_PROJ = 2 * D_GMLP + D_SSM
N_EXPERTS = 32
TOP_K = 4
D_EXPERT = D_MODEL
SWIGLU_LIMIT = 7.0
SWIGLU_ALPHA = 1.702
MOE_BLOCK = 128
EPS = 1e-5

kernel_name = "hybrid_sgu_s5_moe_trunk"


def rms_norm(x, w):
    xf = x.astype(jnp.float32)
    y = xf * lax.rsqrt(jnp.mean(xf * xf, axis=-1, keepdims=True) + EPS)
    return (y * w.astype(jnp.float32)).astype(x.dtype)


def layer_norm(x, w, b):
    xf = x.astype(jnp.float32)
    mu = jnp.mean(xf, axis=-1, keepdims=True)
    var = jnp.mean(jnp.square(xf - mu), axis=-1, keepdims=True)
    y = (xf - mu) * lax.rsqrt(var + EPS)
    return (y * w.astype(jnp.float32) + b.astype(jnp.float32)).astype(x.dtype)


def chunked_sgu(u, v, ln_w, ln_b, w_s, b_s):
    B, L, _ = u.shape
    u = jax.nn.gelu(u)
    v = layer_norm(jax.nn.gelu(v), ln_w, ln_b)
    v = v.reshape(B, L // CHUNK, CHUNK, GMLP_HEADS, GMLP_HEAD_DIM)
    causal = jnp.tril(jnp.ones((CHUNK, CHUNK), dtype=bool))
    w = jnp.where(causal[None], w_s, jnp.zeros_like(w_s)).astype(v.dtype)
    mixed = jnp.einsum('hts,bcshd->bcthd', w, v) + b_s.T.astype(v.dtype)[None, None, :, :, None]
    return u * mixed.reshape(B, L, D_GMLP)


def s5_mixer(u, a_re, a_im, b_re, b_im, c_re, c_im, d, log_dt, glu_w, glu_b):
    B, L, _ = u.shape
    f32 = jnp.float32
    uf = u.astype(f32).reshape(B, L, SSM_GROUPS, SSM_GROUP)
    a = lax.complex(a_re.astype(f32), a_im.astype(f32))
    dt = jnp.exp(log_dt.astype(f32))[:, None]
    a_bar = jnp.exp(dt * a)
    b = lax.complex(b_re.astype(f32), b_im.astype(f32))
    b_bar = ((a_bar - 1.0) / a)[..., None] * b
    bu = lax.complex(jnp.einsum('blgh,gph->blgp', uf, b_bar.real),
                     jnp.einsum('blgh,gph->blgp', uf, b_bar.imag))
    a_seq = jnp.broadcast_to(a_bar, bu.shape)

    def combine(left, right):
        a_l, s_l = left
        a_r, s_r = right
        return a_r * a_l, a_r * s_l + s_r

    _, states = lax.associative_scan(combine, (a_seq, bu), axis=1)
    c = lax.complex(c_re.astype(f32), c_im.astype(f32))
    y = jnp.einsum('blgp,ghp->blgh', states, c).real + d.astype(f32).reshape(SSM_GROUPS, SSM_GROUP) * uf
    y = jax.nn.gelu(y.reshape(B, L, D_SSM)).astype(u.dtype)
    return y * jax.nn.sigmoid(y @ glu_w + glu_b)


def moe(h, router_w, router_b, w_gu, b_gu, w_dn, b_dn):
    B, L, D = h.shape
    f32 = jnp.float32
    n_tok = B * L
    n_as = n_tok * TOP_K
    n_blocks = -(-n_as // MOE_BLOCK) + N_EXPERTS
    n_pad = n_blocks * MOE_BLOCK
    ht = h.reshape(n_tok, D)
    logits = ht.astype(f32) @ router_w.astype(f32) + router_b.astype(f32)
    top_val, top_idx = lax.top_k(logits, TOP_K)
    gates = jax.nn.softmax(top_val, axis=-1)
    flat_e = top_idx.reshape(-1).astype(jnp.int32)
    flat_tok = jnp.arange(n_as, dtype=jnp.int32) // TOP_K
    flat_g = gates.reshape(-1)
    order = jnp.argsort(flat_e, stable=True)
    sorted_e = flat_e[order]
    counts = jnp.zeros((N_EXPERTS,), jnp.int32).at[flat_e].add(1)
    padded = (counts + MOE_BLOCK - 1) // MOE_BLOCK * MOE_BLOCK
    start = jnp.cumsum(counts) - counts
    padded_end = jnp.cumsum(padded)
    padded_start = padded_end - padded
    rank = jnp.arange(n_as, dtype=jnp.int32) - start[sorted_e]
    dest = padded_start[sorted_e] + rank
    slot_tok = jnp.zeros((n_pad,), jnp.int32).at[dest].set(flat_tok[order])
    slot_w = jnp.zeros((n_pad,), f32).at[dest].set(flat_g[order])
    block_start = jnp.arange(n_blocks, dtype=padded_end.dtype) * MOE_BLOCK
    block_expert = jnp.minimum(jnp.searchsorted(padded_end, block_start, side='right'), N_EXPERTS - 1)
    xb = ht[slot_tok].reshape(n_blocks, MOE_BLOCK, D)

    def expert_block(args):
        xe, e = args
        gu = xe @ w_gu[e] + b_gu[e]
        glu, lin = jnp.split(gu, 2, axis=-1)
        glu = jnp.minimum(glu, SWIGLU_LIMIT)
        lin = jnp.clip(lin, -SWIGLU_LIMIT, SWIGLU_LIMIT)
        act = glu * jax.nn.sigmoid(SWIGLU_ALPHA * glu) * (lin + 1.0)
        return act @ w_dn[e] + b_dn[e]

    yb = lax.map(expert_block, (xb, block_expert))
    contrib = yb.reshape(n_pad, D) * slot_w[:, None].astype(yb.dtype)
    return jax.ops.segment_sum(contrib, slot_tok, num_segments=n_tok).reshape(B, L, D)


def setup_inputs(seed: int = 0) -> dict:
    key = jax.random.key(seed)
    ks = jax.random.split(key, 32)
    f32 = jnp.float32
    nrm = lambda k, shape, s: jax.random.normal(k, shape, f32) * s
    x = jax.random.normal(ks[0], (BATCH, SEQ, D_MODEL), f32)
    attn_norm_w = 1.0 + nrm(ks[1], (DEPTH, D_MODEL), 0.02)
    w_in = nrm(ks[2], (DEPTH, D_MODEL, D_IN_PROJ), D_MODEL ** -0.5)
    sgu_ln_w = 1.0 + nrm(ks[3], (DEPTH, D_GMLP), 0.02)
    sgu_ln_b = nrm(ks[4], (DEPTH, D_GMLP), 0.02)
    sgu_w = nrm(ks[5], (DEPTH, GMLP_HEADS, CHUNK, CHUNK), CHUNK ** -0.5)
    sgu_b = 1.0 + nrm(ks[6], (DEPTH, GMLP_HEADS, CHUNK), 0.02)
    n = jnp.arange(SSM_STATE, dtype=f32)
    ssm_a_re = -0.5 + nrm(ks[7], (DEPTH, SSM_GROUPS, SSM_STATE), 0.01)
    ssm_a_im = math.pi * n + nrm(ks[8], (DEPTH, SSM_GROUPS, SSM_STATE), 0.01)
    b_scale = (2.0 * SSM_GROUP) ** -0.5
    ssm_b_re = nrm(ks[9], (DEPTH, SSM_GROUPS, SSM_STATE, SSM_GROUP), b_scale)
    ssm_b_im = nrm(ks[10], (DEPTH, SSM_GROUPS, SSM_STATE, SSM_GROUP), b_scale)
    c_scale = (2.0 * SSM_STATE) ** -0.5
    ssm_c_re = nrm(ks[11], (DEPTH, SSM_GROUPS, SSM_GROUP, SSM_STATE), c_scale)
    ssm_c_im = nrm(ks[12], (DEPTH, SSM_GROUPS, SSM_GROUP, SSM_STATE), c_scale)
    ssm_d = nrm(ks[13], (DEPTH, D_SSM), 1.0)
    ssm_log_dt = jax.random.uniform(ks[14], (DEPTH, SSM_GROUPS), f32, math.log(DT_MIN), math.log(DT_MAX))
    ssm_glu_w = nrm(ks[15], (DEPTH, D_SSM, D_SSM), D_SSM ** -0.5)
    ssm_glu_b = nrm(ks[16], (DEPTH, D_SSM), 0.02)
    out_norm_a = 1.0 + nrm(ks[17], (DEPTH, D_GMLP), 0.02)
    out_norm_b = 1.0 + nrm(ks[18], (DEPTH, D_SSM), 0.02)
    w_out = nrm(ks[19], (DEPTH, D_MIX, D_MODEL), D_MIX ** -0.5)
    ffn_norm_w = 1.0 + nrm(ks[20], (DEPTH, D_MODEL), 0.02)
    router_w = nrm(ks[21], (DEPTH, D_MODEL, N_EXPERTS), D_MODEL ** -0.5)
    router_b = nrm(ks[22], (DEPTH, N_EXPERTS), 0.01)
    w_gate_up = nrm(ks[23], (DEPTH, N_EXPERTS, D_MODEL, 2 * D_EXPERT), D_MODEL ** -0.5)
    b_gate_up = nrm(ks[24], (DEPTH, N_EXPERTS, 2 * D_EXPERT), 0.02)
    w_down = nrm(ks[25], (DEPTH, N_EXPERTS, D_EXPERT, D_MODEL), D_EXPERT ** -0.5)
    b_down = nrm(ks[26], (DEPTH, N_EXPERTS, D_MODEL), 0.02)
    final_norm_w = 1.0 + nrm(ks[27], (D_MODEL,), 0.02)
    return {"x": x, "attn_norm_w": attn_norm_w, "w_in": w_in,
            "sgu_ln_w": sgu_ln_w, "sgu_ln_b": sgu_ln_b, "sgu_w": sgu_w, "sgu_b": sgu_b,
            "ssm_a_re": ssm_a_re, "ssm_a_im": ssm_a_im, "ssm_b_re": ssm_b_re, "ssm_b_im": ssm_b_im,
            "ssm_c_re": ssm_c_re, "ssm_c_im": ssm_c_im, "ssm_d": ssm_d, "ssm_log_dt": ssm_log_dt,
            "ssm_glu_w": ssm_glu_w, "ssm_glu_b": ssm_glu_b,
            "out_norm_a": out_norm_a, "out_norm_b": out_norm_b, "w_out": w_out,
            "ffn_norm_w": ffn_norm_w, "router_w": router_w, "router_b": router_b,
            "w_gate_up": w_gate_up, "b_gate_up": b_gate_up, "w_down": w_down, "b_down": b_down,
            "final_norm_w": final_norm_w}


def reference(x, attn_norm_w, w_in, sgu_ln_w, sgu_ln_b, sgu_w, sgu_b,
              ssm_a_re, ssm_a_im, ssm_b_re, ssm_b_im, ssm_c_re, ssm_c_im, ssm_d, ssm_log_dt,
              ssm_glu_w, ssm_glu_b, out_norm_a, out_norm_b, w_out,
              ffn_norm_w, router_w, router_b, w_gate_up, b_gate_up, w_down, b_down,
              final_norm_w):
    for layer in range(DEPTH):
        h = rms_norm(x, attn_norm_w[layer])
        proj = h @ w_in[layer]
        u_g, v_g, u_s = jnp.split(proj, [D_GMLP, 2 * D_GMLP], axis=-1)
        y_a = chunked_sgu(u_g, v_g, sgu_ln_w[layer], sgu_ln_b[layer], sgu_w[layer], sgu_b[layer])
        y_b = s5_mixer(u_s, ssm_a_re[layer], ssm_a_im[layer], ssm_b_re[layer], ssm_b_im[layer],
                       ssm_c_re[layer], ssm_c_im[layer], ssm_d[layer], ssm_log_dt[layer],
                       ssm_glu_w[layer], ssm_glu_b[layer])
        mix = jnp.concatenate([rms_norm(y_a, out_norm_a[layer]), rms_norm(y_b, out_norm_b[layer])], axis=-1)
        x = x + mix @ w_out[layer]
        h = rms_norm(x, ffn_norm_w[layer])
        x = x + moe(h, router_w[layer], router_b[layer], w_gate_up[layer], b_gate_up[layer],
                    w_down[layer], b_down[layer])
    return rms_norm(x, final_norm_w)
```

```python
import functools
import math

import jax
import jax.numpy as jnp
from jax import lax
from jax.experimental import pallas as pl
from jax.experimental.pallas import tpu as pltpu

F32 = jnp.float32
BF16 = jnp.bfloat16
I32 = jnp.int32

EPS = 1e-5
N_HEADS = 4
CHUNK = 128
SSM_GROUP = 16
SSM_STATE = 64
SSM_T = 16
N_EXPERTS = 32
TOP_K = 4
SWIGLU_LIMIT = 7.0
SWIGLU_ALPHA = 1.702
LANES = 128

ROW_TILE = 512
EXPERT_TILE = 256
N_DUMP_BLOCKS = 4
VMEM_LIMIT = 48 * 1024 * 1024


def _gelu(x):
    return 0.5 * x * (1.0 + jnp.tanh(math.sqrt(2.0 / math.pi) * (x + 0.044715 * (x * x * x))))


def _sigmoid(x):
    return 1.0 / (1.0 + jnp.exp(-x))


def _rms(x, w):
    return x * lax.rsqrt(jnp.mean(x * x, axis=-1, keepdims=True) + EPS) * w


def _front_kernel(x_ref, nw_ref, win_ref, lnw_ref, lnb_ref, ws_ref, bst_ref, ona_ref,
                  ya_ref, us_ref, mixed_ref):
    d_g = ya_ref.shape[1]
    hd = d_g // N_HEADS
    x = x_ref[...]
    h = _rms(x, nw_ref[...]).astype(BF16)
    proj = jnp.dot(h, win_ref[...], preferred_element_type=F32)
    u = _gelu(proj[:, :d_g])
    v = _gelu(proj[:, d_g:2 * d_g])
    us_ref[...] = proj[:, 2 * d_g:].astype(BF16)
    mu = jnp.mean(v, axis=-1, keepdims=True)
    vc = v - mu
    var = jnp.mean(vc * vc, axis=-1, keepdims=True)
    vb = (vc * lax.rsqrt(var + EPS) * lnw_ref[...] + lnb_ref[...]).astype(BF16)
    row = lax.broadcasted_iota(I32, (CHUNK, CHUNK), 0)
    col = lax.broadcasted_iota(I32, (CHUNK, CHUNK), 1)
    causal = row >= col
    for hh in range(N_HEADS):
        w = jnp.where(causal, ws_ref[hh], 0.0).astype(BF16)
        bias = bst_ref[:, hh:hh + 1]
        for c in range(x.shape[0] // CHUNK):
            vv = vb[c * CHUNK:(c + 1) * CHUNK, hh * hd:(hh + 1) * hd]
            m = jnp.dot(w, vv, preferred_element_type=F32) + bias
            mixed_ref[c * CHUNK:(c + 1) * CHUNK, hh * hd:(hh + 1) * hd] = m
    ya = u * mixed_ref[...]
    ya_ref[...] = _rms(ya, ona_ref[...]).astype(BF16)


def _front(x2, nw, w_in, ln_w, ln_b, w_s, b_s, on_a):
    n, d = x2.shape
    d_g = ln_w.shape[0]
    d_s = w_in.shape[1] - 2 * d_g
    tm = ROW_TILE
    const2 = lambda i: (0, 0)
    return pl.pallas_call(
        _front_kernel,
        grid=(n // tm,),
        in_specs=[
            pl.BlockSpec((tm, d), lambda i: (i, 0)),
            pl.BlockSpec((1, d), const2),
            pl.BlockSpec(w_in.shape, const2),
            pl.BlockSpec((1, d_g), const2),
            pl.BlockSpec((1, d_g), const2),
            pl.BlockSpec(w_s.shape, lambda i: (0, 0, 0)),
            pl.BlockSpec((CHUNK, N_HEADS), const2),
            pl.BlockSpec((1, d_g), const2),
        ],
        out_specs=[
            pl.BlockSpec((tm, d_g), lambda i: (i, 0)),
            pl.BlockSpec((tm, d_s), lambda i: (i, 0)),
        ],
        out_shape=[
            jax.ShapeDtypeStruct((n, d_g), BF16),
            jax.ShapeDtypeStruct((n, d_s), BF16),
        ],
        scratch_shapes=[pltpu.VMEM((tm, d_g), F32)],
        compiler_params=pltpu.CompilerParams(
            dimension_semantics=("arbitrary",), vmem_limit_bytes=VMEM_LIMIT),
        name="front",
    )(x2, nw.reshape(1, d), w_in.astype(BF16), ln_w.reshape(1, d_g), ln_b.reshape(1, d_g),
      w_s, b_s.T, on_a.reshape(1, d_g))


def _s5_weights(a_re, a_im, b_re, b_im, c_re, c_im, d, log_dt):
    g, p = a_re.shape
    hch = b_re.shape[-1]
    t = SSM_T
    a = lax.complex(a_re.astype(F32), a_im.astype(F32))
    dt = jnp.exp(log_dt.astype(F32))[:, None]
    dta = dt * a
    a_bar = jnp.exp(dta)
    b_bar = ((a_bar - 1.0) / a)[..., None] * lax.complex(b_re.astype(F32), b_im.astype(F32))
    c = lax.complex(c_re.astype(F32), c_im.astype(F32))
    lags = jnp.arange(t + 1, dtype=F32)
    pw = jnp.exp(lags[None, :, None] * dta[:, None, :])
    kern = jnp.einsum('gpi,glp,gop->glio', b_bar, pw[:, :t], c).real
    s_idx = jnp.arange(t)[:, None]
    t_idx = jnp.arange(t)[None, :]
    lag = t_idx - s_idx
    blocks = kern[:, jnp.clip(lag, 0, t - 1)]
    blocks = jnp.where((lag >= 0)[None, :, :, None, None], blocks, 0.0)
    eye_t = (lag == 0)[None, :, :, None, None]
    dmat = d.astype(F32).reshape(g, hch)[:, None, None, None, :] * jnp.eye(hch, dtype=F32)[None, None, None]
    blocks = blocks + jnp.where(eye_t, dmat, 0.0)
    wk = blocks.transpose(0, 1, 3, 2, 4).reshape(g, t * hch, t * hch)
    so = pw[:, t - 1 - jnp.arange(t)][:, :, :, None] * b_bar[:, None, :, :]
    so = so.transpose(0, 1, 3, 2).reshape(g, t * hch, p)
    wso = jnp.concatenate([so.real, so.imag], axis=-1)
    si = c[:, None, :, :] * pw[:, 1:t + 1][:, :, None, :]
    si = si.transpose(0, 3, 1, 2).reshape(g, p, t * hch)
    wsi = jnp.concatenate([si.real, -si.imag], axis=1)
    return wk.astype(BF16), wso.astype(BF16), wsi.astype(BF16), dta


def _s5_scan_powers(dta, n_steps):
    steps = (SSM_T * (2.0 ** jnp.arange(n_steps, dtype=F32)))
    ap = jnp.exp(steps[None, :, None] * dta[:, None, :])
    mul_same = jnp.concatenate([ap.real, ap.real], axis=-1)
    mul_swap = jnp.concatenate([-ap.imag, ap.imag], axis=-1)
    return jnp.stack([mul_same, mul_swap], axis=2)


def _s5_kernel(u_ref, wk_ref, wso_ref, wsi_ref, ap_ref, y_ref, *, blocks_per_seq):
    u = u_ref[0]
    rows = u.shape[0]
    two_p = wso_ref.shape[2]
    s = jnp.dot(u, wso_ref[0], preferred_element_type=F32)
    pos = lax.broadcasted_iota(I32, (rows, two_p), 0) % blocks_per_seq
    n_steps = ap_ref.shape[1]
    x = s
    for k in range(n_steps):
        sh = 1 << k
        prev = pltpu.roll(x, sh, 0)
        prev_sw = pltpu.roll(prev, two_p // 2, 1)
        upd = prev * ap_ref[0, k, 0:1, :] + prev_sw * ap_ref[0, k, 1:2, :]
        x = x + jnp.where(pos >= sh, upd, 0.0)
    xin = jnp.where(pos >= 1, pltpu.roll(x, 1, 0), 0.0)
    y = jnp.dot(u, wk_ref[0], preferred_element_type=F32)
    y = y + jnp.dot(xin.astype(BF16), wsi_ref[0], preferred_element_type=F32)
    y_ref[0] = y.astype(BF16)


def _s5(u2, wk, wso, wsi, apw, blocks_per_seq):
    g, rows, k = u2.shape
    two_p = wso.shape[2]
    n_steps = apw.shape[1]
    return pl.pallas_call(
        functools.partial(_s5_kernel, blocks_per_seq=blocks_per_seq),
        grid=(g,),
        in_specs=[
            pl.BlockSpec((1, rows, k), lambda i: (i, 0, 0)),
            pl.BlockSpec((1, k, k), lambda i: (i, 0, 0)),
            pl.BlockSpec((1, k, two_p), lambda i: (i, 0, 0)),
            pl.BlockSpec((1, two_p, k), lambda i: (i, 0, 0)),
            pl.BlockSpec((1, n_steps, 2, two_p), lambda i: (i, 0, 0, 0)),
        ],
        out_specs=pl.BlockSpec((1, rows, k), lambda i: (i, 0, 0)),
        out_shape=jax.ShapeDtypeStruct((g, rows, k), BF16),
        compiler_params=pltpu.CompilerParams(
            dimension_semantics=("arbitrary",), vmem_limit_bytes=VMEM_LIMIT),
        name="s5",
    )(u2, wk, wso, wsi, apw)


def _back_kernel(x_ref, ya_ref, ys_ref, gw_ref, gb_ref, onb_ref, wo_ref, fnw_ref,
                 rwh_ref, rwl_ref, rb_ref,
                 x1_ref, h2_ref, eidx_ref, rank_ref, gate_ref, cnt_ref, carry_ref):
    i = pl.program_id(0)
    tm = x_ref.shape[0]
    d_g = ya_ref.shape[1]

    @pl.when(i == 0)
    def _():
        carry_ref[...] = jnp.zeros_like(carry_ref)

    y = _gelu(ys_ref[...].astype(F32))
    z = jnp.dot(y.astype(BF16), gw_ref[...], preferred_element_type=F32) + gb_ref[...]
    yb = y * _sigmoid(z)
    ybn = _rms(yb, onb_ref[...]).astype(BF16)
    x1 = (x_ref[...]
          + jnp.dot(ya_ref[...], wo_ref[:d_g, :], preferred_element_type=F32)
          + jnp.dot(ybn, wo_ref[d_g:, :], preferred_element_type=F32))
    x1_ref[...] = x1
    h2 = _rms(x1, fnw_ref[...])
    h2_ref[...] = h2

    hh = h2.astype(BF16)
    hl = (h2 - hh.astype(F32)).astype(BF16)
    logits = (jnp.dot(hh, rwh_ref[...], preferred_element_type=F32)
              + jnp.dot(hl, rwh_ref[...], preferred_element_type=F32)
              + jnp.dot(hh, rwl_ref[...], preferred_element_type=F32)
              + rb_ref[...])
    lane = lax.broadcasted_iota(I32, (tm, LANES), 1)
    neg = jnp.float32(-jnp.inf)
    work = jnp.where(lane < N_EXPERTS, logits, neg)
    vals, hots = [], []
    eidx = jnp.zeros((tm, LANES), I32)
    for k in range(TOP_K):
        m = jnp.max(work, axis=-1, keepdims=True)
        idx = jnp.min(jnp.where(work == m, lane, LANES), axis=-1, keepdims=True)
        hot = lane == idx
        vals.append(m)
        hots.append(hot)
        eidx = jnp.where(lane == k, idx, eidx)
        work = jnp.where(hot, neg, work)
    exps = [jnp.exp(v - vals[0]) for v in vals]
    denom = exps[0] + exps[1] + exps[2] + exps[3]
    gate = jnp.zeros((tm, LANES), F32)
    for k in range(TOP_K):
        gate = jnp.where(lane == k, exps[k] / denom, gate)

    sel = (hots[0] | hots[1] | hots[2] | hots[3])
    sel_f = jnp.where(sel, 1.0, 0.0)
    r_i = lax.broadcasted_iota(I32, (tm, tm), 0)
    c_i = lax.broadcasted_iota(I32, (tm, tm), 1)
    lower = jnp.where(r_i > c_i, 1.0, 0.0).astype(BF16)
    cum = jnp.dot(lower, sel_f.astype(BF16), preferred_element_type=F32) + carry_ref[...]
    rank = jnp.zeros((tm, LANES), I32)
    for k in range(TOP_K):
        rk = jnp.sum(jnp.where(hots[k], cum, 0.0), axis=-1, keepdims=True)
        rank = jnp.where(lane == k, rk.astype(I32), rank)
    carry_ref[...] = carry_ref[...] + jnp.sum(sel_f, axis=0, keepdims=True)

    eidx_ref[...] = eidx
    rank_ref[...] = rank
    gate_ref[...] = gate
    cnt_ref[...] = carry_ref[...]


def _back(x2, ya, ys, glu_w, glu_b, on_b, w_out, fn_w, router_w, router_b):
    n, d = x2.shape
    d_g = ya.shape[1]
    d_s = ys.shape[1]
    tm = ROW_TILE
    const2 = lambda i: (0, 0)
    rw = jnp.zeros((d, LANES), F32).at[:, :N_EXPERTS].set(router_w.astype(F32))
    rw_hi = rw.astype(BF16)
    rw_lo = (rw - rw_hi.astype(F32)).astype(BF16)
    rb = jnp.zeros((1, LANES), F32).at[0, :N_EXPERTS].set(router_b.astype(F32))
    tok_tile = lambda i: (i, 0)
    return pl.pallas_call(
        _back_kernel,
        grid=(n // tm,),
        in_specs=[
            pl.BlockSpec((tm, d), tok_tile),
            pl.BlockSpec((tm, d_g), tok_tile),
            pl.BlockSpec((tm, d_s), tok_tile),
            pl.BlockSpec((d_s, d_s), const2),
            pl.BlockSpec((1, d_s), const2),
            pl.BlockSpec((1, d_s), const2),
            pl.BlockSpec((d_g + d_s, d), const2),
            pl.BlockSpec((1, d), const2),
            pl.BlockSpec((d, LANES), const2),
            pl.BlockSpec((d, LANES), const2),
            pl.BlockSpec((1, LANES), const2),
        ],
        out_specs=[
            pl.BlockSpec((tm, d), tok_tile),
            pl.BlockSpec((tm, d), tok_tile),
            pl.BlockSpec((tm, LANES), tok_tile),
            pl.BlockSpec((tm, LANES), tok_tile),
            pl.BlockSpec((tm, LANES), tok_tile),
            pl.BlockSpec((1, LANES), const2),
        ],
        out_shape=[
            jax.ShapeDtypeStruct((n, d), F32),
            jax.ShapeDtypeStruct((n, d), F32),
            jax.ShapeDtypeStruct((n, LANES), I32),
            jax.ShapeDtypeStruct((n, LANES), I32),
            jax.ShapeDtypeStruct((n, LANES), F32),
            jax.ShapeDtypeStruct((1, LANES), F32),
        ],
        scratch_shapes=[pltpu.VMEM((1, LANES), F32)],
        compiler_params=pltpu.CompilerParams(
            dimension_semantics=("arbitrary",), vmem_limit_bytes=VMEM_LIMIT),
        name="back",
    )(x2, ya, ys, glu_w.astype(BF16), glu_b.reshape(1, d_s), on_b.reshape(1, d_s),
      w_out.astype(BF16), fn_w.reshape(1, d), rw_hi, rw_lo, rb)


def _expert_kernel(be_ref, bv_ref, idx_hbm, h_hbm, wgu_ref, bgu_ref, wdn_ref, bdn_ref,
                   y_hbm, idx_smem, xbuf, ybuf, sem_idx, sem_g, sem_s):
    del be_ref
    i = pl.program_id(0)
    te = xbuf.shape[0]
    d_e = wdn_ref.shape[1]

    def gather_copy(r, tok):
        return pltpu.make_async_copy(h_hbm.at[pl.ds(tok, 1)], xbuf.at[pl.ds(r, 1)], sem_g)

    def scatter_copy(r, dst):
        return pltpu.make_async_copy(ybuf.at[pl.ds(r, 1)], y_hbm.at[pl.ds(dst, 1)], sem_s)

    @pl.when(i == 0)
    def _():
        n_real = y_hbm.shape[0] - N_DUMP_BLOCKS * te
        ybuf[...] = jnp.zeros_like(ybuf)
        for j in range(N_DUMP_BLOCKS):
            cp0 = pltpu.make_async_copy(ybuf, y_hbm.at[pl.ds(n_real + j * te, te)], sem_s)
            cp0.start()
            cp0.wait()

    @pl.when(bv_ref[i] > 0)
    def _():
        cp = pltpu.make_async_copy(idx_hbm.at[i], idx_smem, sem_idx)
        cp.start()
        cp.wait()

        def g_start(r, c):
            gather_copy(r, idx_smem[r // LANES, r % LANES]).start()
            return c

        def g_wait(r, c):
            gather_copy(r, 0).wait()
            return c

        lax.fori_loop(0, te, g_start, 0)
        lax.fori_loop(0, te, g_wait, 0)

        x = xbuf[...].astype(BF16)
        gu = jnp.dot(x, wgu_ref[0], preferred_element_type=F32) + bgu_ref[0]
        glu = jnp.minimum(gu[:, :d_e], SWIGLU_LIMIT)
        lin = jnp.clip(gu[:, d_e:], -SWIGLU_LIMIT, SWIGLU_LIMIT)
        act = glu * _sigmoid(SWIGLU_ALPHA * glu) * (lin + 1.0)
        ybuf[...] = jnp.dot(act.astype(BF16), wdn_ref[0], preferred_element_type=F32) + bdn_ref[0]

        rows_per = te // LANES

        def s_start(r, c):
            scatter_copy(r, idx_smem[rows_per + r // LANES, r % LANES]).start()
            return c

        def s_wait(r, c):
            scatter_copy(r, 0).wait()
            return c

        lax.fori_loop(0, te, s_start, 0)
        lax.fori_loop(0, te, s_wait, 0)


def _experts(block_expert, block_valid, idx_slab, h2, w_gu, b_gu, w_dn, b_dn, n_out_rows):
    n, d = h2.shape
    n_blocks = block_expert.shape[0]
    te = EXPERT_TILE
    d_gu = w_gu.shape[2]
    d_e = w_dn.shape[1]
    e_map3 = lambda i, be, bv: (be[i], 0, 0)
    grid_spec = pltpu.PrefetchScalarGridSpec(
        num_scalar_prefetch=2,
        grid=(n_blocks,),
        in_specs=[
            pl.BlockSpec(memory_space=pl.ANY),
            pl.BlockSpec(memory_space=pl.ANY),
            pl.BlockSpec((1, d, d_gu), e_map3),
            pl.BlockSpec((1, 1, d_gu), e_map3),
            pl.BlockSpec((1, d_e, d), e_map3),
            pl.BlockSpec((1, 1, d), e_map3),
        ],
        out_specs=pl.BlockSpec(memory_space=pl.ANY),
        scratch_shapes=[
            pltpu.SMEM(idx_slab.shape[1:], I32),
            pltpu.VMEM((te, d), F32),
            pltpu.VMEM((te, d), F32),
            pltpu.SemaphoreType.DMA,
            pltpu.SemaphoreType.DMA,
            pltpu.SemaphoreType.DMA,
        ],
    )
    return pl.pallas_call(
        _expert_kernel,
        grid_spec=grid_spec,
        out_shape=jax.ShapeDtypeStruct((n_out_rows, d), F32),
        compiler_params=pltpu.CompilerParams(
            dimension_semantics=("arbitrary",), vmem_limit_bytes=VMEM_LIMIT),
        name="experts",
    )(block_expert, block_valid, idx_slab, h2, w_gu.astype(BF16),
      b_gu.reshape(b_gu.shape[0], 1, d_gu), w_dn.astype(BF16), b_dn.reshape(b_dn.shape[0], 1, d))


def _routing_tables(eidx, rank, counts, n):
    te = EXPERT_TILE
    n_as = n * TOP_K
    n_blocks = n_as // te + N_EXPERTS
    n_pad = n_blocks * te
    padded = (counts + te - 1) // te * te
    pend = jnp.cumsum(padded)
    pstart = pend - padded
    dest = pstart[eidx] + rank
    slot_a = jnp.full((n_pad,), -1, I32).at[dest.reshape(-1)].set(jnp.arange(n_as, dtype=I32))
    valid = slot_a >= 0
    tok = slot_a >> 2
    k = slot_a & (TOP_K - 1)
    slot = jnp.arange(n_pad, dtype=I32)
    n_dump = N_DUMP_BLOCKS * te
    src = jnp.where(valid, tok, 0)
    dst = jnp.where(valid, k * n + tok, n_as + slot % n_dump)
    rows_per = te // LANES
    slab = jnp.zeros((n_blocks, 8, LANES), I32)
    slab = slab.at[:, :rows_per].set(src.reshape(n_blocks, rows_per, LANES))
    slab = slab.at[:, rows_per:2 * rows_per].set(dst.reshape(n_blocks, rows_per, LANES))
    bstart = jnp.arange(n_blocks, dtype=I32) * te
    block_valid = (bstart < pend[-1]).astype(I32)
    block_expert = jnp.minimum(jnp.searchsorted(pend, bstart, side='right'), N_EXPERTS - 1).astype(I32)
    last_valid = jnp.maximum(pend[-1] // te - 1, 0)
    block_expert = jnp.where(block_valid > 0, block_expert, block_expert[last_valid])
    return block_expert, block_valid, slab, n_as + n_dump


def _combine_kernel(x1_ref, y0_ref, y1_ref, y2_ref, y3_ref, g_ref, fw_ref, o_ref, *, final_norm):
    g = g_ref[...]
    out = x1_ref[...]
    for k, y_ref in enumerate((y0_ref, y1_ref, y2_ref, y3_ref)):
        out = out + g[:, k:k + 1] * y_ref[...]
    if final_norm:
        out = _rms(out, fw_ref[...])
    o_ref[...] = out


def _combine(x1, y4, gate, final_w, final_norm):
    n, d = x1.shape
    tm = ROW_TILE
    nt = n // tm
    tok_tile = lambda i: (i, 0)
    y_specs = [pl.BlockSpec((tm, d), functools.partial(lambda i, k: (k * nt + i, 0), k=k))
               for k in range(TOP_K)]
    return pl.pallas_call(
        functools.partial(_combine_kernel, final_norm=final_norm),
        grid=(nt,),
        in_specs=[pl.BlockSpec((tm, d), tok_tile)] + y_specs + [
            pl.BlockSpec((tm, LANES), tok_tile),
            pl.BlockSpec((1, d), lambda i: (0, 0)),
        ],
        out_specs=pl.BlockSpec((tm, d), tok_tile),
        out_shape=jax.ShapeDtypeStruct((n, d), F32),
        compiler_params=pltpu.CompilerParams(
            dimension_semantics=("arbitrary",), vmem_limit_bytes=VMEM_LIMIT),
        name="combine",
    )(x1, y4, y4, y4, y4, gate, final_w.reshape(1, d))


def kernel(x, attn_norm_w, w_in, sgu_ln_w, sgu_ln_b, sgu_w, sgu_b, ssm_a_re, ssm_a_im, ssm_b_re, ssm_b_im, ssm_c_re, ssm_c_im, ssm_d, ssm_log_dt, ssm_glu_w, ssm_glu_b, out_norm_a, out_norm_b, w_out, ffn_norm_w, router_w, router_b, w_gate_up, b_gate_up, w_down, b_down, final_norm_w):
    b, l, d = x.shape
    n = b * l
    depth = w_in.shape[0]
    groups = ssm_a_re.shape[1]
    d_s = groups * SSM_GROUP
    blocks_per_seq = l // SSM_T
    n_steps = max(1, (blocks_per_seq - 1).bit_length())
    x2 = x.reshape(n, d).astype(F32)
    for layer in range(depth):
        ya, us = _front(x2, attn_norm_w[layer], w_in[layer], sgu_ln_w[layer], sgu_ln_b[layer],
                        sgu_w[layer], sgu_b[layer], out_norm_a[layer])
        wk, wso, wsi, dta = _s5_weights(ssm_a_re[layer], ssm_a_im[layer], ssm_b_re[layer], ssm_b_im[layer],
                                        ssm_c_re[layer], ssm_c_im[layer], ssm_d[layer], ssm_log_dt[layer])
        apw = _s5_scan_powers(dta, n_steps)
        u2 = us.reshape(n // SSM_T, SSM_T, groups, SSM_GROUP).transpose(2, 0, 1, 3)
        u2 = u2.reshape(groups, n // SSM_T, SSM_T * SSM_GROUP)
        y2 = _s5(u2, wk, wso, wsi, apw, blocks_per_seq)
        ys = y2.reshape(groups, n // SSM_T, SSM_T, SSM_GROUP).transpose(1, 2, 0, 3).reshape(n, d_s)
        x1, h2, eidx, rank, gate, cnt = _back(x2, ya, ys, ssm_glu_w[layer], ssm_glu_b[layer],
                                              out_norm_b[layer], w_out[layer], ffn_norm_w[layer],
                                              router_w[layer], router_b[layer])
        counts = cnt[0, :N_EXPERTS].astype(I32)
        block_expert, block_valid, slab, n_rows = _routing_tables(
            eidx[:, :TOP_K], rank[:, :TOP_K], counts, n)
        y4 = _experts(block_expert, block_valid, slab, h2, w_gate_up[layer], b_gate_up[layer],
                      w_down[layer], b_down[layer], n_rows)
        x2 = _combine(x1, y4, gate, final_norm_w, final_norm=(layer == depth - 1))
    return x2.reshape(b, l, d).astype(x.dtype)
```

```python
import functools
import math

import jax
import jax.numpy as jnp
from jax import lax
from jax.experimental import pallas as pl
from jax.experimental.pallas import tpu as pltpu

F32 = jnp.float32
BF16 = jnp.bfloat16
I32 = jnp.int32

EPS = 1e-5
N_HEADS = 4
CHUNK = 128
SSM_GROUP = 16
SSM_STATE = 64
SSM_T = 16
N_EXPERTS = 32
TOP_K = 4
SWIGLU_LIMIT = 7.0
SWIGLU_ALPHA = 1.702
LANES = 128
SUBLANES = 8
GROUPS_PER_LANE_BLOCK = LANES // SSM_GROUP

ROW_TILE = 512
EXPERT_TILE = 256
VMEM_LIMIT = 56 * 1024 * 1024


def _gelu(x):
    return 0.5 * x * (1.0 + jnp.tanh(math.sqrt(2.0 / math.pi) * (x + 0.044715 * (x * x * x))))


def _sigmoid(x):
    return 1.0 / (1.0 + jnp.exp(-x))


def _rms(x, w):
    return x * lax.rsqrt(jnp.mean(x * x, axis=-1, keepdims=True) + EPS) * w


def _params(**kw):
    return pltpu.CompilerParams(dimension_semantics=kw.pop("sem"), vmem_limit_bytes=VMEM_LIMIT, **kw)


def _front_kernel(x_ref, nw_ref, win_ref, lnw_ref, lnb_ref, ws_ref, bst_ref, ona_ref,
                  ya_ref, us_ref, mixed_ref, us_scr):
    d_g = ya_ref.shape[1]
    hd = d_g // N_HEADS
    tm = x_ref.shape[0]
    x = x_ref[...]
    h = _rms(x, nw_ref[...]).astype(BF16)
    proj = jnp.dot(h, win_ref[...], preferred_element_type=F32)
    u = _gelu(proj[:, :d_g])
    v = _gelu(proj[:, d_g:2 * d_g])
    n_lb = us_scr.shape[0]
    for q in range(n_lb):
        us_scr[q] = proj[:, 2 * d_g + q * LANES:2 * d_g + (q + 1) * LANES]
    mu = jnp.mean(v, axis=-1, keepdims=True)
    vc = v - mu
    var = jnp.mean(vc * vc, axis=-1, keepdims=True)
    vb = (vc * lax.rsqrt(var + EPS) * lnw_ref[...] + lnb_ref[...]).astype(BF16)
    row = lax.broadcasted_iota(I32, (CHUNK, CHUNK), 0)
    col = lax.broadcasted_iota(I32, (CHUNK, CHUNK), 1)
    causal = row >= col
    for hh in range(N_HEADS):
        w = jnp.where(causal, ws_ref[hh], 0.0).astype(BF16)
        bias = bst_ref[:, hh:hh + 1]
        for c in range(tm // CHUNK):
            vv = vb[c * CHUNK:(c + 1) * CHUNK, hh * hd:(hh + 1) * hd]
            m = jnp.dot(w, vv, preferred_element_type=F32) + bias
            mixed_ref[c * CHUNK:(c + 1) * CHUNK, hh * hd:(hh + 1) * hd] = m
    ya = u * mixed_ref[...]
    ya_ref[...] = _rms(ya, ona_ref[...]).astype(BF16)
    for s in range(SSM_T):
        for q in range(n_lb):
            us_ref[s, :, q * LANES:(q + 1) * LANES] = (
                us_scr[q, pl.ds(s, tm // SSM_T, stride=SSM_T), :].astype(BF16))


def _front(x2, nw, w_in, ln_w, ln_b, w_s, b_s, on_a):
    n, d = x2.shape
    d_g = ln_w.shape[0]
    d_s = w_in.shape[1] - 2 * d_g
    tm = ROW_TILE
    const2 = lambda i: (0, 0)
    return pl.pallas_call(
        _front_kernel,
        grid=(n // tm,),
        in_specs=[
            pl.BlockSpec((tm, d), lambda i: (i, 0)),
            pl.BlockSpec((1, d), const2),
            pl.BlockSpec(w_in.shape, const2),
            pl.BlockSpec((1, d_g), const2),
            pl.BlockSpec((1, d_g), const2),
            pl.BlockSpec(w_s.shape, lambda i: (0, 0, 0)),
            pl.BlockSpec((CHUNK, N_HEADS), const2),
            pl.BlockSpec((1, d_g), const2),
        ],
        out_specs=[
            pl.BlockSpec((tm, d_g), lambda i: (i, 0)),
            pl.BlockSpec((SSM_T, tm // SSM_T, d_s), lambda i: (0, i, 0)),
        ],
        out_shape=[
            jax.ShapeDtypeStruct((n, d_g), BF16),
            jax.ShapeDtypeStruct((SSM_T, n // SSM_T, d_s), BF16),
        ],
        scratch_shapes=[pltpu.VMEM((tm, d_g), F32), pltpu.VMEM((d_s // LANES, tm, LANES), F32)],
        compiler_params=_params(sem=("arbitrary",)),
        name="front",
    )(x2, nw.reshape(1, d), w_in.astype(BF16), ln_w.reshape(1, d_g), ln_b.reshape(1, d_g),
      w_s, b_s.T, on_a.reshape(1, d_g))


def _s5_weights(a_re, a_im, b_re, b_im, c_re, c_im, d, log_dt, n_steps):
    g, p = a_re.shape
    hch = b_re.shape[-1]
    t = SSM_T
    gl = GROUPS_PER_LANE_BLOCK
    lb = g // gl
    a = lax.complex(a_re.astype(F32), a_im.astype(F32))
    dt = jnp.exp(log_dt.astype(F32))[:, None]
    dta = dt * a
    a_bar = jnp.exp(dta)
    b_bar = ((a_bar - 1.0) / a)[..., None] * lax.complex(b_re.astype(F32), b_im.astype(F32))
    c = lax.complex(c_re.astype(F32), c_im.astype(F32))
    lags = jnp.arange(t + 1, dtype=F32)
    pw = jnp.exp(lags[None, :, None] * dta[:, None, :])
    kern = jnp.einsum('gpi,glp,gop->glio', b_bar, pw[:, :t], c).real
    s_idx = jnp.arange(t)[:, None]
    t_idx = jnp.arange(t)[None, :]
    lag = t_idx - s_idx
    blocks = kern[:, jnp.clip(lag, 0, t - 1)]
    blocks = jnp.where((lag >= 0)[None, :, :, None, None], blocks, 0.0)
    dmat = d.astype(F32).reshape(g, hch)[:, None, None, None, :] * jnp.eye(hch, dtype=F32)[None, None, None]
    blocks = blocks + jnp.where((lag == 0)[None, :, :, None, None], dmat, 0.0)
    eye_g = jnp.eye(gl, dtype=F32)
    blocks = blocks.reshape(lb, gl, t, t, hch, hch)
    wk = jnp.einsum('agstio,gh->asgitho', blocks, eye_g).reshape(lb, t * gl * hch, t * gl * hch)
    so = pw[:, t - 1 - jnp.arange(t)][:, :, :, None] * b_bar[:, None, :, :]
    so = jnp.stack([so.real, so.imag], axis=0).reshape(2, lb, gl, t, p, hch)
    wso = jnp.einsum('cagspi,gh->asgichp', so, eye_g).reshape(lb, t * gl * hch, 2 * gl * p)
    si = c[:, None, :, :] * pw[:, 1:t + 1][:, :, None, :]
    si = jnp.stack([si.real, -si.imag], axis=0).reshape(2, lb, gl, t, hch, p)
    wsi = jnp.einsum('cagtop,gh->acgptho', si, eye_g).reshape(lb, 2 * gl * p, t * gl * hch)
    steps = SSM_T * (2.0 ** jnp.arange(n_steps, dtype=F32))
    ap = jnp.exp(steps[None, :, None] * dta[:, None, :])
    ap = ap.reshape(lb, gl, n_steps, p).transpose(0, 2, 1, 3).reshape(lb, n_steps, gl * p)
    mul_same = jnp.concatenate([ap.real, ap.real], axis=-1)
    mul_swap = jnp.concatenate([-ap.imag, ap.imag], axis=-1)
    apw = jnp.stack([mul_same, mul_swap], axis=2)
    return wk.astype(BF16), wso.astype(BF16), wsi.astype(BF16), apw


def _s5_kernel(u_ref, wk_ref, wso_ref, wsi_ref, ap_ref, y_ref):
    t_blk = u_ref.shape[0]
    rows = u_ref.shape[1]
    n_state = wso_ref.shape[2]
    xcat = jnp.concatenate([u_ref[s] for s in range(t_blk)], axis=1)
    x = jnp.dot(xcat, wso_ref[0], preferred_element_type=F32)
    pos = lax.broadcasted_iota(I32, (rows, n_state), 0)
    for k in range(ap_ref.shape[1]):
        sh = 1 << k
        prev = pltpu.roll(x, sh, 0)
        prev_sw = pltpu.roll(prev, n_state // 2, 1)
        upd = prev * ap_ref[0, k, 0:1, :] + prev_sw * ap_ref[0, k, 1:2, :]
        x = x + jnp.where(pos >= sh, upd, 0.0)
    xin = jnp.where(pos >= 1, pltpu.roll(x, 1, 0), 0.0).astype(BF16)
    for j in range(t_blk // 2):
        k_hi = (2 * j + 2) * LANES
        lo = 2 * j * LANES
        y = jnp.dot(xcat[:, :k_hi], wk_ref[0, :k_hi, lo:lo + 2 * LANES], preferred_element_type=F32)
        y = y + jnp.dot(xin, wsi_ref[0, :, lo:lo + 2 * LANES], preferred_element_type=F32)
        y_ref[2 * j] = y[:, :LANES].astype(BF16)
        y_ref[2 * j + 1] = y[:, LANES:].astype(BF16)


def _s5(us3, wk, wso, wsi, apw, blocks_per_seq):
    t_blk, n_blocks, d_s = us3.shape
    lb = wk.shape[0]
    n_seq = n_blocks // blocks_per_seq
    w_map = lambda a, b: (a, 0, 0)
    io_spec = pl.BlockSpec((t_blk, blocks_per_seq, LANES), lambda a, b: (0, b, a))
    return pl.pallas_call(
        _s5_kernel,
        grid=(lb, n_seq),
        in_specs=[
            io_spec,
            pl.BlockSpec((1,) + wk.shape[1:], w_map),
            pl.BlockSpec((1,) + wso.shape[1:], w_map),
            pl.BlockSpec((1,) + wsi.shape[1:], w_map),
            pl.BlockSpec((1,) + apw.shape[1:], lambda a, b: (a, 0, 0, 0)),
        ],
        out_specs=io_spec,
        out_shape=jax.ShapeDtypeStruct(us3.shape, BF16),
        compiler_params=_params(sem=("arbitrary", "arbitrary")),
        name="s5",
    )(us3, wk, wso, wsi, apw)


def _back_kernel(x_ref, ya_ref, ys_ref, gw_ref, gb_ref, onb_ref, wo_ref, fnw_ref,
                 rwh_ref, rwl_ref, rb_ref,
                 x1_ref, h2_ref, gate_ref, er_ref, cnt_ref, carry_ref, ys_scr):
    i = pl.program_id(0)
    tm = x_ref.shape[0]
    d_g = ya_ref.shape[1]

    @pl.when(i == 0)
    def _():
        carry_ref[...] = jnp.zeros_like(carry_ref)

    n_lb = ys_scr.shape[0]
    for t in range(SSM_T):
        for q in range(n_lb):
            ys_scr[q, pl.ds(t, tm // SSM_T, stride=SSM_T), :] = (
                ys_ref[t, :, q * LANES:(q + 1) * LANES].astype(F32))
    y = _gelu(jnp.concatenate([ys_scr[q] for q in range(n_lb)], axis=1))
    z = jnp.dot(y.astype(BF16), gw_ref[...], preferred_element_type=F32) + gb_ref[...]
    yb = y * _sigmoid(z)
    ybn = _rms(yb, onb_ref[...]).astype(BF16)
    x1 = (x_ref[...]
          + jnp.dot(ya_ref[...], wo_ref[:d_g, :], preferred_element_type=F32)
          + jnp.dot(ybn, wo_ref[d_g:, :], preferred_element_type=F32))
    x1_ref[...] = x1
    h2 = _rms(x1, fnw_ref[...])
    h2_ref[...] = h2

    hh = h2.astype(BF16)
    hl = (h2 - hh.astype(F32)).astype(BF16)
    logits = (jnp.dot(hh, rwh_ref[...], preferred_element_type=F32)
              + jnp.dot(hl, rwh_ref[...], preferred_element_type=F32)
              + jnp.dot(hh, rwl_ref[...], preferred_element_type=F32)
              + rb_ref[...])
    lane = lax.broadcasted_iota(I32, (tm, LANES), 1)
    neg = jnp.float32(-jnp.inf)
    work = jnp.where(lane < N_EXPERTS, logits, neg)
    vals, hots, idxs = [], [], []
    for k in range(TOP_K):
        m = jnp.max(work, axis=-1, keepdims=True)
        idx = jnp.min(jnp.where(work == m, lane, LANES), axis=-1, keepdims=True)
        hot = lane == idx
        vals.append(m)
        hots.append(hot)
        idxs.append(idx)
        work = jnp.where(hot, neg, work)
    exps = [jnp.exp(v - vals[0]) for v in vals]
    denom = exps[0] + exps[1] + exps[2] + exps[3]
    gate = jnp.zeros((tm, LANES), F32)
    for k in range(TOP_K):
        gate = jnp.where(lane == k, exps[k] / denom, gate)
    gate_ref[...] = gate

    sel = (hots[0] | hots[1] | hots[2] | hots[3])
    sel_f = jnp.where(sel, 1.0, 0.0)
    r_i = lax.broadcasted_iota(I32, (tm, tm), 0)
    c_i = lax.broadcasted_iota(I32, (tm, tm), 1)
    lower = jnp.where(r_i > c_i, 1.0, 0.0).astype(BF16)
    cum = jnp.dot(lower, sel_f.astype(BF16), preferred_element_type=F32) + carry_ref[...]
    er = jnp.zeros((tm, LANES), I32)
    for k in range(TOP_K):
        rk = jnp.sum(jnp.where(hots[k], cum, 0.0), axis=-1, keepdims=True)
        er = jnp.where(lane == k, idxs[k], er)
        er = jnp.where(lane == TOP_K + k, rk.astype(I32), er)
    carry_ref[...] = carry_ref[...] + jnp.sum(sel_f, axis=0, keepdims=True)
    cnt_ref[...] = carry_ref[...]
    er_ref[0] = jnp.transpose(er)[:SUBLANES, :]


def _back(x2, ya, ys3, glu_w, glu_b, on_b, w_out, fn_w, router_w, router_b):
    n, d = x2.shape
    d_g = ya.shape[1]
    d_s = ys3.shape[2]
    tm = ROW_TILE
    nt = n // tm
    const2 = lambda i: (0, 0)
    rw = jnp.zeros((d, LANES), F32).at[:, :N_EXPERTS].set(router_w.astype(F32))
    rw_hi = rw.astype(BF16)
    rw_lo = (rw - rw_hi.astype(F32)).astype(BF16)
    rb = jnp.zeros((1, LANES), F32).at[0, :N_EXPERTS].set(router_b.astype(F32))
    tok_tile = lambda i: (i, 0)
    return pl.pallas_call(
        _back_kernel,
        grid=(nt,),
        in_specs=[
            pl.BlockSpec((tm, d), tok_tile),
            pl.BlockSpec((tm, d_g), tok_tile),
            pl.BlockSpec((SSM_T, tm // SSM_T, d_s), lambda i: (0, i, 0)),
            pl.BlockSpec((d_s, d_s), const2),
            pl.BlockSpec((1, d_s), const2),
            pl.BlockSpec((1, d_s), const2),
            pl.BlockSpec((d_g + d_s, d), const2),
            pl.BlockSpec((1, d), const2),
            pl.BlockSpec((d, LANES), const2),
            pl.BlockSpec((d, LANES), const2),
            pl.BlockSpec((1, LANES), const2),
        ],
        out_specs=[
            pl.BlockSpec((tm, d), tok_tile),
            pl.BlockSpec((tm, d), tok_tile),
            pl.BlockSpec((tm, LANES), tok_tile),
            pl.BlockSpec((1, SUBLANES, tm), lambda i: (i, 0, 0)),
            pl.BlockSpec((1, LANES), const2),
        ],
        out_shape=[
            jax.ShapeDtypeStruct((n, d), F32),
            jax.ShapeDtypeStruct((n, d), F32),
            jax.ShapeDtypeStruct((n, LANES), F32),
            jax.ShapeDtypeStruct((nt, SUBLANES, tm), I32),
            jax.ShapeDtypeStruct((1, LANES), F32),
        ],
        scratch_shapes=[pltpu.VMEM((1, LANES), F32), pltpu.VMEM((d_s // LANES, tm, LANES), F32)],
        compiler_params=_params(sem=("arbitrary",)),
        name="back",
    )(x2, ya, ys3, glu_w.astype(BF16), glu_b.reshape(1, d_s), on_b.reshape(1, d_s),
      w_out.astype(BF16), fn_w.reshape(1, d), rw_hi, rw_lo, rb)


def _routing_tables(er, counts, n):
    te = EXPERT_TILE
    n_blocks = n * TOP_K // te + N_EXPERTS
    nb = (counts + te - 1) // te
    cum = jnp.cumsum(nb)
    pstart = (cum - nb) * te
    dest = pstart[er[:, :TOP_K, :]] + er[:, TOP_K:2 * TOP_K, :]
    dest = jnp.concatenate([dest, jnp.zeros_like(dest)], axis=1)
    total = cum[-1]
    j = jnp.arange(n_blocks, dtype=I32)
    e_j = jnp.sum((cum[None, :] <= jnp.minimum(j, total - 1)[:, None]).astype(I32), axis=1)
    valid = (j < total).astype(I32)
    last_block = jnp.where(nb > 0, (cum - 1) * te, -1)
    last_block = jnp.concatenate([last_block, total[None]])
    return dest.astype(I32), e_j.astype(I32), valid, last_block.astype(I32)


def _dispatch_kernel(last_ref, dest_hbm, h_ref, xs_hbm, idx_smem, zbuf, sem_idx, sem_rows, sem_z):
    i = pl.program_id(0)
    nt = pl.num_programs(0)
    tm = h_ref.shape[0]
    te = zbuf.shape[0]
    slot = i % 2

    def idx_copy(t, s):
        return pltpu.make_async_copy(dest_hbm.at[t], idx_smem.at[s], sem_idx.at[s])

    @pl.when(i == 0)
    def _():
        zbuf[...] = jnp.zeros_like(zbuf)

        def zero_copy(e):
            row = pl.multiple_of(jnp.maximum(last_ref[e], 0), te)
            return pltpu.make_async_copy(zbuf, xs_hbm.at[pl.ds(row, te)], sem_z)

        for e in range(N_EXPERTS):
            @pl.when(last_ref[e] >= 0)
            def _():
                zero_copy(e).start()
        for e in range(N_EXPERTS):
            @pl.when(last_ref[e] >= 0)
            def _():
                zero_copy(e).wait()
        n_blocks = xs_hbm.shape[0] // te
        total = last_ref[N_EXPERTS]

        def tail_copy(j):
            return pltpu.make_async_copy(zbuf, xs_hbm.at[pl.ds(j * te, te)], sem_z)

        for j in range(n_blocks - N_EXPERTS, n_blocks):
            @pl.when(j >= total)
            def _():
                tail_copy(j).start()
        for j in range(n_blocks - N_EXPERTS, n_blocks):
            @pl.when(j >= total)
            def _():
                tail_copy(j).wait()
        idx_copy(0, 0).start()

    @pl.when(i + 1 < nt)
    def _():
        idx_copy(i + 1, 1 - slot).start()

    idx_copy(i, slot).wait()

    def body(n, c):
        for k in range(TOP_K):
            pltpu.make_async_copy(h_ref.at[pl.ds(n, 1)],
                                  xs_hbm.at[pl.ds(idx_smem[slot, k, n], 1)],
                                  sem_rows.at[k]).start(priority=k % 2)
        return c
    lax.fori_loop(0, tm, body, 0)
    for k in range(TOP_K):
        pltpu.make_async_copy(h_ref, xs_hbm.at[pl.ds(0, tm)], sem_rows.at[k]).wait()


def _dispatch(last_block, dest, h2, n_rows):
    n, d = h2.shape
    tm = ROW_TILE
    grid_spec = pltpu.PrefetchScalarGridSpec(
        num_scalar_prefetch=1,
        grid=(n // tm,),
        in_specs=[
            pl.BlockSpec(memory_space=pl.ANY),
            pl.BlockSpec((tm, d), lambda i, last: (i, 0)),
        ],
        out_specs=pl.BlockSpec(memory_space=pl.ANY),
        scratch_shapes=[
            pltpu.SMEM((2, SUBLANES, tm), I32),
            pltpu.VMEM((EXPERT_TILE, d), F32),
            pltpu.SemaphoreType.DMA((2,)),
            pltpu.SemaphoreType.DMA((TOP_K,)),
            pltpu.SemaphoreType.DMA,
        ],
    )
    return pl.pallas_call(
        _dispatch_kernel,
        grid_spec=grid_spec,
        out_shape=jax.ShapeDtypeStruct((n_rows, d), F32),
        compiler_params=_params(sem=("arbitrary",)),
        name="dispatch",
    )(last_block, dest, h2)


def _expert_kernel(be_ref, bv_ref, xs_ref, wgu_ref, bgu_ref, wdn_ref, bdn_ref, ys_ref):
    del be_ref
    i = pl.program_id(0)
    d_e = wdn_ref.shape[1]

    @pl.when(bv_ref[i] > 0)
    def _():
        x = xs_ref[...].astype(BF16)
        gu = jnp.dot(x, wgu_ref[0], preferred_element_type=F32) + bgu_ref[0]
        glu = jnp.minimum(gu[:, :d_e], SWIGLU_LIMIT)
        lin = jnp.clip(gu[:, d_e:], -SWIGLU_LIMIT, SWIGLU_LIMIT)
        act = glu * _sigmoid(SWIGLU_ALPHA * glu) * (lin + 1.0)
        ys_ref[...] = jnp.dot(act.astype(BF16), wdn_ref[0], preferred_element_type=F32) + bdn_ref[0]

    @pl.when(bv_ref[i] == 0)
    def _():
        ys_ref[...] = jnp.zeros_like(ys_ref)


def _experts(block_expert, block_valid, xs, w_gu, b_gu, w_dn, b_dn):
    d = xs.shape[1]
    n_blocks = block_expert.shape[0]
    te = EXPERT_TILE
    d_gu = w_gu.shape[2]
    d_e = w_dn.shape[1]
    e_map3 = lambda i, be, bv: (be[i], 0, 0)
    x_map = lambda i, be, bv: (jnp.where(bv[i] > 0, i, 0), 0)
    grid_spec = pltpu.PrefetchScalarGridSpec(
        num_scalar_prefetch=2,
        grid=(n_blocks,),
        in_specs=[
            pl.BlockSpec((te, d), x_map),
            pl.BlockSpec((1, d, d_gu), e_map3),
            pl.BlockSpec((1, 1, d_gu), e_map3),
            pl.BlockSpec((1, d_e, d), e_map3),
            pl.BlockSpec((1, 1, d), e_map3),
        ],
        out_specs=pl.BlockSpec((te, d), lambda i, be, bv: (i, 0)),
    )
    return pl.pallas_call(
        _expert_kernel,
        grid_spec=grid_spec,
        out_shape=jax.ShapeDtypeStruct((n_blocks * te, d), F32),
        compiler_params=_params(sem=("arbitrary",)),
        name="experts",
    )(block_expert, block_valid, xs, w_gu.astype(BF16),
      b_gu.reshape(b_gu.shape[0], 1, d_gu), w_dn.astype(BF16), b_dn.reshape(b_dn.shape[0], 1, d))


def _combine_kernel(dest_hbm, x1_ref, g_ref, fw_ref, ys_hbm, o_ref,
                    idx_smem, ybuf, sem_idx, sem_rows, *, final_norm):
    i = pl.program_id(0)
    nt = pl.num_programs(0)
    tm = x1_ref.shape[0]
    slot = i % 2

    def idx_copy(t, s):
        return pltpu.make_async_copy(dest_hbm.at[t], idx_smem.at[s], sem_idx.at[s])

    def issue_rows(s):
        def body(n, c):
            for k in range(TOP_K):
                pltpu.make_async_copy(ys_hbm.at[pl.ds(idx_smem[s, k, n], 1)],
                                      ybuf.at[s, k, pl.ds(n, 1)],
                                      sem_rows.at[s, k]).start(priority=k % 2)
            return c
        lax.fori_loop(0, tm, body, 0)

    def wait_rows(s):
        for k in range(TOP_K):
            pltpu.make_async_copy(ys_hbm.at[pl.ds(0, tm)], ybuf.at[s, k], sem_rows.at[s, k]).wait()

    @pl.when(i == 0)
    def _():
        cp = idx_copy(0, 0)
        cp.start()
        cp.wait()
        issue_rows(0)

    @pl.when(i + 1 < nt)
    def _():
        cp = idx_copy(i + 1, 1 - slot)
        cp.start()
        cp.wait()
        issue_rows(1 - slot)

    wait_rows(slot)
    g = g_ref[...]
    out = x1_ref[...]
    for k in range(TOP_K):
        out = out + g[:, k:k + 1] * ybuf[slot, k]
    if final_norm:
        out = _rms(out, fw_ref[...])
    o_ref[...] = out


def _combine(x1, ys, gate, dest, final_w, final_norm):
    n, d = x1.shape
    tm = ROW_TILE
    nt = n // tm
    tok_tile = lambda i: (i, 0)
    return pl.pallas_call(
        functools.partial(_combine_kernel, final_norm=final_norm),
        grid=(nt,),
        in_specs=[
            pl.BlockSpec(memory_space=pl.ANY),
            pl.BlockSpec((tm, d), tok_tile),
            pl.BlockSpec((tm, LANES), tok_tile),
            pl.BlockSpec((1, d), lambda i: (0, 0)),
            pl.BlockSpec(memory_space=pl.ANY),
        ],
        out_specs=pl.BlockSpec((tm, d), tok_tile),
        out_shape=jax.ShapeDtypeStruct((n, d), F32),
        scratch_shapes=[
            pltpu.SMEM((2, SUBLANES, tm), I32),
            pltpu.VMEM((2, TOP_K, tm, d), F32),
            pltpu.SemaphoreType.DMA((2,)),
            pltpu.SemaphoreType.DMA((2, TOP_K)),
        ],
        compiler_params=_params(sem=("arbitrary",)),
        name="combine",
    )(dest, x1, gate, final_w.reshape(1, d), ys)


def kernel(x, attn_norm_w, w_in, sgu_ln_w, sgu_ln_b, sgu_w, sgu_b, ssm_a_re, ssm_a_im, ssm_b_re, ssm_b_im, ssm_c_re, ssm_c_im, ssm_d, ssm_log_dt, ssm_glu_w, ssm_glu_b, out_norm_a, out_norm_b, w_out, ffn_norm_w, router_w, router_b, w_gate_up, b_gate_up, w_down, b_down, final_norm_w):
    b, l, d = x.shape
    n = b * l
    depth = w_in.shape[0]
    blocks_per_seq = l // SSM_T
    n_steps = max(1, (blocks_per_seq - 1).bit_length())
    n_rows = (n * TOP_K // EXPERT_TILE + N_EXPERTS) * EXPERT_TILE
    x2 = x.reshape(n, d).astype(F32)
    for layer in range(depth):
        ya, us3 = _front(x2, attn_norm_w[layer], w_in[layer], sgu_ln_w[layer], sgu_ln_b[layer],
                         sgu_w[layer], sgu_b[layer], out_norm_a[layer])
        wk, wso, wsi, apw = _s5_weights(ssm_a_re[layer], ssm_a_im[layer], ssm_b_re[layer], ssm_b_im[layer],
                                        ssm_c_re[layer], ssm_c_im[layer], ssm_d[layer], ssm_log_dt[layer],
                                        n_steps)
        ys3 = _s5(us3, wk, wso, wsi, apw, blocks_per_seq)
        x1, h2, gate, er, cnt = _back(x2, ya, ys3, ssm_glu_w[layer], ssm_glu_b[layer],
                                      out_norm_b[layer], w_out[layer], ffn_norm_w[layer],
                                      router_w[layer], router_b[layer])
        counts = cnt[0, :N_EXPERTS].astype(I32)
        dest, block_expert, block_valid, last_block = _routing_tables(er, counts, n)
        xs = _dispatch(last_block, dest, h2, n_rows)
        ys = _experts(block_expert, block_valid, xs, w_gate_up[layer], b_gate_up[layer],
                      w_down[layer], b_down[layer])
        x2 = _combine(x1, ys, gate, dest, final_norm_w, final_norm=(layer == depth - 1))
    return x2.reshape(b, l, d).astype(x.dtype)
```

```python
import functools
import math

import jax
import jax.numpy as jnp
from jax import lax
from jax.experimental import pallas as pl
from jax.experimental.pallas import tpu as pltpu

F32 = jnp.float32
BF16 = jnp.bfloat16
I32 = jnp.int32

EPS = 1e-5
N_HEADS = 4
CHUNK = 128
SSM_GROUP = 16
SSM_STATE = 64
SSM_T = 16
N_EXPERTS = 32
TOP_K = 4
SWIGLU_LIMIT = 7.0
SWIGLU_ALPHA = 1.702
LANES = 128
SUBLANES = 8
GROUPS_PER_LANE_BLOCK = LANES // SSM_GROUP

ROW_TILE = 512
EXPERT_TILE = 256
VMEM_LIMIT = 56 * 1024 * 1024


def _gelu(x):
    return 0.5 * x * (1.0 + jnp.tanh(math.sqrt(2.0 / math.pi) * (x + 0.044715 * (x * x * x))))


def _sigmoid(x):
    return 1.0 / (1.0 + jnp.exp(-x))


def _rms(x, w):
    return x * lax.rsqrt(jnp.mean(x * x, axis=-1, keepdims=True) + EPS) * w


def _params(**kw):
    return pltpu.CompilerParams(dimension_semantics=kw.pop("sem"), vmem_limit_bytes=VMEM_LIMIT, **kw)


def _front_kernel(x_ref, nw_ref, win_ref, lnw_ref, lnb_ref, ws_ref, bst_ref, ona_ref,
                  ya_ref, us_ref, mixed_ref, us_scr):
    d_g = ya_ref.shape[1]
    hd = d_g // N_HEADS
    tm = x_ref.shape[0]
    x = x_ref[...]
    h = _rms(x, nw_ref[...]).astype(BF16)
    proj = jnp.dot(h, win_ref[...], preferred_element_type=F32)
    u = _gelu(proj[:, :d_g])
    v = _gelu(proj[:, d_g:2 * d_g])
    n_lb = us_scr.shape[0]
    for q in range(n_lb):
        us_scr[q] = proj[:, 2 * d_g + q * LANES:2 * d_g + (q + 1) * LANES]
    mu = jnp.mean(v, axis=-1, keepdims=True)
    vc = v - mu
    var = jnp.mean(vc * vc, axis=-1, keepdims=True)
    vb = (vc * lax.rsqrt(var + EPS) * lnw_ref[...] + lnb_ref[...]).astype(BF16)
    row = lax.broadcasted_iota(I32, (CHUNK, CHUNK), 0)
    col = lax.broadcasted_iota(I32, (CHUNK, CHUNK), 1)
    causal = row >= col
    for hh in range(N_HEADS):
        w = jnp.where(causal, ws_ref[hh], 0.0).astype(BF16)
        bias = bst_ref[:, hh:hh + 1]
        for c in range(tm // CHUNK):
            vv = vb[c * CHUNK:(c + 1) * CHUNK, hh * hd:(hh + 1) * hd]
            m = jnp.dot(w, vv, preferred_element_type=F32) + bias
            mixed_ref[c * CHUNK:(c + 1) * CHUNK, hh * hd:(hh + 1) * hd] = m
    ya = u * mixed_ref[...]
    ya_ref[...] = _rms(ya, ona_ref[...]).astype(BF16)
    for s in range(SSM_T):
        for q in range(n_lb):
            us_ref[s, :, q * LANES:(q + 1) * LANES] = (
                us_scr[q, pl.ds(s, tm // SSM_T, stride=SSM_T), :].astype(BF16))


def _front(x2, nw, w_in, ln_w, ln_b, w_s, b_s, on_a):
    n, d = x2.shape
    d_g = ln_w.shape[0]
    d_s = w_in.shape[1] - 2 * d_g
    tm = ROW_TILE
    const2 = lambda i: (0, 0)
    return pl.pallas_call(
        _front_kernel,
        grid=(n // tm,),
        in_specs=[
            pl.BlockSpec((tm, d), lambda i: (i, 0)),
            pl.BlockSpec((1, d), const2),
            pl.BlockSpec(w_in.shape, const2),
            pl.BlockSpec((1, d_g), const2),
            pl.BlockSpec((1, d_g), const2),
            pl.BlockSpec(w_s.shape, lambda i: (0, 0, 0)),
            pl.BlockSpec((CHUNK, N_HEADS), const2),
            pl.BlockSpec((1, d_g), const2),
        ],
        out_specs=[
            pl.BlockSpec((tm, d_g), lambda i: (i, 0)),
            pl.BlockSpec((SSM_T, tm // SSM_T, d_s), lambda i: (0, i, 0)),
        ],
        out_shape=[
            jax.ShapeDtypeStruct((n, d_g), BF16),
            jax.ShapeDtypeStruct((SSM_T, n // SSM_T, d_s), BF16),
        ],
        scratch_shapes=[pltpu.VMEM((tm, d_g), F32), pltpu.VMEM((d_s // LANES, tm, LANES), F32)],
        compiler_params=_params(sem=("arbitrary",)),
        name="front",
    )(x2, nw.reshape(1, d), w_in.astype(BF16), ln_w.reshape(1, d_g), ln_b.reshape(1, d_g),
      w_s, b_s.T, on_a.reshape(1, d_g))


def _s5_tables(a_re, a_im, b_re, b_im, c_re, c_im, d, log_dt, n_steps):
    g, p = a_re.shape
    hch = b_re.shape[-1]
    t = SSM_T
    gl = GROUPS_PER_LANE_BLOCK
    lb = g // gl
    a = lax.complex(a_re.astype(F32), a_im.astype(F32))
    dt = jnp.exp(log_dt.astype(F32))[:, None]
    dta = dt * a
    a_bar = jnp.exp(dta)
    b_bar = ((a_bar - 1.0) / a)[..., None] * lax.complex(b_re.astype(F32), b_im.astype(F32))
    c = lax.complex(c_re.astype(F32), c_im.astype(F32))
    lags = jnp.arange(t + 1, dtype=F32)
    pw = jnp.exp(lags[None, :, None] * dta[:, None, :])
    kern = jnp.einsum('gpi,glp,gop->glio', b_bar, pw[:, :t], c).real
    dmat = d.astype(F32).reshape(g, 1, 1, hch) * jnp.eye(hch, dtype=F32)[None, None]
    kern = kern + jnp.where((jnp.arange(t) == 0)[None, :, None, None], dmat, 0.0)
    eye_g = jnp.eye(gl, dtype=F32)
    mrow = jnp.einsum('aglio,gh->agilho', kern.reshape(lb, gl, t, hch, hch), eye_g)
    mrow = mrow.reshape(lb, gl * hch, t * gl * hch)
    so = pw[:, t - 1 - jnp.arange(t)][:, :, :, None] * b_bar[:, None, :, :]
    so = jnp.stack([so.real, so.imag], axis=0).reshape(2, lb, gl, t, p, hch)
    soc = so.transpose(1, 3, 2, 5, 0, 4).reshape(lb, t * gl * hch, 2 * p)
    si = c[:, None, :, :] * pw[:, 1:t + 1][:, :, None, :]
    si = jnp.stack([si.real, -si.imag], axis=0).reshape(2, lb, gl, t, hch, p)
    sic = si.transpose(1, 0, 5, 3, 2, 4).reshape(lb, 2 * p, t * gl * hch)
    steps = SSM_T * (2.0 ** jnp.arange(n_steps, dtype=F32))
    ap = jnp.exp(steps[None, :, None] * dta[:, None, :])
    ap = ap.reshape(lb, gl, n_steps, p).transpose(0, 2, 1, 3).reshape(lb, n_steps, gl * p)
    mul_same = jnp.concatenate([ap.real, ap.real], axis=-1)
    mul_swap = jnp.concatenate([-ap.imag, ap.imag], axis=-1)
    apw = jnp.stack([mul_same, mul_swap], axis=2)
    return mrow.astype(BF16), soc.astype(BF16), sic.astype(BF16), apw


def _s5_kernel(u_ref, mrow_ref, soc_ref, sic_ref, ap_ref, y_ref, wk_scr, wso_scr, wsi_scr):
    t_blk = u_ref.shape[0]
    rows = u_ref.shape[1]
    n_in = wk_scr.shape[0]
    n_state = wso_scr.shape[1]
    two_p = soc_ref.shape[2]
    p = two_p // 2
    half = n_state // 2

    @pl.when(pl.program_id(1) == 0)
    def _():
        for s in range(t_blk):
            if s > 0:
                wk_scr[s * LANES:(s + 1) * LANES, :s * LANES] = jnp.zeros((LANES, s * LANES), BF16)
            wk_scr[s * LANES:(s + 1) * LANES, s * LANES:] = mrow_ref[0, :, :n_in - s * LANES]
        r_e = lax.broadcasted_iota(I32, (two_p, n_state), 0)
        c_e = lax.broadcasted_iota(I32, (two_p, n_state), 1)
        spread = jnp.where((r_e // p == c_e // half) & (r_e % p == c_e % p), 1.0, 0.0).astype(BF16)
        full = jnp.dot(soc_ref[0], spread, preferred_element_type=F32)
        r_g = (lax.broadcasted_iota(I32, (n_in, n_state), 0) // SSM_GROUP) % GROUPS_PER_LANE_BLOCK
        c_g = (lax.broadcasted_iota(I32, (n_in, n_state), 1) % half) // p
        wso_scr[...] = jnp.where(r_g == c_g, full, 0.0).astype(BF16)
        r_e = lax.broadcasted_iota(I32, (n_state, two_p), 0)
        c_e = lax.broadcasted_iota(I32, (n_state, two_p), 1)
        gather = jnp.where((c_e // p == r_e // half) & (c_e % p == r_e % p), 1.0, 0.0).astype(BF16)
        full = jnp.dot(gather, sic_ref[0], preferred_element_type=F32)
        r_g = (lax.broadcasted_iota(I32, (n_state, n_in), 0) % half) // p
        c_g = (lax.broadcasted_iota(I32, (n_state, n_in), 1) // SSM_GROUP) % GROUPS_PER_LANE_BLOCK
        wsi_scr[...] = jnp.where(r_g == c_g, full, 0.0).astype(BF16)

    xcat = jnp.concatenate([u_ref[s] for s in range(t_blk)], axis=1)
    x = jnp.dot(xcat, wso_scr[...], preferred_element_type=F32)
    pos = lax.broadcasted_iota(I32, (rows, n_state), 0)
    for k in range(ap_ref.shape[1]):
        sh = 1 << k
        prev = pltpu.roll(x, sh, 0)
        prev_sw = pltpu.roll(prev, half, 1)
        upd = prev * ap_ref[0, k, 0:1, :] + prev_sw * ap_ref[0, k, 1:2, :]
        x = x + jnp.where(pos >= sh, upd, 0.0)
    xin = jnp.where(pos >= 1, pltpu.roll(x, 1, 0), 0.0).astype(BF16)
    for j in range(t_blk // 2):
        k_hi = (2 * j + 2) * LANES
        lo = 2 * j * LANES
        y = jnp.dot(xcat[:, :k_hi], wk_scr[:k_hi, lo:lo + 2 * LANES], preferred_element_type=F32)
        y = y + jnp.dot(xin, wsi_scr[:, lo:lo + 2 * LANES], preferred_element_type=F32)
        y_ref[2 * j] = y[:, :LANES].astype(BF16)
        y_ref[2 * j + 1] = y[:, LANES:].astype(BF16)


def _s5(us3, mrow, soc, sic, apw, blocks_per_seq):
    t_blk, n_blocks, d_s = us3.shape
    lb = mrow.shape[0]
    n_seq = n_blocks // blocks_per_seq
    n_in = t_blk * LANES
    n_state = apw.shape[3]
    w_map = lambda a, b: (a, 0, 0)
    io_spec = pl.BlockSpec((t_blk, blocks_per_seq, LANES), lambda a, b: (0, b, a))
    return pl.pallas_call(
        _s5_kernel,
        grid=(lb, n_seq),
        in_specs=[
            io_spec,
            pl.BlockSpec((1,) + mrow.shape[1:], w_map),
            pl.BlockSpec((1,) + soc.shape[1:], w_map),
            pl.BlockSpec((1,) + sic.shape[1:], w_map),
            pl.BlockSpec((1,) + apw.shape[1:], lambda a, b: (a, 0, 0, 0)),
        ],
        out_specs=io_spec,
        out_shape=jax.ShapeDtypeStruct(us3.shape, BF16),
        scratch_shapes=[
            pltpu.VMEM((n_in, n_in), BF16),
            pltpu.VMEM((n_in, n_state), BF16),
            pltpu.VMEM((n_state, n_in), BF16),
        ],
        compiler_params=_params(sem=("arbitrary", "arbitrary")),
        name="s5",
    )(us3, mrow, soc, sic, apw)


def _back_kernel(x_ref, ya_ref, ys_ref, gw_ref, gb_ref, onb_ref, wo_ref, fnw_ref,
                 rwh_ref, rwl_ref, rb_ref,
                 x1_ref, h2_ref, gate_ref, er_ref, cnt_ref, carry_ref, ys_scr):
    i = pl.program_id(0)
    tm = x_ref.shape[0]
    d_g = ya_ref.shape[1]

    @pl.when(i == 0)
    def _():
        carry_ref[...] = jnp.zeros_like(carry_ref)

    n_lb = ys_scr.shape[0]
    for t in range(SSM_T):
        for q in range(n_lb):
            ys_scr[q, pl.ds(t, tm // SSM_T, stride=SSM_T), :] = (
                ys_ref[t, :, q * LANES:(q + 1) * LANES].astype(F32))
    y = _gelu(jnp.concatenate([ys_scr[q] for q in range(n_lb)], axis=1))
    z = jnp.dot(y.astype(BF16), gw_ref[...], preferred_element_type=F32) + gb_ref[...]
    yb = y * _sigmoid(z)
    ybn = _rms(yb, onb_ref[...]).astype(BF16)
    x1 = (x_ref[...]
          + jnp.dot(ya_ref[...], wo_ref[:d_g, :], preferred_element_type=F32)
          + jnp.dot(ybn, wo_ref[d_g:, :], preferred_element_type=F32))
    x1_ref[...] = x1
    h2 = _rms(x1, fnw_ref[...])
    h2_ref[...] = h2

    hh = h2.astype(BF16)
    hl = (h2 - hh.astype(F32)).astype(BF16)
    logits = (jnp.dot(hh, rwh_ref[...], preferred_element_type=F32)
              + jnp.dot(hl, rwh_ref[...], preferred_element_type=F32)
              + jnp.dot(hh, rwl_ref[...], preferred_element_type=F32)
              + rb_ref[...])
    lane = lax.broadcasted_iota(I32, (tm, LANES), 1)
    neg = jnp.float32(-jnp.inf)
    work = jnp.where(lane < N_EXPERTS, logits, neg)
    vals, hots, idxs = [], [], []
    for k in range(TOP_K):
        m = jnp.max(work, axis=-1, keepdims=True)
        idx = jnp.min(jnp.where(work == m, lane, LANES), axis=-1, keepdims=True)
        hot = lane == idx
        vals.append(m)
        hots.append(hot)
        idxs.append(idx)
        work = jnp.where(hot, neg, work)
    exps = [jnp.exp(v - vals[0]) for v in vals]
    denom = exps[0] + exps[1] + exps[2] + exps[3]
    gate = jnp.zeros((tm, LANES), F32)
    for k in range(TOP_K):
        gate = jnp.where(lane == k, exps[k] / denom, gate)
    gate_ref[...] = gate

    sel = (hots[0] | hots[1] | hots[2] | hots[3])
    sel_f = jnp.where(sel, 1.0, 0.0)
    r_i = lax.broadcasted_iota(I32, (tm, tm), 0)
    c_i = lax.broadcasted_iota(I32, (tm, tm), 1)
    lower = jnp.where(r_i > c_i, 1.0, 0.0).astype(BF16)
    cum = jnp.dot(lower, sel_f.astype(BF16), preferred_element_type=F32) + carry_ref[...]
    er = jnp.zeros((tm, LANES), I32)
    for k in range(TOP_K):
        rk = jnp.sum(jnp.where(hots[k], cum, 0.0), axis=-1, keepdims=True)
        er = jnp.where(lane == k, idxs[k], er)
        er = jnp.where(lane == TOP_K + k, rk.astype(I32), er)
    carry_ref[...] = carry_ref[...] + jnp.sum(sel_f, axis=0, keepdims=True)
    cnt_ref[...] = carry_ref[...]
    er_ref[0] = jnp.transpose(er)[:SUBLANES, :]


def _back(x2, ya, ys3, glu_w, glu_b, on_b, w_out, fn_w, router_w, router_b):
    n, d = x2.shape
    d_g = ya.shape[1]
    d_s = ys3.shape[2]
    tm = ROW_TILE
    nt = n // tm
    const2 = lambda i: (0, 0)
    rw = jnp.zeros((d, LANES), F32).at[:, :N_EXPERTS].set(router_w.astype(F32))
    rw_hi = rw.astype(BF16)
    rw_lo = (rw - rw_hi.astype(F32)).astype(BF16)
    rb = jnp.zeros((1, LANES), F32).at[0, :N_EXPERTS].set(router_b.astype(F32))
    tok_tile = lambda i: (i, 0)
    return pl.pallas_call(
        _back_kernel,
        grid=(nt,),
        in_specs=[
            pl.BlockSpec((tm, d), tok_tile),
            pl.BlockSpec((tm, d_g), tok_tile),
            pl.BlockSpec((SSM_T, tm // SSM_T, d_s), lambda i: (0, i, 0)),
            pl.BlockSpec((d_s, d_s), const2),
            pl.BlockSpec((1, d_s), const2),
            pl.BlockSpec((1, d_s), const2),
            pl.BlockSpec((d_g + d_s, d), const2),
            pl.BlockSpec((1, d), const2),
            pl.BlockSpec((d, LANES), const2),
            pl.BlockSpec((d, LANES), const2),
            pl.BlockSpec((1, LANES), const2),
        ],
        out_specs=[
            pl.BlockSpec((tm, d), tok_tile),
            pl.BlockSpec((tm, d), tok_tile),
            pl.BlockSpec((tm, LANES), tok_tile),
            pl.BlockSpec((1, SUBLANES, tm), lambda i: (i, 0, 0)),
            pl.BlockSpec((1, LANES), const2),
        ],
        out_shape=[
            jax.ShapeDtypeStruct((n, d), F32),
            jax.ShapeDtypeStruct((n, d), F32),
            jax.ShapeDtypeStruct((n, LANES), F32),
            jax.ShapeDtypeStruct((nt, SUBLANES, tm), I32),
            jax.ShapeDtypeStruct((1, LANES), F32),
        ],
        scratch_shapes=[pltpu.VMEM((1, LANES), F32), pltpu.VMEM((d_s // LANES, tm, LANES), F32)],
        compiler_params=_params(sem=("arbitrary",)),
        name="back",
    )(x2, ya, ys3, glu_w.astype(BF16), glu_b.reshape(1, d_s), on_b.reshape(1, d_s),
      w_out.astype(BF16), fn_w.reshape(1, d), rw_hi, rw_lo, rb)


def _routing_tables(er, counts, n):
    te = EXPERT_TILE
    n_blocks = n * TOP_K // te + N_EXPERTS
    nb = (counts + te - 1) // te
    cum = jnp.cumsum(nb)
    pstart = (cum - nb) * te
    e_sel = er[:, :TOP_K, :, None] == jnp.arange(N_EXPERTS, dtype=I32)
    dest = jnp.sum(jnp.where(e_sel, pstart, 0), axis=-1) + er[:, TOP_K:2 * TOP_K, :]
    dest = jnp.concatenate([dest, jnp.zeros_like(dest)], axis=1)
    total = cum[-1]
    j = jnp.arange(n_blocks, dtype=I32)
    e_j = jnp.sum((cum[None, :] <= jnp.minimum(j, total - 1)[:, None]).astype(I32), axis=1)
    valid = (j < total).astype(I32)
    last_block = jnp.where(nb > 0, (cum - 1) * te, -1)
    last_block = jnp.concatenate([last_block, total[None]])
    return dest.astype(I32), e_j.astype(I32), valid, last_block.astype(I32)


def _dispatch_kernel(last_ref, dest_hbm, h_ref, xs_hbm, idx_smem, zbuf, sem_idx, sem_rows, sem_z):
    i = pl.program_id(0)
    nt = pl.num_programs(0)
    tm = h_ref.shape[0]
    te = zbuf.shape[0]
    slot = i % 2

    def idx_copy(t, s):
        return pltpu.make_async_copy(dest_hbm.at[t], idx_smem.at[s], sem_idx.at[s])

    @pl.when(i == 0)
    def _():
        zbuf[...] = jnp.zeros_like(zbuf)

        def zero_copy(e):
            row = pl.multiple_of(jnp.maximum(last_ref[e], 0), te)
            return pltpu.make_async_copy(zbuf, xs_hbm.at[pl.ds(row, te)], sem_z)

        for e in range(N_EXPERTS):
            @pl.when(last_ref[e] >= 0)
            def _():
                zero_copy(e).start()
        for e in range(N_EXPERTS):
            @pl.when(last_ref[e] >= 0)
            def _():
                zero_copy(e).wait()
        n_blocks = xs_hbm.shape[0] // te
        total = last_ref[N_EXPERTS]

        def tail_copy(j):
            return pltpu.make_async_copy(zbuf, xs_hbm.at[pl.ds(j * te, te)], sem_z)

        for j in range(n_blocks - N_EXPERTS, n_blocks):
            @pl.when(j >= total)
            def _():
                tail_copy(j).start()
        for j in range(n_blocks - N_EXPERTS, n_blocks):
            @pl.when(j >= total)
            def _():
                tail_copy(j).wait()
        idx_copy(0, 0).start()

    @pl.when(i + 1 < nt)
    def _():
        idx_copy(i + 1, 1 - slot).start()

    idx_copy(i, slot).wait()

    def body(n, c):
        for k in range(TOP_K):
            pltpu.make_async_copy(h_ref.at[pl.ds(n, 1)],
                                  xs_hbm.at[pl.ds(idx_smem[slot, k, n], 1)],
                                  sem_rows.at[k]).start(priority=k % 2)
        return c
    lax.fori_loop(0, tm, body, 0)
    for k in range(TOP_K):
        pltpu.make_async_copy(h_ref, xs_hbm.at[pl.ds(0, tm)], sem_rows.at[k]).wait()


def _dispatch(last_block, dest, h2, n_rows):
    n, d = h2.shape
    tm = ROW_TILE
    grid_spec = pltpu.PrefetchScalarGridSpec(
        num_scalar_prefetch=1,
        grid=(n // tm,),
        in_specs=[
            pl.BlockSpec(memory_space=pl.ANY),
            pl.BlockSpec((tm, d), lambda i, last: (i, 0)),
        ],
        out_specs=pl.BlockSpec(memory_space=pl.ANY),
        scratch_shapes=[
            pltpu.SMEM((2, SUBLANES, tm), I32),
            pltpu.VMEM((EXPERT_TILE, d), F32),
            pltpu.SemaphoreType.DMA((2,)),
            pltpu.SemaphoreType.DMA((TOP_K,)),
            pltpu.SemaphoreType.DMA,
        ],
    )
    return pl.pallas_call(
        _dispatch_kernel,
        grid_spec=grid_spec,
        out_shape=jax.ShapeDtypeStruct((n_rows, d), F32),
        compiler_params=_params(sem=("arbitrary",)),
        name="dispatch",
    )(last_block, dest, h2)


def _expert_kernel(be_ref, bv_ref, xs_ref, wgu_ref, bgu_ref, wdn_ref, bdn_ref, ys_ref):
    del be_ref
    i = pl.program_id(0)
    d_e = wdn_ref.shape[1]

    @pl.when(bv_ref[i] > 0)
    def _():
        x = xs_ref[...].astype(BF16)
        gu = jnp.dot(x, wgu_ref[0], preferred_element_type=F32) + bgu_ref[0]
        glu = jnp.minimum(gu[:, :d_e], SWIGLU_LIMIT)
        lin = jnp.clip(gu[:, d_e:], -SWIGLU_LIMIT, SWIGLU_LIMIT)
        act = glu * _sigmoid(SWIGLU_ALPHA * glu) * (lin + 1.0)
        ys_ref[...] = jnp.dot(act.astype(BF16), wdn_ref[0], preferred_element_type=F32) + bdn_ref[0]

    @pl.when(bv_ref[i] == 0)
    def _():
        ys_ref[...] = jnp.zeros_like(ys_ref)


def _experts(block_expert, block_valid, xs, w_gu, b_gu, w_dn, b_dn, layer):
    d = xs.shape[1]
    n_blocks = block_expert.shape[0]
    te = EXPERT_TILE
    d_gu = w_gu.shape[3]
    d_e = w_dn.shape[2]
    e_map3 = lambda i, be, bv: (layer * N_EXPERTS + be[i], 0, 0)
    w_gu = w_gu.reshape((-1,) + w_gu.shape[2:])
    w_dn = w_dn.reshape((-1,) + w_dn.shape[2:])
    b_gu = b_gu.reshape(-1, 1, d_gu)
    b_dn = b_dn.reshape(-1, 1, d)
    x_map = lambda i, be, bv: (jnp.where(bv[i] > 0, i, 0), 0)
    grid_spec = pltpu.PrefetchScalarGridSpec(
        num_scalar_prefetch=2,
        grid=(n_blocks,),
        in_specs=[
            pl.BlockSpec((te, d), x_map),
            pl.BlockSpec((1, d, d_gu), e_map3),
            pl.BlockSpec((1, 1, d_gu), e_map3),
            pl.BlockSpec((1, d_e, d), e_map3),
            pl.BlockSpec((1, 1, d), e_map3),
        ],
        out_specs=pl.BlockSpec((te, d), lambda i, be, bv: (i, 0)),
    )
    return pl.pallas_call(
        _expert_kernel,
        grid_spec=grid_spec,
        out_shape=jax.ShapeDtypeStruct((n_blocks * te, d), F32),
        compiler_params=_params(sem=("arbitrary",)),
        name="experts",
    )(block_expert, block_valid, xs, w_gu, b_gu, w_dn, b_dn)


def _combine_kernel(dest_hbm, x1_ref, g_ref, fw_ref, ys_hbm, o_ref,
                    idx_smem, ybuf, sem_idx, sem_rows, *, final_norm):
    i = pl.program_id(0)
    nt = pl.num_programs(0)
    tm = x1_ref.shape[0]
    slot = i % 2

    def idx_copy(t, s):
        return pltpu.make_async_copy(dest_hbm.at[t], idx_smem.at[s], sem_idx.at[s])

    def issue_rows(s):
        def body(n, c):
            for k in range(TOP_K):
                pltpu.make_async_copy(ys_hbm.at[pl.ds(idx_smem[s, k, n], 1)],
                                      ybuf.at[s, k, pl.ds(n, 1)],
                                      sem_rows.at[s, k]).start(priority=k % 2)
            return c
        lax.fori_loop(0, tm, body, 0)

    def wait_rows(s):
        for k in range(TOP_K):
            pltpu.make_async_copy(ys_hbm.at[pl.ds(0, tm)], ybuf.at[s, k], sem_rows.at[s, k]).wait()

    @pl.when(i == 0)
    def _():
        cp = idx_copy(0, 0)
        cp.start()
        cp.wait()
        issue_rows(0)

    @pl.when(i + 1 < nt)
    def _():
        cp = idx_copy(i + 1, 1 - slot)
        cp.start()
        cp.wait()
        issue_rows(1 - slot)

    wait_rows(slot)
    g = g_ref[...]
    out = x1_ref[...]
    for k in range(TOP_K):
        out = out + g[:, k:k + 1] * ybuf[slot, k]
    if final_norm:
        out = _rms(out, fw_ref[...])
    o_ref[...] = out


def _combine(x1, ys, gate, dest, final_w, final_norm):
    n, d = x1.shape
    tm = ROW_TILE
    nt = n // tm
    tok_tile = lambda i: (i, 0)
    return pl.pallas_call(
        functools.partial(_combine_kernel, final_norm=final_norm),
        grid=(nt,),
        in_specs=[
            pl.BlockSpec(memory_space=pl.ANY),
            pl.BlockSpec((tm, d), tok_tile),
            pl.BlockSpec((tm, LANES), tok_tile),
            pl.BlockSpec((1, d), lambda i: (0, 0)),
            pl.BlockSpec(memory_space=pl.ANY),
        ],
        out_specs=pl.BlockSpec((tm, d), tok_tile),
        out_shape=jax.ShapeDtypeStruct((n, d), F32),
        scratch_shapes=[
            pltpu.SMEM((2, SUBLANES, tm), I32),
            pltpu.VMEM((2, TOP_K, tm, d), F32),
            pltpu.SemaphoreType.DMA((2,)),
            pltpu.SemaphoreType.DMA((2, TOP_K)),
        ],
        compiler_params=_params(sem=("arbitrary",)),
        name="combine",
    )(dest, x1, gate, final_w.reshape(1, d), ys)


def kernel(x, attn_norm_w, w_in, sgu_ln_w, sgu_ln_b, sgu_w, sgu_b, ssm_a_re, ssm_a_im, ssm_b_re, ssm_b_im, ssm_c_re, ssm_c_im, ssm_d, ssm_log_dt, ssm_glu_w, ssm_glu_b, out_norm_a, out_norm_b, w_out, ffn_norm_w, router_w, router_b, w_gate_up, b_gate_up, w_down, b_down, final_norm_w):
    b, l, d = x.shape
    n = b * l
    depth = w_in.shape[0]
    blocks_per_seq = l // SSM_T
    n_steps = max(1, (blocks_per_seq - 1).bit_length())
    n_rows = (n * TOP_K // EXPERT_TILE + N_EXPERTS) * EXPERT_TILE
    x2 = x.reshape(n, d).astype(F32)
    w_gu_bf = w_gate_up.astype(BF16)
    w_dn_bf = w_down.astype(BF16)
    for layer in range(depth):
        ya, us3 = _front(x2, attn_norm_w[layer], w_in[layer], sgu_ln_w[layer], sgu_ln_b[layer],
                         sgu_w[layer], sgu_b[layer], out_norm_a[layer])
        mrow, soc, sic, apw = _s5_tables(ssm_a_re[layer], ssm_a_im[layer], ssm_b_re[layer], ssm_b_im[layer],
                                         ssm_c_re[layer], ssm_c_im[layer], ssm_d[layer], ssm_log_dt[layer],
                                         n_steps)
        ys3 = _s5(us3, mrow, soc, sic, apw, blocks_per_seq)
        x1, h2, gate, er, cnt = _back(x2, ya, ys3, ssm_glu_w[layer], ssm_glu_b[layer],
                                      out_norm_b[layer], w_out[layer], ffn_norm_w[layer],
                                      router_w[layer], router_b[layer])
        counts = cnt[0, :N_EXPERTS].astype(I32)
        dest, block_expert, block_valid, last_block = _routing_tables(er, counts, n)
        xs = _dispatch(last_block, dest, h2, n_rows)
        ys = _experts(block_expert, block_valid, xs, w_gu_bf, b_gate_up, w_dn_bf, b_down, layer)
        x2 = _combine(x1, ys, gate, dest, final_norm_w, final_norm=(layer == depth - 1))
    return x2.reshape(b, l, d).astype(x.dtype)
```

```python
import functools
import math

import jax
import jax.numpy as jnp
from jax import lax
from jax.experimental import pallas as pl
from jax.experimental.pallas import tpu as pltpu

F32 = jnp.float32
BF16 = jnp.bfloat16
I32 = jnp.int32

EPS = 1e-5
N_HEADS = 4
CHUNK = 128
SSM_GROUP = 16
SSM_STATE = 64
SSM_T = 16
N_EXPERTS = 32
TOP_K = 4
SWIGLU_LIMIT = 7.0
SWIGLU_ALPHA = 1.702
LANES = 128
SUBLANES = 8
GROUPS_PER_LANE_BLOCK = LANES // SSM_GROUP

ROW_TILE = 512
EXPERT_TILE = 256
VMEM_LIMIT = 56 * 1024 * 1024


def _gelu(x):
    return 0.5 * x * (1.0 + jnp.tanh(math.sqrt(2.0 / math.pi) * (x + 0.044715 * (x * x * x))))


def _sigmoid(x):
    return 1.0 / (1.0 + jnp.exp(-x))


def _rms(x, w):
    return x * lax.rsqrt(jnp.mean(x * x, axis=-1, keepdims=True) + EPS) * w


def _store_rows(ref, val):
    rows, d = val.shape
    nsl = d // LANES
    for s in range(nsl):
        ref[pl.ds(s, rows, stride=nsl), :] = val[:, s * LANES:(s + 1) * LANES]


def _load_row_slab(ref, s, rows, nsl):
    return ref[pl.ds(s, rows, stride=nsl), :]


def _params(**kw):
    return pltpu.CompilerParams(dimension_semantics=kw.pop("sem"), vmem_limit_bytes=VMEM_LIMIT, **kw)


def _front_kernel(x_ref, nw_ref, win_ref, lnw_ref, lnb_ref, ws_ref, bst_ref, ona_ref,
                  ya_ref, us_ref, mixed_ref, us_scr):
    d_g = ya_ref.shape[1]
    hd = d_g // N_HEADS
    tm = x_ref.shape[0]
    x = x_ref[...]
    h = _rms(x, nw_ref[...]).astype(BF16)
    proj = jnp.dot(h, win_ref[...], preferred_element_type=F32)
    u = _gelu(proj[:, :d_g])
    v = _gelu(proj[:, d_g:2 * d_g])
    n_lb = us_scr.shape[0]
    for q in range(n_lb):
        us_scr[q] = proj[:, 2 * d_g + q * LANES:2 * d_g + (q + 1) * LANES]
    mu = jnp.mean(v, axis=-1, keepdims=True)
    vc = v - mu
    var = jnp.mean(vc * vc, axis=-1, keepdims=True)
    vb = (vc * lax.rsqrt(var + EPS) * lnw_ref[...] + lnb_ref[...]).astype(BF16)
    row = lax.broadcasted_iota(I32, (CHUNK, CHUNK), 0)
    col = lax.broadcasted_iota(I32, (CHUNK, CHUNK), 1)
    causal = row >= col
    for hh in range(N_HEADS):
        w = jnp.where(causal, ws_ref[hh], 0.0).astype(BF16)
        bias = bst_ref[:, hh:hh + 1]
        for c in range(tm // CHUNK):
            vv = vb[c * CHUNK:(c + 1) * CHUNK, hh * hd:(hh + 1) * hd]
            m = jnp.dot(w, vv, preferred_element_type=F32) + bias
            mixed_ref[c * CHUNK:(c + 1) * CHUNK, hh * hd:(hh + 1) * hd] = m
    ya = u * mixed_ref[...]
    ya_ref[...] = _rms(ya, ona_ref[...]).astype(BF16)
    for s in range(SSM_T):
        for q in range(n_lb):
            us_ref[s, :, q * LANES:(q + 1) * LANES] = (
                us_scr[q, pl.ds(s, tm // SSM_T, stride=SSM_T), :].astype(BF16))


def _front(x2, nw, w_in, ln_w, ln_b, w_s, b_s, on_a):
    n, d = x2.shape
    d_g = ln_w.shape[0]
    d_s = w_in.shape[1] - 2 * d_g
    tm = ROW_TILE
    const2 = lambda i: (0, 0)
    return pl.pallas_call(
        _front_kernel,
        grid=(n // tm,),
        in_specs=[
            pl.BlockSpec((tm, d), lambda i: (i, 0)),
            pl.BlockSpec((1, d), const2),
            pl.BlockSpec(w_in.shape, const2),
            pl.BlockSpec((1, d_g), const2),
            pl.BlockSpec((1, d_g), const2),
            pl.BlockSpec(w_s.shape, lambda i: (0, 0, 0)),
            pl.BlockSpec((CHUNK, N_HEADS), const2),
            pl.BlockSpec((1, d_g), const2),
        ],
        out_specs=[
            pl.BlockSpec((tm, d_g), lambda i: (i, 0)),
            pl.BlockSpec((SSM_T, tm // SSM_T, d_s), lambda i: (0, i, 0)),
        ],
        out_shape=[
            jax.ShapeDtypeStruct((n, d_g), BF16),
            jax.ShapeDtypeStruct((SSM_T, n // SSM_T, d_s), BF16),
        ],
        scratch_shapes=[pltpu.VMEM((tm, d_g), F32), pltpu.VMEM((d_s // LANES, tm, LANES), F32)],
        compiler_params=_params(sem=("arbitrary",)),
        name="front",
    )(x2, nw.reshape(1, d), w_in.astype(BF16), ln_w.reshape(1, d_g), ln_b.reshape(1, d_g),
      w_s, b_s.T, on_a.reshape(1, d_g))


def _s5_tables(a_re, a_im, b_re, b_im, c_re, c_im, d, log_dt, n_steps):
    g, p = a_re.shape
    hch = b_re.shape[-1]
    t = SSM_T
    gl = GROUPS_PER_LANE_BLOCK
    lb = g // gl
    a = lax.complex(a_re.astype(F32), a_im.astype(F32))
    dt = jnp.exp(log_dt.astype(F32))[:, None]
    dta = dt * a
    a_bar = jnp.exp(dta)
    b_bar = ((a_bar - 1.0) / a)[..., None] * lax.complex(b_re.astype(F32), b_im.astype(F32))
    c = lax.complex(c_re.astype(F32), c_im.astype(F32))
    lags = jnp.arange(t + 1, dtype=F32)
    pw = jnp.exp(lags[None, :, None] * dta[:, None, :])
    kern = jnp.einsum('gpi,glp,gop->glio', b_bar, pw[:, :t], c).real
    dmat = d.astype(F32).reshape(g, 1, 1, hch) * jnp.eye(hch, dtype=F32)[None, None]
    kern = kern + jnp.where((jnp.arange(t) == 0)[None, :, None, None], dmat, 0.0)
    eye_g = jnp.eye(gl, dtype=F32)
    mrow = jnp.einsum('aglio,gh->agilho', kern.reshape(lb, gl, t, hch, hch), eye_g)
    mrow = mrow.reshape(lb, gl * hch, t * gl * hch)
    so = pw[:, t - 1 - jnp.arange(t)][:, :, :, None] * b_bar[:, None, :, :]
    so = jnp.stack([so.real, so.imag], axis=0).reshape(2, lb, gl, t, p, hch)
    soc = so.transpose(1, 3, 2, 5, 0, 4).reshape(lb, t * gl * hch, 2 * p)
    si = c[:, None, :, :] * pw[:, 1:t + 1][:, :, None, :]
    si = jnp.stack([si.real, -si.imag], axis=0).reshape(2, lb, gl, t, hch, p)
    sic = si.transpose(1, 0, 5, 3, 2, 4).reshape(lb, 2 * p, t * gl * hch)
    steps = SSM_T * (2.0 ** jnp.arange(n_steps, dtype=F32))
    ap = jnp.exp(steps[None, :, None] * dta[:, None, :])
    ap = ap.reshape(lb, gl, n_steps, p).transpose(0, 2, 1, 3).reshape(lb, n_steps, gl * p)
    mul_same = jnp.concatenate([ap.real, ap.real], axis=-1)
    mul_swap = jnp.concatenate([-ap.imag, ap.imag], axis=-1)
    apw = jnp.stack([mul_same, mul_swap], axis=2)
    return mrow.astype(BF16), soc.astype(BF16), sic.astype(BF16), apw


def _s5_kernel(u_ref, mrow_ref, soc_ref, sic_ref, ap_ref, y_ref, wk_scr, wso_scr, wsi_scr):
    t_blk = u_ref.shape[0]
    rows = u_ref.shape[1]
    n_in = wk_scr.shape[0]
    n_state = wso_scr.shape[1]
    two_p = soc_ref.shape[2]
    p = two_p // 2
    half = n_state // 2

    @pl.when(pl.program_id(1) == 0)
    def _():
        for s in range(t_blk):
            if s > 0:
                wk_scr[s * LANES:(s + 1) * LANES, :s * LANES] = jnp.zeros((LANES, s * LANES), BF16)
            wk_scr[s * LANES:(s + 1) * LANES, s * LANES:] = mrow_ref[0, :, :n_in - s * LANES]
        r_e = lax.broadcasted_iota(I32, (two_p, n_state), 0)
        c_e = lax.broadcasted_iota(I32, (two_p, n_state), 1)
        spread = jnp.where((r_e // p == c_e // half) & (r_e % p == c_e % p), 1.0, 0.0).astype(BF16)
        full = jnp.dot(soc_ref[0], spread, preferred_element_type=F32)
        r_g = (lax.broadcasted_iota(I32, (n_in, n_state), 0) // SSM_GROUP) % GROUPS_PER_LANE_BLOCK
        c_g = (lax.broadcasted_iota(I32, (n_in, n_state), 1) % half) // p
        wso_scr[...] = jnp.where(r_g == c_g, full, 0.0).astype(BF16)
        r_e = lax.broadcasted_iota(I32, (n_state, two_p), 0)
        c_e = lax.broadcasted_iota(I32, (n_state, two_p), 1)
        gather = jnp.where((c_e // p == r_e // half) & (c_e % p == r_e % p), 1.0, 0.0).astype(BF16)
        full = jnp.dot(gather, sic_ref[0], preferred_element_type=F32)
        r_g = (lax.broadcasted_iota(I32, (n_state, n_in), 0) % half) // p
        c_g = (lax.broadcasted_iota(I32, (n_state, n_in), 1) // SSM_GROUP) % GROUPS_PER_LANE_BLOCK
        wsi_scr[...] = jnp.where(r_g == c_g, full, 0.0).astype(BF16)

    xcat = jnp.concatenate([u_ref[s] for s in range(t_blk)], axis=1)
    x = jnp.dot(xcat, wso_scr[...], preferred_element_type=F32)
    pos = lax.broadcasted_iota(I32, (rows, n_state), 0)
    for k in range(ap_ref.shape[1]):
        sh = 1 << k
        prev = pltpu.roll(x, sh, 0)
        prev_sw = pltpu.roll(prev, half, 1)
        upd = prev * ap_ref[0, k, 0:1, :] + prev_sw * ap_ref[0, k, 1:2, :]
        x = x + jnp.where(pos >= sh, upd, 0.0)
    xin = jnp.where(pos >= 1, pltpu.roll(x, 1, 0), 0.0).astype(BF16)
    for j in range(t_blk // 2):
        k_hi = (2 * j + 2) * LANES
        lo = 2 * j * LANES
        y = jnp.dot(xcat[:, :k_hi], wk_scr[:k_hi, lo:lo + 2 * LANES], preferred_element_type=F32)
        y = y + jnp.dot(xin, wsi_scr[:, lo:lo + 2 * LANES], preferred_element_type=F32)
        y_ref[2 * j] = y[:, :LANES].astype(BF16)
        y_ref[2 * j + 1] = y[:, LANES:].astype(BF16)


def _s5(us3, mrow, soc, sic, apw, blocks_per_seq):
    t_blk, n_blocks, d_s = us3.shape
    lb = mrow.shape[0]
    n_seq = n_blocks // blocks_per_seq
    n_in = t_blk * LANES
    n_state = apw.shape[3]
    w_map = lambda a, b: (a, 0, 0)
    io_spec = pl.BlockSpec((t_blk, blocks_per_seq, LANES), lambda a, b: (0, b, a))
    return pl.pallas_call(
        _s5_kernel,
        grid=(lb, n_seq),
        in_specs=[
            io_spec,
            pl.BlockSpec((1,) + mrow.shape[1:], w_map),
            pl.BlockSpec((1,) + soc.shape[1:], w_map),
            pl.BlockSpec((1,) + sic.shape[1:], w_map),
            pl.BlockSpec((1,) + apw.shape[1:], lambda a, b: (a, 0, 0, 0)),
        ],
        out_specs=io_spec,
        out_shape=jax.ShapeDtypeStruct(us3.shape, BF16),
        scratch_shapes=[
            pltpu.VMEM((n_in, n_in), BF16),
            pltpu.VMEM((n_in, n_state), BF16),
            pltpu.VMEM((n_state, n_in), BF16),
        ],
        compiler_params=_params(sem=("arbitrary", "arbitrary")),
        name="s5",
    )(us3, mrow, soc, sic, apw)


def _back_kernel(x_ref, ya_ref, ys_ref, gw_ref, gb_ref, onb_ref, wo_ref, fnw_ref,
                 rwh_ref, rwl_ref, rb_ref,
                 x1_ref, h2_ref, gate_ref, er_ref, cnt_ref, carry_ref, ys_scr):
    i = pl.program_id(0)
    tm = x_ref.shape[0]
    d_g = ya_ref.shape[1]

    @pl.when(i == 0)
    def _():
        carry_ref[...] = jnp.zeros_like(carry_ref)

    n_lb = ys_scr.shape[0]
    for t in range(SSM_T):
        for q in range(n_lb):
            ys_scr[q, pl.ds(t, tm // SSM_T, stride=SSM_T), :] = (
                ys_ref[t, :, q * LANES:(q + 1) * LANES].astype(F32))
    y = _gelu(jnp.concatenate([ys_scr[q] for q in range(n_lb)], axis=1))
    z = jnp.dot(y.astype(BF16), gw_ref[...], preferred_element_type=F32) + gb_ref[...]
    yb = y * _sigmoid(z)
    ybn = _rms(yb, onb_ref[...]).astype(BF16)
    x1 = (x_ref[...]
          + jnp.dot(ya_ref[...], wo_ref[:d_g, :], preferred_element_type=F32)
          + jnp.dot(ybn, wo_ref[d_g:, :], preferred_element_type=F32))
    x1_ref[...] = x1
    h2 = _rms(x1, fnw_ref[...])
    _store_rows(h2_ref, h2)

    hh = h2.astype(BF16)
    hl = (h2 - hh.astype(F32)).astype(BF16)
    logits = (jnp.dot(hh, rwh_ref[...], preferred_element_type=F32)
              + jnp.dot(hl, rwh_ref[...], preferred_element_type=F32)
              + jnp.dot(hh, rwl_ref[...], preferred_element_type=F32)
              + rb_ref[...])
    lane = lax.broadcasted_iota(I32, (tm, LANES), 1)
    neg = jnp.float32(-jnp.inf)
    work = jnp.where(lane < N_EXPERTS, logits, neg)
    vals, hots, idxs = [], [], []
    for k in range(TOP_K):
        m = jnp.max(work, axis=-1, keepdims=True)
        idx = jnp.min(jnp.where(work == m, lane, LANES), axis=-1, keepdims=True)
        hot = lane == idx
        vals.append(m)
        hots.append(hot)
        idxs.append(idx)
        work = jnp.where(hot, neg, work)
    exps = [jnp.exp(v - vals[0]) for v in vals]
    denom = exps[0] + exps[1] + exps[2] + exps[3]
    gate = jnp.zeros((tm, LANES), F32)
    for k in range(TOP_K):
        gate = jnp.where(lane == k, exps[k] / denom, gate)
    gate_ref[...] = gate

    sel = (hots[0] | hots[1] | hots[2] | hots[3])
    sel_f = jnp.where(sel, 1.0, 0.0)
    r_i = lax.broadcasted_iota(I32, (tm, tm), 0)
    c_i = lax.broadcasted_iota(I32, (tm, tm), 1)
    lower = jnp.where(r_i > c_i, 1.0, 0.0).astype(BF16)
    cum = jnp.dot(lower, sel_f.astype(BF16), preferred_element_type=F32) + carry_ref[...]
    er = jnp.zeros((tm, LANES), I32)
    for k in range(TOP_K):
        rk = jnp.sum(jnp.where(hots[k], cum, 0.0), axis=-1, keepdims=True)
        er = jnp.where(lane == k, idxs[k], er)
        er = jnp.where(lane == TOP_K + k, rk.astype(I32), er)
    carry_ref[...] = carry_ref[...] + jnp.sum(sel_f, axis=0, keepdims=True)
    cnt_ref[...] = carry_ref[...]
    er_ref[0] = jnp.transpose(er)[:SUBLANES, :]


def _back(x2, ya, ys3, glu_w, glu_b, on_b, w_out, fn_w, router_w, router_b):
    n, d = x2.shape
    d_g = ya.shape[1]
    d_s = ys3.shape[2]
    tm = ROW_TILE
    nt = n // tm
    const2 = lambda i: (0, 0)
    rw = jnp.zeros((d, LANES), F32).at[:, :N_EXPERTS].set(router_w.astype(F32))
    rw_hi = rw.astype(BF16)
    rw_lo = (rw - rw_hi.astype(F32)).astype(BF16)
    rb = jnp.zeros((1, LANES), F32).at[0, :N_EXPERTS].set(router_b.astype(F32))
    tok_tile = lambda i: (i, 0)
    return pl.pallas_call(
        _back_kernel,
        grid=(nt,),
        in_specs=[
            pl.BlockSpec((tm, d), tok_tile),
            pl.BlockSpec((tm, d_g), tok_tile),
            pl.BlockSpec((SSM_T, tm // SSM_T, d_s), lambda i: (0, i, 0)),
            pl.BlockSpec((d_s, d_s), const2),
            pl.BlockSpec((1, d_s), const2),
            pl.BlockSpec((1, d_s), const2),
            pl.BlockSpec((d_g + d_s, d), const2),
            pl.BlockSpec((1, d), const2),
            pl.BlockSpec((d, LANES), const2),
            pl.BlockSpec((d, LANES), const2),
            pl.BlockSpec((1, LANES), const2),
        ],
        out_specs=[
            pl.BlockSpec((tm, d), tok_tile),
            pl.BlockSpec((tm * d // LANES, LANES), tok_tile),
            pl.BlockSpec((tm, LANES), tok_tile),
            pl.BlockSpec((1, SUBLANES, tm), lambda i: (i, 0, 0)),
            pl.BlockSpec((1, LANES), const2),
        ],
        out_shape=[
            jax.ShapeDtypeStruct((n, d), F32),
            jax.ShapeDtypeStruct((n * d // LANES, LANES), F32),
            jax.ShapeDtypeStruct((n, LANES), F32),
            jax.ShapeDtypeStruct((nt, SUBLANES, tm), I32),
            jax.ShapeDtypeStruct((1, LANES), F32),
        ],
        scratch_shapes=[pltpu.VMEM((1, LANES), F32), pltpu.VMEM((d_s // LANES, tm, LANES), F32)],
        compiler_params=_params(sem=("arbitrary",)),
        name="back",
    )(x2, ya, ys3, glu_w.astype(BF16), glu_b.reshape(1, d_s), on_b.reshape(1, d_s),
      w_out.astype(BF16), fn_w.reshape(1, d), rw_hi, rw_lo, rb)


def _routing_tables(er, counts, n, row_sl):
    te = EXPERT_TILE
    n_blocks = n * TOP_K // te + N_EXPERTS
    nb = (counts + te - 1) // te
    cum = jnp.cumsum(nb)
    pstart = (cum - nb) * te
    e_sel = er[:, :TOP_K, :, None] == jnp.arange(N_EXPERTS, dtype=I32)
    dest = jnp.sum(jnp.where(e_sel, pstart, 0), axis=-1) + er[:, TOP_K:2 * TOP_K, :]
    dest = jnp.concatenate([dest, jnp.zeros_like(dest)], axis=1) * row_sl
    total = cum[-1]
    j = jnp.arange(n_blocks, dtype=I32)
    e_j = jnp.sum((cum[None, :] <= jnp.minimum(j, total - 1)[:, None]).astype(I32), axis=1)
    valid = (j < total).astype(I32)
    last_block = jnp.where(nb > 0, (cum - 1) * te * row_sl, -1)
    last_block = jnp.concatenate([last_block, total[None]])
    return dest.astype(I32), e_j.astype(I32), valid, last_block.astype(I32)


def _dispatch_kernel(last_ref, dest_hbm, h_ref, xs_hbm, idx_smem, zbuf, sem_idx, sem_rows, sem_z,
                     *, row_sl):
    i = pl.program_id(0)
    nt = pl.num_programs(0)
    tm = h_ref.shape[0] // row_sl
    blk = zbuf.shape[0]
    slot = i % 2

    def idx_copy(t, s):
        return pltpu.make_async_copy(dest_hbm.at[t], idx_smem.at[s], sem_idx.at[s])

    @pl.when(i == 0)
    def _():
        zbuf[...] = jnp.zeros_like(zbuf)

        def zero_copy(e):
            row = pl.multiple_of(jnp.maximum(last_ref[e], 0), blk)
            return pltpu.make_async_copy(zbuf, xs_hbm.at[pl.ds(row, blk)], sem_z)

        for e in range(N_EXPERTS):
            @pl.when(last_ref[e] >= 0)
            def _():
                zero_copy(e).start()
        for e in range(N_EXPERTS):
            @pl.when(last_ref[e] >= 0)
            def _():
                zero_copy(e).wait()
        n_blocks = xs_hbm.shape[0] // blk
        total = last_ref[N_EXPERTS]

        def tail_copy(j):
            return pltpu.make_async_copy(zbuf, xs_hbm.at[pl.ds(j * blk, blk)], sem_z)

        for j in range(n_blocks - N_EXPERTS, n_blocks):
            @pl.when(j >= total)
            def _():
                tail_copy(j).start()
        for j in range(n_blocks - N_EXPERTS, n_blocks):
            @pl.when(j >= total)
            def _():
                tail_copy(j).wait()
        idx_copy(0, 0).start()

    @pl.when(i + 1 < nt)
    def _():
        idx_copy(i + 1, 1 - slot).start()

    idx_copy(i, slot).wait()

    def body(n, c):
        src = h_ref.at[pl.ds(pl.multiple_of(n * row_sl, row_sl), row_sl)]
        for k in range(TOP_K):
            dst = xs_hbm.at[pl.ds(pl.multiple_of(idx_smem[slot, k, n], row_sl), row_sl)]
            pltpu.make_async_copy(src, dst, sem_rows.at[k]).start(priority=k % 2)
        return c
    lax.fori_loop(0, tm, body, 0, unroll=8)
    for k in range(TOP_K):
        pltpu.make_async_copy(h_ref, xs_hbm.at[pl.ds(0, tm * row_sl)], sem_rows.at[k]).wait()


def _dispatch(last_block, dest, h2t, n_rows, row_sl):
    tm = ROW_TILE
    nt = h2t.shape[0] // (tm * row_sl)
    grid_spec = pltpu.PrefetchScalarGridSpec(
        num_scalar_prefetch=1,
        grid=(nt,),
        in_specs=[
            pl.BlockSpec(memory_space=pl.ANY),
            pl.BlockSpec((tm * row_sl, LANES), lambda i, last: (i, 0)),
        ],
        out_specs=pl.BlockSpec(memory_space=pl.ANY),
        scratch_shapes=[
            pltpu.SMEM((2, SUBLANES, tm), I32),
            pltpu.VMEM((EXPERT_TILE * row_sl, LANES), F32),
            pltpu.SemaphoreType.DMA((2,)),
            pltpu.SemaphoreType.DMA((TOP_K,)),
            pltpu.SemaphoreType.DMA,
        ],
    )
    return pl.pallas_call(
        functools.partial(_dispatch_kernel, row_sl=row_sl),
        grid_spec=grid_spec,
        out_shape=jax.ShapeDtypeStruct((n_rows * row_sl, LANES), F32),
        compiler_params=_params(sem=("arbitrary",)),
        name="dispatch",
    )(last_block, dest, h2t)


def _expert_kernel(be_ref, bv_ref, xs_ref, wgu_ref, bgu_ref, wdn_ref, bdn_ref, ys_ref):
    del be_ref
    i = pl.program_id(0)
    d_e = wdn_ref.shape[1]

    @pl.when(bv_ref[i] > 0)
    def _():
        nsl = wgu_ref.shape[1] // LANES
        te = xs_ref.shape[0] // nsl
        x = jnp.concatenate([_load_row_slab(xs_ref, s, te, nsl).astype(BF16) for s in range(nsl)], axis=1)
        gu = jnp.dot(x, wgu_ref[0], preferred_element_type=F32) + bgu_ref[0]
        glu = jnp.minimum(gu[:, :d_e], SWIGLU_LIMIT)
        lin = jnp.clip(gu[:, d_e:], -SWIGLU_LIMIT, SWIGLU_LIMIT)
        act = glu * _sigmoid(SWIGLU_ALPHA * glu) * (lin + 1.0)
        _store_rows(ys_ref, jnp.dot(act.astype(BF16), wdn_ref[0], preferred_element_type=F32) + bdn_ref[0])

    @pl.when(bv_ref[i] == 0)
    def _():
        ys_ref[...] = jnp.zeros_like(ys_ref)


def _experts(block_expert, block_valid, xs, w_gu, b_gu, w_dn, b_dn, layer):
    d = w_gu.shape[2]
    n_blocks = block_expert.shape[0]
    te = EXPERT_TILE * d // LANES
    d_gu = w_gu.shape[3]
    d_e = w_dn.shape[2]
    e_map3 = lambda i, be, bv: (layer * N_EXPERTS + be[i], 0, 0)
    w_gu = w_gu.reshape((-1,) + w_gu.shape[2:])
    w_dn = w_dn.reshape((-1,) + w_dn.shape[2:])
    b_gu = b_gu.reshape(-1, 1, d_gu)
    b_dn = b_dn.reshape(-1, 1, d)
    x_map = lambda i, be, bv: (jnp.where(bv[i] > 0, i, 0), 0)
    grid_spec = pltpu.PrefetchScalarGridSpec(
        num_scalar_prefetch=2,
        grid=(n_blocks,),
        in_specs=[
            pl.BlockSpec((te, LANES), x_map),
            pl.BlockSpec((1, d, d_gu), e_map3),
            pl.BlockSpec((1, 1, d_gu), e_map3),
            pl.BlockSpec((1, d_e, d), e_map3),
            pl.BlockSpec((1, 1, d), e_map3),
        ],
        out_specs=pl.BlockSpec((te, LANES), lambda i, be, bv: (i, 0)),
    )
    return pl.pallas_call(
        _expert_kernel,
        grid_spec=grid_spec,
        out_shape=jax.ShapeDtypeStruct((n_blocks * te, LANES), F32),
        compiler_params=_params(sem=("arbitrary",)),
        name="experts",
    )(block_expert, block_valid, xs, w_gu, b_gu, w_dn, b_dn)


def _combine_kernel(dest_hbm, x1_ref, g_ref, fw_ref, ys_hbm, o_ref,
                    idx_smem, ybuf, sem_idx, sem_rows, *, final_norm):
    i = pl.program_id(0)
    nt = pl.num_programs(0)
    tm, d = x1_ref.shape
    nsl = d // LANES
    slot = i % 2

    def idx_copy(t, s):
        return pltpu.make_async_copy(dest_hbm.at[t], idx_smem.at[s], sem_idx.at[s])

    def issue_rows(s):
        def body(n, c):
            for k in range(TOP_K):
                src = ys_hbm.at[pl.ds(pl.multiple_of(idx_smem[s, k, n], nsl), nsl)]
                dst = ybuf.at[s, k, pl.ds(pl.multiple_of(n * nsl, nsl), nsl)]
                pltpu.make_async_copy(src, dst, sem_rows.at[s, k]).start(priority=k % 2)
            return c
        lax.fori_loop(0, tm, body, 0, unroll=8)

    def wait_rows(s):
        for k in range(TOP_K):
            pltpu.make_async_copy(ys_hbm.at[pl.ds(0, tm * nsl)], ybuf.at[s, k], sem_rows.at[s, k]).wait()

    @pl.when(i == 0)
    def _():
        cp = idx_copy(0, 0)
        cp.start()
        cp.wait()
        issue_rows(0)

    @pl.when(i + 1 < nt)
    def _():
        cp = idx_copy(i + 1, 1 - slot)
        cp.start()
        cp.wait()
        issue_rows(1 - slot)

    wait_rows(slot)
    g = g_ref[...]
    slabs = []
    for s in range(nsl):
        acc = x1_ref[:, s * LANES:(s + 1) * LANES]
        for k in range(TOP_K):
            acc = acc + g[:, k:k + 1] * _load_row_slab(ybuf.at[slot, k], s, tm, nsl)
        slabs.append(acc)
    out = jnp.concatenate(slabs, axis=1)
    if final_norm:
        out = _rms(out, fw_ref[...])
    o_ref[...] = out


def _combine(x1, ys, gate, dest, final_w, final_norm):
    n, d = x1.shape
    tm = ROW_TILE
    nt = n // tm
    tok_tile = lambda i: (i, 0)
    return pl.pallas_call(
        functools.partial(_combine_kernel, final_norm=final_norm),
        grid=(nt,),
        in_specs=[
            pl.BlockSpec(memory_space=pl.ANY),
            pl.BlockSpec((tm, d), tok_tile),
            pl.BlockSpec((tm, LANES), tok_tile),
            pl.BlockSpec((1, d), lambda i: (0, 0)),
            pl.BlockSpec(memory_space=pl.ANY),
        ],
        out_specs=pl.BlockSpec((tm, d), tok_tile),
        out_shape=jax.ShapeDtypeStruct((n, d), F32),
        scratch_shapes=[
            pltpu.SMEM((2, SUBLANES, tm), I32),
            pltpu.VMEM((2, TOP_K, tm * d // LANES, LANES), F32),
            pltpu.SemaphoreType.DMA((2,)),
            pltpu.SemaphoreType.DMA((2, TOP_K)),
        ],
        compiler_params=_params(sem=("arbitrary",)),
        name="combine",
    )(dest, x1, gate, final_w.reshape(1, d), ys)


def kernel(x, attn_norm_w, w_in, sgu_ln_w, sgu_ln_b, sgu_w, sgu_b, ssm_a_re, ssm_a_im, ssm_b_re, ssm_b_im, ssm_c_re, ssm_c_im, ssm_d, ssm_log_dt, ssm_glu_w, ssm_glu_b, out_norm_a, out_norm_b, w_out, ffn_norm_w, router_w, router_b, w_gate_up, b_gate_up, w_down, b_down, final_norm_w):
    b, l, d = x.shape
    n = b * l
    depth = w_in.shape[0]
    blocks_per_seq = l // SSM_T
    n_steps = max(1, (blocks_per_seq - 1).bit_length())
    n_rows = (n * TOP_K // EXPERT_TILE + N_EXPERTS) * EXPERT_TILE
    x2 = x.reshape(n, d).astype(F32)
    w_gu_bf = w_gate_up.astype(BF16)
    w_dn_bf = w_down.astype(BF16)
    for layer in range(depth):
        ya, us3 = _front(x2, attn_norm_w[layer], w_in[layer], sgu_ln_w[layer], sgu_ln_b[layer],
                         sgu_w[layer], sgu_b[layer], out_norm_a[layer])
        mrow, soc, sic, apw = _s5_tables(ssm_a_re[layer], ssm_a_im[layer], ssm_b_re[layer], ssm_b_im[layer],
                                         ssm_c_re[layer], ssm_c_im[layer], ssm_d[layer], ssm_log_dt[layer],
                                         n_steps)
        ys3 = _s5(us3, mrow, soc, sic, apw, blocks_per_seq)
        x1, h2, gate, er, cnt = _back(x2, ya, ys3, ssm_glu_w[layer], ssm_glu_b[layer],
                                      out_norm_b[layer], w_out[layer], ffn_norm_w[layer],
                                      router_w[layer], router_b[layer])
        counts = cnt[0, :N_EXPERTS].astype(I32)
        dest, block_expert, block_valid, last_block = _routing_tables(er, counts, n, d // LANES)
        xs = _dispatch(last_block, dest, h2, n_rows, d // LANES)
        ys = _experts(block_expert, block_valid, xs, w_gu_bf, b_gate_up, w_dn_bf, b_down, layer)
        x2 = _combine(x1, ys, gate, dest, final_norm_w, final_norm=(layer == depth - 1))
    return x2.reshape(b, l, d).astype(x.dtype)
```

```python
import functools
import math

import jax
import jax.numpy as jnp
from jax import lax
from jax.experimental import pallas as pl
from jax.experimental.pallas import tpu as pltpu

F32 = jnp.float32
BF16 = jnp.bfloat16
I32 = jnp.int32

EPS = 1e-5
N_HEADS = 4
CHUNK = 128
SSM_GROUP = 16
SSM_STATE = 64
SSM_T = 16
N_EXPERTS = 32
TOP_K = 4
SWIGLU_LIMIT = 7.0
SWIGLU_ALPHA = 1.702
LANES = 128
SUBLANES = 8
GROUPS_PER_LANE_BLOCK = LANES // SSM_GROUP

ROW_TILE = 512
EXPERT_TILE = 512
EXPERT_SUB = 256
EXPERT_CHUNK = 256
DMA_UNROLL = 8
COMBINE_ROWS = 64
VMEM_LIMIT = 56 * 1024 * 1024


def _gelu(x):
    return 0.5 * x * (1.0 + jnp.tanh(math.sqrt(2.0 / math.pi) * (x + 0.044715 * (x * x * x))))


def _sigmoid(x):
    return 1.0 / (1.0 + jnp.exp(-x))


def _rms(x, w):
    return x * lax.rsqrt(jnp.mean(x * x, axis=-1, keepdims=True) + EPS) * w


def _store_rows(ref, val, row0=0):
    rows, d = val.shape
    nsl = d // LANES
    for s in range(nsl):
        ref[pl.ds(row0 * nsl + s, rows, stride=nsl), :] = val[:, s * LANES:(s + 1) * LANES]


def _load_row_slab(ref, s, rows, nsl, row0=0):
    return ref[pl.ds(row0 * nsl + s, rows, stride=nsl), :]


def _params(**kw):
    return pltpu.CompilerParams(dimension_semantics=kw.pop("sem"), vmem_limit_bytes=VMEM_LIMIT, **kw)


def _front_kernel(x_ref, nw_ref, win_ref, lnw_ref, lnb_ref, ws_ref, bst_ref, ona_ref,
                  ya_ref, us_ref, mixed_ref, us_scr):
    d_g = ya_ref.shape[1]
    hd = d_g // N_HEADS
    tm = x_ref.shape[0]
    x = x_ref[...]
    h = _rms(x, nw_ref[...]).astype(BF16)
    proj = jnp.dot(h, win_ref[...], preferred_element_type=F32)
    u = _gelu(proj[:, :d_g])
    v = _gelu(proj[:, d_g:2 * d_g])
    n_lb = us_scr.shape[0]
    for q in range(n_lb):
        us_scr[q] = proj[:, 2 * d_g + q * LANES:2 * d_g + (q + 1) * LANES]
    mu = jnp.mean(v, axis=-1, keepdims=True)
    vc = v - mu
    var = jnp.mean(vc * vc, axis=-1, keepdims=True)
    vb = (vc * lax.rsqrt(var + EPS) * lnw_ref[...] + lnb_ref[...]).astype(BF16)
    row = lax.broadcasted_iota(I32, (CHUNK, CHUNK), 0)
    col = lax.broadcasted_iota(I32, (CHUNK, CHUNK), 1)
    causal = row >= col
    for hh in range(N_HEADS):
        w = jnp.where(causal, ws_ref[hh], 0.0).astype(BF16)
        bias = bst_ref[:, hh:hh + 1]
        for c in range(tm // CHUNK):
            vv = vb[c * CHUNK:(c + 1) * CHUNK, hh * hd:(hh + 1) * hd]
            m = jnp.dot(w, vv, preferred_element_type=F32) + bias
            mixed_ref[c * CHUNK:(c + 1) * CHUNK, hh * hd:(hh + 1) * hd] = m
    ya = u * mixed_ref[...]
    ya_ref[...] = _rms(ya, ona_ref[...]).astype(BF16)
    for s in range(SSM_T):
        for q in range(n_lb):
            us_ref[s, :, q * LANES:(q + 1) * LANES] = (
                us_scr[q, pl.ds(s, tm // SSM_T, stride=SSM_T), :].astype(BF16))


def _front(x2, nw, w_in, ln_w, ln_b, w_s, b_s, on_a):
    n, d = x2.shape
    d_g = ln_w.shape[0]
    d_s = w_in.shape[1] - 2 * d_g
    tm = ROW_TILE
    const2 = lambda i: (0, 0)
    return pl.pallas_call(
        _front_kernel,
        grid=(n // tm,),
        in_specs=[
            pl.BlockSpec((tm, d), lambda i: (i, 0)),
            pl.BlockSpec((1, d), const2),
            pl.BlockSpec(w_in.shape, const2),
            pl.BlockSpec((1, d_g), const2),
            pl.BlockSpec((1, d_g), const2),
            pl.BlockSpec(w_s.shape, lambda i: (0, 0, 0)),
            pl.BlockSpec((CHUNK, N_HEADS), const2),
            pl.BlockSpec((1, d_g), const2),
        ],
        out_specs=[
            pl.BlockSpec((tm, d_g), lambda i: (i, 0)),
            pl.BlockSpec((SSM_T, tm // SSM_T, d_s), lambda i: (0, i, 0)),
        ],
        out_shape=[
            jax.ShapeDtypeStruct((n, d_g), BF16),
            jax.ShapeDtypeStruct((SSM_T, n // SSM_T, d_s), BF16),
        ],
        scratch_shapes=[pltpu.VMEM((tm, d_g), F32), pltpu.VMEM((d_s // LANES, tm, LANES), F32)],
        compiler_params=_params(sem=("arbitrary",)),
        name="front",
    )(x2, nw.reshape(1, d), w_in.astype(BF16), ln_w.reshape(1, d_g), ln_b.reshape(1, d_g),
      w_s, b_s.T, on_a.reshape(1, d_g))


def _s5_tables(a_re, a_im, b_re, b_im, c_re, c_im, d, log_dt, n_steps):
    depth, g, p = a_re.shape
    hch = b_re.shape[-1]
    t = SSM_T
    gl = GROUPS_PER_LANE_BLOCK
    r = depth * (g // gl)
    a = lax.complex(a_re.astype(F32), a_im.astype(F32)).reshape(r, gl, p)
    dt = jnp.exp(log_dt.astype(F32)).reshape(r, gl, 1)
    dta = dt * a
    a_bar = jnp.exp(dta)
    b = lax.complex(b_re.astype(F32), b_im.astype(F32)).reshape(r, gl, p, hch)
    b_bar = ((a_bar - 1.0) / a)[..., None] * b
    c = lax.complex(c_re.astype(F32), c_im.astype(F32)).reshape(r, gl, hch, p)
    lags = jnp.arange(t + 1, dtype=F32)
    pw = jnp.exp(lags[None, :, None, None] * dta.transpose(0, 2, 1)[:, None])
    pwx = jnp.repeat(pw, hch, axis=-1)
    bb = b_bar.transpose(0, 2, 1, 3).reshape(r, p, gl * hch)
    cc = c.transpose(0, 3, 1, 2).reshape(r, p, gl * hch)
    q = pwx[:, :t] * bb[:, None]
    si = pwx[:, 1:] * cc[:, None]
    k2 = (jnp.einsum('rlpx,rpy->rlxy', q.real, cc.real, precision=lax.Precision.HIGHEST)
          - jnp.einsum('rlpx,rpy->rlxy', q.imag, cc.imag, precision=lax.Precision.HIGHEST))
    lane_g = jnp.arange(gl * hch) // hch
    k2 = jnp.where(lane_g[:, None] == lane_g[None, :], k2, 0.0)
    skip = jnp.eye(gl * hch, dtype=F32) * d.astype(F32).reshape(r, 1, gl * hch)
    k2 = k2.at[:, 0].add(skip)
    so3 = jnp.swapaxes(jnp.concatenate([q.real, q.imag], axis=2), 2, 3)
    si2 = jnp.concatenate([si.real, -si.imag], axis=2)
    steps = SSM_T * (2.0 ** jnp.arange(n_steps, dtype=F32))
    ap = jnp.exp(steps[None, :, None, None] * dta[:, None]).reshape(r, n_steps, gl * p)
    mul_same = jnp.concatenate([ap.real, ap.real], axis=-1)
    mul_swap = jnp.concatenate([-ap.imag, ap.imag], axis=-1)
    apw = jnp.stack([mul_same, mul_swap], axis=2)
    return k2.astype(BF16), so3.astype(BF16), si2.astype(BF16), apw


def _s5_kernel(u_ref, k2_ref, so3_ref, si2_ref, ap_ref, y_ref, wk_scr, wso_scr, wsi_scr):
    t_blk = u_ref.shape[0]
    rows = u_ref.shape[1]
    n_state = wso_scr.shape[1]
    two_p = so3_ref.shape[3]
    p = two_p // 2
    half = n_state // 2

    @pl.when(pl.program_id(1) == 0)
    def _():
        for s in range(t_blk):
            if s > 0:
                wk_scr[s * LANES:(s + 1) * LANES, :s * LANES] = jnp.zeros((LANES, s * LANES), BF16)
            for t in range(s, t_blk):
                wk_scr[s * LANES:(s + 1) * LANES, t * LANES:(t + 1) * LANES] = k2_ref[0, t - s]
        r_e = lax.broadcasted_iota(I32, (two_p, n_state), 0)
        c_e = lax.broadcasted_iota(I32, (two_p, n_state), 1)
        spread = jnp.where((r_e // p == c_e // half) & (r_e % p == c_e % p), 1.0, 0.0).astype(BF16)
        r_g = lax.broadcasted_iota(I32, (LANES, n_state), 0) // SSM_GROUP
        c_g = (lax.broadcasted_iota(I32, (LANES, n_state), 1) % half) // p
        own = r_g == c_g
        for s in range(t_blk):
            full = jnp.dot(so3_ref[0, t_blk - 1 - s], spread, preferred_element_type=F32)
            wso_scr[s * LANES:(s + 1) * LANES, :] = jnp.where(own, full, 0.0).astype(BF16)
        r_e = lax.broadcasted_iota(I32, (n_state, two_p), 0)
        c_e = lax.broadcasted_iota(I32, (n_state, two_p), 1)
        gather = jnp.where((c_e // p == r_e // half) & (c_e % p == r_e % p), 1.0, 0.0).astype(BF16)
        r_g = (lax.broadcasted_iota(I32, (n_state, LANES), 0) % half) // p
        c_g = lax.broadcasted_iota(I32, (n_state, LANES), 1) // SSM_GROUP
        own = r_g == c_g
        for t in range(t_blk):
            full = jnp.dot(gather, si2_ref[0, t], preferred_element_type=F32)
            wsi_scr[:, t * LANES:(t + 1) * LANES] = jnp.where(own, full, 0.0).astype(BF16)

    xcat = jnp.concatenate([u_ref[s] for s in range(t_blk)], axis=1)
    x = jnp.dot(xcat, wso_scr[...], preferred_element_type=F32)
    pos = lax.broadcasted_iota(I32, (rows, n_state), 0)
    for k in range(ap_ref.shape[1]):
        sh = 1 << k
        prev = pltpu.roll(x, sh, 0)
        prev_sw = pltpu.roll(prev, half, 1)
        upd = prev * ap_ref[0, k, 0:1, :] + prev_sw * ap_ref[0, k, 1:2, :]
        x = x + jnp.where(pos >= sh, upd, 0.0)
    xin = jnp.where(pos >= 1, pltpu.roll(x, 1, 0), 0.0).astype(BF16)
    for j in range(t_blk // 2):
        k_hi = (2 * j + 2) * LANES
        lo = 2 * j * LANES
        y = jnp.dot(xcat[:, :k_hi], wk_scr[:k_hi, lo:lo + 2 * LANES], preferred_element_type=F32)
        y = y + jnp.dot(xin, wsi_scr[:, lo:lo + 2 * LANES], preferred_element_type=F32)
        y_ref[2 * j] = y[:, :LANES].astype(BF16)
        y_ref[2 * j + 1] = y[:, LANES:].astype(BF16)


def _s5(us3, k2, so3, si2, apw, blocks_per_seq, layer):
    t_blk, n_blocks, d_s = us3.shape
    lb = d_s // LANES
    n_seq = n_blocks // blocks_per_seq
    n_in = t_blk * LANES
    n_state = apw.shape[3]
    w_map = lambda a, b: (layer * lb + a, 0, 0, 0)
    io_spec = pl.BlockSpec((t_blk, blocks_per_seq, LANES), lambda a, b: (0, b, a))
    return pl.pallas_call(
        _s5_kernel,
        grid=(lb, n_seq),
        in_specs=[
            io_spec,
            pl.BlockSpec((1,) + k2.shape[1:], w_map),
            pl.BlockSpec((1,) + so3.shape[1:], w_map),
            pl.BlockSpec((1,) + si2.shape[1:], w_map),
            pl.BlockSpec((1,) + apw.shape[1:], w_map),
        ],
        out_specs=io_spec,
        out_shape=jax.ShapeDtypeStruct(us3.shape, BF16),
        scratch_shapes=[
            pltpu.VMEM((n_in, n_in), BF16),
            pltpu.VMEM((n_in, n_state), BF16),
            pltpu.VMEM((n_state, n_in), BF16),
        ],
        compiler_params=_params(sem=("arbitrary", "arbitrary")),
        name="s5",
    )(us3, k2, so3, si2, apw)


def _back_kernel(x_ref, ya_ref, ys_ref, gw_ref, gb_ref, onb_ref, wo_ref, fnw_ref,
                 rwh_ref, rwl_ref, rb_ref,
                 x1_ref, h2_ref, gate_ref, er_ref, cnt_ref, carry_ref, ys_scr):
    i = pl.program_id(0)
    tm = x_ref.shape[0]
    d_g = ya_ref.shape[1]

    @pl.when(i == 0)
    def _():
        carry_ref[...] = jnp.zeros_like(carry_ref)

    n_lb = ys_scr.shape[0]
    for t in range(SSM_T):
        for q in range(n_lb):
            ys_scr[q, pl.ds(t, tm // SSM_T, stride=SSM_T), :] = (
                ys_ref[t, :, q * LANES:(q + 1) * LANES].astype(F32))
    y = _gelu(jnp.concatenate([ys_scr[q] for q in range(n_lb)], axis=1))
    z = jnp.dot(y.astype(BF16), gw_ref[...], preferred_element_type=F32) + gb_ref[...]
    yb = y * _sigmoid(z)
    ybn = _rms(yb, onb_ref[...]).astype(BF16)
    x1 = (x_ref[...]
          + jnp.dot(ya_ref[...], wo_ref[:d_g, :], preferred_element_type=F32)
          + jnp.dot(ybn, wo_ref[d_g:, :], preferred_element_type=F32))
    x1_ref[...] = x1
    h2 = _rms(x1, fnw_ref[...])
    _store_rows(h2_ref, h2)

    hh = h2.astype(BF16)
    hl = (h2 - hh.astype(F32)).astype(BF16)
    logits = (jnp.dot(hh, rwh_ref[...], preferred_element_type=F32)
              + jnp.dot(hl, rwh_ref[...], preferred_element_type=F32)
              + jnp.dot(hh, rwl_ref[...], preferred_element_type=F32)
              + rb_ref[...])
    lane = lax.broadcasted_iota(I32, (tm, LANES), 1)
    neg = jnp.float32(-jnp.inf)
    work = jnp.where(lane < N_EXPERTS, logits, neg)
    vals, hots, idxs = [], [], []
    for k in range(TOP_K):
        m = jnp.max(work, axis=-1, keepdims=True)
        idx = jnp.min(jnp.where(work == m, lane, LANES), axis=-1, keepdims=True)
        hot = lane == idx
        vals.append(m)
        hots.append(hot)
        idxs.append(idx)
        work = jnp.where(hot, neg, work)
    exps = [jnp.exp(v - vals[0]) for v in vals]
    denom = exps[0] + exps[1] + exps[2] + exps[3]
    gate = jnp.zeros((tm, LANES), F32)
    for k in range(TOP_K):
        gate = jnp.where(lane == k, exps[k] / denom, gate)
    gate_ref[...] = gate

    sel = (hots[0] | hots[1] | hots[2] | hots[3])
    sel_f = jnp.where(sel, 1.0, 0.0)
    r_i = lax.broadcasted_iota(I32, (tm, tm), 0)
    c_i = lax.broadcasted_iota(I32, (tm, tm), 1)
    lower = jnp.where(r_i > c_i, 1.0, 0.0).astype(BF16)
    cum = jnp.dot(lower, sel_f.astype(BF16), preferred_element_type=F32) + carry_ref[...]
    er = jnp.zeros((tm, LANES), I32)
    for k in range(TOP_K):
        rk = jnp.sum(jnp.where(hots[k], cum, 0.0), axis=-1, keepdims=True)
        er = jnp.where(lane == k, idxs[k], er)
        er = jnp.where(lane == TOP_K + k, rk.astype(I32), er)
    carry_ref[...] = carry_ref[...] + jnp.sum(sel_f, axis=0, keepdims=True)
    cnt_ref[...] = carry_ref[...]
    er_ref[0] = jnp.transpose(er)[:SUBLANES, :]


def _back(x2, ya, ys3, glu_w, glu_b, on_b, w_out, fn_w, router_w, router_b):
    n, d = x2.shape
    d_g = ya.shape[1]
    d_s = ys3.shape[2]
    tm = ROW_TILE
    nt = n // tm
    const2 = lambda i: (0, 0)
    rw = jnp.zeros((d, LANES), F32).at[:, :N_EXPERTS].set(router_w.astype(F32))
    rw_hi = rw.astype(BF16)
    rw_lo = (rw - rw_hi.astype(F32)).astype(BF16)
    rb = jnp.zeros((1, LANES), F32).at[0, :N_EXPERTS].set(router_b.astype(F32))
    tok_tile = lambda i: (i, 0)
    return pl.pallas_call(
        _back_kernel,
        grid=(nt,),
        in_specs=[
            pl.BlockSpec((tm, d), tok_tile),
            pl.BlockSpec((tm, d_g), tok_tile),
            pl.BlockSpec((SSM_T, tm // SSM_T, d_s), lambda i: (0, i, 0)),
            pl.BlockSpec((d_s, d_s), const2),
            pl.BlockSpec((1, d_s), const2),
            pl.BlockSpec((1, d_s), const2),
            pl.BlockSpec((d_g + d_s, d), const2),
            pl.BlockSpec((1, d), const2),
            pl.BlockSpec((d, LANES), const2),
            pl.BlockSpec((d, LANES), const2),
            pl.BlockSpec((1, LANES), const2),
        ],
        out_specs=[
            pl.BlockSpec((tm, d), tok_tile),
            pl.BlockSpec((tm * d // LANES, LANES), tok_tile),
            pl.BlockSpec((tm, LANES), tok_tile),
            pl.BlockSpec((1, SUBLANES, tm), lambda i: (i, 0, 0)),
            pl.BlockSpec((1, LANES), const2),
        ],
        out_shape=[
            jax.ShapeDtypeStruct((n, d), F32),
            jax.ShapeDtypeStruct((n * d // LANES, LANES), F32),
            jax.ShapeDtypeStruct((n, LANES), F32),
            jax.ShapeDtypeStruct((nt, SUBLANES, tm), I32),
            jax.ShapeDtypeStruct((1, LANES), F32),
        ],
        scratch_shapes=[pltpu.VMEM((1, LANES), F32), pltpu.VMEM((d_s // LANES, tm, LANES), F32)],
        compiler_params=_params(sem=("arbitrary",)),
        name="back",
    )(x2, ya, ys3, glu_w.astype(BF16), glu_b.reshape(1, d_s), on_b.reshape(1, d_s),
      w_out.astype(BF16), fn_w.reshape(1, d), rw_hi, rw_lo, rb)


def _routing_tables(er, counts, n, row_sl):
    te = EXPERT_TILE
    n_blocks = n * TOP_K // te + N_EXPERTS
    nb = (counts + te - 1) // te
    cum = jnp.cumsum(nb)
    pstart = (cum - nb) * te
    e_sel = er[:, :TOP_K, :, None] == jnp.arange(N_EXPERTS, dtype=I32)
    dest = jnp.sum(jnp.where(e_sel, pstart, 0), axis=-1) + er[:, TOP_K:2 * TOP_K, :]
    dest = (dest * row_sl).reshape(dest.shape[0], -1)
    total = cum[-1]
    j = jnp.arange(n_blocks, dtype=I32)
    e_j = jnp.sum((cum[None, :] <= jnp.minimum(j, total - 1)[:, None]).astype(I32), axis=1)
    valid = (j < total).astype(I32)
    last_block = jnp.where(nb > 0, (cum - 1) * te * row_sl, -1)
    last_block = jnp.concatenate([last_block, total[None]])
    first = valid * (j == (cum - nb)[e_j]).astype(I32)
    e_ids = jnp.arange(N_EXPERTS, dtype=I32)
    later = (e_ids[None, :] > e_j[:, None]) & (nb > 0)[None, :]
    nxt = jnp.min(jnp.where(later, e_ids[None, :], N_EXPERTS), axis=1)
    nxt = jnp.where(nxt < N_EXPERTS, nxt, -1)
    return dest.astype(I32), e_j.astype(I32), valid, first, nxt.astype(I32), last_block.astype(I32)


def _dispatch_kernel(last_ref, dest_hbm, h_ref, xs_hbm, idx_smem, zbuf, sem_idx, sem_rows, sem_z,
                     *, row_sl):
    i = pl.program_id(0)
    nt = pl.num_programs(0)
    tm = h_ref.shape[0] // row_sl
    blk = zbuf.shape[0]
    slot = i % 2

    def idx_copy(t, s):
        n_idx = TOP_K * tm
        dst = idx_smem.at[pl.ds(pl.multiple_of(s * n_idx, n_idx), n_idx)]
        return pltpu.make_async_copy(dest_hbm.at[t], dst, sem_idx.at[s])

    @pl.when(i == 0)
    def _():
        zbuf[...] = jnp.zeros_like(zbuf)

        def zero_copy(e):
            row = pl.multiple_of(jnp.maximum(last_ref[e], 0), blk)
            return pltpu.make_async_copy(zbuf, xs_hbm.at[pl.ds(row, blk)], sem_z)

        for e in range(N_EXPERTS):
            @pl.when(last_ref[e] >= 0)
            def _():
                zero_copy(e).start()
        for e in range(N_EXPERTS):
            @pl.when(last_ref[e] >= 0)
            def _():
                zero_copy(e).wait()
        n_blocks = xs_hbm.shape[0] // blk
        total = last_ref[N_EXPERTS]

        def tail_copy(j):
            return pltpu.make_async_copy(zbuf, xs_hbm.at[pl.ds(j * blk, blk)], sem_z)

        for j in range(n_blocks - N_EXPERTS, n_blocks):
            @pl.when(j >= total)
            def _():
                tail_copy(j).start()
        for j in range(n_blocks - N_EXPERTS, n_blocks):
            @pl.when(j >= total)
            def _():
                tail_copy(j).wait()
        idx_copy(0, 0).start()

    @pl.when(i + 1 < nt)
    def _():
        idx_copy(i + 1, 1 - slot).start()

    idx_copy(i, slot).wait()

    def body(j, carry):
        n0 = j * DMA_UNROLL
        i0 = slot * (TOP_K * tm) + n0
        for u in range(DMA_UNROLL):
            src = h_ref.at[pl.ds(pl.multiple_of((n0 + u) * row_sl, row_sl), row_sl)]
            for k in range(TOP_K):
                dst = xs_hbm.at[pl.ds(pl.multiple_of(idx_smem[i0 + (k * tm + u)], row_sl), row_sl)]
                pltpu.make_async_copy(src, dst, sem_rows.at[k]).start(priority=k % 2)
        return carry
    lax.fori_loop(0, tm // DMA_UNROLL, body, 0)
    for k in range(TOP_K):
        pltpu.make_async_copy(h_ref, xs_hbm.at[pl.ds(0, tm * row_sl)], sem_rows.at[k]).wait()


def _dispatch(last_block, dest, h2t, n_rows, row_sl):
    tm = ROW_TILE
    nt = h2t.shape[0] // (tm * row_sl)
    grid_spec = pltpu.PrefetchScalarGridSpec(
        num_scalar_prefetch=1,
        grid=(nt,),
        in_specs=[
            pl.BlockSpec(memory_space=pl.ANY),
            pl.BlockSpec((tm * row_sl, LANES), lambda i, last: (i, 0)),
        ],
        out_specs=pl.BlockSpec(memory_space=pl.ANY),
        scratch_shapes=[
            pltpu.SMEM((2 * TOP_K * tm,), I32),
            pltpu.VMEM((EXPERT_TILE * row_sl, LANES), F32),
            pltpu.SemaphoreType.DMA((2,)),
            pltpu.SemaphoreType.DMA((TOP_K,)),
            pltpu.SemaphoreType.DMA,
        ],
    )
    return pl.pallas_call(
        functools.partial(_dispatch_kernel, row_sl=row_sl),
        grid_spec=grid_spec,
        out_shape=jax.ShapeDtypeStruct((n_rows * row_sl, LANES), F32),
        compiler_params=_params(sem=("arbitrary",)),
        name="dispatch",
    )(last_block, dest, h2t)


def _expert_kernel(be_ref, bv_ref, first_ref, next_ref, xs_ref, wgu_hbm, bgu_ref, wdn_hbm, bdn_ref, ys_ref,
                   stage_gu, stage_dn, wgu_bf, wdn_bf, act_scr, sem_w, *, layer):
    i = pl.program_id(0)
    d, d_gu = wgu_bf.shape
    d_e = wdn_bf.shape[0]
    nsl = d // LANES
    te = xs_ref.shape[0] // nsl

    def weight_copies(e):
        row = layer * N_EXPERTS + e
        return (pltpu.make_async_copy(wgu_hbm.at[row], stage_gu, sem_w.at[0]),
                pltpu.make_async_copy(wdn_hbm.at[row], stage_dn, sem_w.at[1]))

    @pl.when(i == 0)
    def _():
        for cp in weight_copies(be_ref[0]):
            cp.start(priority=1)

    @pl.when(first_ref[i] > 0)
    def _():
        for cp in weight_copies(be_ref[i]):
            cp.wait()
        rows = 128
        for r in range(0, d, rows):
            wgu_bf[r:r + rows, :] = stage_gu[r:r + rows, :].astype(BF16)
        for r in range(0, d_e, rows):
            wdn_bf[r:r + rows, :] = stage_dn[r:r + rows, :].astype(BF16)

        @pl.when(next_ref[i] >= 0)
        def _():
            for cp in weight_copies(next_ref[i]):
                cp.start(priority=1)

    @pl.when(bv_ref[i] > 0)
    def _():
        for r0 in range(0, te, EXPERT_SUB):
            x = jnp.concatenate([_load_row_slab(xs_ref, s, EXPERT_SUB, nsl, r0).astype(BF16)
                                 for s in range(nsl)], axis=1)
            for c in range(d_e // EXPERT_CHUNK):
                lo = c * EXPERT_CHUNK
                hi = lo + EXPERT_CHUNK
                glu = jnp.dot(x, wgu_bf[:, lo:hi], preferred_element_type=F32) + bgu_ref[0, :, lo:hi]
                lin = (jnp.dot(x, wgu_bf[:, d_e + lo:d_e + hi], preferred_element_type=F32)
                       + bgu_ref[0, :, d_e + lo:d_e + hi])
                glu = jnp.minimum(glu, SWIGLU_LIMIT)
                lin = jnp.clip(lin, -SWIGLU_LIMIT, SWIGLU_LIMIT)
                act_scr[r0:r0 + EXPERT_SUB, lo:hi] = (
                    glu * _sigmoid(SWIGLU_ALPHA * glu) * (lin + 1.0)).astype(BF16)
            y = jnp.dot(act_scr[r0:r0 + EXPERT_SUB, :], wdn_bf[...], preferred_element_type=F32) + bdn_ref[0]
            _store_rows(ys_ref, y, r0)

    @pl.when(bv_ref[i] == 0)
    def _():
        ys_ref[...] = jnp.zeros_like(ys_ref)


def _experts(block_expert, block_valid, block_first, block_next, xs, w_gu, b_gu, w_dn, b_dn, layer):
    d = w_gu.shape[2]
    n_blocks = block_expert.shape[0]
    te = EXPERT_TILE * d // LANES
    d_gu = w_gu.shape[3]
    d_e = w_dn.shape[2]
    e_map3 = lambda i, be, bv, bf, bn: (layer * N_EXPERTS + be[i], 0, 0)
    w_gu = w_gu.reshape((-1,) + w_gu.shape[2:])
    w_dn = w_dn.reshape((-1,) + w_dn.shape[2:])
    b_gu = b_gu.reshape(-1, 1, d_gu)
    b_dn = b_dn.reshape(-1, 1, d)
    x_map = lambda i, be, bv, bf, bn: (jnp.where(bv[i] > 0, i, 0), 0)
    grid_spec = pltpu.PrefetchScalarGridSpec(
        num_scalar_prefetch=4,
        grid=(n_blocks,),
        in_specs=[
            pl.BlockSpec((te, LANES), x_map),
            pl.BlockSpec(memory_space=pl.ANY),
            pl.BlockSpec((1, 1, d_gu), e_map3),
            pl.BlockSpec(memory_space=pl.ANY),
            pl.BlockSpec((1, 1, d), e_map3),
        ],
        out_specs=pl.BlockSpec((te, LANES), lambda i, be, bv, bf, bn: (i, 0)),
        scratch_shapes=[
            pltpu.VMEM((d, d_gu), F32),
            pltpu.VMEM((d_e, d), F32),
            pltpu.VMEM((d, d_gu), BF16),
            pltpu.VMEM((d_e, d), BF16),
            pltpu.VMEM((EXPERT_TILE, d_e), BF16),
            pltpu.SemaphoreType.DMA((2,)),
        ],
    )
    return pl.pallas_call(
        functools.partial(_expert_kernel, layer=layer),
        grid_spec=grid_spec,
        out_shape=jax.ShapeDtypeStruct((n_blocks * te, LANES), F32),
        compiler_params=_params(sem=("arbitrary",)),
        name="experts",
    )(block_expert, block_valid, block_first, block_next, xs, w_gu, b_gu, w_dn, b_dn)


def _combine_kernel(dest_hbm, x1_ref, g_ref, fw_ref, ys_hbm, o_ref,
                    idx_smem, ybuf, sem_idx, sem_rows, *, final_norm):
    i = pl.program_id(0)
    nt = pl.num_programs(0)
    tm, d = x1_ref.shape
    nsl = d // LANES
    slot = i % 2

    def idx_copy(t, s):
        n_idx = TOP_K * tm
        dst = idx_smem.at[pl.ds(pl.multiple_of(s * n_idx, n_idx), n_idx)]
        return pltpu.make_async_copy(dest_hbm.at[t], dst, sem_idx.at[s])

    def issue_rows(s):
        def body(j, carry):
            n0 = j * DMA_UNROLL
            i0 = s * (TOP_K * tm) + n0
            for u in range(DMA_UNROLL):
                for k in range(TOP_K):
                    src = ys_hbm.at[pl.ds(pl.multiple_of(idx_smem[i0 + (k * tm + u)], nsl), nsl)]
                    dst = ybuf.at[s, k, pl.ds(pl.multiple_of((n0 + u) * nsl, nsl), nsl)]
                    pltpu.make_async_copy(src, dst, sem_rows.at[s, k]).start(priority=k % 2)
            return carry
        lax.fori_loop(0, tm // DMA_UNROLL, body, 0)

    def wait_rows(s):
        for k in range(TOP_K):
            pltpu.make_async_copy(ys_hbm.at[pl.ds(0, tm * nsl)], ybuf.at[s, k], sem_rows.at[s, k]).wait()

    @pl.when(i == 0)
    def _():
        cp = idx_copy(0, 0)
        cp.start()
        cp.wait()
        issue_rows(0)

        @pl.when(nt > 1)
        def _():
            idx_copy(1, 1).start()

    @pl.when(i + 1 < nt)
    def _():
        idx_copy(i + 1, 1 - slot).wait()
        issue_rows(1 - slot)

        @pl.when(i + 2 < nt)
        def _():
            idx_copy(i + 2, slot).start()

    wait_rows(slot)
    for r0 in range(0, tm, COMBINE_ROWS):
        g = g_ref[r0:r0 + COMBINE_ROWS, :]
        gk = [jnp.broadcast_to(g[:, k:k + 1], (COMBINE_ROWS, LANES)) for k in range(TOP_K)]
        slabs = []
        for s in range(nsl):
            acc = x1_ref[r0:r0 + COMBINE_ROWS, s * LANES:(s + 1) * LANES]
            for k in range(TOP_K):
                acc = acc + gk[k] * _load_row_slab(ybuf.at[slot, k], s, COMBINE_ROWS, nsl, r0)
            slabs.append(acc)
        out = jnp.concatenate(slabs, axis=1)
        if final_norm:
            out = _rms(out, fw_ref[...])
        o_ref[r0:r0 + COMBINE_ROWS, :] = out


def _combine(x1, ys, gate, dest, final_w, final_norm):
    n, d = x1.shape
    tm = ROW_TILE
    nt = n // tm
    tok_tile = lambda i: (i, 0)
    return pl.pallas_call(
        functools.partial(_combine_kernel, final_norm=final_norm),
        grid=(nt,),
        in_specs=[
            pl.BlockSpec(memory_space=pl.ANY),
            pl.BlockSpec((tm, d), tok_tile),
            pl.BlockSpec((tm, LANES), tok_tile),
            pl.BlockSpec((1, d), lambda i: (0, 0)),
            pl.BlockSpec(memory_space=pl.ANY),
        ],
        out_specs=pl.BlockSpec((tm, d), tok_tile),
        out_shape=jax.ShapeDtypeStruct((n, d), F32),
        scratch_shapes=[
            pltpu.SMEM((2 * TOP_K * tm,), I32),
            pltpu.VMEM((2, TOP_K, tm * d // LANES, LANES), F32),
            pltpu.SemaphoreType.DMA((2,)),
            pltpu.SemaphoreType.DMA((2, TOP_K)),
        ],
        compiler_params=_params(sem=("arbitrary",)),
        name="combine",
    )(dest, x1, gate, final_w.reshape(1, d), ys)


def kernel(x, attn_norm_w, w_in, sgu_ln_w, sgu_ln_b, sgu_w, sgu_b, ssm_a_re, ssm_a_im, ssm_b_re, ssm_b_im, ssm_c_re, ssm_c_im, ssm_d, ssm_log_dt, ssm_glu_w, ssm_glu_b, out_norm_a, out_norm_b, w_out, ffn_norm_w, router_w, router_b, w_gate_up, b_gate_up, w_down, b_down, final_norm_w):
    b, l, d = x.shape
    n = b * l
    depth = w_in.shape[0]
    blocks_per_seq = l // SSM_T
    n_steps = max(1, (blocks_per_seq - 1).bit_length())
    n_rows = (n * TOP_K // EXPERT_TILE + N_EXPERTS) * EXPERT_TILE
    x2 = x.reshape(n, d).astype(F32)
    k2, so3, si2, apw = _s5_tables(ssm_a_re, ssm_a_im, ssm_b_re, ssm_b_im, ssm_c_re, ssm_c_im, ssm_d,
                                   ssm_log_dt, n_steps)
    for layer in range(depth):
        ya, us3 = _front(x2, attn_norm_w[layer], w_in[layer], sgu_ln_w[layer], sgu_ln_b[layer],
                         sgu_w[layer], sgu_b[layer], out_norm_a[layer])
        ys3 = _s5(us3, k2, so3, si2, apw, blocks_per_seq, layer)
        x1, h2, gate, er, cnt = _back(x2, ya, ys3, ssm_glu_w[layer], ssm_glu_b[layer],
                                      out_norm_b[layer], w_out[layer], ffn_norm_w[layer],
                                      router_w[layer], router_b[layer])
        counts = cnt[0, :N_EXPERTS].astype(I32)
        dest, block_expert, block_valid, block_first, block_next, last_block = _routing_tables(
            er, counts, n, d // LANES)
        xs = _dispatch(last_block, dest, h2, n_rows, d // LANES)
        ys = _experts(block_expert, block_valid, block_first, block_next, xs,
                      w_gate_up, b_gate_up, w_down, b_down, layer)
        x2 = _combine(x1, ys, gate, dest, final_norm_w, final_norm=(layer == depth - 1))
    return x2.reshape(b, l, d).astype(x.dtype)
```

```python
import functools
import math

import jax
import jax.numpy as jnp
from jax import lax
from jax.experimental import pallas as pl
from jax.experimental.pallas import tpu as pltpu

F32 = jnp.float32
BF16 = jnp.bfloat16
I32 = jnp.int32

EPS = 1e-5
N_HEADS = 4
CHUNK = 128
SSM_GROUP = 16
SSM_STATE = 64
SSM_T = 16
N_EXPERTS = 32
TOP_K = 4
SWIGLU_LIMIT = 7.0
SWIGLU_ALPHA = 1.702
LANES = 128
SUBLANES = 8
GROUPS_PER_LANE_BLOCK = LANES // SSM_GROUP

ROW_TILE = 512
EXPERT_TILE = 512
EXPERT_SUB = 512
EXPERT_CHUNK = 256
DMA_UNROLL = 8
COMBINE_ROWS = 64
VMEM_LIMIT = 56 * 1024 * 1024


def _gelu(x):
    return 0.5 * x * (1.0 + jnp.tanh(math.sqrt(2.0 / math.pi) * (x + 0.044715 * (x * x * x))))


def _sigmoid(x):
    return 1.0 / (1.0 + jnp.exp(-x))


def _rms(x, w):
    return x * lax.rsqrt(jnp.mean(x * x, axis=-1, keepdims=True) + EPS) * w


def _store_rows(ref, val, row0=0):
    rows, d = val.shape
    nsl = d // LANES
    for s in range(nsl):
        ref[pl.ds(row0 * nsl + s, rows, stride=nsl), :] = val[:, s * LANES:(s + 1) * LANES]


def _load_row_slab(ref, s, rows, nsl, row0=0):
    return ref[pl.ds(row0 * nsl + s, rows, stride=nsl), :]


def _params(**kw):
    return pltpu.CompilerParams(dimension_semantics=kw.pop("sem"), vmem_limit_bytes=VMEM_LIMIT, **kw)


def _front_kernel(x_ref, nw_ref, win_ref, lnw_ref, lnb_ref, ws_ref, bst_ref, ona_ref,
                  ya_ref, us_ref, mixed_ref, us_scr):
    d_g = ya_ref.shape[1]
    hd = d_g // N_HEADS
    tm = x_ref.shape[0]
    x = x_ref[...]
    h = _rms(x, nw_ref[...]).astype(BF16)
    proj = jnp.dot(h, win_ref[...], preferred_element_type=F32)
    u = _gelu(proj[:, :d_g])
    v = _gelu(proj[:, d_g:2 * d_g])
    n_lb = us_scr.shape[0]
    for q in range(n_lb):
        us_scr[q] = proj[:, 2 * d_g + q * LANES:2 * d_g + (q + 1) * LANES]
    mu = jnp.mean(v, axis=-1, keepdims=True)
    vc = v - mu
    var = jnp.mean(vc * vc, axis=-1, keepdims=True)
    vb = (vc * lax.rsqrt(var + EPS) * lnw_ref[...] + lnb_ref[...]).astype(BF16)
    row = lax.broadcasted_iota(I32, (CHUNK, CHUNK), 0)
    col = lax.broadcasted_iota(I32, (CHUNK, CHUNK), 1)
    causal = row >= col
    for hh in range(N_HEADS):
        w = jnp.where(causal, ws_ref[hh], 0.0).astype(BF16)
        bias = bst_ref[:, hh:hh + 1]
        for c in range(tm // CHUNK):
            vv = vb[c * CHUNK:(c + 1) * CHUNK, hh * hd:(hh + 1) * hd]
            m = jnp.dot(w, vv, preferred_element_type=F32) + bias
            mixed_ref[c * CHUNK:(c + 1) * CHUNK, hh * hd:(hh + 1) * hd] = m
    ya = u * mixed_ref[...]
    ya_ref[...] = _rms(ya, ona_ref[...]).astype(BF16)
    for s in range(SSM_T):
        for q in range(n_lb):
            us_ref[s, :, q * LANES:(q + 1) * LANES] = (
                us_scr[q, pl.ds(s, tm // SSM_T, stride=SSM_T), :].astype(BF16))


def _front(x2, nw, w_in, ln_w, ln_b, w_s, b_s, on_a):
    n, d = x2.shape
    d_g = ln_w.shape[0]
    d_s = w_in.shape[1] - 2 * d_g
    tm = ROW_TILE
    const2 = lambda i: (0, 0)
    return pl.pallas_call(
        _front_kernel,
        grid=(n // tm,),
        in_specs=[
            pl.BlockSpec((tm, d), lambda i: (i, 0)),
            pl.BlockSpec((1, d), const2),
            pl.BlockSpec(w_in.shape, const2),
            pl.BlockSpec((1, d_g), const2),
            pl.BlockSpec((1, d_g), const2),
            pl.BlockSpec(w_s.shape, lambda i: (0, 0, 0)),
            pl.BlockSpec((CHUNK, N_HEADS), const2),
            pl.BlockSpec((1, d_g), const2),
        ],
        out_specs=[
            pl.BlockSpec((tm, d_g), lambda i: (i, 0)),
            pl.BlockSpec((SSM_T, tm // SSM_T, d_s), lambda i: (0, i, 0)),
        ],
        out_shape=[
            jax.ShapeDtypeStruct((n, d_g), BF16),
            jax.ShapeDtypeStruct((SSM_T, n // SSM_T, d_s), BF16),
        ],
        scratch_shapes=[pltpu.VMEM((tm, d_g), F32), pltpu.VMEM((d_s // LANES, tm, LANES), F32)],
        compiler_params=_params(sem=("arbitrary",)),
        name="front",
    )(x2, nw.reshape(1, d), w_in.astype(BF16), ln_w.reshape(1, d_g), ln_b.reshape(1, d_g),
      w_s, b_s.T, on_a.reshape(1, d_g))


def _s5_tables(a_re, a_im, b_re, b_im, c_re, c_im, d, log_dt, n_steps):
    depth, g, p = a_re.shape
    hch = b_re.shape[-1]
    t = SSM_T
    gl = GROUPS_PER_LANE_BLOCK
    r = depth * (g // gl)
    a = lax.complex(a_re.astype(F32), a_im.astype(F32)).reshape(r, gl, p)
    dt = jnp.exp(log_dt.astype(F32)).reshape(r, gl, 1)
    dta = dt * a
    a_bar = jnp.exp(dta)
    b = lax.complex(b_re.astype(F32), b_im.astype(F32)).reshape(r, gl, p, hch)
    b_bar = ((a_bar - 1.0) / a)[..., None] * b
    c = lax.complex(c_re.astype(F32), c_im.astype(F32)).reshape(r, gl, hch, p)
    lags = jnp.arange(t + 1, dtype=F32)
    pw = jnp.exp(lags[None, :, None, None] * dta.transpose(0, 2, 1)[:, None])
    pwx = jnp.repeat(pw, hch, axis=-1)
    bb = b_bar.transpose(0, 2, 1, 3).reshape(r, p, gl * hch)
    cc = c.transpose(0, 3, 1, 2).reshape(r, p, gl * hch)
    q = pwx[:, :t] * bb[:, None]
    si = pwx[:, 1:] * cc[:, None]
    k2 = (jnp.einsum('rlpx,rpy->rlxy', q.real, cc.real, precision=lax.Precision.HIGHEST)
          - jnp.einsum('rlpx,rpy->rlxy', q.imag, cc.imag, precision=lax.Precision.HIGHEST))
    lane_g = jnp.arange(gl * hch) // hch
    k2 = jnp.where(lane_g[:, None] == lane_g[None, :], k2, 0.0)
    skip = jnp.eye(gl * hch, dtype=F32) * d.astype(F32).reshape(r, 1, gl * hch)
    k2 = k2.at[:, 0].add(skip)
    so3 = jnp.swapaxes(jnp.concatenate([q.real, q.imag], axis=2), 2, 3)
    si2 = jnp.concatenate([si.real, -si.imag], axis=2)
    steps = SSM_T * (2.0 ** jnp.arange(n_steps, dtype=F32))
    ap = jnp.exp(steps[None, :, None, None] * dta[:, None]).reshape(r, n_steps, gl * p)
    mul_same = jnp.concatenate([ap.real, ap.real], axis=-1)
    mul_swap = jnp.concatenate([-ap.imag, ap.imag], axis=-1)
    apw = jnp.stack([mul_same, mul_swap], axis=2)
    return k2.astype(BF16), so3.astype(BF16), si2.astype(BF16), apw


def _s5_kernel(u_ref, k2_ref, so3_ref, si2_ref, ap_ref, y_ref, wk_scr, wso_scr, wsi_scr):
    t_blk = u_ref.shape[0]
    rows = u_ref.shape[1]
    n_state = wso_scr.shape[1]
    two_p = so3_ref.shape[3]
    p = two_p // 2
    half = n_state // 2

    @pl.when(pl.program_id(1) == 0)
    def _():
        for s in range(t_blk):
            if s > 0:
                wk_scr[s * LANES:(s + 1) * LANES, :s * LANES] = jnp.zeros((LANES, s * LANES), BF16)
            for t in range(s, t_blk):
                wk_scr[s * LANES:(s + 1) * LANES, t * LANES:(t + 1) * LANES] = k2_ref[0, t - s]
        r_e = lax.broadcasted_iota(I32, (two_p, n_state), 0)
        c_e = lax.broadcasted_iota(I32, (two_p, n_state), 1)
        spread = jnp.where((r_e // p == c_e // half) & (r_e % p == c_e % p), 1.0, 0.0).astype(BF16)
        r_g = lax.broadcasted_iota(I32, (LANES, n_state), 0) // SSM_GROUP
        c_g = (lax.broadcasted_iota(I32, (LANES, n_state), 1) % half) // p
        own = r_g == c_g
        for s in range(t_blk):
            full = jnp.dot(so3_ref[0, t_blk - 1 - s], spread, preferred_element_type=F32)
            wso_scr[s * LANES:(s + 1) * LANES, :] = jnp.where(own, full, 0.0).astype(BF16)
        r_e = lax.broadcasted_iota(I32, (n_state, two_p), 0)
        c_e = lax.broadcasted_iota(I32, (n_state, two_p), 1)
        gather = jnp.where((c_e // p == r_e // half) & (c_e % p == r_e % p), 1.0, 0.0).astype(BF16)
        r_g = (lax.broadcasted_iota(I32, (n_state, LANES), 0) % half) // p
        c_g = lax.broadcasted_iota(I32, (n_state, LANES), 1) // SSM_GROUP
        own = r_g == c_g
        for t in range(t_blk):
            full = jnp.dot(gather, si2_ref[0, t], preferred_element_type=F32)
            wsi_scr[:, t * LANES:(t + 1) * LANES] = jnp.where(own, full, 0.0).astype(BF16)

    xcat = jnp.concatenate([u_ref[s] for s in range(t_blk)], axis=1)
    x = jnp.dot(xcat, wso_scr[...], preferred_element_type=F32)
    pos = lax.broadcasted_iota(I32, (rows, n_state), 0)
    for k in range(ap_ref.shape[1]):
        sh = 1 << k
        prev = pltpu.roll(x, sh, 0)
        prev_sw = pltpu.roll(prev, half, 1)
        upd = prev * ap_ref[0, k, 0:1, :] + prev_sw * ap_ref[0, k, 1:2, :]
        x = x + jnp.where(pos >= sh, upd, 0.0)
    xin = jnp.where(pos >= 1, pltpu.roll(x, 1, 0), 0.0).astype(BF16)
    for j in range(t_blk // 2):
        k_hi = (2 * j + 2) * LANES
        lo = 2 * j * LANES
        y = jnp.dot(xcat[:, :k_hi], wk_scr[:k_hi, lo:lo + 2 * LANES], preferred_element_type=F32)
        y = y + jnp.dot(xin, wsi_scr[:, lo:lo + 2 * LANES], preferred_element_type=F32)
        y_ref[2 * j] = y[:, :LANES].astype(BF16)
        y_ref[2 * j + 1] = y[:, LANES:].astype(BF16)


def _s5(us3, k2, so3, si2, apw, blocks_per_seq, layer):
    t_blk, n_blocks, d_s = us3.shape
    lb = d_s // LANES
    n_seq = n_blocks // blocks_per_seq
    n_in = t_blk * LANES
    n_state = apw.shape[3]
    w_map = lambda a, b: (layer * lb + a, 0, 0, 0)
    io_spec = pl.BlockSpec((t_blk, blocks_per_seq, LANES), lambda a, b: (0, b, a))
    return pl.pallas_call(
        _s5_kernel,
        grid=(lb, n_seq),
        in_specs=[
            io_spec,
            pl.BlockSpec((1,) + k2.shape[1:], w_map),
            pl.BlockSpec((1,) + so3.shape[1:], w_map),
            pl.BlockSpec((1,) + si2.shape[1:], w_map),
            pl.BlockSpec((1,) + apw.shape[1:], w_map),
        ],
        out_specs=io_spec,
        out_shape=jax.ShapeDtypeStruct(us3.shape, BF16),
        scratch_shapes=[
            pltpu.VMEM((n_in, n_in), BF16),
            pltpu.VMEM((n_in, n_state), BF16),
            pltpu.VMEM((n_state, n_in), BF16),
        ],
        compiler_params=_params(sem=("arbitrary", "arbitrary")),
        name="s5",
    )(us3, k2, so3, si2, apw)


def _back_kernel(x_ref, ya_ref, ys_ref, gw_ref, gb_ref, onb_ref, wo_ref, fnw_ref,
                 rwh_ref, rwl_ref, rb_ref,
                 x1_ref, h2_ref, gate_ref, er_ref, cnt_ref, carry_ref, ys_scr):
    i = pl.program_id(0)
    tm = x_ref.shape[0]
    d_g = ya_ref.shape[1]

    @pl.when(i == 0)
    def _():
        carry_ref[...] = jnp.zeros_like(carry_ref)

    n_lb = ys_scr.shape[0]
    for t in range(SSM_T):
        for q in range(n_lb):
            ys_scr[q, pl.ds(t, tm // SSM_T, stride=SSM_T), :] = (
                ys_ref[t, :, q * LANES:(q + 1) * LANES].astype(F32))
    y = _gelu(jnp.concatenate([ys_scr[q] for q in range(n_lb)], axis=1))
    z = jnp.dot(y.astype(BF16), gw_ref[...], preferred_element_type=F32) + gb_ref[...]
    yb = y * _sigmoid(z)
    ybn = _rms(yb, onb_ref[...]).astype(BF16)
    x1 = (x_ref[...]
          + jnp.dot(ya_ref[...], wo_ref[:d_g, :], preferred_element_type=F32)
          + jnp.dot(ybn, wo_ref[d_g:, :], preferred_element_type=F32))
    x1_ref[...] = x1
    h2 = _rms(x1, fnw_ref[...])
    _store_rows(h2_ref, h2)

    hh = h2.astype(BF16)
    hl = (h2 - hh.astype(F32)).astype(BF16)
    logits = (jnp.dot(hh, rwh_ref[...], preferred_element_type=F32)
              + jnp.dot(hl, rwh_ref[...], preferred_element_type=F32)
              + jnp.dot(hh, rwl_ref[...], preferred_element_type=F32)
              + rb_ref[...])
    lane = lax.broadcasted_iota(I32, (tm, LANES), 1)
    neg = jnp.float32(-jnp.inf)
    work = jnp.where(lane < N_EXPERTS, logits, neg)
    vals, hots, idxs = [], [], []
    for k in range(TOP_K):
        m = jnp.max(work, axis=-1, keepdims=True)
        idx = jnp.min(jnp.where(work == m, lane, LANES), axis=-1, keepdims=True)
        hot = lane == idx
        vals.append(m)
        hots.append(hot)
        idxs.append(idx)
        work = jnp.where(hot, neg, work)
    exps = [jnp.exp(v - vals[0]) for v in vals]
    denom = exps[0] + exps[1] + exps[2] + exps[3]
    gate = jnp.zeros((tm, LANES), F32)
    for k in range(TOP_K):
        gate = jnp.where(lane == k, exps[k] / denom, gate)
    gate_ref[...] = gate

    sel = (hots[0] | hots[1] | hots[2] | hots[3])
    sel_f = jnp.where(sel, 1.0, 0.0)
    r_i = lax.broadcasted_iota(I32, (tm, tm), 0)
    c_i = lax.broadcasted_iota(I32, (tm, tm), 1)
    lower = jnp.where(r_i > c_i, 1.0, 0.0).astype(BF16)
    cum = jnp.dot(lower, sel_f.astype(BF16), preferred_element_type=F32) + carry_ref[...]
    er = jnp.zeros((tm, LANES), I32)
    for k in range(TOP_K):
        rk = jnp.sum(jnp.where(hots[k], cum, 0.0), axis=-1, keepdims=True)
        er = jnp.where(lane == k, idxs[k], er)
        er = jnp.where(lane == TOP_K + k, rk.astype(I32), er)
    carry_ref[...] = carry_ref[...] + jnp.sum(sel_f, axis=0, keepdims=True)
    cnt_ref[...] = carry_ref[...]
    er_ref[0] = jnp.transpose(er)[:SUBLANES, :]


def _back(x2, ya, ys3, glu_w, glu_b, on_b, w_out, fn_w, router_w, router_b):
    n, d = x2.shape
    d_g = ya.shape[1]
    d_s = ys3.shape[2]
    tm = ROW_TILE
    nt = n // tm
    const2 = lambda i: (0, 0)
    rw = jnp.zeros((d, LANES), F32).at[:, :N_EXPERTS].set(router_w.astype(F32))
    rw_hi = rw.astype(BF16)
    rw_lo = (rw - rw_hi.astype(F32)).astype(BF16)
    rb = jnp.zeros((1, LANES), F32).at[0, :N_EXPERTS].set(router_b.astype(F32))
    tok_tile = lambda i: (i, 0)
    return pl.pallas_call(
        _back_kernel,
        grid=(nt,),
        in_specs=[
            pl.BlockSpec((tm, d), tok_tile),
            pl.BlockSpec((tm, d_g), tok_tile),
            pl.BlockSpec((SSM_T, tm // SSM_T, d_s), lambda i: (0, i, 0)),
            pl.BlockSpec((d_s, d_s), const2),
            pl.BlockSpec((1, d_s), const2),
            pl.BlockSpec((1, d_s), const2),
            pl.BlockSpec((d_g + d_s, d), const2),
            pl.BlockSpec((1, d), const2),
            pl.BlockSpec((d, LANES), const2),
            pl.BlockSpec((d, LANES), const2),
            pl.BlockSpec((1, LANES), const2),
        ],
        out_specs=[
            pl.BlockSpec((tm, d), tok_tile),
            pl.BlockSpec((tm * d // LANES, LANES), tok_tile),
            pl.BlockSpec((tm, LANES), tok_tile),
            pl.BlockSpec((1, SUBLANES, tm), lambda i: (i, 0, 0)),
            pl.BlockSpec((1, LANES), const2),
        ],
        out_shape=[
            jax.ShapeDtypeStruct((n, d), F32),
            jax.ShapeDtypeStruct((n * d // LANES, LANES), F32),
            jax.ShapeDtypeStruct((n, LANES), F32),
            jax.ShapeDtypeStruct((nt, SUBLANES, tm), I32),
            jax.ShapeDtypeStruct((1, LANES), F32),
        ],
        scratch_shapes=[pltpu.VMEM((1, LANES), F32), pltpu.VMEM((d_s // LANES, tm, LANES), F32)],
        compiler_params=_params(sem=("arbitrary",)),
        name="back",
    )(x2, ya, ys3, glu_w.astype(BF16), glu_b.reshape(1, d_s), on_b.reshape(1, d_s),
      w_out.astype(BF16), fn_w.reshape(1, d), rw_hi, rw_lo, rb)


def _routing_tables(er, counts, n, row_sl):
    te = EXPERT_TILE
    n_blocks = n * TOP_K // te + N_EXPERTS
    nb = (counts + te - 1) // te
    cum = jnp.cumsum(nb)
    pstart = (cum - nb) * te
    e_sel = er[:, :TOP_K, :, None] == jnp.arange(N_EXPERTS, dtype=I32)
    dest = jnp.sum(jnp.where(e_sel, pstart, 0), axis=-1) + er[:, TOP_K:2 * TOP_K, :]
    dest = (dest * row_sl).reshape(dest.shape[0], -1)
    total = cum[-1]
    j = jnp.arange(n_blocks, dtype=I32)
    e_j = jnp.sum((cum[None, :] <= jnp.minimum(j, total - 1)[:, None]).astype(I32), axis=1)
    valid = (j < total).astype(I32)
    last_block = jnp.where(nb > 0, (cum - 1) * te * row_sl, -1)
    last_block = jnp.concatenate([last_block, total[None]])
    first = valid * (j == (cum - nb)[e_j]).astype(I32)
    e_ids = jnp.arange(N_EXPERTS, dtype=I32)
    later = (e_ids[None, :] > e_j[:, None]) & (nb > 0)[None, :]
    nxt = jnp.min(jnp.where(later, e_ids[None, :], N_EXPERTS), axis=1)
    nxt = jnp.where(nxt < N_EXPERTS, nxt, -1)
    return dest.astype(I32), e_j.astype(I32), valid, first, nxt.astype(I32), last_block.astype(I32)


def _dispatch_kernel(last_ref, dest_hbm, h_ref, xs_hbm, idx_smem, zbuf, sem_idx, sem_rows, sem_z,
                     *, row_sl):
    i = pl.program_id(0)
    nt = pl.num_programs(0)
    tm = h_ref.shape[0] // row_sl
    blk = zbuf.shape[0]
    slot = i % 2

    def idx_copy(t, s):
        n_idx = TOP_K * tm
        dst = idx_smem.at[pl.ds(pl.multiple_of(s * n_idx, n_idx), n_idx)]
        return pltpu.make_async_copy(dest_hbm.at[t], dst, sem_idx.at[s])

    @pl.when(i == 0)
    def _():
        zbuf[...] = jnp.zeros_like(zbuf)

        def zero_copy(e):
            row = pl.multiple_of(jnp.maximum(last_ref[e], 0), blk)
            return pltpu.make_async_copy(zbuf, xs_hbm.at[pl.ds(row, blk)], sem_z)

        for e in range(N_EXPERTS):
            @pl.when(last_ref[e] >= 0)
            def _():
                zero_copy(e).start()
        for e in range(N_EXPERTS):
            @pl.when(last_ref[e] >= 0)
            def _():
                zero_copy(e).wait()
        n_blocks = xs_hbm.shape[0] // blk
        total = last_ref[N_EXPERTS]

        def tail_copy(j):
            return pltpu.make_async_copy(zbuf, xs_hbm.at[pl.ds(j * blk, blk)], sem_z)

        for j in range(n_blocks - N_EXPERTS, n_blocks):
            @pl.when(j >= total)
            def _():
                tail_copy(j).start()
        for j in range(n_blocks - N_EXPERTS, n_blocks):
            @pl.when(j >= total)
            def _():
                tail_copy(j).wait()
        idx_copy(0, 0).start()

    @pl.when(i + 1 < nt)
    def _():
        idx_copy(i + 1, 1 - slot).start()

    idx_copy(i, slot).wait()

    def body(j, carry):
        n0 = j * DMA_UNROLL
        i0 = slot * (TOP_K * tm) + n0
        for u in range(DMA_UNROLL):
            src = h_ref.at[pl.ds(pl.multiple_of((n0 + u) * row_sl, row_sl), row_sl)]
            for k in range(TOP_K):
                dst = xs_hbm.at[pl.ds(pl.multiple_of(idx_smem[i0 + (k * tm + u)], row_sl), row_sl)]
                pltpu.make_async_copy(src, dst, sem_rows.at[k]).start(priority=k % 2)
        return carry
    lax.fori_loop(0, tm // DMA_UNROLL, body, 0)
    for k in range(TOP_K):
        pltpu.make_async_copy(h_ref, xs_hbm.at[pl.ds(0, tm * row_sl)], sem_rows.at[k]).wait()


def _dispatch(last_block, dest, h2t, n_rows, row_sl):
    tm = ROW_TILE
    nt = h2t.shape[0] // (tm * row_sl)
    grid_spec = pltpu.PrefetchScalarGridSpec(
        num_scalar_prefetch=1,
        grid=(nt,),
        in_specs=[
            pl.BlockSpec(memory_space=pl.ANY),
            pl.BlockSpec((tm * row_sl, LANES), lambda i, last: (i, 0)),
        ],
        out_specs=pl.BlockSpec(memory_space=pl.ANY),
        scratch_shapes=[
            pltpu.SMEM((2 * TOP_K * tm,), I32),
            pltpu.VMEM((EXPERT_TILE * row_sl, LANES), F32),
            pltpu.SemaphoreType.DMA((2,)),
            pltpu.SemaphoreType.DMA((TOP_K,)),
            pltpu.SemaphoreType.DMA,
        ],
    )
    return pl.pallas_call(
        functools.partial(_dispatch_kernel, row_sl=row_sl),
        grid_spec=grid_spec,
        out_shape=jax.ShapeDtypeStruct((n_rows * row_sl, LANES), F32),
        compiler_params=_params(sem=("arbitrary",)),
        name="dispatch",
    )(last_block, dest, h2t)


def _expert_kernel(be_ref, bv_ref, first_ref, next_ref, xs_ref, wgu_hbm, bgu_ref, wdn_hbm, bdn_ref, ys_ref,
                   stage_gu, stage_dn, wgu_bf, wdn_bf, act_scr, sem_w, *, layer):
    i = pl.program_id(0)
    d, d_gu = wgu_bf.shape
    d_e = wdn_bf.shape[0]
    nsl = d // LANES
    te = xs_ref.shape[0] // nsl

    def weight_copies(e):
        row = layer * N_EXPERTS + e
        return (pltpu.make_async_copy(wgu_hbm.at[row], stage_gu, sem_w.at[0]),
                pltpu.make_async_copy(wdn_hbm.at[row], stage_dn, sem_w.at[1]))

    @pl.when(i == 0)
    def _():
        for cp in weight_copies(be_ref[0]):
            cp.start(priority=1)

    @pl.when(first_ref[i] > 0)
    def _():
        for cp in weight_copies(be_ref[i]):
            cp.wait()
        rows = 128
        for r in range(0, d, rows):
            wgu_bf[r:r + rows, :] = stage_gu[r:r + rows, :].astype(BF16)
        for r in range(0, d_e, rows):
            wdn_bf[r:r + rows, :] = stage_dn[r:r + rows, :].astype(BF16)

        @pl.when(next_ref[i] >= 0)
        def _():
            for cp in weight_copies(next_ref[i]):
                cp.start(priority=1)

    @pl.when(bv_ref[i] > 0)
    def _():
        for r0 in range(0, te, EXPERT_SUB):
            x = jnp.concatenate([_load_row_slab(xs_ref, s, EXPERT_SUB, nsl, r0).astype(BF16)
                                 for s in range(nsl)], axis=1)
            for c in range(d_e // EXPERT_CHUNK):
                lo = c * EXPERT_CHUNK
                hi = lo + EXPERT_CHUNK
                glu = jnp.dot(x, wgu_bf[:, lo:hi], preferred_element_type=F32) + bgu_ref[0, :, lo:hi]
                lin = (jnp.dot(x, wgu_bf[:, d_e + lo:d_e + hi], preferred_element_type=F32)
                       + bgu_ref[0, :, d_e + lo:d_e + hi])
                glu = jnp.minimum(glu, SWIGLU_LIMIT)
                lin = jnp.clip(lin, -SWIGLU_LIMIT, SWIGLU_LIMIT)
                act_scr[r0:r0 + EXPERT_SUB, lo:hi] = (
                    glu * _sigmoid(SWIGLU_ALPHA * glu) * (lin + 1.0)).astype(BF16)
            y = jnp.dot(act_scr[r0:r0 + EXPERT_SUB, :], wdn_bf[...], preferred_element_type=F32) + bdn_ref[0]
            _store_rows(ys_ref, y, r0)

    @pl.when(bv_ref[i] == 0)
    def _():
        ys_ref[...] = jnp.zeros_like(ys_ref)


def _experts(block_expert, block_valid, block_first, block_next, xs, w_gu, b_gu, w_dn, b_dn, layer):
    d = w_gu.shape[2]
    n_blocks = block_expert.shape[0]
    te = EXPERT_TILE * d // LANES
    d_gu = w_gu.shape[3]
    d_e = w_dn.shape[2]
    e_map3 = lambda i, be, bv, bf, bn: (layer * N_EXPERTS + be[i], 0, 0)
    w_gu = w_gu.reshape((-1,) + w_gu.shape[2:])
    w_dn = w_dn.reshape((-1,) + w_dn.shape[2:])
    b_gu = b_gu.reshape(-1, 1, d_gu)
    b_dn = b_dn.reshape(-1, 1, d)
    x_map = lambda i, be, bv, bf, bn: (jnp.where(bv[i] > 0, i, 0), 0)
    grid_spec = pltpu.PrefetchScalarGridSpec(
        num_scalar_prefetch=4,
        grid=(n_blocks,),
        in_specs=[
            pl.BlockSpec((te, LANES), x_map),
            pl.BlockSpec(memory_space=pl.ANY),
            pl.BlockSpec((1, 1, d_gu), e_map3),
            pl.BlockSpec(memory_space=pl.ANY),
            pl.BlockSpec((1, 1, d), e_map3),
        ],
        out_specs=pl.BlockSpec((te, LANES), lambda i, be, bv, bf, bn: (i, 0)),
        scratch_shapes=[
            pltpu.VMEM((d, d_gu), F32),
            pltpu.VMEM((d_e, d), F32),
            pltpu.VMEM((d, d_gu), BF16),
            pltpu.VMEM((d_e, d), BF16),
            pltpu.VMEM((EXPERT_TILE, d_e), BF16),
            pltpu.SemaphoreType.DMA((2,)),
        ],
    )
    return pl.pallas_call(
        functools.partial(_expert_kernel, layer=layer),
        grid_spec=grid_spec,
        out_shape=jax.ShapeDtypeStruct((n_blocks * te, LANES), F32),
        compiler_params=_params(sem=("arbitrary",)),
        name="experts",
    )(block_expert, block_valid, block_first, block_next, xs, w_gu, b_gu, w_dn, b_dn)


def _combine_kernel(dest_hbm, x1_ref, g_ref, fw_ref, ys_hbm, o_ref,
                    idx_smem, ybuf, sem_idx, sem_rows, *, final_norm):
    i = pl.program_id(0)
    nt = pl.num_programs(0)
    tm, d = x1_ref.shape
    nsl = d // LANES
    slot = i % 2

    def idx_copy(t, s):
        n_idx = TOP_K * tm
        dst = idx_smem.at[pl.ds(pl.multiple_of(s * n_idx, n_idx), n_idx)]
        return pltpu.make_async_copy(dest_hbm.at[t], dst, sem_idx.at[s])

    def issue_rows(s):
        def body(j, carry):
            n0 = j * DMA_UNROLL
            i0 = s * (TOP_K * tm) + n0
            for u in range(DMA_UNROLL):
                for k in range(TOP_K):
                    src = ys_hbm.at[pl.ds(pl.multiple_of(idx_smem[i0 + (k * tm + u)], nsl), nsl)]
                    dst = ybuf.at[s, k, pl.ds(pl.multiple_of((n0 + u) * nsl, nsl), nsl)]
                    pltpu.make_async_copy(src, dst, sem_rows.at[s, k]).start(priority=k % 2)
            return carry
        lax.fori_loop(0, tm // DMA_UNROLL, body, 0)

    def wait_rows(s):
        for k in range(TOP_K):
            pltpu.make_async_copy(ys_hbm.at[pl.ds(0, tm * nsl)], ybuf.at[s, k], sem_rows.at[s, k]).wait()

    @pl.when(i == 0)
    def _():
        cp = idx_copy(0, 0)
        cp.start()
        cp.wait()
        issue_rows(0)

        @pl.when(nt > 1)
        def _():
            idx_copy(1, 1).start()

    @pl.when(i + 1 < nt)
    def _():
        idx_copy(i + 1, 1 - slot).wait()
        issue_rows(1 - slot)

        @pl.when(i + 2 < nt)
        def _():
            idx_copy(i + 2, slot).start()

    wait_rows(slot)
    for r0 in range(0, tm, COMBINE_ROWS):
        g = g_ref[r0:r0 + COMBINE_ROWS, :]
        gk = [jnp.broadcast_to(g[:, k:k + 1], (COMBINE_ROWS, LANES)) for k in range(TOP_K)]
        slabs = []
        for s in range(nsl):
            acc = x1_ref[r0:r0 + COMBINE_ROWS, s * LANES:(s + 1) * LANES]
            for k in range(TOP_K):
                acc = acc + gk[k] * _load_row_slab(ybuf.at[slot, k], s, COMBINE_ROWS, nsl, r0)
            slabs.append(acc)
        out = jnp.concatenate(slabs, axis=1)
        if final_norm:
            out = _rms(out, fw_ref[...])
        o_ref[r0:r0 + COMBINE_ROWS, :] = out


def _combine(x1, ys, gate, dest, final_w, final_norm):
    n, d = x1.shape
    tm = ROW_TILE
    nt = n // tm
    tok_tile = lambda i: (i, 0)
    return pl.pallas_call(
        functools.partial(_combine_kernel, final_norm=final_norm),
        grid=(nt,),
        in_specs=[
            pl.BlockSpec(memory_space=pl.ANY),
            pl.BlockSpec((tm, d), tok_tile),
            pl.BlockSpec((tm, LANES), tok_tile),
            pl.BlockSpec((1, d), lambda i: (0, 0)),
            pl.BlockSpec(memory_space=pl.ANY),
        ],
        out_specs=pl.BlockSpec((tm, d), tok_tile),
        out_shape=jax.ShapeDtypeStruct((n, d), F32),
        scratch_shapes=[
            pltpu.SMEM((2 * TOP_K * tm,), I32),
            pltpu.VMEM((2, TOP_K, tm * d // LANES, LANES), F32),
            pltpu.SemaphoreType.DMA((2,)),
            pltpu.SemaphoreType.DMA((2, TOP_K)),
        ],
        compiler_params=_params(sem=("arbitrary",)),
        name="combine",
    )(dest, x1, gate, final_w.reshape(1, d), ys)


def kernel(x, attn_norm_w, w_in, sgu_ln_w, sgu_ln_b, sgu_w, sgu_b, ssm_a_re, ssm_a_im, ssm_b_re, ssm_b_im, ssm_c_re, ssm_c_im, ssm_d, ssm_log_dt, ssm_glu_w, ssm_glu_b, out_norm_a, out_norm_b, w_out, ffn_norm_w, router_w, router_b, w_gate_up, b_gate_up, w_down, b_down, final_norm_w):
    b, l, d = x.shape
    n = b * l
    depth = w_in.shape[0]
    blocks_per_seq = l // SSM_T
    n_steps = max(1, (blocks_per_seq - 1).bit_length())
    n_rows = (n * TOP_K // EXPERT_TILE + N_EXPERTS) * EXPERT_TILE
    x2 = x.reshape(n, d).astype(F32)
    k2, so3, si2, apw = _s5_tables(ssm_a_re, ssm_a_im, ssm_b_re, ssm_b_im, ssm_c_re, ssm_c_im, ssm_d,
                                   ssm_log_dt, n_steps)
    for layer in range(depth):
        ya, us3 = _front(x2, attn_norm_w[layer], w_in[layer], sgu_ln_w[layer], sgu_ln_b[layer],
                         sgu_w[layer], sgu_b[layer], out_norm_a[layer])
        ys3 = _s5(us3, k2, so3, si2, apw, blocks_per_seq, layer)
        x1, h2, gate, er, cnt = _back(x2, ya, ys3, ssm_glu_w[layer], ssm_glu_b[layer],
                                      out_norm_b[layer], w_out[layer], ffn_norm_w[layer],
                                      router_w[layer], router_b[layer])
        counts = cnt[0, :N_EXPERTS].astype(I32)
        dest, block_expert, block_valid, block_first, block_next, last_block = _routing_tables(
            er, counts, n, d // LANES)
        xs = _dispatch(last_block, dest, h2, n_rows, d // LANES)
        ys = _experts(block_expert, block_valid, block_first, block_next, xs,
                      w_gate_up, b_gate_up, w_down, b_down, layer)
        x2 = _combine(x1, ys, gate, dest, final_norm_w, final_norm=(layer == depth - 1))
    return x2.reshape(b, l, d).astype(x.dtype)
```

```python
import functools
import math

import jax
import jax.numpy as jnp
from jax import lax
from jax.experimental import pallas as pl
from jax.experimental.pallas import tpu as pltpu

F32 = jnp.float32
BF16 = jnp.bfloat16
I32 = jnp.int32

EPS = 1e-5
N_HEADS = 4
CHUNK = 128
SSM_GROUP = 16
SSM_STATE = 64
SSM_T = 16
N_EXPERTS = 32
TOP_K = 4
SWIGLU_LIMIT = 7.0
SWIGLU_ALPHA = 1.702
LANES = 128
SUBLANES = 8
GROUPS_PER_LANE_BLOCK = LANES // SSM_GROUP

ROW_TILE = 512
EXPERT_TILE = 512
EXPERT_CHUNK = 256
DMA_UNROLL = 8
COMBINE_ROWS = 64
VMEM_LIMIT = 56 * 1024 * 1024


def _gelu(x):
    return 0.5 * x * (1.0 + jnp.tanh(math.sqrt(2.0 / math.pi) * (x + 0.044715 * (x * x * x))))


def _sigmoid(x):
    return 1.0 / (1.0 + jnp.exp(-x))


def _rms(x, w):
    return x * lax.rsqrt(jnp.mean(x * x, axis=-1, keepdims=True) + EPS) * w


def _store_rows(ref, val, row0=0):
    rows, d = val.shape
    nsl = d // LANES
    for s in range(nsl):
        ref[pl.ds(row0 * nsl + s, rows, stride=nsl), :] = val[:, s * LANES:(s + 1) * LANES]


def _load_row_slab(ref, s, rows, nsl, row0=0):
    return ref[pl.ds(row0 * nsl + s, rows, stride=nsl), :]


def _params(**kw):
    return pltpu.CompilerParams(dimension_semantics=kw.pop("sem"), vmem_limit_bytes=VMEM_LIMIT, **kw)


def _front_kernel(x_ref, nw_ref, win_ref, lnw_ref, lnb_ref, ws_ref, bst_ref, ona_ref,
                  ya_ref, us_ref, mixed_ref, us_scr):
    d_g = ya_ref.shape[1]
    hd = d_g // N_HEADS
    tm = x_ref.shape[0]
    x = x_ref[...]
    h = _rms(x, nw_ref[...]).astype(BF16)
    proj = jnp.dot(h, win_ref[...], preferred_element_type=F32)
    u = _gelu(proj[:, :d_g])
    v = _gelu(proj[:, d_g:2 * d_g])
    n_lb = us_scr.shape[0]
    for q in range(n_lb):
        us_scr[q] = proj[:, 2 * d_g + q * LANES:2 * d_g + (q + 1) * LANES]
    mu = jnp.mean(v, axis=-1, keepdims=True)
    vc = v - mu
    var = jnp.mean(vc * vc, axis=-1, keepdims=True)
    vb = (vc * lax.rsqrt(var + EPS) * lnw_ref[...] + lnb_ref[...]).astype(BF16)
    row = lax.broadcasted_iota(I32, (CHUNK, CHUNK), 0)
    col = lax.broadcasted_iota(I32, (CHUNK, CHUNK), 1)
    causal = row >= col
    for hh in range(N_HEADS):
        w = jnp.where(causal, ws_ref[hh], 0.0).astype(BF16)
        bias = bst_ref[:, hh:hh + 1]
        for c in range(tm // CHUNK):
            vv = vb[c * CHUNK:(c + 1) * CHUNK, hh * hd:(hh + 1) * hd]
            m = jnp.dot(w, vv, preferred_element_type=F32) + bias
            mixed_ref[c * CHUNK:(c + 1) * CHUNK, hh * hd:(hh + 1) * hd] = m
    ya = u * mixed_ref[...]
    ya_ref[...] = _rms(ya, ona_ref[...]).astype(BF16)
    for s in range(SSM_T):
        for q in range(n_lb):
            us_ref[s, :, q * LANES:(q + 1) * LANES] = (
                us_scr[q, pl.ds(s, tm // SSM_T, stride=SSM_T), :].astype(BF16))


def _front(x2, nw, w_in, ln_w, ln_b, w_s, b_s, on_a):
    n, d = x2.shape
    d_g = ln_w.shape[0]
    d_s = w_in.shape[1] - 2 * d_g
    tm = ROW_TILE
    const2 = lambda i: (0, 0)
    return pl.pallas_call(
        _front_kernel,
        grid=(n // tm,),
        in_specs=[
            pl.BlockSpec((tm, d), lambda i: (i, 0)),
            pl.BlockSpec((1, d), const2),
            pl.BlockSpec(w_in.shape, const2),
            pl.BlockSpec((1, d_g), const2),
            pl.BlockSpec((1, d_g), const2),
            pl.BlockSpec(w_s.shape, lambda i: (0, 0, 0)),
            pl.BlockSpec((CHUNK, N_HEADS), const2),
            pl.BlockSpec((1, d_g), const2),
        ],
        out_specs=[
            pl.BlockSpec((tm, d_g), lambda i: (i, 0)),
            pl.BlockSpec((SSM_T, tm // SSM_T, d_s), lambda i: (0, i, 0)),
        ],
        out_shape=[
            jax.ShapeDtypeStruct((n, d_g), BF16),
            jax.ShapeDtypeStruct((SSM_T, n // SSM_T, d_s), BF16),
        ],
        scratch_shapes=[pltpu.VMEM((tm, d_g), F32), pltpu.VMEM((d_s // LANES, tm, LANES), F32)],
        compiler_params=_params(sem=("arbitrary",)),
        name="front",
    )(x2, nw.reshape(1, d), w_in.astype(BF16), ln_w.reshape(1, d_g), ln_b.reshape(1, d_g),
      w_s, b_s.T, on_a.reshape(1, d_g))


def _s5_tables(a_re, a_im, b_re, b_im, c_re, c_im, d, log_dt, n_steps):
    depth, g, p = a_re.shape
    hch = b_re.shape[-1]
    t = SSM_T
    gl = GROUPS_PER_LANE_BLOCK
    r = depth * (g // gl)
    a = lax.complex(a_re.astype(F32), a_im.astype(F32)).reshape(r, gl, p)
    dt = jnp.exp(log_dt.astype(F32)).reshape(r, gl, 1)
    dta = dt * a
    a_bar = jnp.exp(dta)
    b = lax.complex(b_re.astype(F32), b_im.astype(F32)).reshape(r, gl, p, hch)
    b_bar = ((a_bar - 1.0) / a)[..., None] * b
    c = lax.complex(c_re.astype(F32), c_im.astype(F32)).reshape(r, gl, hch, p)
    lags = jnp.arange(t + 1, dtype=F32)
    pw = jnp.exp(lags[None, :, None, None] * dta.transpose(0, 2, 1)[:, None])
    pwx = jnp.repeat(pw, hch, axis=-1)
    bb = b_bar.transpose(0, 2, 1, 3).reshape(r, p, gl * hch)
    cc = c.transpose(0, 3, 1, 2).reshape(r, p, gl * hch)
    q = pwx[:, :t] * bb[:, None]
    si = pwx[:, 1:] * cc[:, None]
    k2 = (jnp.einsum('rlpx,rpy->rlxy', q.real, cc.real, precision=lax.Precision.HIGHEST)
          - jnp.einsum('rlpx,rpy->rlxy', q.imag, cc.imag, precision=lax.Precision.HIGHEST))
    lane_g = jnp.arange(gl * hch) // hch
    k2 = jnp.where(lane_g[:, None] == lane_g[None, :], k2, 0.0)
    skip = jnp.eye(gl * hch, dtype=F32) * d.astype(F32).reshape(r, 1, gl * hch)
    k2 = k2.at[:, 0].add(skip)
    so3 = jnp.swapaxes(jnp.concatenate([q.real, q.imag], axis=2), 2, 3)
    si2 = jnp.concatenate([si.real, -si.imag], axis=2)
    steps = SSM_T * (2.0 ** jnp.arange(n_steps, dtype=F32))
    ap = jnp.exp(steps[None, :, None, None] * dta[:, None]).reshape(r, n_steps, gl * p)
    mul_same = jnp.concatenate([ap.real, ap.real], axis=-1)
    mul_swap = jnp.concatenate([-ap.imag, ap.imag], axis=-1)
    apw = jnp.stack([mul_same, mul_swap], axis=2)
    return k2.astype(BF16), so3.astype(BF16), si2.astype(BF16), apw


def _s5_kernel(u_ref, k2_ref, so3_ref, si2_ref, ap_ref, y_ref, wk_scr, wso_scr, wsi_scr):
    t_blk = u_ref.shape[0]
    rows = u_ref.shape[1]
    n_state = wso_scr.shape[1]
    two_p = so3_ref.shape[3]
    p = two_p // 2
    half = n_state // 2

    @pl.when(pl.program_id(1) == 0)
    def _():
        for s in range(t_blk):
            if s > 0:
                wk_scr[s * LANES:(s + 1) * LANES, :s * LANES] = jnp.zeros((LANES, s * LANES), BF16)
            for t in range(s, t_blk):
                wk_scr[s * LANES:(s + 1) * LANES, t * LANES:(t + 1) * LANES] = k2_ref[0, t - s]
        r_e = lax.broadcasted_iota(I32, (two_p, n_state), 0)
        c_e = lax.broadcasted_iota(I32, (two_p, n_state), 1)
        spread = jnp.where((r_e // p == c_e // half) & (r_e % p == c_e % p), 1.0, 0.0).astype(BF16)
        r_g = lax.broadcasted_iota(I32, (LANES, n_state), 0) // SSM_GROUP
        c_g = (lax.broadcasted_iota(I32, (LANES, n_state), 1) % half) // p
        own = r_g == c_g
        for s in range(t_blk):
            full = jnp.dot(so3_ref[0, t_blk - 1 - s], spread, preferred_element_type=F32)
            wso_scr[s * LANES:(s + 1) * LANES, :] = jnp.where(own, full, 0.0).astype(BF16)
        r_e = lax.broadcasted_iota(I32, (n_state, two_p), 0)
        c_e = lax.broadcasted_iota(I32, (n_state, two_p), 1)
        gather = jnp.where((c_e // p == r_e // half) & (c_e % p == r_e % p), 1.0, 0.0).astype(BF16)
        r_g = (lax.broadcasted_iota(I32, (n_state, LANES), 0) % half) // p
        c_g = lax.broadcasted_iota(I32, (n_state, LANES), 1) // SSM_GROUP
        own = r_g == c_g
        for t in range(t_blk):
            full = jnp.dot(gather, si2_ref[0, t], preferred_element_type=F32)
            wsi_scr[:, t * LANES:(t + 1) * LANES] = jnp.where(own, full, 0.0).astype(BF16)

    xcat = jnp.concatenate([u_ref[s] for s in range(t_blk)], axis=1)
    x = jnp.dot(xcat, wso_scr[...], preferred_element_type=F32)
    pos = lax.broadcasted_iota(I32, (rows, n_state), 0)
    for k in range(ap_ref.shape[1]):
        sh = 1 << k
        prev = pltpu.roll(x, sh, 0)
        prev_sw = pltpu.roll(prev, half, 1)
        upd = prev * ap_ref[0, k, 0:1, :] + prev_sw * ap_ref[0, k, 1:2, :]
        x = x + jnp.where(pos >= sh, upd, 0.0)
    xin = jnp.where(pos >= 1, pltpu.roll(x, 1, 0), 0.0).astype(BF16)
    for j in range(t_blk // 2):
        k_hi = (2 * j + 2) * LANES
        lo = 2 * j * LANES
        y = jnp.dot(xcat[:, :k_hi], wk_scr[:k_hi, lo:lo + 2 * LANES], preferred_element_type=F32)
        y = y + jnp.dot(xin, wsi_scr[:, lo:lo + 2 * LANES], preferred_element_type=F32)
        y_ref[2 * j] = y[:, :LANES].astype(BF16)
        y_ref[2 * j + 1] = y[:, LANES:].astype(BF16)


def _s5(us3, k2, so3, si2, apw, blocks_per_seq, layer):
    t_blk, n_blocks, d_s = us3.shape
    lb = d_s // LANES
    n_seq = n_blocks // blocks_per_seq
    n_in = t_blk * LANES
    n_state = apw.shape[3]
    w_map = lambda a, b: (layer * lb + a, 0, 0, 0)
    io_spec = pl.BlockSpec((t_blk, blocks_per_seq, LANES), lambda a, b: (0, b, a))
    return pl.pallas_call(
        _s5_kernel,
        grid=(lb, n_seq),
        in_specs=[
            io_spec,
            pl.BlockSpec((1,) + k2.shape[1:], w_map),
            pl.BlockSpec((1,) + so3.shape[1:], w_map),
            pl.BlockSpec((1,) + si2.shape[1:], w_map),
            pl.BlockSpec((1,) + apw.shape[1:], w_map),
        ],
        out_specs=io_spec,
        out_shape=jax.ShapeDtypeStruct(us3.shape, BF16),
        scratch_shapes=[
            pltpu.VMEM((n_in, n_in), BF16),
            pltpu.VMEM((n_in, n_state), BF16),
            pltpu.VMEM((n_state, n_in), BF16),
        ],
        compiler_params=_params(sem=("arbitrary", "arbitrary")),
        name="s5",
    )(us3, k2, so3, si2, apw)


def _back_kernel(x_ref, ya_ref, ys_ref, gw_ref, gb_ref, onb_ref, wo_ref, fnw_ref,
                 rw_ref, rb_ref,
                 x1_ref, h2_ref, gate_ref, er_ref, cnt_ref, carry_ref, ys_scr):
    i = pl.program_id(0)
    tm = x_ref.shape[0]
    d_g = ya_ref.shape[1]

    @pl.when(i == 0)
    def _():
        carry_ref[...] = jnp.zeros_like(carry_ref)

    n_lb = ys_scr.shape[0]
    for t in range(SSM_T):
        for q in range(n_lb):
            ys_scr[q, pl.ds(t, tm // SSM_T, stride=SSM_T), :] = (
                ys_ref[t, :, q * LANES:(q + 1) * LANES].astype(F32))
    y = _gelu(jnp.concatenate([ys_scr[q] for q in range(n_lb)], axis=1))
    z = jnp.dot(y.astype(BF16), gw_ref[...], preferred_element_type=F32) + gb_ref[...]
    yb = y * _sigmoid(z)
    ybn = _rms(yb, onb_ref[...]).astype(BF16)
    x1 = (x_ref[...]
          + jnp.dot(ya_ref[...], wo_ref[:d_g, :], preferred_element_type=F32)
          + jnp.dot(ybn, wo_ref[d_g:, :], preferred_element_type=F32))
    x1_ref[...] = x1
    h2 = _rms(x1, fnw_ref[...])
    _store_rows(h2_ref, h2)

    hh = h2.astype(BF16)
    hl = (h2 - hh.astype(F32)).astype(BF16)
    p_hi = jnp.dot(hh, rw_ref[...], preferred_element_type=F32)
    p_lo = jnp.dot(hl, rw_ref[:, :LANES], preferred_element_type=F32)
    logits = p_hi[:, :LANES] + p_lo + p_hi[:, LANES:] + rb_ref[...]
    lane = lax.broadcasted_iota(I32, (tm, LANES), 1)
    lane_f = lane.astype(F32)
    neg = jnp.float32(-jnp.inf)
    work = jnp.where(lane < N_EXPERTS, logits, neg)
    vals, hots, idxs = [], [], []
    for k in range(TOP_K):
        m = jnp.max(work, axis=-1, keepdims=True)
        idx = jnp.min(jnp.where(work == m, lane_f, float(LANES)), axis=-1, keepdims=True)
        hot = lane_f == idx
        vals.append(m)
        hots.append(hot)
        idxs.append(idx.astype(I32))
        work = jnp.where(hot, neg, work)
    exps = [jnp.exp(v - vals[0]) for v in vals]
    denom = exps[0] + exps[1] + exps[2] + exps[3]
    gate = jnp.zeros((tm, LANES), F32)
    for k in range(TOP_K):
        gate = jnp.where(lane == k, exps[k] / denom, gate)
    gate_ref[...] = gate

    sel = (hots[0] | hots[1] | hots[2] | hots[3])
    sel_f = jnp.where(sel, 1.0, 0.0)
    r_i = lax.broadcasted_iota(I32, (tm, tm), 0)
    c_i = lax.broadcasted_iota(I32, (tm, tm), 1)
    lower = jnp.where(r_i > c_i, 1.0, 0.0).astype(BF16)
    cum = jnp.dot(lower, sel_f.astype(BF16), preferred_element_type=F32) + carry_ref[...]
    er = jnp.zeros((tm, LANES), I32)
    for k in range(TOP_K):
        rk = jnp.sum(jnp.where(hots[k], cum, 0.0), axis=-1, keepdims=True)
        er = jnp.where(lane == k, idxs[k], er)
        er = jnp.where(lane == TOP_K + k, rk.astype(I32), er)
    carry_ref[...] = carry_ref[...] + jnp.sum(sel_f, axis=0, keepdims=True)
    cnt_ref[...] = carry_ref[...]
    er_ref[0] = jnp.transpose(er)[:SUBLANES, :]


def _back(x2, ya, ys3, glu_w, glu_b, on_b, w_out, fn_w, router_w, router_b):
    n, d = x2.shape
    d_g = ya.shape[1]
    d_s = ys3.shape[2]
    tm = ROW_TILE
    nt = n // tm
    const2 = lambda i: (0, 0)
    rw = jnp.zeros((d, LANES), F32).at[:, :N_EXPERTS].set(router_w.astype(F32))
    rw_hi = rw.astype(BF16)
    rw_lo = (rw - rw_hi.astype(F32)).astype(BF16)
    rw_split = jnp.concatenate([rw_hi, rw_lo], axis=1)
    rb = jnp.zeros((1, LANES), F32).at[0, :N_EXPERTS].set(router_b.astype(F32))
    tok_tile = lambda i: (i, 0)
    return pl.pallas_call(
        _back_kernel,
        grid=(nt,),
        in_specs=[
            pl.BlockSpec((tm, d), tok_tile),
            pl.BlockSpec((tm, d_g), tok_tile),
            pl.BlockSpec((SSM_T, tm // SSM_T, d_s), lambda i: (0, i, 0)),
            pl.BlockSpec((d_s, d_s), const2),
            pl.BlockSpec((1, d_s), const2),
            pl.BlockSpec((1, d_s), const2),
            pl.BlockSpec((d_g + d_s, d), const2),
            pl.BlockSpec((1, d), const2),
            pl.BlockSpec((d, 2 * LANES), const2),
            pl.BlockSpec((1, LANES), const2),
        ],
        out_specs=[
            pl.BlockSpec((tm, d), tok_tile),
            pl.BlockSpec((tm * d // LANES, LANES), tok_tile),
            pl.BlockSpec((tm, LANES), tok_tile),
            pl.BlockSpec((1, SUBLANES, tm), lambda i: (i, 0, 0)),
            pl.BlockSpec((1, LANES), const2),
        ],
        out_shape=[
            jax.ShapeDtypeStruct((n, d), F32),
            jax.ShapeDtypeStruct((n * d // LANES, LANES), F32),
            jax.ShapeDtypeStruct((n, LANES), F32),
            jax.ShapeDtypeStruct((nt, SUBLANES, tm), I32),
            jax.ShapeDtypeStruct((1, LANES), F32),
        ],
        scratch_shapes=[pltpu.VMEM((1, LANES), F32), pltpu.VMEM((d_s // LANES, tm, LANES), F32)],
        compiler_params=_params(sem=("arbitrary",)),
        name="back",
    )(x2, ya, ys3, glu_w.astype(BF16), glu_b.reshape(1, d_s), on_b.reshape(1, d_s),
      w_out.astype(BF16), fn_w.reshape(1, d), rw_split, rb)


def _routing_tables(er, counts, n, row_sl):
    te = EXPERT_TILE
    n_blocks = n * TOP_K // te + N_EXPERTS
    nb = (counts + te - 1) // te
    cum = jnp.cumsum(nb)
    pstart = (cum - nb) * te
    e_sel = er[:, :TOP_K, :, None] == jnp.arange(N_EXPERTS, dtype=I32)
    dest = jnp.sum(jnp.where(e_sel, pstart, 0), axis=-1) + er[:, TOP_K:2 * TOP_K, :]
    dest = (dest * row_sl).reshape(dest.shape[0], -1)
    total = cum[-1]
    j = jnp.arange(n_blocks, dtype=I32)
    e_j = jnp.sum((cum[None, :] <= jnp.minimum(j, total - 1)[:, None]).astype(I32), axis=1)
    valid = (j < total).astype(I32)
    last_block = jnp.where(nb > 0, (cum - 1) * te * row_sl, -1)
    last_block = jnp.concatenate([last_block, total[None]])
    first = valid * (j == (cum - nb)[e_j]).astype(I32)
    e_ids = jnp.arange(N_EXPERTS, dtype=I32)
    later = (e_ids[None, :] > e_j[:, None]) & (nb > 0)[None, :]
    nxt = jnp.min(jnp.where(later, e_ids[None, :], N_EXPERTS), axis=1)
    nxt = jnp.where(nxt < N_EXPERTS, nxt, -1)
    rows = valid * jnp.clip(counts[e_j] - (j - (cum - nb)[e_j]) * te, 0, te)
    return dest.astype(I32), e_j.astype(I32), rows.astype(I32), first, nxt.astype(I32), last_block.astype(I32)


def _dispatch_kernel(last_ref, dest_hbm, h_ref, xs_hbm, idx_smem, zbuf, sem_idx, sem_rows, sem_z,
                     *, row_sl):
    i = pl.program_id(0)
    nt = pl.num_programs(0)
    tm = h_ref.shape[0] // row_sl
    blk = zbuf.shape[0]
    slot = i % 2

    def idx_copy(t, s):
        n_idx = TOP_K * tm
        dst = idx_smem.at[pl.ds(pl.multiple_of(s * n_idx, n_idx), n_idx)]
        return pltpu.make_async_copy(dest_hbm.at[t], dst, sem_idx.at[s])

    @pl.when(i == 0)
    def _():
        zbuf[...] = jnp.zeros_like(zbuf)

        def zero_copy(e):
            row = pl.multiple_of(jnp.maximum(last_ref[e], 0), blk)
            return pltpu.make_async_copy(zbuf, xs_hbm.at[pl.ds(row, blk)], sem_z)

        for e in range(N_EXPERTS):
            @pl.when(last_ref[e] >= 0)
            def _():
                zero_copy(e).start()
        for e in range(N_EXPERTS):
            @pl.when(last_ref[e] >= 0)
            def _():
                zero_copy(e).wait()
        n_blocks = xs_hbm.shape[0] // blk
        total = last_ref[N_EXPERTS]

        def tail_copy(j):
            return pltpu.make_async_copy(zbuf, xs_hbm.at[pl.ds(j * blk, blk)], sem_z)

        for j in range(n_blocks - N_EXPERTS, n_blocks):
            @pl.when(j >= total)
            def _():
                tail_copy(j).start()
        for j in range(n_blocks - N_EXPERTS, n_blocks):
            @pl.when(j >= total)
            def _():
                tail_copy(j).wait()
        idx_copy(0, 0).start()

    @pl.when(i + 1 < nt)
    def _():
        idx_copy(i + 1, 1 - slot).start()

    idx_copy(i, slot).wait()

    def body(j, carry):
        n0 = j * DMA_UNROLL
        i0 = slot * (TOP_K * tm) + n0
        for u in range(DMA_UNROLL):
            src = h_ref.at[pl.ds(pl.multiple_of((n0 + u) * row_sl, row_sl), row_sl)]
            for k in range(TOP_K):
                dst = xs_hbm.at[pl.ds(pl.multiple_of(idx_smem[i0 + (k * tm + u)], row_sl), row_sl)]
                pltpu.make_async_copy(src, dst, sem_rows.at[k]).start(priority=k % 2)
        return carry
    lax.fori_loop(0, tm // DMA_UNROLL, body, 0)
    for k in range(TOP_K):
        pltpu.make_async_copy(h_ref, xs_hbm.at[pl.ds(0, tm * row_sl)], sem_rows.at[k]).wait()


def _dispatch(last_block, dest, h2t, n_rows, row_sl):
    tm = ROW_TILE
    nt = h2t.shape[0] // (tm * row_sl)
    grid_spec = pltpu.PrefetchScalarGridSpec(
        num_scalar_prefetch=1,
        grid=(nt,),
        in_specs=[
            pl.BlockSpec(memory_space=pl.ANY),
            pl.BlockSpec((tm * row_sl, LANES), lambda i, last: (i, 0)),
        ],
        out_specs=pl.BlockSpec(memory_space=pl.ANY),
        scratch_shapes=[
            pltpu.SMEM((2 * TOP_K * tm,), I32),
            pltpu.VMEM((EXPERT_TILE * row_sl, LANES), F32),
            pltpu.SemaphoreType.DMA((2,)),
            pltpu.SemaphoreType.DMA((TOP_K,)),
            pltpu.SemaphoreType.DMA,
        ],
    )
    return pl.pallas_call(
        functools.partial(_dispatch_kernel, row_sl=row_sl),
        grid_spec=grid_spec,
        out_shape=jax.ShapeDtypeStruct((n_rows * row_sl, LANES), F32),
        compiler_params=_params(sem=("arbitrary",)),
        name="dispatch",
    )(last_block, dest, h2t)


def _expert_kernel(be_ref, nv_ref, first_ref, next_ref, xs_ref, wgu_hbm, bgu_ref, wdn_hbm, bdn_ref, ys_ref,
                   stage_gu, stage_dn, wgu_bf, wdn_bf, act_scr, sem_w, *, layer):
    i = pl.program_id(0)
    d, d_gu = wgu_bf.shape
    d_e = wdn_bf.shape[0]
    nsl = d // LANES
    te = xs_ref.shape[0] // nsl

    def weight_copies(e):
        row = layer * N_EXPERTS + e
        return (pltpu.make_async_copy(wgu_hbm.at[row], stage_gu, sem_w.at[0]),
                pltpu.make_async_copy(wdn_hbm.at[row], stage_dn, sem_w.at[1]))

    @pl.when(i == 0)
    def _():
        for cp in weight_copies(be_ref[0]):
            cp.start(priority=1)

    @pl.when(first_ref[i] > 0)
    def _():
        for cp in weight_copies(be_ref[i]):
            cp.wait()
        rows = 128
        for r in range(0, d, rows):
            wgu_bf[r:r + rows, :] = stage_gu[r:r + rows, :].astype(BF16)
        for r in range(0, d_e, rows):
            wdn_bf[r:r + rows, :] = stage_dn[r:r + rows, :].astype(BF16)

        @pl.when(next_ref[i] >= 0)
        def _():
            for cp in weight_copies(next_ref[i]):
                cp.start(priority=1)

    def run_rows(m):
        x = jnp.concatenate([_load_row_slab(xs_ref, s, m, nsl).astype(BF16) for s in range(nsl)], axis=1)
        for c in range(d_e // EXPERT_CHUNK):
            lo = c * EXPERT_CHUNK
            hi = lo + EXPERT_CHUNK
            glu = jnp.dot(x, wgu_bf[:, lo:hi], preferred_element_type=F32) + bgu_ref[0, :, lo:hi]
            lin = (jnp.dot(x, wgu_bf[:, d_e + lo:d_e + hi], preferred_element_type=F32)
                   + bgu_ref[0, :, d_e + lo:d_e + hi])
            glu = jnp.minimum(glu, SWIGLU_LIMIT)
            lin = jnp.clip(lin, -SWIGLU_LIMIT, SWIGLU_LIMIT)
            act_scr[:m, lo:hi] = (glu * _sigmoid(SWIGLU_ALPHA * glu) * (lin + 1.0)).astype(BF16)
        y = jnp.dot(act_scr[:m, :], wdn_bf[...], preferred_element_type=F32) + bdn_ref[0]
        _store_rows(ys_ref, y)
        if m < te:
            ys_ref[m * nsl:, :] = jnp.zeros(((te - m) * nsl, LANES), F32)

    nv = nv_ref[i]
    sizes = [te // 4, te // 2, te]
    lower = 0
    for m in sizes:
        @pl.when((nv > lower) & (nv <= m))
        def _(m=m):
            run_rows(m)
        lower = m

    @pl.when(nv == 0)
    def _():
        ys_ref[...] = jnp.zeros_like(ys_ref)


def _experts(block_expert, block_rows, block_first, block_next, xs, w_gu, b_gu, w_dn, b_dn, layer):
    d = w_gu.shape[2]
    n_blocks = block_expert.shape[0]
    te = EXPERT_TILE * d // LANES
    d_gu = w_gu.shape[3]
    d_e = w_dn.shape[2]
    e_map3 = lambda i, be, bv, bf, bn: (layer * N_EXPERTS + be[i], 0, 0)
    w_gu = w_gu.reshape((-1,) + w_gu.shape[2:])
    w_dn = w_dn.reshape((-1,) + w_dn.shape[2:])
    b_gu = b_gu.reshape(-1, 1, d_gu)
    b_dn = b_dn.reshape(-1, 1, d)
    x_map = lambda i, be, nv, bf, bn: (jnp.where(nv[i] > 0, i, 0), 0)
    grid_spec = pltpu.PrefetchScalarGridSpec(
        num_scalar_prefetch=4,
        grid=(n_blocks,),
        in_specs=[
            pl.BlockSpec((te, LANES), x_map),
            pl.BlockSpec(memory_space=pl.ANY),
            pl.BlockSpec((1, 1, d_gu), e_map3),
            pl.BlockSpec(memory_space=pl.ANY),
            pl.BlockSpec((1, 1, d), e_map3),
        ],
        out_specs=pl.BlockSpec((te, LANES), lambda i, be, bv, bf, bn: (i, 0)),
        scratch_shapes=[
            pltpu.VMEM((d, d_gu), F32),
            pltpu.VMEM((d_e, d), F32),
            pltpu.VMEM((d, d_gu), BF16),
            pltpu.VMEM((d_e, d), BF16),
            pltpu.VMEM((EXPERT_TILE, d_e), BF16),
            pltpu.SemaphoreType.DMA((2,)),
        ],
    )
    return pl.pallas_call(
        functools.partial(_expert_kernel, layer=layer),
        grid_spec=grid_spec,
        out_shape=jax.ShapeDtypeStruct((n_blocks * te, LANES), F32),
        compiler_params=_params(sem=("arbitrary",)),
        name="experts",
    )(block_expert, block_rows, block_first, block_next, xs, w_gu, b_gu, w_dn, b_dn)


def _combine_kernel(dest_hbm, x1_ref, g_ref, fw_ref, ys_hbm, o_ref,
                    idx_smem, ybuf, sem_idx, sem_rows, *, final_norm):
    i = pl.program_id(0)
    nt = pl.num_programs(0)
    tm, d = x1_ref.shape
    nsl = d // LANES
    slot = i % 2

    def idx_copy(t, s):
        n_idx = TOP_K * tm
        dst = idx_smem.at[pl.ds(pl.multiple_of(s * n_idx, n_idx), n_idx)]
        return pltpu.make_async_copy(dest_hbm.at[t], dst, sem_idx.at[s])

    def issue_rows(s):
        def body(j, carry):
            n0 = j * DMA_UNROLL
            i0 = s * (TOP_K * tm) + n0
            for u in range(DMA_UNROLL):
                for k in range(TOP_K):
                    src = ys_hbm.at[pl.ds(pl.multiple_of(idx_smem[i0 + (k * tm + u)], nsl), nsl)]
                    dst = ybuf.at[s, k, pl.ds(pl.multiple_of((n0 + u) * nsl, nsl), nsl)]
                    pltpu.make_async_copy(src, dst, sem_rows.at[s, k]).start(priority=k % 2)
            return carry
        lax.fori_loop(0, tm // DMA_UNROLL, body, 0)

    def wait_rows(s):
        for k in range(TOP_K):
            pltpu.make_async_copy(ys_hbm.at[pl.ds(0, tm * nsl)], ybuf.at[s, k], sem_rows.at[s, k]).wait()

    @pl.when(i == 0)
    def _():
        cp = idx_copy(0, 0)
        cp.start()
        cp.wait()
        issue_rows(0)

        @pl.when(nt > 1)
        def _():
            idx_copy(1, 1).start()

    @pl.when(i + 1 < nt)
    def _():
        idx_copy(i + 1, 1 - slot).wait()
        issue_rows(1 - slot)

        @pl.when(i + 2 < nt)
        def _():
            idx_copy(i + 2, slot).start()

    wait_rows(slot)
    for r0 in range(0, tm, COMBINE_ROWS):
        g = g_ref[r0:r0 + COMBINE_ROWS, :]
        gk = [jnp.broadcast_to(g[:, k:k + 1], (COMBINE_ROWS, LANES)) for k in range(TOP_K)]
        slabs = []
        for s in range(nsl):
            acc = x1_ref[r0:r0 + COMBINE_ROWS, s * LANES:(s + 1) * LANES]
            for k in range(TOP_K):
                acc = acc + gk[k] * _load_row_slab(ybuf.at[slot, k], s, COMBINE_ROWS, nsl, r0)
            slabs.append(acc)
        out = jnp.concatenate(slabs, axis=1)
        if final_norm:
            out = _rms(out, fw_ref[...])
        o_ref[r0:r0 + COMBINE_ROWS, :] = out


def _combine(x1, ys, gate, dest, final_w, final_norm):
    n, d = x1.shape
    tm = ROW_TILE
    nt = n // tm
    tok_tile = lambda i: (i, 0)
    return pl.pallas_call(
        functools.partial(_combine_kernel, final_norm=final_norm),
        grid=(nt,),
        in_specs=[
            pl.BlockSpec(memory_space=pl.ANY),
            pl.BlockSpec((tm, d), tok_tile),
            pl.BlockSpec((tm, LANES), tok_tile),
            pl.BlockSpec((1, d), lambda i: (0, 0)),
            pl.BlockSpec(memory_space=pl.ANY),
        ],
        out_specs=pl.BlockSpec((tm, d), tok_tile),
        out_shape=jax.ShapeDtypeStruct((n, d), F32),
        scratch_shapes=[
            pltpu.SMEM((2 * TOP_K * tm,), I32),
            pltpu.VMEM((2, TOP_K, tm * d // LANES, LANES), F32),
            pltpu.SemaphoreType.DMA((2,)),
            pltpu.SemaphoreType.DMA((2, TOP_K)),
        ],
        compiler_params=_params(sem=("arbitrary",)),
        name="combine",
    )(dest, x1, gate, final_w.reshape(1, d), ys)


def kernel(x, attn_norm_w, w_in, sgu_ln_w, sgu_ln_b, sgu_w, sgu_b, ssm_a_re, ssm_a_im, ssm_b_re, ssm_b_im, ssm_c_re, ssm_c_im, ssm_d, ssm_log_dt, ssm_glu_w, ssm_glu_b, out_norm_a, out_norm_b, w_out, ffn_norm_w, router_w, router_b, w_gate_up, b_gate_up, w_down, b_down, final_norm_w):
    b, l, d = x.shape
    n = b * l
    depth = w_in.shape[0]
    blocks_per_seq = l // SSM_T
    n_steps = max(1, (blocks_per_seq - 1).bit_length())
    n_rows = (n * TOP_K // EXPERT_TILE + N_EXPERTS) * EXPERT_TILE
    x2 = x.reshape(n, d).astype(F32)
    k2, so3, si2, apw = _s5_tables(ssm_a_re, ssm_a_im, ssm_b_re, ssm_b_im, ssm_c_re, ssm_c_im, ssm_d,
                                   ssm_log_dt, n_steps)
    for layer in range(depth):
        ya, us3 = _front(x2, attn_norm_w[layer], w_in[layer], sgu_ln_w[layer], sgu_ln_b[layer],
                         sgu_w[layer], sgu_b[layer], out_norm_a[layer])
        ys3 = _s5(us3, k2, so3, si2, apw, blocks_per_seq, layer)
        x1, h2, gate, er, cnt = _back(x2, ya, ys3, ssm_glu_w[layer], ssm_glu_b[layer],
                                      out_norm_b[layer], w_out[layer], ffn_norm_w[layer],
                                      router_w[layer], router_b[layer])
        counts = cnt[0, :N_EXPERTS].astype(I32)
        dest, block_expert, block_rows, block_first, block_next, last_block = _routing_tables(
            er, counts, n, d // LANES)
        xs = _dispatch(last_block, dest, h2, n_rows, d // LANES)
        ys = _experts(block_expert, block_rows, block_first, block_next, xs,
                      w_gate_up, b_gate_up, w_down, b_down, layer)
        x2 = _combine(x1, ys, gate, dest, final_norm_w, final_norm=(layer == depth - 1))
    return x2.reshape(b, l, d).astype(x.dtype)
```

```python
import functools
import math

import jax
import jax.numpy as jnp
from jax import lax
from jax.experimental import pallas as pl
from jax.experimental.pallas import tpu as pltpu

F32 = jnp.float32
BF16 = jnp.bfloat16
I32 = jnp.int32

EPS = 1e-5
N_HEADS = 4
CHUNK = 128
SSM_GROUP = 16
SSM_STATE = 64
SSM_T = 16
N_EXPERTS = 32
TOP_K = 4
SWIGLU_LIMIT = 7.0
SWIGLU_ALPHA = 1.702
LANES = 128
SUBLANES = 8
GROUPS_PER_LANE_BLOCK = LANES // SSM_GROUP

ROW_TILE = 512
EXPERT_TILE = 512
EXPERT_CHUNK = 256
DMA_UNROLL = 8
COMBINE_ROWS = 64
VMEM_LIMIT = 56 * 1024 * 1024


def _gelu(x):
    return 0.5 * x * (1.0 + jnp.tanh(math.sqrt(2.0 / math.pi) * (x + 0.044715 * (x * x * x))))


def _sigmoid(x):
    return 1.0 / (1.0 + jnp.exp(-x))


def _rms(x, w):
    return x * lax.rsqrt(jnp.mean(x * x, axis=-1, keepdims=True) + EPS) * w


def _store_rows(ref, val, row0=0):
    rows, d = val.shape
    nsl = d // LANES
    for s in range(nsl):
        ref[pl.ds(row0 * nsl + s, rows, stride=nsl), :] = val[:, s * LANES:(s + 1) * LANES]


def _load_row_slab(ref, s, rows, nsl, row0=0):
    return ref[pl.ds(row0 * nsl + s, rows, stride=nsl), :]


def _params(**kw):
    return pltpu.CompilerParams(dimension_semantics=kw.pop("sem"), vmem_limit_bytes=VMEM_LIMIT, **kw)


def _front_kernel(x_ref, nw_ref, win_ref, lnw_ref, lnb_ref, ws_ref, bst_ref, ona_ref,
                  ya_ref, us_ref, mixed_ref, us_scr):
    d_g = ya_ref.shape[1]
    hd = d_g // N_HEADS
    tm = x_ref.shape[0]
    x = x_ref[...]
    h = _rms(x, nw_ref[...]).astype(BF16)
    proj = jnp.dot(h, win_ref[...], preferred_element_type=F32)
    u = _gelu(proj[:, :d_g])
    v = _gelu(proj[:, d_g:2 * d_g])
    n_lb = us_scr.shape[0]
    for q in range(n_lb):
        us_scr[q] = proj[:, 2 * d_g + q * LANES:2 * d_g + (q + 1) * LANES]
    mu = jnp.mean(v, axis=-1, keepdims=True)
    vc = v - mu
    var = jnp.mean(vc * vc, axis=-1, keepdims=True)
    vb = (vc * lax.rsqrt(var + EPS) * lnw_ref[...] + lnb_ref[...]).astype(BF16)
    row = lax.broadcasted_iota(I32, (CHUNK, CHUNK), 0)
    col = lax.broadcasted_iota(I32, (CHUNK, CHUNK), 1)
    causal = row >= col
    for hh in range(N_HEADS):
        w = jnp.where(causal, ws_ref[hh], 0.0).astype(BF16)
        bias = bst_ref[:, hh:hh + 1]
        for c in range(tm // CHUNK):
            vv = vb[c * CHUNK:(c + 1) * CHUNK, hh * hd:(hh + 1) * hd]
            m = jnp.dot(w, vv, preferred_element_type=F32) + bias
            mixed_ref[c * CHUNK:(c + 1) * CHUNK, hh * hd:(hh + 1) * hd] = m
    ya = u * mixed_ref[...]
    ya_ref[...] = _rms(ya, ona_ref[...]).astype(BF16)
    for s in range(SSM_T):
        for q in range(n_lb):
            us_ref[s, :, q * LANES:(q + 1) * LANES] = (
                us_scr[q, pl.ds(s, tm // SSM_T, stride=SSM_T), :].astype(BF16))


def _front(x2, nw, w_in, ln_w, ln_b, w_s, b_s, on_a):
    n, d = x2.shape
    d_g = ln_w.shape[0]
    d_s = w_in.shape[1] - 2 * d_g
    tm = ROW_TILE
    const2 = lambda i: (0, 0)
    return pl.pallas_call(
        _front_kernel,
        grid=(n // tm,),
        in_specs=[
            pl.BlockSpec((tm, d), lambda i: (i, 0)),
            pl.BlockSpec((1, d), const2),
            pl.BlockSpec(w_in.shape, const2),
            pl.BlockSpec((1, d_g), const2),
            pl.BlockSpec((1, d_g), const2),
            pl.BlockSpec(w_s.shape, lambda i: (0, 0, 0)),
            pl.BlockSpec((CHUNK, N_HEADS), const2),
            pl.BlockSpec((1, d_g), const2),
        ],
        out_specs=[
            pl.BlockSpec((tm, d_g), lambda i: (i, 0)),
            pl.BlockSpec((SSM_T, tm // SSM_T, d_s), lambda i: (0, i, 0)),
        ],
        out_shape=[
            jax.ShapeDtypeStruct((n, d_g), BF16),
            jax.ShapeDtypeStruct((SSM_T, n // SSM_T, d_s), BF16),
        ],
        scratch_shapes=[pltpu.VMEM((tm, d_g), F32), pltpu.VMEM((d_s // LANES, tm, LANES), F32)],
        compiler_params=_params(sem=("arbitrary",)),
        name="front",
    )(x2, nw.reshape(1, d), w_in.astype(BF16), ln_w.reshape(1, d_g), ln_b.reshape(1, d_g),
      w_s, b_s.T, on_a.reshape(1, d_g))


def _s5_tables(a_re, a_im, b_re, b_im, c_re, c_im, d, log_dt, n_steps):
    depth, g, p = a_re.shape
    hch = b_re.shape[-1]
    t = SSM_T
    gl = GROUPS_PER_LANE_BLOCK
    r = depth * (g // gl)
    a = lax.complex(a_re.astype(F32), a_im.astype(F32)).reshape(r, gl, p)
    dt = jnp.exp(log_dt.astype(F32)).reshape(r, gl, 1)
    dta = dt * a
    a_bar = jnp.exp(dta)
    b = lax.complex(b_re.astype(F32), b_im.astype(F32)).reshape(r, gl, p, hch)
    b_bar = ((a_bar - 1.0) / a)[..., None] * b
    c = lax.complex(c_re.astype(F32), c_im.astype(F32)).reshape(r, gl, hch, p)
    lags = jnp.arange(t + 1, dtype=F32)
    pw = jnp.exp(lags[None, :, None, None] * dta.transpose(0, 2, 1)[:, None])
    pwx = jnp.repeat(pw, hch, axis=-1)
    bb = b_bar.transpose(0, 2, 1, 3).reshape(r, p, gl * hch)
    cc = c.transpose(0, 3, 1, 2).reshape(r, p, gl * hch)
    q = pwx[:, :t] * bb[:, None]
    si = pwx[:, 1:] * cc[:, None]
    k2 = (jnp.einsum('rlpx,rpy->rlxy', q.real, cc.real, precision=lax.Precision.HIGHEST)
          - jnp.einsum('rlpx,rpy->rlxy', q.imag, cc.imag, precision=lax.Precision.HIGHEST))
    lane_g = jnp.arange(gl * hch) // hch
    k2 = jnp.where(lane_g[:, None] == lane_g[None, :], k2, 0.0)
    skip = jnp.eye(gl * hch, dtype=F32) * d.astype(F32).reshape(r, 1, gl * hch)
    k2 = k2.at[:, 0].add(skip)
    so3 = jnp.swapaxes(jnp.concatenate([q.real, q.imag], axis=2), 2, 3)
    si2 = jnp.concatenate([si.real, -si.imag], axis=2)
    steps = SSM_T * (2.0 ** jnp.arange(n_steps, dtype=F32))
    ap = jnp.exp(steps[None, :, None, None] * dta[:, None]).reshape(r, n_steps, gl * p)
    mul_same = jnp.concatenate([ap.real, ap.real], axis=-1)
    mul_swap = jnp.concatenate([-ap.imag, ap.imag], axis=-1)
    apw = jnp.stack([mul_same, mul_swap], axis=2)
    return k2.astype(BF16), so3.astype(BF16), si2.astype(BF16), apw


def _s5_kernel(u_ref, k2_ref, so3_ref, si2_ref, ap_ref, y_ref, wk_scr, wso_scr, wsi_scr):
    t_blk = u_ref.shape[0]
    rows = u_ref.shape[1]
    n_state = wso_scr.shape[1]
    two_p = so3_ref.shape[3]
    p = two_p // 2
    half = n_state // 2

    @pl.when(pl.program_id(1) == 0)
    def _():
        for s in range(t_blk):
            if s > 0:
                wk_scr[s * LANES:(s + 1) * LANES, :s * LANES] = jnp.zeros((LANES, s * LANES), BF16)
            for t in range(s, t_blk):
                wk_scr[s * LANES:(s + 1) * LANES, t * LANES:(t + 1) * LANES] = k2_ref[0, t - s]
        r_e = lax.broadcasted_iota(I32, (two_p, n_state), 0)
        c_e = lax.broadcasted_iota(I32, (two_p, n_state), 1)
        spread = jnp.where((r_e // p == c_e // half) & (r_e % p == c_e % p), 1.0, 0.0).astype(BF16)
        r_g = lax.broadcasted_iota(I32, (LANES, n_state), 0) // SSM_GROUP
        c_g = (lax.broadcasted_iota(I32, (LANES, n_state), 1) % half) // p
        own = r_g == c_g
        for s in range(t_blk):
            full = jnp.dot(so3_ref[0, t_blk - 1 - s], spread, preferred_element_type=F32)
            wso_scr[s * LANES:(s + 1) * LANES, :] = jnp.where(own, full, 0.0).astype(BF16)
        r_e = lax.broadcasted_iota(I32, (n_state, two_p), 0)
        c_e = lax.broadcasted_iota(I32, (n_state, two_p), 1)
        gather = jnp.where((c_e // p == r_e // half) & (c_e % p == r_e % p), 1.0, 0.0).astype(BF16)
        r_g = (lax.broadcasted_iota(I32, (n_state, LANES), 0) % half) // p
        c_g = lax.broadcasted_iota(I32, (n_state, LANES), 1) // SSM_GROUP
        own = r_g == c_g
        for t in range(t_blk):
            full = jnp.dot(gather, si2_ref[0, t], preferred_element_type=F32)
            wsi_scr[:, t * LANES:(t + 1) * LANES] = jnp.where(own, full, 0.0).astype(BF16)

    xcat = jnp.concatenate([u_ref[s] for s in range(t_blk)], axis=1)
    x = jnp.dot(xcat, wso_scr[...], preferred_element_type=F32)
    pos = lax.broadcasted_iota(I32, (rows, n_state), 0)
    for k in range(ap_ref.shape[1]):
        sh = 1 << k
        prev = pltpu.roll(x, sh, 0)
        prev_sw = pltpu.roll(prev, half, 1)
        upd = prev * ap_ref[0, k, 0:1, :] + prev_sw * ap_ref[0, k, 1:2, :]
        x = x + jnp.where(pos >= sh, upd, 0.0)
    xin = jnp.where(pos >= 1, pltpu.roll(x, 1, 0), 0.0).astype(BF16)
    for j in range(t_blk // 2):
        k_hi = (2 * j + 2) * LANES
        lo = 2 * j * LANES
        y = jnp.dot(xcat[:, :k_hi], wk_scr[:k_hi, lo:lo + 2 * LANES], preferred_element_type=F32)
        y = y + jnp.dot(xin, wsi_scr[:, lo:lo + 2 * LANES], preferred_element_type=F32)
        y_ref[2 * j] = y[:, :LANES].astype(BF16)
        y_ref[2 * j + 1] = y[:, LANES:].astype(BF16)


def _s5(us3, k2, so3, si2, apw, blocks_per_seq, layer):
    t_blk, n_blocks, d_s = us3.shape
    lb = d_s // LANES
    n_seq = n_blocks // blocks_per_seq
    n_in = t_blk * LANES
    n_state = apw.shape[3]
    w_map = lambda a, b: (layer * lb + a, 0, 0, 0)
    io_spec = pl.BlockSpec((t_blk, blocks_per_seq, LANES), lambda a, b: (0, b, a))
    return pl.pallas_call(
        _s5_kernel,
        grid=(lb, n_seq),
        in_specs=[
            io_spec,
            pl.BlockSpec((1,) + k2.shape[1:], w_map),
            pl.BlockSpec((1,) + so3.shape[1:], w_map),
            pl.BlockSpec((1,) + si2.shape[1:], w_map),
            pl.BlockSpec((1,) + apw.shape[1:], w_map),
        ],
        out_specs=io_spec,
        out_shape=jax.ShapeDtypeStruct(us3.shape, BF16),
        scratch_shapes=[
            pltpu.VMEM((n_in, n_in), BF16),
            pltpu.VMEM((n_in, n_state), BF16),
            pltpu.VMEM((n_state, n_in), BF16),
        ],
        compiler_params=_params(sem=("arbitrary", "arbitrary")),
        name="s5",
    )(us3, k2, so3, si2, apw)


def _back_kernel(x_ref, ya_ref, ys_ref, gw_ref, gb_ref, onb_ref, wo_ref, fnw_ref,
                 rw_ref, rb_ref,
                 x1_ref, h2_ref, gate_ref, er_ref, cnt_ref, carry_ref, ys_scr):
    i = pl.program_id(0)
    tm = x_ref.shape[0]
    d_g = ya_ref.shape[1]

    @pl.when(i == 0)
    def _():
        carry_ref[...] = jnp.zeros_like(carry_ref)

    n_lb = ys_scr.shape[0]
    for t in range(SSM_T):
        for q in range(n_lb):
            ys_scr[q, pl.ds(t, tm // SSM_T, stride=SSM_T), :] = (
                ys_ref[t, :, q * LANES:(q + 1) * LANES].astype(F32))
    y = _gelu(jnp.concatenate([ys_scr[q] for q in range(n_lb)], axis=1))
    z = jnp.dot(y.astype(BF16), gw_ref[...], preferred_element_type=F32) + gb_ref[...]
    yb = y * _sigmoid(z)
    ybn = _rms(yb, onb_ref[...]).astype(BF16)
    x1 = (x_ref[...]
          + jnp.dot(ya_ref[...], wo_ref[:d_g, :], preferred_element_type=F32)
          + jnp.dot(ybn, wo_ref[d_g:, :], preferred_element_type=F32))
    x1_ref[...] = x1
    h2 = _rms(x1, fnw_ref[...])
    _store_rows(h2_ref, h2)

    hh = h2.astype(BF16)
    hl = (h2 - hh.astype(F32)).astype(BF16)
    p_hi = jnp.dot(hh, rw_ref[...], preferred_element_type=F32)
    p_lo = jnp.dot(hl, rw_ref[:, :LANES], preferred_element_type=F32)
    logits = p_hi[:, :LANES] + p_lo + p_hi[:, LANES:] + rb_ref[...]
    lane = lax.broadcasted_iota(I32, (tm, LANES), 1)
    lane_f = lane.astype(F32)
    neg = jnp.float32(-jnp.inf)
    work = jnp.where(lane < N_EXPERTS, logits, neg)
    vals, hots, idxs = [], [], []
    for k in range(TOP_K):
        m = jnp.max(work, axis=-1, keepdims=True)
        idx = jnp.min(jnp.where(work == m, lane_f, float(LANES)), axis=-1, keepdims=True)
        hot = lane_f == idx
        vals.append(m)
        hots.append(hot)
        idxs.append(idx.astype(I32))
        work = jnp.where(hot, neg, work)
    exps = [jnp.exp(v - vals[0]) for v in vals]
    denom = exps[0] + exps[1] + exps[2] + exps[3]
    gate = jnp.zeros((tm, LANES), F32)
    for k in range(TOP_K):
        gate = jnp.where(lane == k, exps[k] / denom, gate)
    gate_ref[...] = gate

    sel = (hots[0] | hots[1] | hots[2] | hots[3])
    sel_f = jnp.where(sel, 1.0, 0.0)
    r_i = lax.broadcasted_iota(I32, (tm, tm), 0)
    c_i = lax.broadcasted_iota(I32, (tm, tm), 1)
    lower = jnp.where(r_i > c_i, 1.0, 0.0).astype(BF16)
    cum = jnp.dot(lower, sel_f.astype(BF16), preferred_element_type=F32) + carry_ref[...]
    er = jnp.zeros((tm, LANES), I32)
    for k in range(TOP_K):
        rk = jnp.sum(jnp.where(hots[k], cum, 0.0), axis=-1, keepdims=True)
        er = jnp.where(lane == k, idxs[k], er)
        er = jnp.where(lane == TOP_K + k, rk.astype(I32), er)
    carry_ref[...] = carry_ref[...] + jnp.sum(sel_f, axis=0, keepdims=True)
    cnt_ref[...] = carry_ref[...]
    er_ref[0] = jnp.transpose(er)[:SUBLANES, :]


def _back(x2, ya, ys3, glu_w, glu_b, on_b, w_out, fn_w, router_w, router_b):
    n, d = x2.shape
    d_g = ya.shape[1]
    d_s = ys3.shape[2]
    tm = ROW_TILE
    nt = n // tm
    const2 = lambda i: (0, 0)
    rw = jnp.zeros((d, LANES), F32).at[:, :N_EXPERTS].set(router_w.astype(F32))
    rw_hi = rw.astype(BF16)
    rw_lo = (rw - rw_hi.astype(F32)).astype(BF16)
    rw_split = jnp.concatenate([rw_hi, rw_lo], axis=1)
    rb = jnp.zeros((1, LANES), F32).at[0, :N_EXPERTS].set(router_b.astype(F32))
    tok_tile = lambda i: (i, 0)
    return pl.pallas_call(
        _back_kernel,
        grid=(nt,),
        in_specs=[
            pl.BlockSpec((tm, d), tok_tile),
            pl.BlockSpec((tm, d_g), tok_tile),
            pl.BlockSpec((SSM_T, tm // SSM_T, d_s), lambda i: (0, i, 0)),
            pl.BlockSpec((d_s, d_s), const2),
            pl.BlockSpec((1, d_s), const2),
            pl.BlockSpec((1, d_s), const2),
            pl.BlockSpec((d_g + d_s, d), const2),
            pl.BlockSpec((1, d), const2),
            pl.BlockSpec((d, 2 * LANES), const2),
            pl.BlockSpec((1, LANES), const2),
        ],
        out_specs=[
            pl.BlockSpec((tm, d), tok_tile),
            pl.BlockSpec((tm * d // LANES, LANES), tok_tile),
            pl.BlockSpec((tm, LANES), tok_tile),
            pl.BlockSpec((1, SUBLANES, tm), lambda i: (i, 0, 0)),
            pl.BlockSpec((1, LANES), const2),
        ],
        out_shape=[
            jax.ShapeDtypeStruct((n, d), F32),
            jax.ShapeDtypeStruct((n * d // LANES, LANES), F32),
            jax.ShapeDtypeStruct((n, LANES), F32),
            jax.ShapeDtypeStruct((nt, SUBLANES, tm), I32),
            jax.ShapeDtypeStruct((1, LANES), F32),
        ],
        scratch_shapes=[pltpu.VMEM((1, LANES), F32), pltpu.VMEM((d_s // LANES, tm, LANES), F32)],
        compiler_params=_params(sem=("arbitrary",)),
        name="back",
    )(x2, ya, ys3, glu_w.astype(BF16), glu_b.reshape(1, d_s), on_b.reshape(1, d_s),
      w_out.astype(BF16), fn_w.reshape(1, d), rw_split, rb)


def _routing_tables(er, counts, n, row_sl):
    te = EXPERT_TILE
    n_blocks = n * TOP_K // te + N_EXPERTS
    nb = (counts + te - 1) // te
    cum = jnp.cumsum(nb)
    pstart = (cum - nb) * te
    e_sel = er[:, :TOP_K, :, None] == jnp.arange(N_EXPERTS, dtype=I32)
    dest = jnp.sum(jnp.where(e_sel, pstart, 0), axis=-1) + er[:, TOP_K:2 * TOP_K, :]
    dest = (dest * row_sl).reshape(dest.shape[0], -1)
    total = cum[-1]
    j = jnp.arange(n_blocks, dtype=I32)
    e_j = jnp.sum((cum[None, :] <= jnp.minimum(j, total - 1)[:, None]).astype(I32), axis=1)
    valid = (j < total).astype(I32)
    last_block = jnp.where(nb > 0, (cum - 1) * te * row_sl, -1)
    last_block = jnp.concatenate([last_block, total[None]])
    e_ids = jnp.arange(N_EXPERTS, dtype=I32)
    own = e_ids[None, :] == e_j[:, None]
    start_j = jnp.sum(jnp.where(own, cum - nb, 0), axis=1)
    count_j = jnp.sum(jnp.where(own, counts, 0), axis=1)
    first = valid * (j == start_j).astype(I32)
    later = (e_ids[None, :] > e_j[:, None]) & (nb > 0)[None, :]
    nxt = jnp.min(jnp.where(later, e_ids[None, :], N_EXPERTS), axis=1)
    nxt = jnp.where(nxt < N_EXPERTS, nxt, -1)
    rows = valid * jnp.clip(count_j - (j - start_j) * te, 0, te)
    return dest.astype(I32), e_j.astype(I32), rows.astype(I32), first, nxt.astype(I32), last_block.astype(I32)


def _dispatch_kernel(last_ref, dest_hbm, h_ref, xs_hbm, idx_smem, zbuf, sem_idx, sem_rows, sem_z,
                     *, row_sl):
    i = pl.program_id(0)
    nt = pl.num_programs(0)
    tm = h_ref.shape[0] // row_sl
    blk = zbuf.shape[0]
    slot = i % 2

    def idx_copy(t, s):
        n_idx = TOP_K * tm
        dst = idx_smem.at[pl.ds(pl.multiple_of(s * n_idx, n_idx), n_idx)]
        return pltpu.make_async_copy(dest_hbm.at[t], dst, sem_idx.at[s])

    @pl.when(i == 0)
    def _():
        zbuf[...] = jnp.zeros_like(zbuf)

        def zero_copy(e):
            row = pl.multiple_of(jnp.maximum(last_ref[e], 0), blk)
            return pltpu.make_async_copy(zbuf, xs_hbm.at[pl.ds(row, blk)], sem_z)

        for e in range(N_EXPERTS):
            @pl.when(last_ref[e] >= 0)
            def _():
                zero_copy(e).start()
        for e in range(N_EXPERTS):
            @pl.when(last_ref[e] >= 0)
            def _():
                zero_copy(e).wait()
        n_blocks = xs_hbm.shape[0] // blk
        total = last_ref[N_EXPERTS]

        def tail_copy(j):
            return pltpu.make_async_copy(zbuf, xs_hbm.at[pl.ds(j * blk, blk)], sem_z)

        for j in range(n_blocks - N_EXPERTS, n_blocks):
            @pl.when(j >= total)
            def _():
                tail_copy(j).start()
        for j in range(n_blocks - N_EXPERTS, n_blocks):
            @pl.when(j >= total)
            def _():
                tail_copy(j).wait()
        idx_copy(0, 0).start()

    @pl.when(i + 1 < nt)
    def _():
        idx_copy(i + 1, 1 - slot).start()

    idx_copy(i, slot).wait()

    def body(j, carry):
        n0 = j * DMA_UNROLL
        i0 = slot * (TOP_K * tm) + n0
        for u in range(DMA_UNROLL):
            src = h_ref.at[pl.ds(pl.multiple_of((n0 + u) * row_sl, row_sl), row_sl)]
            for k in range(TOP_K):
                dst = xs_hbm.at[pl.ds(pl.multiple_of(idx_smem[i0 + (k * tm + u)], row_sl), row_sl)]
                pltpu.make_async_copy(src, dst, sem_rows.at[k]).start(priority=k % 2)
        return carry
    lax.fori_loop(0, tm // DMA_UNROLL, body, 0)
    for k in range(TOP_K):
        pltpu.make_async_copy(h_ref, xs_hbm.at[pl.ds(0, tm * row_sl)], sem_rows.at[k]).wait()


def _dispatch(last_block, dest, h2t, n_rows, row_sl):
    tm = ROW_TILE
    nt = h2t.shape[0] // (tm * row_sl)
    grid_spec = pltpu.PrefetchScalarGridSpec(
        num_scalar_prefetch=1,
        grid=(nt,),
        in_specs=[
            pl.BlockSpec(memory_space=pl.ANY),
            pl.BlockSpec((tm * row_sl, LANES), lambda i, last: (i, 0)),
        ],
        out_specs=pl.BlockSpec(memory_space=pl.ANY),
        scratch_shapes=[
            pltpu.SMEM((2 * TOP_K * tm,), I32),
            pltpu.VMEM((EXPERT_TILE * row_sl, LANES), F32),
            pltpu.SemaphoreType.DMA((2,)),
            pltpu.SemaphoreType.DMA((TOP_K,)),
            pltpu.SemaphoreType.DMA,
        ],
    )
    return pl.pallas_call(
        functools.partial(_dispatch_kernel, row_sl=row_sl),
        grid_spec=grid_spec,
        out_shape=jax.ShapeDtypeStruct((n_rows * row_sl, LANES), F32),
        compiler_params=_params(sem=("arbitrary",)),
        name="dispatch",
    )(last_block, dest, h2t)


def _expert_kernel(be_ref, nv_ref, first_ref, next_ref, xs_ref, wgu_hbm, bgu_ref, wdn_hbm, bdn_ref, ys_ref,
                   stage_gu, stage_dn, wgu_bf, wdn_bf, act_scr, sem_w, *, layer):
    i = pl.program_id(0)
    d, d_gu = wgu_bf.shape
    d_e = wdn_bf.shape[0]
    nsl = d // LANES
    te = xs_ref.shape[0] // nsl

    def weight_copies(e):
        row = layer * N_EXPERTS + e
        return (pltpu.make_async_copy(wgu_hbm.at[row], stage_gu, sem_w.at[0]),
                pltpu.make_async_copy(wdn_hbm.at[row], stage_dn, sem_w.at[1]))

    @pl.when(i == 0)
    def _():
        for cp in weight_copies(be_ref[0]):
            cp.start(priority=1)

    @pl.when(first_ref[i] > 0)
    def _():
        for cp in weight_copies(be_ref[i]):
            cp.wait()
        rows = 128
        for r in range(0, d, rows):
            wgu_bf[r:r + rows, :] = stage_gu[r:r + rows, :].astype(BF16)
        for r in range(0, d_e, rows):
            wdn_bf[r:r + rows, :] = stage_dn[r:r + rows, :].astype(BF16)

        @pl.when(next_ref[i] >= 0)
        def _():
            for cp in weight_copies(next_ref[i]):
                cp.start(priority=1)

    def run_rows(m):
        x = jnp.concatenate([_load_row_slab(xs_ref, s, m, nsl).astype(BF16) for s in range(nsl)], axis=1)
        for c in range(d_e // EXPERT_CHUNK):
            lo = c * EXPERT_CHUNK
            hi = lo + EXPERT_CHUNK
            glu = jnp.dot(x, wgu_bf[:, lo:hi], preferred_element_type=F32) + bgu_ref[0, :, lo:hi]
            lin = (jnp.dot(x, wgu_bf[:, d_e + lo:d_e + hi], preferred_element_type=F32)
                   + bgu_ref[0, :, d_e + lo:d_e + hi])
            glu = jnp.minimum(glu, SWIGLU_LIMIT)
            lin = jnp.clip(lin, -SWIGLU_LIMIT, SWIGLU_LIMIT)
            act_scr[:m, lo:hi] = (glu * _sigmoid(SWIGLU_ALPHA * glu) * (lin + 1.0)).astype(BF16)
        y = jnp.dot(act_scr[:m, :], wdn_bf[...], preferred_element_type=F32) + bdn_ref[0]
        _store_rows(ys_ref, y)
        if m < te:
            ys_ref[m * nsl:, :] = jnp.zeros(((te - m) * nsl, LANES), F32)

    nv = nv_ref[i]
    sizes = [te // 4, te // 2, te]
    lower = 0
    for m in sizes:
        @pl.when((nv > lower) & (nv <= m))
        def _(m=m):
            run_rows(m)
        lower = m

    @pl.when(nv == 0)
    def _():
        ys_ref[...] = jnp.zeros_like(ys_ref)


def _experts(block_expert, block_rows, block_first, block_next, xs, w_gu, b_gu, w_dn, b_dn, layer):
    d = w_gu.shape[2]
    n_blocks = block_expert.shape[0]
    te = EXPERT_TILE * d // LANES
    d_gu = w_gu.shape[3]
    d_e = w_dn.shape[2]
    e_map3 = lambda i, be, bv, bf, bn: (layer * N_EXPERTS + be[i], 0, 0)
    w_gu = w_gu.reshape((-1,) + w_gu.shape[2:])
    w_dn = w_dn.reshape((-1,) + w_dn.shape[2:])
    b_gu = b_gu.reshape(-1, 1, d_gu)
    b_dn = b_dn.reshape(-1, 1, d)
    x_map = lambda i, be, nv, bf, bn: (jnp.where(nv[i] > 0, i, 0), 0)
    grid_spec = pltpu.PrefetchScalarGridSpec(
        num_scalar_prefetch=4,
        grid=(n_blocks,),
        in_specs=[
            pl.BlockSpec((te, LANES), x_map),
            pl.BlockSpec(memory_space=pl.ANY),
            pl.BlockSpec((1, 1, d_gu), e_map3),
            pl.BlockSpec(memory_space=pl.ANY),
            pl.BlockSpec((1, 1, d), e_map3),
        ],
        out_specs=pl.BlockSpec((te, LANES), lambda i, be, bv, bf, bn: (i, 0)),
        scratch_shapes=[
            pltpu.VMEM((d, d_gu), F32),
            pltpu.VMEM((d_e, d), F32),
            pltpu.VMEM((d, d_gu), BF16),
            pltpu.VMEM((d_e, d), BF16),
            pltpu.VMEM((EXPERT_TILE, d_e), BF16),
            pltpu.SemaphoreType.DMA((2,)),
        ],
    )
    return pl.pallas_call(
        functools.partial(_expert_kernel, layer=layer),
        grid_spec=grid_spec,
        out_shape=jax.ShapeDtypeStruct((n_blocks * te, LANES), F32),
        compiler_params=_params(sem=("arbitrary",)),
        name="experts",
    )(block_expert, block_rows, block_first, block_next, xs, w_gu, b_gu, w_dn, b_dn)


def _combine_kernel(dest_hbm, x1_ref, g_ref, fw_ref, ys_hbm, o_ref,
                    idx_smem, ybuf, sem_idx, sem_rows, *, final_norm):
    i = pl.program_id(0)
    nt = pl.num_programs(0)
    tm, d = x1_ref.shape
    nsl = d // LANES
    slot = i % 2

    def idx_copy(t, s):
        n_idx = TOP_K * tm
        dst = idx_smem.at[pl.ds(pl.multiple_of(s * n_idx, n_idx), n_idx)]
        return pltpu.make_async_copy(dest_hbm.at[t], dst, sem_idx.at[s])

    def issue_rows(s):
        def body(j, carry):
            n0 = j * DMA_UNROLL
            i0 = s * (TOP_K * tm) + n0
            for u in range(DMA_UNROLL):
                for k in range(TOP_K):
                    src = ys_hbm.at[pl.ds(pl.multiple_of(idx_smem[i0 + (k * tm + u)], nsl), nsl)]
                    dst = ybuf.at[s, k, pl.ds(pl.multiple_of((n0 + u) * nsl, nsl), nsl)]
                    pltpu.make_async_copy(src, dst, sem_rows.at[s, k]).start(priority=k % 2)
            return carry
        lax.fori_loop(0, tm // DMA_UNROLL, body, 0)

    def wait_rows(s):
        for k in range(TOP_K):
            pltpu.make_async_copy(ys_hbm.at[pl.ds(0, tm * nsl)], ybuf.at[s, k], sem_rows.at[s, k]).wait()

    @pl.when(i == 0)
    def _():
        cp = idx_copy(0, 0)
        cp.start()
        cp.wait()
        issue_rows(0)

        @pl.when(nt > 1)
        def _():
            idx_copy(1, 1).start()

    @pl.when(i + 1 < nt)
    def _():
        idx_copy(i + 1, 1 - slot).wait()
        issue_rows(1 - slot)

        @pl.when(i + 2 < nt)
        def _():
            idx_copy(i + 2, slot).start()

    wait_rows(slot)
    for r0 in range(0, tm, COMBINE_ROWS):
        g = g_ref[r0:r0 + COMBINE_ROWS, :]
        gk = [jnp.broadcast_to(g[:, k:k + 1], (COMBINE_ROWS, LANES)) for k in range(TOP_K)]
        slabs = []
        for s in range(nsl):
            acc = x1_ref[r0:r0 + COMBINE_ROWS, s * LANES:(s + 1) * LANES]
            for k in range(TOP_K):
                acc = acc + gk[k] * _load_row_slab(ybuf.at[slot, k], s, COMBINE_ROWS, nsl, r0)
            slabs.append(acc)
        out = jnp.concatenate(slabs, axis=1)
        if final_norm:
            out = _rms(out, fw_ref[...])
        o_ref[r0:r0 + COMBINE_ROWS, :] = out


def _combine(x1, ys, gate, dest, final_w, final_norm):
    n, d = x1.shape
    tm = ROW_TILE
    nt = n // tm
    tok_tile = lambda i: (i, 0)
    return pl.pallas_call(
        functools.partial(_combine_kernel, final_norm=final_norm),
        grid=(nt,),
        in_specs=[
            pl.BlockSpec(memory_space=pl.ANY),
            pl.BlockSpec((tm, d), tok_tile),
            pl.BlockSpec((tm, LANES), tok_tile),
            pl.BlockSpec((1, d), lambda i: (0, 0)),
            pl.BlockSpec(memory_space=pl.ANY),
        ],
        out_specs=pl.BlockSpec((tm, d), tok_tile),
        out_shape=jax.ShapeDtypeStruct((n, d), F32),
        scratch_shapes=[
            pltpu.SMEM((2 * TOP_K * tm,), I32),
            pltpu.VMEM((2, TOP_K, tm * d // LANES, LANES), F32),
            pltpu.SemaphoreType.DMA((2,)),
            pltpu.SemaphoreType.DMA((2, TOP_K)),
        ],
        compiler_params=_params(sem=("arbitrary",)),
        name="combine",
    )(dest, x1, gate, final_w.reshape(1, d), ys)


def kernel(x, attn_norm_w, w_in, sgu_ln_w, sgu_ln_b, sgu_w, sgu_b, ssm_a_re, ssm_a_im, ssm_b_re, ssm_b_im, ssm_c_re, ssm_c_im, ssm_d, ssm_log_dt, ssm_glu_w, ssm_glu_b, out_norm_a, out_norm_b, w_out, ffn_norm_w, router_w, router_b, w_gate_up, b_gate_up, w_down, b_down, final_norm_w):
    b, l, d = x.shape
    n = b * l
    depth = w_in.shape[0]
    blocks_per_seq = l // SSM_T
    n_steps = max(1, (blocks_per_seq - 1).bit_length())
    n_rows = (n * TOP_K // EXPERT_TILE + N_EXPERTS) * EXPERT_TILE
    x2 = x.reshape(n, d).astype(F32)
    k2, so3, si2, apw = _s5_tables(ssm_a_re, ssm_a_im, ssm_b_re, ssm_b_im, ssm_c_re, ssm_c_im, ssm_d,
                                   ssm_log_dt, n_steps)
    for layer in range(depth):
        ya, us3 = _front(x2, attn_norm_w[layer], w_in[layer], sgu_ln_w[layer], sgu_ln_b[layer],
                         sgu_w[layer], sgu_b[layer], out_norm_a[layer])
        ys3 = _s5(us3, k2, so3, si2, apw, blocks_per_seq, layer)
        x1, h2, gate, er, cnt = _back(x2, ya, ys3, ssm_glu_w[layer], ssm_glu_b[layer],
                                      out_norm_b[layer], w_out[layer], ffn_norm_w[layer],
                                      router_w[layer], router_b[layer])
        counts = cnt[0, :N_EXPERTS].astype(I32)
        dest, block_expert, block_rows, block_first, block_next, last_block = _routing_tables(
            er, counts, n, d // LANES)
        xs = _dispatch(last_block, dest, h2, n_rows, d // LANES)
        ys = _experts(block_expert, block_rows, block_first, block_next, xs,
                      w_gate_up, b_gate_up, w_down, b_down, layer)
        x2 = _combine(x1, ys, gate, dest, final_norm_w, final_norm=(layer == depth - 1))
    return x2.reshape(b, l, d).astype(x.dtype)
```

```python
import functools
import math

import jax
import jax.numpy as jnp
from jax import lax
from jax.experimental import pallas as pl
from jax.experimental.pallas import tpu as pltpu

F32 = jnp.float32
BF16 = jnp.bfloat16
I32 = jnp.int32

EPS = 1e-5
N_HEADS = 4
CHUNK = 128
SSM_GROUP = 16
SSM_STATE = 64
SSM_T = 16
N_EXPERTS = 32
TOP_K = 4
SWIGLU_LIMIT = 7.0
SWIGLU_ALPHA = 1.702
LANES = 128
SUBLANES = 8
GROUPS_PER_LANE_BLOCK = LANES // SSM_GROUP

ROW_TILE = 512
EXPERT_TILE = 512
EXPERT_CHUNK = 256
DMA_UNROLL = 8
COMBINE_ROWS = 64
VMEM_LIMIT = 56 * 1024 * 1024


def _gelu(x):
    return 0.5 * x * (1.0 + jnp.tanh(math.sqrt(2.0 / math.pi) * (x + 0.044715 * (x * x * x))))


def _sigmoid(x):
    return 1.0 / (1.0 + jnp.exp(-x))


def _rms(x, w):
    return x * lax.rsqrt(jnp.mean(x * x, axis=-1, keepdims=True) + EPS) * w


def _store_rows(ref, val, row0=0):
    rows, d = val.shape
    nsl = d // LANES
    for s in range(nsl):
        ref[pl.ds(row0 * nsl + s, rows, stride=nsl), :] = val[:, s * LANES:(s + 1) * LANES]


def _load_row_slab(ref, s, rows, nsl, row0=0):
    return ref[pl.ds(row0 * nsl + s, rows, stride=nsl), :]


def _params(**kw):
    return pltpu.CompilerParams(dimension_semantics=kw.pop("sem"), vmem_limit_bytes=VMEM_LIMIT, **kw)


def _front_kernel(x_ref, nw_ref, win_ref, lnw_ref, lnb_ref, ws_ref, bst_ref, ona_ref,
                  ya_ref, us_ref, mixed_ref, us_scr):
    d_g = ya_ref.shape[1]
    hd = d_g // N_HEADS
    tm = x_ref.shape[0]
    x = x_ref[...]
    h = _rms(x, nw_ref[...]).astype(BF16)
    proj = jnp.dot(h, win_ref[...], preferred_element_type=F32)
    u = _gelu(proj[:, :d_g])
    v = _gelu(proj[:, d_g:2 * d_g])
    n_lb = us_scr.shape[0]
    for q in range(n_lb):
        us_scr[q] = proj[:, 2 * d_g + q * LANES:2 * d_g + (q + 1) * LANES]
    mu = jnp.mean(v, axis=-1, keepdims=True)
    vc = v - mu
    var = jnp.mean(vc * vc, axis=-1, keepdims=True)
    vb = (vc * lax.rsqrt(var + EPS) * lnw_ref[...] + lnb_ref[...]).astype(BF16)
    row = lax.broadcasted_iota(I32, (CHUNK, CHUNK), 0)
    col = lax.broadcasted_iota(I32, (CHUNK, CHUNK), 1)
    causal = row >= col
    for hh in range(N_HEADS):
        w = jnp.where(causal, ws_ref[hh], 0.0).astype(BF16)
        bias = bst_ref[:, hh:hh + 1]
        for c in range(tm // CHUNK):
            vv = vb[c * CHUNK:(c + 1) * CHUNK, hh * hd:(hh + 1) * hd]
            m = jnp.dot(w, vv, preferred_element_type=F32) + bias
            mixed_ref[c * CHUNK:(c + 1) * CHUNK, hh * hd:(hh + 1) * hd] = m
    ya = u * mixed_ref[...]
    ya_ref[...] = _rms(ya, ona_ref[...]).astype(BF16)
    for s in range(SSM_T):
        for q in range(n_lb):
            us_ref[s, :, q * LANES:(q + 1) * LANES] = (
                us_scr[q, pl.ds(s, tm // SSM_T, stride=SSM_T), :].astype(BF16))


def _front(x2, nw, w_in, ln_w, ln_b, w_s, b_s, on_a):
    n, d = x2.shape
    d_g = ln_w.shape[0]
    d_s = w_in.shape[1] - 2 * d_g
    tm = ROW_TILE
    const2 = lambda i: (0, 0)
    return pl.pallas_call(
        _front_kernel,
        grid=(n // tm,),
        in_specs=[
            pl.BlockSpec((tm, d), lambda i: (i, 0)),
            pl.BlockSpec((1, d), const2),
            pl.BlockSpec(w_in.shape, const2),
            pl.BlockSpec((1, d_g), const2),
            pl.BlockSpec((1, d_g), const2),
            pl.BlockSpec(w_s.shape, lambda i: (0, 0, 0)),
            pl.BlockSpec((CHUNK, N_HEADS), const2),
            pl.BlockSpec((1, d_g), const2),
        ],
        out_specs=[
            pl.BlockSpec((tm, d_g), lambda i: (i, 0)),
            pl.BlockSpec((SSM_T, tm // SSM_T, d_s), lambda i: (0, i, 0)),
        ],
        out_shape=[
            jax.ShapeDtypeStruct((n, d_g), BF16),
            jax.ShapeDtypeStruct((SSM_T, n // SSM_T, d_s), BF16),
        ],
        scratch_shapes=[pltpu.VMEM((tm, d_g), F32), pltpu.VMEM((d_s // LANES, tm, LANES), F32)],
        compiler_params=_params(sem=("arbitrary",)),
        name="front",
    )(x2, nw.reshape(1, d), w_in.astype(BF16), ln_w.reshape(1, d_g), ln_b.reshape(1, d_g),
      w_s, b_s.T, on_a.reshape(1, d_g))


def _s5_tables(a_re, a_im, b_re, b_im, c_re, c_im, d, log_dt, n_steps):
    depth, g, p = a_re.shape
    hch = b_re.shape[-1]
    t = SSM_T
    gl = GROUPS_PER_LANE_BLOCK
    r = depth * (g // gl)
    a = lax.complex(a_re.astype(F32), a_im.astype(F32)).reshape(r, gl, p)
    dt = jnp.exp(log_dt.astype(F32)).reshape(r, gl, 1)
    dta = dt * a
    a_bar = jnp.exp(dta)
    b = lax.complex(b_re.astype(F32), b_im.astype(F32)).reshape(r, gl, p, hch)
    b_bar = ((a_bar - 1.0) / a)[..., None] * b
    c = lax.complex(c_re.astype(F32), c_im.astype(F32)).reshape(r, gl, hch, p)
    lags = jnp.arange(t + 1, dtype=F32)
    pw = jnp.exp(lags[None, :, None, None] * dta.transpose(0, 2, 1)[:, None])
    pwx = jnp.repeat(pw, hch, axis=-1)
    bb = b_bar.transpose(0, 2, 1, 3).reshape(r, p, gl * hch)
    cc = c.transpose(0, 3, 1, 2).reshape(r, p, gl * hch)
    q = pwx[:, :t] * bb[:, None]
    si = pwx[:, 1:] * cc[:, None]
    k2 = (jnp.einsum('rlpx,rpy->rlxy', q.real, cc.real, precision=lax.Precision.HIGHEST)
          - jnp.einsum('rlpx,rpy->rlxy', q.imag, cc.imag, precision=lax.Precision.HIGHEST))
    lane_g = jnp.arange(gl * hch) // hch
    k2 = jnp.where(lane_g[:, None] == lane_g[None, :], k2, 0.0)
    skip = jnp.eye(gl * hch, dtype=F32) * d.astype(F32).reshape(r, 1, gl * hch)
    k2 = k2.at[:, 0].add(skip)
    so3 = jnp.swapaxes(jnp.concatenate([q.real, q.imag], axis=2), 2, 3)
    si2 = jnp.concatenate([si.real, -si.imag], axis=2)
    steps = SSM_T * (2.0 ** jnp.arange(n_steps, dtype=F32))
    ap = jnp.exp(steps[None, :, None, None] * dta[:, None]).reshape(r, n_steps, gl * p)
    mul_same = jnp.concatenate([ap.real, ap.real], axis=-1)
    mul_swap = jnp.concatenate([-ap.imag, ap.imag], axis=-1)
    apw = jnp.stack([mul_same, mul_swap], axis=2)
    return k2.astype(BF16), so3.astype(BF16), si2.astype(BF16), apw


def _s5_kernel(u_ref, k2_ref, so3_ref, si2_ref, ap_ref, y_ref, wk_scr, wso_scr, wsi_scr):
    t_blk = u_ref.shape[0]
    rows = u_ref.shape[1]
    n_state = wso_scr.shape[1]
    two_p = so3_ref.shape[3]
    p = two_p // 2
    half = n_state // 2

    @pl.when(pl.program_id(1) == 0)
    def _():
        for s in range(t_blk):
            if s > 0:
                wk_scr[s * LANES:(s + 1) * LANES, :s * LANES] = jnp.zeros((LANES, s * LANES), BF16)
            for t in range(s, t_blk):
                wk_scr[s * LANES:(s + 1) * LANES, t * LANES:(t + 1) * LANES] = k2_ref[0, t - s]
        r_e = lax.broadcasted_iota(I32, (two_p, n_state), 0)
        c_e = lax.broadcasted_iota(I32, (two_p, n_state), 1)
        spread = jnp.where((r_e // p == c_e // half) & (r_e % p == c_e % p), 1.0, 0.0).astype(BF16)
        r_g = lax.broadcasted_iota(I32, (LANES, n_state), 0) // SSM_GROUP
        c_g = (lax.broadcasted_iota(I32, (LANES, n_state), 1) % half) // p
        own = r_g == c_g
        for s in range(t_blk):
            full = jnp.dot(so3_ref[0, t_blk - 1 - s], spread, preferred_element_type=F32)
            wso_scr[s * LANES:(s + 1) * LANES, :] = jnp.where(own, full, 0.0).astype(BF16)
        r_e = lax.broadcasted_iota(I32, (n_state, two_p), 0)
        c_e = lax.broadcasted_iota(I32, (n_state, two_p), 1)
        gather = jnp.where((c_e // p == r_e // half) & (c_e % p == r_e % p), 1.0, 0.0).astype(BF16)
        r_g = (lax.broadcasted_iota(I32, (n_state, LANES), 0) % half) // p
        c_g = lax.broadcasted_iota(I32, (n_state, LANES), 1) // SSM_GROUP
        own = r_g == c_g
        for t in range(t_blk):
            full = jnp.dot(gather, si2_ref[0, t], preferred_element_type=F32)
            wsi_scr[:, t * LANES:(t + 1) * LANES] = jnp.where(own, full, 0.0).astype(BF16)

    xcat = jnp.concatenate([u_ref[s] for s in range(t_blk)], axis=1)
    x = jnp.dot(xcat, wso_scr[...], preferred_element_type=F32)
    pos = lax.broadcasted_iota(I32, (rows, n_state), 0)
    for k in range(ap_ref.shape[1]):
        sh = 1 << k
        prev = pltpu.roll(x, sh, 0)
        prev_sw = pltpu.roll(prev, half, 1)
        upd = prev * ap_ref[0, k, 0:1, :] + prev_sw * ap_ref[0, k, 1:2, :]
        x = x + jnp.where(pos >= sh, upd, 0.0)
    xin = jnp.where(pos >= 1, pltpu.roll(x, 1, 0), 0.0).astype(BF16)
    for j in range(t_blk // 2):
        k_hi = (2 * j + 2) * LANES
        lo = 2 * j * LANES
        y = jnp.dot(xcat[:, :k_hi], wk_scr[:k_hi, lo:lo + 2 * LANES], preferred_element_type=F32)
        y = y + jnp.dot(xin, wsi_scr[:, lo:lo + 2 * LANES], preferred_element_type=F32)
        y_ref[2 * j] = y[:, :LANES].astype(BF16)
        y_ref[2 * j + 1] = y[:, LANES:].astype(BF16)


def _s5(us3, k2, so3, si2, apw, blocks_per_seq, layer):
    t_blk, n_blocks, d_s = us3.shape
    lb = d_s // LANES
    n_seq = n_blocks // blocks_per_seq
    n_in = t_blk * LANES
    n_state = apw.shape[3]
    w_map = lambda a, b: (layer * lb + a, 0, 0, 0)
    io_spec = pl.BlockSpec((t_blk, blocks_per_seq, LANES), lambda a, b: (0, b, a))
    return pl.pallas_call(
        _s5_kernel,
        grid=(lb, n_seq),
        in_specs=[
            io_spec,
            pl.BlockSpec((1,) + k2.shape[1:], w_map),
            pl.BlockSpec((1,) + so3.shape[1:], w_map),
            pl.BlockSpec((1,) + si2.shape[1:], w_map),
            pl.BlockSpec((1,) + apw.shape[1:], w_map),
        ],
        out_specs=io_spec,
        out_shape=jax.ShapeDtypeStruct(us3.shape, BF16),
        scratch_shapes=[
            pltpu.VMEM((n_in, n_in), BF16),
            pltpu.VMEM((n_in, n_state), BF16),
            pltpu.VMEM((n_state, n_in), BF16),
        ],
        compiler_params=_params(sem=("arbitrary", "arbitrary")),
        name="s5",
    )(us3, k2, so3, si2, apw)


def _back_kernel(x_ref, ya_ref, ys_ref, gw_ref, gb_ref, onb_ref, wo_ref, fnw_ref,
                 rw_ref, rb_ref,
                 x1_ref, h2_ref, gate_ref, er_ref, cnt_ref, carry_ref, ys_scr):
    i = pl.program_id(0)
    tm = x_ref.shape[0]
    d_g = ya_ref.shape[1]

    @pl.when(i == 0)
    def _():
        carry_ref[...] = jnp.zeros_like(carry_ref)

    n_lb = ys_scr.shape[0]
    for t in range(SSM_T):
        for q in range(n_lb):
            ys_scr[q, pl.ds(t, tm // SSM_T, stride=SSM_T), :] = (
                ys_ref[t, :, q * LANES:(q + 1) * LANES].astype(F32))
    y = _gelu(jnp.concatenate([ys_scr[q] for q in range(n_lb)], axis=1))
    z = jnp.dot(y.astype(BF16), gw_ref[...], preferred_element_type=F32) + gb_ref[...]
    yb = y * _sigmoid(z)
    ybn = _rms(yb, onb_ref[...]).astype(BF16)
    x1 = (x_ref[...]
          + jnp.dot(ya_ref[...], wo_ref[:d_g, :], preferred_element_type=F32)
          + jnp.dot(ybn, wo_ref[d_g:, :], preferred_element_type=F32))
    x1_ref[...] = x1
    h2 = _rms(x1, fnw_ref[...])
    _store_rows(h2_ref, h2)

    hh = h2.astype(BF16)
    hl = (h2 - hh.astype(F32)).astype(BF16)
    p_hi = jnp.dot(hh, rw_ref[...], preferred_element_type=F32)
    p_lo = jnp.dot(hl, rw_ref[:, :LANES], preferred_element_type=F32)
    logits = p_hi[:, :LANES] + p_lo + p_hi[:, LANES:] + rb_ref[...]
    lane = lax.broadcasted_iota(I32, (tm, LANES), 1)
    lane_f = lane.astype(F32)
    neg = jnp.float32(-jnp.inf)
    work = jnp.where(lane < N_EXPERTS, logits, neg)
    vals, hots, idxs = [], [], []
    for k in range(TOP_K):
        m = jnp.max(work, axis=-1, keepdims=True)
        idx = jnp.min(jnp.where(work == m, lane_f, float(LANES)), axis=-1, keepdims=True)
        hot = lane_f == idx
        vals.append(m)
        hots.append(hot)
        idxs.append(idx.astype(I32))
        work = jnp.where(hot, neg, work)
    exps = [jnp.exp(v - vals[0]) for v in vals]
    denom = exps[0] + exps[1] + exps[2] + exps[3]
    gate = jnp.zeros((tm, LANES), F32)
    for k in range(TOP_K):
        gate = jnp.where(lane == k, exps[k] / denom, gate)
    gate_ref[...] = gate

    sel = (hots[0] | hots[1] | hots[2] | hots[3])
    sel_f = jnp.where(sel, 1.0, 0.0)
    r_i = lax.broadcasted_iota(I32, (tm, tm), 0)
    c_i = lax.broadcasted_iota(I32, (tm, tm), 1)
    lower = jnp.where(r_i > c_i, 1.0, 0.0).astype(BF16)
    cum = jnp.dot(lower, sel_f.astype(BF16), preferred_element_type=F32) + carry_ref[...]
    er = jnp.zeros((tm, LANES), I32)
    for k in range(TOP_K):
        rk = jnp.sum(jnp.where(hots[k], cum, 0.0), axis=-1, keepdims=True)
        er = jnp.where(lane == k, idxs[k], er)
        er = jnp.where(lane == TOP_K + k, rk.astype(I32), er)
    carry_ref[...] = carry_ref[...] + jnp.sum(sel_f, axis=0, keepdims=True)
    cnt_ref[...] = carry_ref[...]
    er_ref[0] = jnp.transpose(er)[:SUBLANES, :]


def _back(x2, ya, ys3, glu_w, glu_b, on_b, w_out, fn_w, router_w, router_b):
    n, d = x2.shape
    d_g = ya.shape[1]
    d_s = ys3.shape[2]
    tm = ROW_TILE
    nt = n // tm
    const2 = lambda i: (0, 0)
    rw = jnp.zeros((d, LANES), F32).at[:, :N_EXPERTS].set(router_w.astype(F32))
    rw_hi = rw.astype(BF16)
    rw_lo = (rw - rw_hi.astype(F32)).astype(BF16)
    rw_split = jnp.concatenate([rw_hi, rw_lo], axis=1)
    rb = jnp.zeros((1, LANES), F32).at[0, :N_EXPERTS].set(router_b.astype(F32))
    tok_tile = lambda i: (i, 0)
    return pl.pallas_call(
        _back_kernel,
        grid=(nt,),
        in_specs=[
            pl.BlockSpec((tm, d), tok_tile),
            pl.BlockSpec((tm, d_g), tok_tile),
            pl.BlockSpec((SSM_T, tm // SSM_T, d_s), lambda i: (0, i, 0)),
            pl.BlockSpec((d_s, d_s), const2),
            pl.BlockSpec((1, d_s), const2),
            pl.BlockSpec((1, d_s), const2),
            pl.BlockSpec((d_g + d_s, d), const2),
            pl.BlockSpec((1, d), const2),
            pl.BlockSpec((d, 2 * LANES), const2),
            pl.BlockSpec((1, LANES), const2),
        ],
        out_specs=[
            pl.BlockSpec((tm, d), tok_tile),
            pl.BlockSpec((tm * d // LANES, LANES), tok_tile),
            pl.BlockSpec((tm, LANES), tok_tile),
            pl.BlockSpec((1, SUBLANES, tm), lambda i: (i, 0, 0)),
            pl.BlockSpec((1, LANES), const2),
        ],
        out_shape=[
            jax.ShapeDtypeStruct((n, d), F32),
            jax.ShapeDtypeStruct((n * d // LANES, LANES), F32),
            jax.ShapeDtypeStruct((n, LANES), F32),
            jax.ShapeDtypeStruct((nt, SUBLANES, tm), I32),
            jax.ShapeDtypeStruct((1, LANES), F32),
        ],
        scratch_shapes=[pltpu.VMEM((1, LANES), F32), pltpu.VMEM((d_s // LANES, tm, LANES), F32)],
        compiler_params=_params(sem=("arbitrary",)),
        name="back",
    )(x2, ya, ys3, glu_w.astype(BF16), glu_b.reshape(1, d_s), on_b.reshape(1, d_s),
      w_out.astype(BF16), fn_w.reshape(1, d), rw_split, rb)


def _routing_tables(er, counts, n, row_sl):
    te = EXPERT_TILE
    n_blocks = n * TOP_K // te + N_EXPERTS
    start = jnp.cumsum(counts) - counts
    e_ids = jnp.arange(N_EXPERTS, dtype=I32)
    e_sel = er[:, :TOP_K, :, None] == e_ids
    dest = jnp.sum(jnp.where(e_sel, start, 0), axis=-1) + er[:, TOP_K:2 * TOP_K, :]
    dest = (dest * row_sl).reshape(dest.shape[0], -1)
    nb = (counts + te - 1) // te
    cum = jnp.cumsum(nb)
    total = cum[-1]
    j = jnp.arange(n_blocks, dtype=I32)
    e_j = jnp.sum((cum[None, :] <= jnp.minimum(j, total - 1)[:, None]).astype(I32), axis=1)
    valid = (j < total).astype(I32)
    own = e_ids[None, :] == e_j[:, None]
    first_j = jnp.sum(jnp.where(own, cum - nb, 0), axis=1)
    count_j = jnp.sum(jnp.where(own, counts, 0), axis=1)
    start_j = jnp.sum(jnp.where(own, start, 0), axis=1)
    first = valid * (j == first_j).astype(I32)
    later = (e_ids[None, :] > e_j[:, None]) & (nb > 0)[None, :]
    nxt = jnp.min(jnp.where(later, e_ids[None, :], N_EXPERTS), axis=1)
    nxt = jnp.where(nxt < N_EXPERTS, nxt, -1)
    rows = valid * jnp.clip(count_j - (j - first_j) * te, 0, te)
    row_off = (start_j + (j - first_j) * te) * row_sl
    return (dest.astype(I32), e_j.astype(I32), rows.astype(I32), first, nxt.astype(I32),
            row_off.astype(I32), total.reshape(1).astype(I32))


def _dispatch_kernel(dest_hbm, h_ref, xs_hbm, idx_smem, zbuf, sem_idx, sem_rows, sem_z, *, row_sl):
    i = pl.program_id(0)
    nt = pl.num_programs(0)
    tm = h_ref.shape[0] // row_sl
    blk = zbuf.shape[0]
    slot = i % 2

    def idx_copy(t, s):
        n_idx = TOP_K * tm
        dst = idx_smem.at[pl.ds(pl.multiple_of(s * n_idx, n_idx), n_idx)]
        return pltpu.make_async_copy(dest_hbm.at[t], dst, sem_idx.at[s])

    @pl.when(i == 0)
    def _():
        zbuf[...] = jnp.zeros_like(zbuf)
        cp = pltpu.make_async_copy(zbuf, xs_hbm.at[pl.ds(xs_hbm.shape[0] - blk, blk)], sem_z)
        cp.start()
        cp.wait()
        idx_copy(0, 0).start()

    @pl.when(i + 1 < nt)
    def _():
        idx_copy(i + 1, 1 - slot).start()

    idx_copy(i, slot).wait()

    def body(j, carry):
        n0 = j * DMA_UNROLL
        i0 = slot * (TOP_K * tm) + n0
        for u in range(DMA_UNROLL):
            src = h_ref.at[pl.ds(pl.multiple_of((n0 + u) * row_sl, row_sl), row_sl)]
            for k in range(TOP_K):
                dst = xs_hbm.at[pl.ds(pl.multiple_of(idx_smem[i0 + (k * tm + u)], row_sl), row_sl)]
                pltpu.make_async_copy(src, dst, sem_rows.at[k]).start(priority=k % 2)
        return carry
    lax.fori_loop(0, tm // DMA_UNROLL, body, 0)
    for k in range(TOP_K):
        pltpu.make_async_copy(h_ref, xs_hbm.at[pl.ds(0, tm * row_sl)], sem_rows.at[k]).wait()


def _dispatch(dest, h2t, n_rows, row_sl):
    tm = ROW_TILE
    nt = h2t.shape[0] // (tm * row_sl)
    return pl.pallas_call(
        functools.partial(_dispatch_kernel, row_sl=row_sl),
        grid=(nt,),
        in_specs=[
            pl.BlockSpec(memory_space=pl.ANY),
            pl.BlockSpec((tm * row_sl, LANES), lambda i: (i, 0)),
        ],
        out_specs=pl.BlockSpec(memory_space=pl.ANY),
        out_shape=jax.ShapeDtypeStruct((n_rows * row_sl, LANES), F32),
        scratch_shapes=[
            pltpu.SMEM((2 * TOP_K * tm,), I32),
            pltpu.VMEM((EXPERT_TILE * row_sl, LANES), F32),
            pltpu.SemaphoreType.DMA((2,)),
            pltpu.SemaphoreType.DMA((TOP_K,)),
            pltpu.SemaphoreType.DMA,
        ],
        compiler_params=_params(sem=("arbitrary",)),
        name="dispatch",
    )(dest, h2t)


def _expert_kernel(be_ref, nv_ref, first_ref, next_ref, off_ref, tot_ref,
                   xs_hbm, wgu_hbm, bgu_ref, wdn_hbm, bdn_ref, ys_hbm,
                   xbuf, ybuf, stage_gu, stage_dn, wgu_bf, wdn_bf, act_scr, sem_x, sem_y, sem_w, *, layer):
    i = pl.program_id(0)
    d, d_gu = wgu_bf.shape
    d_e = wdn_bf.shape[0]
    nsl = d // LANES
    blk = xbuf.shape[1]
    te = blk // nsl
    slot = i % 2
    total = tot_ref[0]

    def weight_copies(e):
        row = layer * N_EXPERTS + e
        return (pltpu.make_async_copy(wgu_hbm.at[row], stage_gu, sem_w.at[0]),
                pltpu.make_async_copy(wdn_hbm.at[row], stage_dn, sem_w.at[1]))

    def x_copy(j, s):
        src = xs_hbm.at[pl.ds(pl.multiple_of(off_ref[j], nsl), blk)]
        return pltpu.make_async_copy(src, xbuf.at[s], sem_x.at[s])

    def y_copy(j, s):
        dst = ys_hbm.at[pl.ds(pl.multiple_of(off_ref[j], nsl), blk)]
        return pltpu.make_async_copy(ybuf.at[s], dst, sem_y.at[s])

    @pl.when(i == 0)
    def _():
        for cp in weight_copies(be_ref[0]):
            cp.start(priority=1)
        x_copy(0, 0).start()
        ybuf[...] = jnp.zeros_like(ybuf)
        cp = pltpu.make_async_copy(ybuf.at[1], ys_hbm.at[pl.ds(ys_hbm.shape[0] - blk, blk)], sem_y.at[1])
        cp.start()
        cp.wait()

    @pl.when(first_ref[i] > 0)
    def _():
        for cp in weight_copies(be_ref[i]):
            cp.wait()
        rows = 128
        for r in range(0, d, rows):
            wgu_bf[r:r + rows, :] = stage_gu[r:r + rows, :].astype(BF16)
        for r in range(0, d_e, rows):
            wdn_bf[r:r + rows, :] = stage_dn[r:r + rows, :].astype(BF16)

        @pl.when(next_ref[i] >= 0)
        def _():
            for cp in weight_copies(next_ref[i]):
                cp.start(priority=1)

    def run_rows(m):
        x_ref = xbuf.at[slot]
        x = jnp.concatenate([_load_row_slab(x_ref, s, m, nsl).astype(BF16) for s in range(nsl)], axis=1)
        for c in range(d_e // EXPERT_CHUNK):
            lo = c * EXPERT_CHUNK
            hi = lo + EXPERT_CHUNK
            glu = jnp.dot(x, wgu_bf[:, lo:hi], preferred_element_type=F32) + bgu_ref[0, :, lo:hi]
            lin = (jnp.dot(x, wgu_bf[:, d_e + lo:d_e + hi], preferred_element_type=F32)
                   + bgu_ref[0, :, d_e + lo:d_e + hi])
            glu = jnp.minimum(glu, SWIGLU_LIMIT)
            lin = jnp.clip(lin, -SWIGLU_LIMIT, SWIGLU_LIMIT)
            act_scr[:m, lo:hi] = (glu * _sigmoid(SWIGLU_ALPHA * glu) * (lin + 1.0)).astype(BF16)
        y = jnp.dot(act_scr[:m, :], wdn_bf[...], preferred_element_type=F32) + bdn_ref[0]
        _store_rows(ybuf.at[slot], y)

    nv = nv_ref[i]

    @pl.when(nv > 0)
    def _():
        x_copy(i, slot).wait()

        @pl.when(i + 1 < total)
        def _():
            x_copy(i + 1, 1 - slot).start()

    sizes = [te // 4, te // 2, te]
    lower = 0
    for m in sizes:
        @pl.when((nv > lower) & (nv <= m))
        def _(m=m):
            run_rows(m)
        lower = m

    @pl.when(nv > 0)
    def _():
        @pl.when(i >= 1)
        def _():
            y_copy(i - 1, 1 - slot).wait()
        y_copy(i, slot).start()

        @pl.when(i == total - 1)
        def _():
            y_copy(i, slot).wait()


def _experts(block_expert, block_rows, block_first, block_next, block_off, total, xs,
             w_gu, b_gu, w_dn, b_dn, layer):
    d = w_gu.shape[2]
    n_blocks = block_expert.shape[0]
    blk = EXPERT_TILE * d // LANES
    d_gu = w_gu.shape[3]
    d_e = w_dn.shape[2]
    e_map3 = lambda i, be, nv, bf, bn, off, tot: (layer * N_EXPERTS + be[i], 0, 0)
    w_gu = w_gu.reshape((-1,) + w_gu.shape[2:])
    w_dn = w_dn.reshape((-1,) + w_dn.shape[2:])
    b_gu = b_gu.reshape(-1, 1, d_gu)
    b_dn = b_dn.reshape(-1, 1, d)
    grid_spec = pltpu.PrefetchScalarGridSpec(
        num_scalar_prefetch=6,
        grid=(n_blocks,),
        in_specs=[
            pl.BlockSpec(memory_space=pl.ANY),
            pl.BlockSpec(memory_space=pl.ANY),
            pl.BlockSpec((1, 1, d_gu), e_map3),
            pl.BlockSpec(memory_space=pl.ANY),
            pl.BlockSpec((1, 1, d), e_map3),
        ],
        out_specs=pl.BlockSpec(memory_space=pl.ANY),
        scratch_shapes=[
            pltpu.VMEM((2, blk, LANES), F32),
            pltpu.VMEM((2, blk, LANES), F32),
            pltpu.VMEM((d, d_gu), F32),
            pltpu.VMEM((d_e, d), F32),
            pltpu.VMEM((d, d_gu), BF16),
            pltpu.VMEM((d_e, d), BF16),
            pltpu.VMEM((EXPERT_TILE, d_e), BF16),
            pltpu.SemaphoreType.DMA((2,)),
            pltpu.SemaphoreType.DMA((2,)),
            pltpu.SemaphoreType.DMA((2,)),
        ],
    )
    return pl.pallas_call(
        functools.partial(_expert_kernel, layer=layer),
        grid_spec=grid_spec,
        out_shape=jax.ShapeDtypeStruct(xs.shape, F32),
        compiler_params=_params(sem=("arbitrary",)),
        name="experts",
    )(block_expert, block_rows, block_first, block_next, block_off, total, xs, w_gu, b_gu, w_dn, b_dn)


def _combine_kernel(dest_hbm, x1_ref, g_ref, fw_ref, ys_hbm, o_ref,
                    idx_smem, ybuf, sem_idx, sem_rows, *, final_norm):
    i = pl.program_id(0)
    nt = pl.num_programs(0)
    tm, d = x1_ref.shape
    nsl = d // LANES
    slot = i % 2

    def idx_copy(t, s):
        n_idx = TOP_K * tm
        dst = idx_smem.at[pl.ds(pl.multiple_of(s * n_idx, n_idx), n_idx)]
        return pltpu.make_async_copy(dest_hbm.at[t], dst, sem_idx.at[s])

    def issue_rows(s):
        def body(j, carry):
            n0 = j * DMA_UNROLL
            i0 = s * (TOP_K * tm) + n0
            for u in range(DMA_UNROLL):
                for k in range(TOP_K):
                    src = ys_hbm.at[pl.ds(pl.multiple_of(idx_smem[i0 + (k * tm + u)], nsl), nsl)]
                    dst = ybuf.at[s, k, pl.ds(pl.multiple_of((n0 + u) * nsl, nsl), nsl)]
                    pltpu.make_async_copy(src, dst, sem_rows.at[s, k]).start(priority=k % 2)
            return carry
        lax.fori_loop(0, tm // DMA_UNROLL, body, 0)

    def wait_rows(s):
        for k in range(TOP_K):
            pltpu.make_async_copy(ys_hbm.at[pl.ds(0, tm * nsl)], ybuf.at[s, k], sem_rows.at[s, k]).wait()

    @pl.when(i == 0)
    def _():
        cp = idx_copy(0, 0)
        cp.start()
        cp.wait()
        issue_rows(0)

        @pl.when(nt > 1)
        def _():
            idx_copy(1, 1).start()

    @pl.when(i + 1 < nt)
    def _():
        idx_copy(i + 1, 1 - slot).wait()
        issue_rows(1 - slot)

        @pl.when(i + 2 < nt)
        def _():
            idx_copy(i + 2, slot).start()

    wait_rows(slot)
    for r0 in range(0, tm, COMBINE_ROWS):
        g = g_ref[r0:r0 + COMBINE_ROWS, :]
        gk = [jnp.broadcast_to(g[:, k:k + 1], (COMBINE_ROWS, LANES)) for k in range(TOP_K)]
        slabs = []
        for s in range(nsl):
            acc = x1_ref[r0:r0 + COMBINE_ROWS, s * LANES:(s + 1) * LANES]
            for k in range(TOP_K):
                acc = acc + gk[k] * _load_row_slab(ybuf.at[slot, k], s, COMBINE_ROWS, nsl, r0)
            slabs.append(acc)
        out = jnp.concatenate(slabs, axis=1)
        if final_norm:
            out = _rms(out, fw_ref[...])
        o_ref[r0:r0 + COMBINE_ROWS, :] = out


def _combine(x1, ys, gate, dest, final_w, final_norm):
    n, d = x1.shape
    tm = ROW_TILE
    nt = n // tm
    tok_tile = lambda i: (i, 0)
    return pl.pallas_call(
        functools.partial(_combine_kernel, final_norm=final_norm),
        grid=(nt,),
        in_specs=[
            pl.BlockSpec(memory_space=pl.ANY),
            pl.BlockSpec((tm, d), tok_tile),
            pl.BlockSpec((tm, LANES), tok_tile),
            pl.BlockSpec((1, d), lambda i: (0, 0)),
            pl.BlockSpec(memory_space=pl.ANY),
        ],
        out_specs=pl.BlockSpec((tm, d), tok_tile),
        out_shape=jax.ShapeDtypeStruct((n, d), F32),
        scratch_shapes=[
            pltpu.SMEM((2 * TOP_K * tm,), I32),
            pltpu.VMEM((2, TOP_K, tm * d // LANES, LANES), F32),
            pltpu.SemaphoreType.DMA((2,)),
            pltpu.SemaphoreType.DMA((2, TOP_K)),
        ],
        compiler_params=_params(sem=("arbitrary",)),
        name="combine",
    )(dest, x1, gate, final_w.reshape(1, d), ys)


def kernel(x, attn_norm_w, w_in, sgu_ln_w, sgu_ln_b, sgu_w, sgu_b, ssm_a_re, ssm_a_im, ssm_b_re, ssm_b_im, ssm_c_re, ssm_c_im, ssm_d, ssm_log_dt, ssm_glu_w, ssm_glu_b, out_norm_a, out_norm_b, w_out, ffn_norm_w, router_w, router_b, w_gate_up, b_gate_up, w_down, b_down, final_norm_w):
    b, l, d = x.shape
    n = b * l
    depth = w_in.shape[0]
    blocks_per_seq = l // SSM_T
    n_steps = max(1, (blocks_per_seq - 1).bit_length())
    n_rows = n * TOP_K + EXPERT_TILE
    x2 = x.reshape(n, d).astype(F32)
    k2, so3, si2, apw = _s5_tables(ssm_a_re, ssm_a_im, ssm_b_re, ssm_b_im, ssm_c_re, ssm_c_im, ssm_d,
                                   ssm_log_dt, n_steps)
    for layer in range(depth):
        ya, us3 = _front(x2, attn_norm_w[layer], w_in[layer], sgu_ln_w[layer], sgu_ln_b[layer],
                         sgu_w[layer], sgu_b[layer], out_norm_a[layer])
        ys3 = _s5(us3, k2, so3, si2, apw, blocks_per_seq, layer)
        x1, h2, gate, er, cnt = _back(x2, ya, ys3, ssm_glu_w[layer], ssm_glu_b[layer],
                                      out_norm_b[layer], w_out[layer], ffn_norm_w[layer],
                                      router_w[layer], router_b[layer])
        counts = cnt[0, :N_EXPERTS].astype(I32)
        dest, block_expert, block_rows, block_first, block_next, block_off, total = _routing_tables(
            er, counts, n, d // LANES)
        xs = _dispatch(dest, h2, n_rows, d // LANES)
        ys = _experts(block_expert, block_rows, block_first, block_next, block_off, total, xs,
                      w_gate_up, b_gate_up, w_down, b_down, layer)
        x2 = _combine(x1, ys, gate, dest, final_norm_w, final_norm=(layer == depth - 1))
    return x2.reshape(b, l, d).astype(x.dtype)
```

```python
import functools
import math

import jax
import jax.numpy as jnp
from jax import lax
from jax.experimental import pallas as pl
from jax.experimental.pallas import tpu as pltpu

F32 = jnp.float32
BF16 = jnp.bfloat16
I32 = jnp.int32

EPS = 1e-5
N_HEADS = 4
CHUNK = 128
SSM_GROUP = 16
SSM_STATE = 64
SSM_T = 16
N_EXPERTS = 32
TOP_K = 4
SWIGLU_LIMIT = 7.0
SWIGLU_ALPHA = 1.702
LANES = 128
SUBLANES = 8
GROUPS_PER_LANE_BLOCK = LANES // SSM_GROUP

ROW_TILE = 512
EXPERT_TILE = 1024
EXPERT_CHUNK = 256
DMA_UNROLL = 8
COMBINE_ROWS = 64
VMEM_LIMIT = 56 * 1024 * 1024


def _gelu(x):
    return 0.5 * x * (1.0 + jnp.tanh(math.sqrt(2.0 / math.pi) * (x + 0.044715 * (x * x * x))))


def _sigmoid(x):
    return 1.0 / (1.0 + jnp.exp(-x))


def _rms(x, w):
    return x * lax.rsqrt(jnp.mean(x * x, axis=-1, keepdims=True) + EPS) * w


def _store_rows(ref, val, row0=0):
    rows, d = val.shape
    nsl = d // LANES
    for s in range(nsl):
        ref[pl.ds(row0 * nsl + s, rows, stride=nsl), :] = val[:, s * LANES:(s + 1) * LANES]


def _load_row_slab(ref, s, rows, nsl, row0=0):
    return ref[pl.ds(row0 * nsl + s, rows, stride=nsl), :]


def _params(**kw):
    return pltpu.CompilerParams(dimension_semantics=kw.pop("sem"), vmem_limit_bytes=VMEM_LIMIT, **kw)


def _front_kernel(x_ref, nw_ref, win_ref, lnw_ref, lnb_ref, ws_ref, bst_ref, ona_ref,
                  ya_ref, us_ref, mixed_ref, us_scr):
    d_g = ya_ref.shape[1]
    hd = d_g // N_HEADS
    tm = x_ref.shape[0]
    x = x_ref[...]
    h = _rms(x, nw_ref[...]).astype(BF16)
    proj = jnp.dot(h, win_ref[...], preferred_element_type=F32)
    u = _gelu(proj[:, :d_g])
    v = _gelu(proj[:, d_g:2 * d_g])
    n_lb = us_scr.shape[0]
    for q in range(n_lb):
        us_scr[q] = proj[:, 2 * d_g + q * LANES:2 * d_g + (q + 1) * LANES]
    mu = jnp.mean(v, axis=-1, keepdims=True)
    vc = v - mu
    var = jnp.mean(vc * vc, axis=-1, keepdims=True)
    vb = (vc * lax.rsqrt(var + EPS) * lnw_ref[...] + lnb_ref[...]).astype(BF16)
    row = lax.broadcasted_iota(I32, (CHUNK, CHUNK), 0)
    col = lax.broadcasted_iota(I32, (CHUNK, CHUNK), 1)
    causal = row >= col
    for hh in range(N_HEADS):
        w = jnp.where(causal, ws_ref[hh], 0.0).astype(BF16)
        bias = bst_ref[:, hh:hh + 1]
        for c in range(tm // CHUNK):
            vv = vb[c * CHUNK:(c + 1) * CHUNK, hh * hd:(hh + 1) * hd]
            m = jnp.dot(w, vv, preferred_element_type=F32) + bias
            mixed_ref[c * CHUNK:(c + 1) * CHUNK, hh * hd:(hh + 1) * hd] = m
    ya = u * mixed_ref[...]
    ya_ref[...] = _rms(ya, ona_ref[...]).astype(BF16)
    for s in range(SSM_T):
        for q in range(n_lb):
            us_ref[s, :, q * LANES:(q + 1) * LANES] = (
                us_scr[q, pl.ds(s, tm // SSM_T, stride=SSM_T), :].astype(BF16))


def _front(x2, nw, w_in, ln_w, ln_b, w_s, b_s, on_a):
    n, d = x2.shape
    d_g = ln_w.shape[0]
    d_s = w_in.shape[1] - 2 * d_g
    tm = ROW_TILE
    const2 = lambda i: (0, 0)
    return pl.pallas_call(
        _front_kernel,
        grid=(n // tm,),
        in_specs=[
            pl.BlockSpec((tm, d), lambda i: (i, 0)),
            pl.BlockSpec((1, d), const2),
            pl.BlockSpec(w_in.shape, const2),
            pl.BlockSpec((1, d_g), const2),
            pl.BlockSpec((1, d_g), const2),
            pl.BlockSpec(w_s.shape, lambda i: (0, 0, 0)),
            pl.BlockSpec((CHUNK, N_HEADS), const2),
            pl.BlockSpec((1, d_g), const2),
        ],
        out_specs=[
            pl.BlockSpec((tm, d_g), lambda i: (i, 0)),
            pl.BlockSpec((SSM_T, tm // SSM_T, d_s), lambda i: (0, i, 0)),
        ],
        out_shape=[
            jax.ShapeDtypeStruct((n, d_g), BF16),
            jax.ShapeDtypeStruct((SSM_T, n // SSM_T, d_s), BF16),
        ],
        scratch_shapes=[pltpu.VMEM((tm, d_g), F32), pltpu.VMEM((d_s // LANES, tm, LANES), F32)],
        compiler_params=_params(sem=("arbitrary",)),
        name="front",
    )(x2, nw.reshape(1, d), w_in.astype(BF16), ln_w.reshape(1, d_g), ln_b.reshape(1, d_g),
      w_s, b_s.T, on_a.reshape(1, d_g))


def _s5_tables(a_re, a_im, b_re, b_im, c_re, c_im, d, log_dt, n_steps):
    depth, g, p = a_re.shape
    hch = b_re.shape[-1]
    t = SSM_T
    gl = GROUPS_PER_LANE_BLOCK
    r = depth * (g // gl)
    a = lax.complex(a_re.astype(F32), a_im.astype(F32)).reshape(r, gl, p)
    dt = jnp.exp(log_dt.astype(F32)).reshape(r, gl, 1)
    dta = dt * a
    a_bar = jnp.exp(dta)
    b = lax.complex(b_re.astype(F32), b_im.astype(F32)).reshape(r, gl, p, hch)
    b_bar = ((a_bar - 1.0) / a)[..., None] * b
    c = lax.complex(c_re.astype(F32), c_im.astype(F32)).reshape(r, gl, hch, p)
    lags = jnp.arange(t + 1, dtype=F32)
    pw = jnp.exp(lags[None, :, None, None] * dta.transpose(0, 2, 1)[:, None])
    pwx = jnp.repeat(pw, hch, axis=-1)
    bb = b_bar.transpose(0, 2, 1, 3).reshape(r, p, gl * hch)
    cc = c.transpose(0, 3, 1, 2).reshape(r, p, gl * hch)
    q = pwx[:, :t] * bb[:, None]
    si = pwx[:, 1:] * cc[:, None]
    k2 = (jnp.einsum('rlpx,rpy->rlxy', q.real, cc.real, precision=lax.Precision.HIGHEST)
          - jnp.einsum('rlpx,rpy->rlxy', q.imag, cc.imag, precision=lax.Precision.HIGHEST))
    lane_g = jnp.arange(gl * hch) // hch
    k2 = jnp.where(lane_g[:, None] == lane_g[None, :], k2, 0.0)
    skip = jnp.eye(gl * hch, dtype=F32) * d.astype(F32).reshape(r, 1, gl * hch)
    k2 = k2.at[:, 0].add(skip)
    so3 = jnp.swapaxes(jnp.concatenate([q.real, q.imag], axis=2), 2, 3)
    si2 = jnp.concatenate([si.real, -si.imag], axis=2)
    steps = SSM_T * (2.0 ** jnp.arange(n_steps, dtype=F32))
    ap = jnp.exp(steps[None, :, None, None] * dta[:, None]).reshape(r, n_steps, gl * p)
    mul_same = jnp.concatenate([ap.real, ap.real], axis=-1)
    mul_swap = jnp.concatenate([-ap.imag, ap.imag], axis=-1)
    apw = jnp.stack([mul_same, mul_swap], axis=2)
    return k2.astype(BF16), so3.astype(BF16), si2.astype(BF16), apw


def _s5_kernel(u_ref, k2_ref, so3_ref, si2_ref, ap_ref, y_ref, wk_scr, wso_scr, wsi_scr):
    t_blk = u_ref.shape[0]
    rows = u_ref.shape[1]
    n_state = wso_scr.shape[1]
    two_p = so3_ref.shape[3]
    p = two_p // 2
    half = n_state // 2

    @pl.when(pl.program_id(1) == 0)
    def _():
        for s in range(t_blk):
            if s > 0:
                wk_scr[s * LANES:(s + 1) * LANES, :s * LANES] = jnp.zeros((LANES, s * LANES), BF16)
            for t in range(s, t_blk):
                wk_scr[s * LANES:(s + 1) * LANES, t * LANES:(t + 1) * LANES] = k2_ref[0, t - s]
        r_e = lax.broadcasted_iota(I32, (two_p, n_state), 0)
        c_e = lax.broadcasted_iota(I32, (two_p, n_state), 1)
        spread = jnp.where((r_e // p == c_e // half) & (r_e % p == c_e % p), 1.0, 0.0).astype(BF16)
        r_g = lax.broadcasted_iota(I32, (LANES, n_state), 0) // SSM_GROUP
        c_g = (lax.broadcasted_iota(I32, (LANES, n_state), 1) % half) // p
        own = r_g == c_g
        for s in range(t_blk):
            full = jnp.dot(so3_ref[0, t_blk - 1 - s], spread, preferred_element_type=F32)
            wso_scr[s * LANES:(s + 1) * LANES, :] = jnp.where(own, full, 0.0).astype(BF16)
        r_e = lax.broadcasted_iota(I32, (n_state, two_p), 0)
        c_e = lax.broadcasted_iota(I32, (n_state, two_p), 1)
        gather = jnp.where((c_e // p == r_e // half) & (c_e % p == r_e % p), 1.0, 0.0).astype(BF16)
        r_g = (lax.broadcasted_iota(I32, (n_state, LANES), 0) % half) // p
        c_g = lax.broadcasted_iota(I32, (n_state, LANES), 1) // SSM_GROUP
        own = r_g == c_g
        for t in range(t_blk):
            full = jnp.dot(gather, si2_ref[0, t], preferred_element_type=F32)
            wsi_scr[:, t * LANES:(t + 1) * LANES] = jnp.where(own, full, 0.0).astype(BF16)

    xcat = jnp.concatenate([u_ref[s] for s in range(t_blk)], axis=1)
    x = jnp.dot(xcat, wso_scr[...], preferred_element_type=F32)
    pos = lax.broadcasted_iota(I32, (rows, n_state), 0)
    for k in range(ap_ref.shape[1]):
        sh = 1 << k
        prev = pltpu.roll(x, sh, 0)
        prev_sw = pltpu.roll(prev, half, 1)
        upd = prev * ap_ref[0, k, 0:1, :] + prev_sw * ap_ref[0, k, 1:2, :]
        x = x + jnp.where(pos >= sh, upd, 0.0)
    xin = jnp.where(pos >= 1, pltpu.roll(x, 1, 0), 0.0).astype(BF16)
    for j in range(t_blk // 2):
        k_hi = (2 * j + 2) * LANES
        lo = 2 * j * LANES
        y = jnp.dot(xcat[:, :k_hi], wk_scr[:k_hi, lo:lo + 2 * LANES], preferred_element_type=F32)
        y = y + jnp.dot(xin, wsi_scr[:, lo:lo + 2 * LANES], preferred_element_type=F32)
        y_ref[2 * j] = y[:, :LANES].astype(BF16)
        y_ref[2 * j + 1] = y[:, LANES:].astype(BF16)


def _s5(us3, k2, so3, si2, apw, blocks_per_seq, layer):
    t_blk, n_blocks, d_s = us3.shape
    lb = d_s // LANES
    n_seq = n_blocks // blocks_per_seq
    n_in = t_blk * LANES
    n_state = apw.shape[3]
    w_map = lambda a, b: (layer * lb + a, 0, 0, 0)
    io_spec = pl.BlockSpec((t_blk, blocks_per_seq, LANES), lambda a, b: (0, b, a))
    return pl.pallas_call(
        _s5_kernel,
        grid=(lb, n_seq),
        in_specs=[
            io_spec,
            pl.BlockSpec((1,) + k2.shape[1:], w_map),
            pl.BlockSpec((1,) + so3.shape[1:], w_map),
            pl.BlockSpec((1,) + si2.shape[1:], w_map),
            pl.BlockSpec((1,) + apw.shape[1:], w_map),
        ],
        out_specs=io_spec,
        out_shape=jax.ShapeDtypeStruct(us3.shape, BF16),
        scratch_shapes=[
            pltpu.VMEM((n_in, n_in), BF16),
            pltpu.VMEM((n_in, n_state), BF16),
            pltpu.VMEM((n_state, n_in), BF16),
        ],
        compiler_params=_params(sem=("arbitrary", "arbitrary")),
        name="s5",
    )(us3, k2, so3, si2, apw)


def _back_kernel(x_ref, ya_ref, ys_ref, gw_ref, gb_ref, onb_ref, wo_ref, fnw_ref,
                 rw_ref, rb_ref,
                 x1_ref, h2_ref, gate_ref, er_ref, cnt_ref, carry_ref, ys_scr):
    i = pl.program_id(0)
    tm = x_ref.shape[0]
    d_g = ya_ref.shape[1]

    @pl.when(i == 0)
    def _():
        carry_ref[...] = jnp.zeros_like(carry_ref)

    n_lb = ys_scr.shape[0]
    for t in range(SSM_T):
        for q in range(n_lb):
            ys_scr[q, pl.ds(t, tm // SSM_T, stride=SSM_T), :] = (
                ys_ref[t, :, q * LANES:(q + 1) * LANES].astype(F32))
    y = _gelu(jnp.concatenate([ys_scr[q] for q in range(n_lb)], axis=1))
    z = jnp.dot(y.astype(BF16), gw_ref[...], preferred_element_type=F32) + gb_ref[...]
    yb = y * _sigmoid(z)
    ybn = _rms(yb, onb_ref[...]).astype(BF16)
    x1 = (x_ref[...]
          + jnp.dot(ya_ref[...], wo_ref[:d_g, :], preferred_element_type=F32)
          + jnp.dot(ybn, wo_ref[d_g:, :], preferred_element_type=F32))
    x1_ref[...] = x1
    h2 = _rms(x1, fnw_ref[...])
    _store_rows(h2_ref, h2)

    hh = h2.astype(BF16)
    hl = (h2 - hh.astype(F32)).astype(BF16)
    p_hi = jnp.dot(hh, rw_ref[...], preferred_element_type=F32)
    p_lo = jnp.dot(hl, rw_ref[:, :LANES], preferred_element_type=F32)
    logits = p_hi[:, :LANES] + p_lo + p_hi[:, LANES:] + rb_ref[...]
    lane = lax.broadcasted_iota(I32, (tm, LANES), 1)
    lane_f = lane.astype(F32)
    neg = jnp.float32(-jnp.inf)
    work = jnp.where(lane < N_EXPERTS, logits, neg)
    vals, hots, idxs = [], [], []
    for k in range(TOP_K):
        m = jnp.max(work, axis=-1, keepdims=True)
        idx = jnp.min(jnp.where(work == m, lane_f, float(LANES)), axis=-1, keepdims=True)
        hot = lane_f == idx
        vals.append(m)
        hots.append(hot)
        idxs.append(idx.astype(I32))
        work = jnp.where(hot, neg, work)
    exps = [jnp.exp(v - vals[0]) for v in vals]
    denom = exps[0] + exps[1] + exps[2] + exps[3]
    gate = jnp.zeros((tm, LANES), F32)
    for k in range(TOP_K):
        gate = jnp.where(lane == k, exps[k] / denom, gate)
    gate_ref[...] = gate

    sel = (hots[0] | hots[1] | hots[2] | hots[3])
    sel_f = jnp.where(sel, 1.0, 0.0)
    r_i = lax.broadcasted_iota(I32, (tm, tm), 0)
    c_i = lax.broadcasted_iota(I32, (tm, tm), 1)
    lower = jnp.where(r_i > c_i, 1.0, 0.0).astype(BF16)
    cum = jnp.dot(lower, sel_f.astype(BF16), preferred_element_type=F32) + carry_ref[...]
    er = jnp.zeros((tm, LANES), I32)
    for k in range(TOP_K):
        rk = jnp.sum(jnp.where(hots[k], cum, 0.0), axis=-1, keepdims=True)
        er = jnp.where(lane == k, idxs[k], er)
        er = jnp.where(lane == TOP_K + k, rk.astype(I32), er)
    carry_ref[...] = carry_ref[...] + jnp.sum(sel_f, axis=0, keepdims=True)
    cnt_ref[...] = carry_ref[...]
    er_ref[0] = jnp.transpose(er)[:SUBLANES, :]


def _back(x2, ya, ys3, glu_w, glu_b, on_b, w_out, fn_w, router_w, router_b):
    n, d = x2.shape
    d_g = ya.shape[1]
    d_s = ys3.shape[2]
    tm = ROW_TILE
    nt = n // tm
    const2 = lambda i: (0, 0)
    rw = jnp.zeros((d, LANES), F32).at[:, :N_EXPERTS].set(router_w.astype(F32))
    rw_hi = rw.astype(BF16)
    rw_lo = (rw - rw_hi.astype(F32)).astype(BF16)
    rw_split = jnp.concatenate([rw_hi, rw_lo], axis=1)
    rb = jnp.zeros((1, LANES), F32).at[0, :N_EXPERTS].set(router_b.astype(F32))
    tok_tile = lambda i: (i, 0)
    return pl.pallas_call(
        _back_kernel,
        grid=(nt,),
        in_specs=[
            pl.BlockSpec((tm, d), tok_tile),
            pl.BlockSpec((tm, d_g), tok_tile),
            pl.BlockSpec((SSM_T, tm // SSM_T, d_s), lambda i: (0, i, 0)),
            pl.BlockSpec((d_s, d_s), const2),
            pl.BlockSpec((1, d_s), const2),
            pl.BlockSpec((1, d_s), const2),
            pl.BlockSpec((d_g + d_s, d), const2),
            pl.BlockSpec((1, d), const2),
            pl.BlockSpec((d, 2 * LANES), const2),
            pl.BlockSpec((1, LANES), const2),
        ],
        out_specs=[
            pl.BlockSpec((tm, d), tok_tile),
            pl.BlockSpec((tm * d // LANES, LANES), tok_tile),
            pl.BlockSpec((tm, LANES), tok_tile),
            pl.BlockSpec((1, SUBLANES, tm), lambda i: (i, 0, 0)),
            pl.BlockSpec((1, LANES), const2),
        ],
        out_shape=[
            jax.ShapeDtypeStruct((n, d), F32),
            jax.ShapeDtypeStruct((n * d // LANES, LANES), F32),
            jax.ShapeDtypeStruct((n, LANES), F32),
            jax.ShapeDtypeStruct((nt, SUBLANES, tm), I32),
            jax.ShapeDtypeStruct((1, LANES), F32),
        ],
        scratch_shapes=[pltpu.VMEM((1, LANES), F32), pltpu.VMEM((d_s // LANES, tm, LANES), F32)],
        compiler_params=_params(sem=("arbitrary",)),
        name="back",
    )(x2, ya, ys3, glu_w.astype(BF16), glu_b.reshape(1, d_s), on_b.reshape(1, d_s),
      w_out.astype(BF16), fn_w.reshape(1, d), rw_split, rb)


def _routing_tables(er, counts, n, row_sl):
    te = EXPERT_TILE
    n_blocks = n * TOP_K // te + N_EXPERTS
    start = jnp.cumsum(counts) - counts
    e_ids = jnp.arange(N_EXPERTS, dtype=I32)
    e_sel = er[:, :TOP_K, :, None] == e_ids
    dest = jnp.sum(jnp.where(e_sel, start, 0), axis=-1) + er[:, TOP_K:2 * TOP_K, :]
    dest = (dest * row_sl).reshape(dest.shape[0], -1)
    nb = (counts + te - 1) // te
    cum = jnp.cumsum(nb)
    total = cum[-1]
    j = jnp.arange(n_blocks, dtype=I32)
    e_j = jnp.sum((cum[None, :] <= jnp.minimum(j, total - 1)[:, None]).astype(I32), axis=1)
    valid = (j < total).astype(I32)
    own = e_ids[None, :] == e_j[:, None]
    first_j = jnp.sum(jnp.where(own, cum - nb, 0), axis=1)
    count_j = jnp.sum(jnp.where(own, counts, 0), axis=1)
    start_j = jnp.sum(jnp.where(own, start, 0), axis=1)
    first = valid * (j == first_j).astype(I32)
    later = (e_ids[None, :] > e_j[:, None]) & (nb > 0)[None, :]
    nxt = jnp.min(jnp.where(later, e_ids[None, :], N_EXPERTS), axis=1)
    nxt = jnp.where(nxt < N_EXPERTS, nxt, -1)
    rows = valid * jnp.clip(count_j - (j - first_j) * te, 0, te)
    row_off = (start_j + (j - first_j) * te) * row_sl
    return (dest.astype(I32), e_j.astype(I32), rows.astype(I32), first, nxt.astype(I32),
            row_off.astype(I32), total.reshape(1).astype(I32))


def _dispatch_kernel(dest_hbm, h_ref, xs_hbm, idx_smem, zbuf, sem_idx, sem_rows, sem_z, *, row_sl):
    i = pl.program_id(0)
    nt = pl.num_programs(0)
    tm = h_ref.shape[0] // row_sl
    blk = zbuf.shape[0]
    slot = i % 2

    def idx_copy(t, s):
        n_idx = TOP_K * tm
        dst = idx_smem.at[pl.ds(pl.multiple_of(s * n_idx, n_idx), n_idx)]
        return pltpu.make_async_copy(dest_hbm.at[t], dst, sem_idx.at[s])

    @pl.when(i == 0)
    def _():
        zbuf[...] = jnp.zeros_like(zbuf)
        cp = pltpu.make_async_copy(zbuf, xs_hbm.at[pl.ds(xs_hbm.shape[0] - blk, blk)], sem_z)
        cp.start()
        cp.wait()
        idx_copy(0, 0).start()

    @pl.when(i + 1 < nt)
    def _():
        idx_copy(i + 1, 1 - slot).start()

    idx_copy(i, slot).wait()

    def body(j, carry):
        n0 = j * DMA_UNROLL
        i0 = slot * (TOP_K * tm) + n0
        for u in range(DMA_UNROLL):
            src = h_ref.at[pl.ds(pl.multiple_of((n0 + u) * row_sl, row_sl), row_sl)]
            for k in range(TOP_K):
                dst = xs_hbm.at[pl.ds(pl.multiple_of(idx_smem[i0 + (k * tm + u)], row_sl), row_sl)]
                pltpu.make_async_copy(src, dst, sem_rows.at[k]).start(priority=k % 2)
        return carry
    lax.fori_loop(0, tm // DMA_UNROLL, body, 0)
    for k in range(TOP_K):
        pltpu.make_async_copy(h_ref, xs_hbm.at[pl.ds(0, tm * row_sl)], sem_rows.at[k]).wait()


def _dispatch(dest, h2t, n_rows, row_sl):
    tm = ROW_TILE
    nt = h2t.shape[0] // (tm * row_sl)
    return pl.pallas_call(
        functools.partial(_dispatch_kernel, row_sl=row_sl),
        grid=(nt,),
        in_specs=[
            pl.BlockSpec(memory_space=pl.ANY),
            pl.BlockSpec((tm * row_sl, LANES), lambda i: (i, 0)),
        ],
        out_specs=pl.BlockSpec(memory_space=pl.ANY),
        out_shape=jax.ShapeDtypeStruct((n_rows * row_sl, LANES), F32),
        scratch_shapes=[
            pltpu.SMEM((2 * TOP_K * tm,), I32),
            pltpu.VMEM((EXPERT_TILE * row_sl, LANES), F32),
            pltpu.SemaphoreType.DMA((2,)),
            pltpu.SemaphoreType.DMA((TOP_K,)),
            pltpu.SemaphoreType.DMA,
        ],
        compiler_params=_params(sem=("arbitrary",)),
        name="dispatch",
    )(dest, h2t)


def _expert_kernel(be_ref, nv_ref, first_ref, next_ref, off_ref, tot_ref,
                   xs_hbm, wgu_hbm, bgu_ref, wdn_hbm, bdn_ref, ys_hbm,
                   xbuf, ybuf, stage_gu, stage_dn, wgu_bf, wdn_bf, act_scr, sem_x, sem_y, sem_w, *, layer):
    i = pl.program_id(0)
    d, d_gu = wgu_bf.shape
    d_e = wdn_bf.shape[0]
    nsl = d // LANES
    blk = xbuf.shape[1]
    te = blk // nsl
    slot = i % 2
    total = tot_ref[0]

    def weight_copies(e):
        row = layer * N_EXPERTS + e
        return (pltpu.make_async_copy(wgu_hbm.at[row], stage_gu, sem_w.at[0]),
                pltpu.make_async_copy(wdn_hbm.at[row], stage_dn, sem_w.at[1]))

    def x_copy(j, s):
        src = xs_hbm.at[pl.ds(pl.multiple_of(off_ref[j], nsl), blk)]
        return pltpu.make_async_copy(src, xbuf.at[s], sem_x.at[s])

    def y_copy(j, s):
        dst = ys_hbm.at[pl.ds(pl.multiple_of(off_ref[j], nsl), blk)]
        return pltpu.make_async_copy(ybuf.at[s], dst, sem_y.at[s])

    @pl.when(i == 0)
    def _():
        for cp in weight_copies(be_ref[0]):
            cp.start(priority=1)
        x_copy(0, 0).start()
        ybuf[...] = jnp.zeros_like(ybuf)
        cp = pltpu.make_async_copy(ybuf.at[1], ys_hbm.at[pl.ds(ys_hbm.shape[0] - blk, blk)], sem_y.at[1])
        cp.start()
        cp.wait()

    @pl.when(first_ref[i] > 0)
    def _():
        for cp in weight_copies(be_ref[i]):
            cp.wait()
        rows = 128
        for r in range(0, d, rows):
            wgu_bf[r:r + rows, :] = stage_gu[r:r + rows, :].astype(BF16)
        for r in range(0, d_e, rows):
            wdn_bf[r:r + rows, :] = stage_dn[r:r + rows, :].astype(BF16)

        @pl.when(next_ref[i] >= 0)
        def _():
            for cp in weight_copies(next_ref[i]):
                cp.start(priority=1)

    def run_rows(m):
        x_ref = xbuf.at[slot]
        x = jnp.concatenate([_load_row_slab(x_ref, s, m, nsl).astype(BF16) for s in range(nsl)], axis=1)
        for c in range(d_e // EXPERT_CHUNK):
            lo = c * EXPERT_CHUNK
            hi = lo + EXPERT_CHUNK
            glu = jnp.dot(x, wgu_bf[:, lo:hi], preferred_element_type=F32) + bgu_ref[0, :, lo:hi]
            lin = (jnp.dot(x, wgu_bf[:, d_e + lo:d_e + hi], preferred_element_type=F32)
                   + bgu_ref[0, :, d_e + lo:d_e + hi])
            glu = jnp.minimum(glu, SWIGLU_LIMIT)
            lin = jnp.clip(lin, -SWIGLU_LIMIT, SWIGLU_LIMIT)
            act_scr[:m, lo:hi] = (glu * _sigmoid(SWIGLU_ALPHA * glu) * (lin + 1.0)).astype(BF16)
        y = jnp.dot(act_scr[:m, :], wdn_bf[...], preferred_element_type=F32) + bdn_ref[0]
        _store_rows(ybuf.at[slot], y)

    nv = nv_ref[i]

    @pl.when(nv > 0)
    def _():
        x_copy(i, slot).wait()

        @pl.when(i + 1 < total)
        def _():
            x_copy(i + 1, 1 - slot).start()

    sizes = [te // 4, te // 2, te]
    lower = 0
    for m in sizes:
        @pl.when((nv > lower) & (nv <= m))
        def _(m=m):
            run_rows(m)
        lower = m

    @pl.when(nv > 0)
    def _():
        @pl.when(i >= 1)
        def _():
            y_copy(i - 1, 1 - slot).wait()
        y_copy(i, slot).start()

        @pl.when(i == total - 1)
        def _():
            y_copy(i, slot).wait()


def _experts(block_expert, block_rows, block_first, block_next, block_off, total, xs,
             w_gu, b_gu, w_dn, b_dn, layer):
    d = w_gu.shape[2]
    n_blocks = block_expert.shape[0]
    blk = EXPERT_TILE * d // LANES
    d_gu = w_gu.shape[3]
    d_e = w_dn.shape[2]
    e_map3 = lambda i, be, nv, bf, bn, off, tot: (layer * N_EXPERTS + be[i], 0, 0)
    w_gu = w_gu.reshape((-1,) + w_gu.shape[2:])
    w_dn = w_dn.reshape((-1,) + w_dn.shape[2:])
    b_gu = b_gu.reshape(-1, 1, d_gu)
    b_dn = b_dn.reshape(-1, 1, d)
    grid_spec = pltpu.PrefetchScalarGridSpec(
        num_scalar_prefetch=6,
        grid=(n_blocks,),
        in_specs=[
            pl.BlockSpec(memory_space=pl.ANY),
            pl.BlockSpec(memory_space=pl.ANY),
            pl.BlockSpec((1, 1, d_gu), e_map3),
            pl.BlockSpec(memory_space=pl.ANY),
            pl.BlockSpec((1, 1, d), e_map3),
        ],
        out_specs=pl.BlockSpec(memory_space=pl.ANY),
        scratch_shapes=[
            pltpu.VMEM((2, blk, LANES), F32),
            pltpu.VMEM((2, blk, LANES), F32),
            pltpu.VMEM((d, d_gu), F32),
            pltpu.VMEM((d_e, d), F32),
            pltpu.VMEM((d, d_gu), BF16),
            pltpu.VMEM((d_e, d), BF16),
            pltpu.VMEM((EXPERT_TILE, d_e), BF16),
            pltpu.SemaphoreType.DMA((2,)),
            pltpu.SemaphoreType.DMA((2,)),
            pltpu.SemaphoreType.DMA((2,)),
        ],
    )
    return pl.pallas_call(
        functools.partial(_expert_kernel, layer=layer),
        grid_spec=grid_spec,
        out_shape=jax.ShapeDtypeStruct(xs.shape, F32),
        compiler_params=_params(sem=("arbitrary",)),
        name="experts",
    )(block_expert, block_rows, block_first, block_next, block_off, total, xs, w_gu, b_gu, w_dn, b_dn)


def _combine_kernel(dest_hbm, x1_ref, g_ref, fw_ref, ys_hbm, o_ref,
                    idx_smem, ybuf, sem_idx, sem_rows, *, final_norm):
    i = pl.program_id(0)
    nt = pl.num_programs(0)
    tm, d = x1_ref.shape
    nsl = d // LANES
    slot = i % 2

    def idx_copy(t, s):
        n_idx = TOP_K * tm
        dst = idx_smem.at[pl.ds(pl.multiple_of(s * n_idx, n_idx), n_idx)]
        return pltpu.make_async_copy(dest_hbm.at[t], dst, sem_idx.at[s])

    def issue_rows(s):
        def body(j, carry):
            n0 = j * DMA_UNROLL
            i0 = s * (TOP_K * tm) + n0
            for u in range(DMA_UNROLL):
                for k in range(TOP_K):
                    src = ys_hbm.at[pl.ds(pl.multiple_of(idx_smem[i0 + (k * tm + u)], nsl), nsl)]
                    dst = ybuf.at[s, k, pl.ds(pl.multiple_of((n0 + u) * nsl, nsl), nsl)]
                    pltpu.make_async_copy(src, dst, sem_rows.at[s, k]).start(priority=k % 2)
            return carry
        lax.fori_loop(0, tm // DMA_UNROLL, body, 0)

    def wait_rows(s):
        for k in range(TOP_K):
            pltpu.make_async_copy(ys_hbm.at[pl.ds(0, tm * nsl)], ybuf.at[s, k], sem_rows.at[s, k]).wait()

    @pl.when(i == 0)
    def _():
        cp = idx_copy(0, 0)
        cp.start()
        cp.wait()
        issue_rows(0)

        @pl.when(nt > 1)
        def _():
            idx_copy(1, 1).start()

    @pl.when(i + 1 < nt)
    def _():
        idx_copy(i + 1, 1 - slot).wait()
        issue_rows(1 - slot)

        @pl.when(i + 2 < nt)
        def _():
            idx_copy(i + 2, slot).start()

    wait_rows(slot)
    for r0 in range(0, tm, COMBINE_ROWS):
        g = g_ref[r0:r0 + COMBINE_ROWS, :]
        gk = [jnp.broadcast_to(g[:, k:k + 1], (COMBINE_ROWS, LANES)) for k in range(TOP_K)]
        slabs = []
        for s in range(nsl):
            acc = x1_ref[r0:r0 + COMBINE_ROWS, s * LANES:(s + 1) * LANES]
            for k in range(TOP_K):
                acc = acc + gk[k] * _load_row_slab(ybuf.at[slot, k], s, COMBINE_ROWS, nsl, r0)
            slabs.append(acc)
        out = jnp.concatenate(slabs, axis=1)
        if final_norm:
            out = _rms(out, fw_ref[...])
        o_ref[r0:r0 + COMBINE_ROWS, :] = out


def _combine(x1, ys, gate, dest, final_w, final_norm):
    n, d = x1.shape
    tm = ROW_TILE
    nt = n // tm
    tok_tile = lambda i: (i, 0)
    return pl.pallas_call(
        functools.partial(_combine_kernel, final_norm=final_norm),
        grid=(nt,),
        in_specs=[
            pl.BlockSpec(memory_space=pl.ANY),
            pl.BlockSpec((tm, d), tok_tile),
            pl.BlockSpec((tm, LANES), tok_tile),
            pl.BlockSpec((1, d), lambda i: (0, 0)),
            pl.BlockSpec(memory_space=pl.ANY),
        ],
        out_specs=pl.BlockSpec((tm, d), tok_tile),
        out_shape=jax.ShapeDtypeStruct((n, d), F32),
        scratch_shapes=[
            pltpu.SMEM((2 * TOP_K * tm,), I32),
            pltpu.VMEM((2, TOP_K, tm * d // LANES, LANES), F32),
            pltpu.SemaphoreType.DMA((2,)),
            pltpu.SemaphoreType.DMA((2, TOP_K)),
        ],
        compiler_params=_params(sem=("arbitrary",)),
        name="combine",
    )(dest, x1, gate, final_w.reshape(1, d), ys)


def kernel(x, attn_norm_w, w_in, sgu_ln_w, sgu_ln_b, sgu_w, sgu_b, ssm_a_re, ssm_a_im, ssm_b_re, ssm_b_im, ssm_c_re, ssm_c_im, ssm_d, ssm_log_dt, ssm_glu_w, ssm_glu_b, out_norm_a, out_norm_b, w_out, ffn_norm_w, router_w, router_b, w_gate_up, b_gate_up, w_down, b_down, final_norm_w):
    b, l, d = x.shape
    n = b * l
    depth = w_in.shape[0]
    blocks_per_seq = l // SSM_T
    n_steps = max(1, (blocks_per_seq - 1).bit_length())
    n_rows = n * TOP_K + EXPERT_TILE
    x2 = x.reshape(n, d).astype(F32)
    k2, so3, si2, apw = _s5_tables(ssm_a_re, ssm_a_im, ssm_b_re, ssm_b_im, ssm_c_re, ssm_c_im, ssm_d,
                                   ssm_log_dt, n_steps)
    for layer in range(depth):
        ya, us3 = _front(x2, attn_norm_w[layer], w_in[layer], sgu_ln_w[layer], sgu_ln_b[layer],
                         sgu_w[layer], sgu_b[layer], out_norm_a[layer])
        ys3 = _s5(us3, k2, so3, si2, apw, blocks_per_seq, layer)
        x1, h2, gate, er, cnt = _back(x2, ya, ys3, ssm_glu_w[layer], ssm_glu_b[layer],
                                      out_norm_b[layer], w_out[layer], ffn_norm_w[layer],
                                      router_w[layer], router_b[layer])
        counts = cnt[0, :N_EXPERTS].astype(I32)
        dest, block_expert, block_rows, block_first, block_next, block_off, total = _routing_tables(
            er, counts, n, d // LANES)
        xs = _dispatch(dest, h2, n_rows, d // LANES)
        ys = _experts(block_expert, block_rows, block_first, block_next, block_off, total, xs,
                      w_gate_up, b_gate_up, w_down, b_down, layer)
        x2 = _combine(x1, ys, gate, dest, final_norm_w, final_norm=(layer == depth - 1))
    return x2.reshape(b, l, d).astype(x.dtype)
```

```python
import functools
import math

import jax
import jax.numpy as jnp
from jax import lax
from jax.experimental import pallas as pl
from jax.experimental.pallas import tpu as pltpu

F32 = jnp.float32
BF16 = jnp.bfloat16
I32 = jnp.int32

EPS = 1e-5
N_HEADS = 4
CHUNK = 128
SSM_GROUP = 16
SSM_STATE = 64
SSM_T = 16
N_EXPERTS = 32
TOP_K = 4
SWIGLU_LIMIT = 7.0
SWIGLU_ALPHA = 1.702
LANES = 128
SUBLANES = 8
GROUPS_PER_LANE_BLOCK = LANES // SSM_GROUP

ROW_TILE = 512
EXPERT_TILE = 1024
EXPERT_CHUNK = 256
DMA_UNROLL = 8
COMBINE_ROWS = 64
VMEM_LIMIT = 56 * 1024 * 1024


def _gelu(x):
    return 0.5 * x * (1.0 + jnp.tanh(math.sqrt(2.0 / math.pi) * (x + 0.044715 * (x * x * x))))


def _sigmoid(x):
    return 1.0 / (1.0 + jnp.exp(-x))


def _rms(x, w):
    return x * lax.rsqrt(jnp.mean(x * x, axis=-1, keepdims=True) + EPS) * w


def _store_rows(ref, val, row0=0):
    rows, d = val.shape
    nsl = d // LANES
    for s in range(nsl):
        ref[pl.ds(row0 * nsl + s, rows, stride=nsl), :] = val[:, s * LANES:(s + 1) * LANES]


def _load_row_slab(ref, s, rows, nsl, row0=0):
    return ref[pl.ds(row0 * nsl + s, rows, stride=nsl), :]


def _params(**kw):
    return pltpu.CompilerParams(dimension_semantics=kw.pop("sem"), vmem_limit_bytes=VMEM_LIMIT, **kw)


def _front_kernel(x_ref, nw_ref, win_ref, lnw_ref, lnb_ref, ws_ref, bst_ref, ona_ref,
                  ya_ref, us_ref, mixed_ref, us_scr):
    d_g = ya_ref.shape[1]
    hd = d_g // N_HEADS
    tm = x_ref.shape[0]
    x = x_ref[...]
    h = _rms(x, nw_ref[...]).astype(BF16)
    proj = jnp.dot(h, win_ref[...], preferred_element_type=F32)
    u = _gelu(proj[:, :d_g])
    v = _gelu(proj[:, d_g:2 * d_g])
    n_lb = us_scr.shape[0]
    for q in range(n_lb):
        us_scr[q] = proj[:, 2 * d_g + q * LANES:2 * d_g + (q + 1) * LANES]
    mu = jnp.mean(v, axis=-1, keepdims=True)
    vc = v - mu
    var = jnp.mean(vc * vc, axis=-1, keepdims=True)
    vb = (vc * lax.rsqrt(var + EPS) * lnw_ref[...] + lnb_ref[...]).astype(BF16)
    row = lax.broadcasted_iota(I32, (CHUNK, CHUNK), 0)
    col = lax.broadcasted_iota(I32, (CHUNK, CHUNK), 1)
    causal = row >= col
    for hh in range(N_HEADS):
        w = jnp.where(causal, ws_ref[hh], 0.0).astype(BF16)
        bias = bst_ref[:, hh:hh + 1]
        for c in range(tm // CHUNK):
            vv = vb[c * CHUNK:(c + 1) * CHUNK, hh * hd:(hh + 1) * hd]
            m = jnp.dot(w, vv, preferred_element_type=F32) + bias
            mixed_ref[c * CHUNK:(c + 1) * CHUNK, hh * hd:(hh + 1) * hd] = m
    ya = u * mixed_ref[...]
    ya_ref[...] = _rms(ya, ona_ref[...]).astype(BF16)
    for s in range(SSM_T):
        for q in range(n_lb):
            us_ref[s, :, q * LANES:(q + 1) * LANES] = (
                us_scr[q, pl.ds(s, tm // SSM_T, stride=SSM_T), :].astype(BF16))


def _front(x2, nw, w_in, ln_w, ln_b, w_s, b_s, on_a):
    n, d = x2.shape
    d_g = ln_w.shape[0]
    d_s = w_in.shape[1] - 2 * d_g
    tm = ROW_TILE
    const2 = lambda i: (0, 0)
    return pl.pallas_call(
        _front_kernel,
        grid=(n // tm,),
        in_specs=[
            pl.BlockSpec((tm, d), lambda i: (i, 0)),
            pl.BlockSpec((1, d), const2),
            pl.BlockSpec(w_in.shape, const2),
            pl.BlockSpec((1, d_g), const2),
            pl.BlockSpec((1, d_g), const2),
            pl.BlockSpec(w_s.shape, lambda i: (0, 0, 0)),
            pl.BlockSpec((CHUNK, N_HEADS), const2),
            pl.BlockSpec((1, d_g), const2),
        ],
        out_specs=[
            pl.BlockSpec((tm, d_g), lambda i: (i, 0)),
            pl.BlockSpec((SSM_T, tm // SSM_T, d_s), lambda i: (0, i, 0)),
        ],
        out_shape=[
            jax.ShapeDtypeStruct((n, d_g), BF16),
            jax.ShapeDtypeStruct((SSM_T, n // SSM_T, d_s), BF16),
        ],
        scratch_shapes=[pltpu.VMEM((tm, d_g), F32), pltpu.VMEM((d_s // LANES, tm, LANES), F32)],
        compiler_params=_params(sem=("arbitrary",)),
        name="front",
    )(x2, nw.reshape(1, d), w_in.astype(BF16), ln_w.reshape(1, d_g), ln_b.reshape(1, d_g),
      w_s, b_s.T, on_a.reshape(1, d_g))


def _s5_tables(a_re, a_im, b_re, b_im, c_re, c_im, d, log_dt, n_steps):
    depth, g, p = a_re.shape
    hch = b_re.shape[-1]
    t = SSM_T
    gl = GROUPS_PER_LANE_BLOCK
    r = depth * (g // gl)
    a = lax.complex(a_re.astype(F32), a_im.astype(F32)).reshape(r, gl, p)
    dt = jnp.exp(log_dt.astype(F32)).reshape(r, gl, 1)
    dta = dt * a
    a_bar = jnp.exp(dta)
    b = lax.complex(b_re.astype(F32), b_im.astype(F32)).reshape(r, gl, p, hch)
    b_bar = ((a_bar - 1.0) / a)[..., None] * b
    c = lax.complex(c_re.astype(F32), c_im.astype(F32)).reshape(r, gl, hch, p)
    lags = jnp.arange(t + 1, dtype=F32)
    pw = jnp.exp(lags[None, :, None, None] * dta.transpose(0, 2, 1)[:, None])
    pwx = jnp.repeat(pw, hch, axis=-1)
    bb = b_bar.transpose(0, 2, 1, 3).reshape(r, p, gl * hch)
    cc = c.transpose(0, 3, 1, 2).reshape(r, p, gl * hch)
    q = pwx[:, :t] * bb[:, None]
    si = pwx[:, 1:] * cc[:, None]
    k2 = (jnp.einsum('rlpx,rpy->rlxy', q.real, cc.real, precision=lax.Precision.HIGHEST)
          - jnp.einsum('rlpx,rpy->rlxy', q.imag, cc.imag, precision=lax.Precision.HIGHEST))
    lane_g = jnp.arange(gl * hch) // hch
    k2 = jnp.where(lane_g[:, None] == lane_g[None, :], k2, 0.0)
    skip = jnp.eye(gl * hch, dtype=F32) * d.astype(F32).reshape(r, 1, gl * hch)
    k2 = k2.at[:, 0].add(skip)
    so3 = jnp.swapaxes(jnp.concatenate([q.real, q.imag], axis=2), 2, 3)
    si2 = jnp.concatenate([si.real, -si.imag], axis=2)
    steps = SSM_T * (2.0 ** jnp.arange(n_steps, dtype=F32))
    ap = jnp.exp(steps[None, :, None, None] * dta[:, None]).reshape(r, n_steps, gl * p)
    mul_same = jnp.concatenate([ap.real, ap.real], axis=-1)
    mul_swap = jnp.concatenate([-ap.imag, ap.imag], axis=-1)
    apw = jnp.stack([mul_same, mul_swap], axis=2)
    return k2.astype(BF16), so3.astype(BF16), si2.astype(BF16), apw


def _s5_kernel(u_ref, k2_ref, so3_ref, si2_ref, ap_ref, y_ref, wk_scr, wso_scr, wsi_scr):
    t_blk = u_ref.shape[0]
    rows = u_ref.shape[1]
    n_state = wso_scr.shape[1]
    two_p = so3_ref.shape[3]
    p = two_p // 2
    half = n_state // 2

    @pl.when(pl.program_id(1) == 0)
    def _():
        for s in range(t_blk):
            if s > 0:
                wk_scr[s * LANES:(s + 1) * LANES, :s * LANES] = jnp.zeros((LANES, s * LANES), BF16)
            for t in range(s, t_blk):
                wk_scr[s * LANES:(s + 1) * LANES, t * LANES:(t + 1) * LANES] = k2_ref[0, t - s]
        r_e = lax.broadcasted_iota(I32, (two_p, n_state), 0)
        c_e = lax.broadcasted_iota(I32, (two_p, n_state), 1)
        spread = jnp.where((r_e // p == c_e // half) & (r_e % p == c_e % p), 1.0, 0.0).astype(BF16)
        r_g = lax.broadcasted_iota(I32, (LANES, n_state), 0) // SSM_GROUP
        c_g = (lax.broadcasted_iota(I32, (LANES, n_state), 1) % half) // p
        own = r_g == c_g
        for s in range(t_blk):
            full = jnp.dot(so3_ref[0, t_blk - 1 - s], spread, preferred_element_type=F32)
            wso_scr[s * LANES:(s + 1) * LANES, :] = jnp.where(own, full, 0.0).astype(BF16)
        r_e = lax.broadcasted_iota(I32, (n_state, two_p), 0)
        c_e = lax.broadcasted_iota(I32, (n_state, two_p), 1)
        gather = jnp.where((c_e // p == r_e // half) & (c_e % p == r_e % p), 1.0, 0.0).astype(BF16)
        r_g = (lax.broadcasted_iota(I32, (n_state, LANES), 0) % half) // p
        c_g = lax.broadcasted_iota(I32, (n_state, LANES), 1) // SSM_GROUP
        own = r_g == c_g
        for t in range(t_blk):
            full = jnp.dot(gather, si2_ref[0, t], preferred_element_type=F32)
            wsi_scr[:, t * LANES:(t + 1) * LANES] = jnp.where(own, full, 0.0).astype(BF16)

    xcat = jnp.concatenate([u_ref[s] for s in range(t_blk)], axis=1)
    x = jnp.dot(xcat, wso_scr[...], preferred_element_type=F32)
    pos = lax.broadcasted_iota(I32, (rows, n_state), 0)
    for k in range(ap_ref.shape[1]):
        sh = 1 << k
        prev = pltpu.roll(x, sh, 0)
        prev_sw = pltpu.roll(prev, half, 1)
        upd = prev * ap_ref[0, k, 0:1, :] + prev_sw * ap_ref[0, k, 1:2, :]
        x = x + jnp.where(pos >= sh, upd, 0.0)
    xin = jnp.where(pos >= 1, pltpu.roll(x, 1, 0), 0.0).astype(BF16)
    for j in range(t_blk // 2):
        k_hi = (2 * j + 2) * LANES
        lo = 2 * j * LANES
        y = jnp.dot(xcat[:, :k_hi], wk_scr[:k_hi, lo:lo + 2 * LANES], preferred_element_type=F32)
        y = y + jnp.dot(xin, wsi_scr[:, lo:lo + 2 * LANES], preferred_element_type=F32)
        y_ref[2 * j] = y[:, :LANES].astype(BF16)
        y_ref[2 * j + 1] = y[:, LANES:].astype(BF16)


def _s5(us3, k2, so3, si2, apw, blocks_per_seq, layer):
    t_blk, n_blocks, d_s = us3.shape
    lb = d_s // LANES
    n_seq = n_blocks // blocks_per_seq
    n_in = t_blk * LANES
    n_state = apw.shape[3]
    w_map = lambda a, b: (layer * lb + a, 0, 0, 0)
    io_spec = pl.BlockSpec((t_blk, blocks_per_seq, LANES), lambda a, b: (0, b, a))
    return pl.pallas_call(
        _s5_kernel,
        grid=(lb, n_seq),
        in_specs=[
            io_spec,
            pl.BlockSpec((1,) + k2.shape[1:], w_map),
            pl.BlockSpec((1,) + so3.shape[1:], w_map),
            pl.BlockSpec((1,) + si2.shape[1:], w_map),
            pl.BlockSpec((1,) + apw.shape[1:], w_map),
        ],
        out_specs=io_spec,
        out_shape=jax.ShapeDtypeStruct(us3.shape, BF16),
        scratch_shapes=[
            pltpu.VMEM((n_in, n_in), BF16),
            pltpu.VMEM((n_in, n_state), BF16),
            pltpu.VMEM((n_state, n_in), BF16),
        ],
        compiler_params=_params(sem=("arbitrary", "arbitrary")),
        name="s5",
    )(us3, k2, so3, si2, apw)


def _back_kernel(x_ref, ya_ref, ys_ref, gw_ref, gb_ref, onb_ref, wo_ref, fnw_ref,
                 rw_ref, rb_ref,
                 x1_ref, h2_ref, gate_ref, er_ref, cnt_ref, carry_ref, ys_scr):
    i = pl.program_id(0)
    tm = x_ref.shape[0]
    d_g = ya_ref.shape[1]

    @pl.when(i == 0)
    def _():
        carry_ref[...] = jnp.zeros_like(carry_ref)

    n_lb = ys_scr.shape[0]
    for t in range(SSM_T):
        for q in range(n_lb):
            ys_scr[q, pl.ds(t, tm // SSM_T, stride=SSM_T), :] = (
                ys_ref[t, :, q * LANES:(q + 1) * LANES].astype(F32))
    y = _gelu(jnp.concatenate([ys_scr[q] for q in range(n_lb)], axis=1))
    z = jnp.dot(y.astype(BF16), gw_ref[...], preferred_element_type=F32) + gb_ref[...]
    yb = y * _sigmoid(z)
    ybn = _rms(yb, onb_ref[...]).astype(BF16)
    x1 = (x_ref[...]
          + jnp.dot(ya_ref[...], wo_ref[:d_g, :], preferred_element_type=F32)
          + jnp.dot(ybn, wo_ref[d_g:, :], preferred_element_type=F32))
    x1_ref[...] = x1
    h2 = _rms(x1, fnw_ref[...])
    _store_rows(h2_ref, h2)

    hh = h2.astype(BF16)
    hl = (h2 - hh.astype(F32)).astype(BF16)
    p_hi = jnp.dot(hh, rw_ref[...], preferred_element_type=F32)
    p_lo = jnp.dot(hl, rw_ref[:, :LANES], preferred_element_type=F32)
    logits = p_hi[:, :LANES] + p_lo + p_hi[:, LANES:] + rb_ref[...]
    lane = lax.broadcasted_iota(I32, (tm, LANES), 1)
    lane_f = lane.astype(F32)
    neg = jnp.float32(-jnp.inf)
    work = jnp.where(lane < N_EXPERTS, logits, neg)
    vals, hots, idxs = [], [], []
    for k in range(TOP_K):
        m = jnp.max(work, axis=-1, keepdims=True)
        idx = jnp.min(jnp.where(work == m, lane_f, float(LANES)), axis=-1, keepdims=True)
        hot = lane_f == idx
        vals.append(m)
        hots.append(hot)
        idxs.append(idx.astype(I32))
        work = jnp.where(hot, neg, work)
    exps = [jnp.exp(v - vals[0]) for v in vals]
    denom = exps[0] + exps[1] + exps[2] + exps[3]
    gate = jnp.zeros((tm, LANES), F32)
    for k in range(TOP_K):
        gate = jnp.where(lane == k, exps[k] / denom, gate)
    gate_ref[...] = gate

    sel = (hots[0] | hots[1] | hots[2] | hots[3])
    sel_f = jnp.where(sel, 1.0, 0.0)
    r_i = lax.broadcasted_iota(I32, (tm, tm), 0)
    c_i = lax.broadcasted_iota(I32, (tm, tm), 1)
    lower = jnp.where(r_i > c_i, 1.0, 0.0).astype(BF16)
    cum = jnp.dot(lower, sel_f.astype(BF16), preferred_element_type=F32) + carry_ref[...]
    er = jnp.zeros((tm, LANES), I32)
    for k in range(TOP_K):
        rk = jnp.sum(jnp.where(hots[k], cum, 0.0), axis=-1, keepdims=True)
        er = jnp.where(lane == k, idxs[k], er)
        er = jnp.where(lane == TOP_K + k, rk.astype(I32), er)
    carry_ref[...] = carry_ref[...] + jnp.sum(sel_f, axis=0, keepdims=True)
    cnt_ref[...] = carry_ref[...]
    er_ref[0] = jnp.transpose(er)[:SUBLANES, :]


def _back(x2, ya, ys3, glu_w, glu_b, on_b, w_out, fn_w, router_w, router_b):
    n, d = x2.shape
    d_g = ya.shape[1]
    d_s = ys3.shape[2]
    tm = ROW_TILE
    nt = n // tm
    const2 = lambda i: (0, 0)
    rw = jnp.zeros((d, LANES), F32).at[:, :N_EXPERTS].set(router_w.astype(F32))
    rw_hi = rw.astype(BF16)
    rw_lo = (rw - rw_hi.astype(F32)).astype(BF16)
    rw_split = jnp.concatenate([rw_hi, rw_lo], axis=1)
    rb = jnp.zeros((1, LANES), F32).at[0, :N_EXPERTS].set(router_b.astype(F32))
    tok_tile = lambda i: (i, 0)
    return pl.pallas_call(
        _back_kernel,
        grid=(nt,),
        in_specs=[
            pl.BlockSpec((tm, d), tok_tile),
            pl.BlockSpec((tm, d_g), tok_tile),
            pl.BlockSpec((SSM_T, tm // SSM_T, d_s), lambda i: (0, i, 0)),
            pl.BlockSpec((d_s, d_s), const2),
            pl.BlockSpec((1, d_s), const2),
            pl.BlockSpec((1, d_s), const2),
            pl.BlockSpec((d_g + d_s, d), const2),
            pl.BlockSpec((1, d), const2),
            pl.BlockSpec((d, 2 * LANES), const2),
            pl.BlockSpec((1, LANES), const2),
        ],
        out_specs=[
            pl.BlockSpec((tm, d), tok_tile),
            pl.BlockSpec((tm * d // LANES, LANES), tok_tile),
            pl.BlockSpec((tm, LANES), tok_tile),
            pl.BlockSpec((1, SUBLANES, tm), lambda i: (i, 0, 0)),
            pl.BlockSpec((1, LANES), const2),
        ],
        out_shape=[
            jax.ShapeDtypeStruct((n, d), F32),
            jax.ShapeDtypeStruct((n * d // LANES, LANES), F32),
            jax.ShapeDtypeStruct((n, LANES), F32),
            jax.ShapeDtypeStruct((nt, SUBLANES, tm), I32),
            jax.ShapeDtypeStruct((1, LANES), F32),
        ],
        scratch_shapes=[pltpu.VMEM((1, LANES), F32), pltpu.VMEM((d_s // LANES, tm, LANES), F32)],
        compiler_params=_params(sem=("arbitrary",)),
        name="back",
    )(x2, ya, ys3, glu_w.astype(BF16), glu_b.reshape(1, d_s), on_b.reshape(1, d_s),
      w_out.astype(BF16), fn_w.reshape(1, d), rw_split, rb)


def _routing_tables(er, counts, n, row_sl):
    te = EXPERT_TILE
    n_blocks = n * TOP_K // te + N_EXPERTS
    start = jnp.cumsum(counts) - counts
    e_ids = jnp.arange(N_EXPERTS, dtype=I32)
    e_sel = er[:, :TOP_K, :, None] == e_ids
    dest = jnp.sum(jnp.where(e_sel, start, 0), axis=-1) + er[:, TOP_K:2 * TOP_K, :]
    dest = (dest * row_sl).reshape(dest.shape[0], -1)
    nb = (counts + te - 1) // te
    cum = jnp.cumsum(nb)
    total = cum[-1]
    j = jnp.arange(n_blocks, dtype=I32)
    e_j = jnp.sum((cum[None, :] <= jnp.minimum(j, total - 1)[:, None]).astype(I32), axis=1)
    valid = (j < total).astype(I32)
    own = e_ids[None, :] == e_j[:, None]
    first_j = jnp.sum(jnp.where(own, cum - nb, 0), axis=1)
    count_j = jnp.sum(jnp.where(own, counts, 0), axis=1)
    start_j = jnp.sum(jnp.where(own, start, 0), axis=1)
    first = valid * (j == first_j).astype(I32)
    later = (e_ids[None, :] > e_j[:, None]) & (nb > 0)[None, :]
    nxt = jnp.min(jnp.where(later, e_ids[None, :], N_EXPERTS), axis=1)
    nxt = jnp.where(nxt < N_EXPERTS, nxt, -1)
    rows = valid * jnp.clip(count_j - (j - first_j) * te, 0, te)
    row_off = (start_j + (j - first_j) * te) * row_sl
    return (dest.astype(I32), e_j.astype(I32), rows.astype(I32), first, nxt.astype(I32),
            row_off.astype(I32), total.reshape(1).astype(I32))


def _dispatch_kernel(dest_hbm, h_ref, xs_hbm, idx_smem, zbuf, sem_idx, sem_rows, sem_z, *, row_sl):
    i = pl.program_id(0)
    nt = pl.num_programs(0)
    tm = h_ref.shape[0] // row_sl
    blk = zbuf.shape[0]
    slot = i % 2

    def idx_copy(t, s):
        n_idx = TOP_K * tm
        dst = idx_smem.at[pl.ds(pl.multiple_of(s * n_idx, n_idx), n_idx)]
        return pltpu.make_async_copy(dest_hbm.at[t], dst, sem_idx.at[s])

    @pl.when(i == 0)
    def _():
        zbuf[...] = jnp.zeros_like(zbuf)
        cp = pltpu.make_async_copy(zbuf, xs_hbm.at[pl.ds(xs_hbm.shape[0] - blk, blk)], sem_z)
        cp.start()
        cp.wait()
        idx_copy(0, 0).start()

    @pl.when(i + 1 < nt)
    def _():
        idx_copy(i + 1, 1 - slot).start()

    idx_copy(i, slot).wait()

    def body(j, carry):
        n0 = j * DMA_UNROLL
        i0 = slot * (TOP_K * tm) + n0
        for u in range(DMA_UNROLL):
            src = h_ref.at[pl.ds(pl.multiple_of((n0 + u) * row_sl, row_sl), row_sl)]
            for k in range(TOP_K):
                dst = xs_hbm.at[pl.ds(pl.multiple_of(idx_smem[i0 + (k * tm + u)], row_sl), row_sl)]
                pltpu.make_async_copy(src, dst, sem_rows.at[k]).start(priority=k % 2)
        return carry
    lax.fori_loop(0, tm // DMA_UNROLL, body, 0)
    for k in range(TOP_K):
        pltpu.make_async_copy(h_ref, xs_hbm.at[pl.ds(0, tm * row_sl)], sem_rows.at[k]).wait()


def _dispatch(dest, h2t, n_rows, row_sl):
    tm = ROW_TILE
    nt = h2t.shape[0] // (tm * row_sl)
    return pl.pallas_call(
        functools.partial(_dispatch_kernel, row_sl=row_sl),
        grid=(nt,),
        in_specs=[
            pl.BlockSpec(memory_space=pl.ANY),
            pl.BlockSpec((tm * row_sl, LANES), lambda i: (i, 0)),
        ],
        out_specs=pl.BlockSpec(memory_space=pl.ANY),
        out_shape=jax.ShapeDtypeStruct((n_rows * row_sl, LANES), F32),
        scratch_shapes=[
            pltpu.SMEM((2 * TOP_K * tm,), I32),
            pltpu.VMEM((EXPERT_TILE * row_sl, LANES), F32),
            pltpu.SemaphoreType.DMA((2,)),
            pltpu.SemaphoreType.DMA((TOP_K,)),
            pltpu.SemaphoreType.DMA,
        ],
        compiler_params=_params(sem=("arbitrary",)),
        name="dispatch",
    )(dest, h2t)


def _expert_kernel(be_ref, nv_ref, first_ref, next_ref, off_ref, tot_ref,
                   xs_hbm, wgu_hbm, bgu_ref, wdn_hbm, bdn_ref, ys_hbm,
                   xbuf, ybuf, stage_gu, stage_dn, wgu_bf, wdn_bf, act_scr, sem_x, sem_y, sem_w, *, layer):
    i = pl.program_id(0)
    d, d_gu = wgu_bf.shape
    d_e = wdn_bf.shape[0]
    nsl = d // LANES
    blk = xbuf.shape[1]
    te = blk // nsl
    slot = i % 2
    total = tot_ref[0]

    def weight_copies(e):
        row = layer * N_EXPERTS + e
        return (pltpu.make_async_copy(wgu_hbm.at[row], stage_gu, sem_w.at[0]),
                pltpu.make_async_copy(wdn_hbm.at[row], stage_dn, sem_w.at[1]))

    def x_copy(j, s):
        src = xs_hbm.at[pl.ds(pl.multiple_of(off_ref[j], nsl), blk)]
        return pltpu.make_async_copy(src, xbuf.at[s], sem_x.at[s])

    def y_copy(j, s):
        dst = ys_hbm.at[pl.ds(pl.multiple_of(off_ref[j], nsl), blk)]
        return pltpu.make_async_copy(ybuf.at[s], dst, sem_y.at[s])

    @pl.when(i == 0)
    def _():
        for cp in weight_copies(be_ref[0]):
            cp.start(priority=1)
        x_copy(0, 0).start()
        ybuf[...] = jnp.zeros_like(ybuf)
        cp = pltpu.make_async_copy(ybuf.at[1], ys_hbm.at[pl.ds(ys_hbm.shape[0] - blk, blk)], sem_y.at[1])
        cp.start()
        cp.wait()

    nv = nv_ref[i]
    sizes = [te // 4, te // 2, te]
    is_first = first_ref[i] > 0
    is_full = nv > sizes[-2]

    def round_weights(cols=None):
        rows = 128
        if cols is None:
            for r in range(0, d, rows):
                wgu_bf[r:r + rows, :] = stage_gu[r:r + rows, :].astype(BF16)
            for r in range(0, d_e, rows):
                wdn_bf[r:r + rows, :] = stage_dn[r:r + rows, :].astype(BF16)
        else:
            lo, hi = cols
            for r in range(0, d, rows):
                wgu_bf[r:r + rows, lo:hi] = stage_gu[r:r + rows, lo:hi].astype(BF16)
                wgu_bf[r:r + rows, d_e + lo:d_e + hi] = stage_gu[r:r + rows, d_e + lo:d_e + hi].astype(BF16)

    @pl.when(is_first)
    def _():
        for cp in weight_copies(be_ref[i]):
            cp.wait()

    @pl.when(is_first & jnp.logical_not(is_full))
    def _():
        round_weights()

    def run_rows(m, rounding=False):
        x_ref = xbuf.at[slot]
        x = jnp.concatenate([_load_row_slab(x_ref, s, m, nsl).astype(BF16) for s in range(nsl)], axis=1)
        for c in range(d_e // EXPERT_CHUNK):
            lo = c * EXPERT_CHUNK
            hi = lo + EXPERT_CHUNK
            if rounding:
                round_weights((lo, hi))
            glu = jnp.dot(x, wgu_bf[:, lo:hi], preferred_element_type=F32) + bgu_ref[0, :, lo:hi]
            lin = (jnp.dot(x, wgu_bf[:, d_e + lo:d_e + hi], preferred_element_type=F32)
                   + bgu_ref[0, :, d_e + lo:d_e + hi])
            glu = jnp.minimum(glu, SWIGLU_LIMIT)
            lin = jnp.clip(lin, -SWIGLU_LIMIT, SWIGLU_LIMIT)
            act_scr[:m, lo:hi] = (glu * _sigmoid(SWIGLU_ALPHA * glu) * (lin + 1.0)).astype(BF16)
        if rounding:
            for r in range(0, d_e, 128):
                wdn_bf[r:r + 128, :] = stage_dn[r:r + 128, :].astype(BF16)
        y = jnp.dot(act_scr[:m, :], wdn_bf[...], preferred_element_type=F32) + bdn_ref[0]
        _store_rows(ybuf.at[slot], y)

    @pl.when(nv > 0)
    def _():
        x_copy(i, slot).wait()

        @pl.when(i + 1 < total)
        def _():
            x_copy(i + 1, 1 - slot).start()

    lower = 0
    for m in sizes[:-1]:
        @pl.when((nv > lower) & (nv <= m))
        def _(m=m):
            run_rows(m)
        lower = m

    @pl.when(is_full & is_first)
    def _():
        run_rows(te, rounding=True)

    @pl.when(is_full & jnp.logical_not(is_first))
    def _():
        run_rows(te)

    @pl.when(is_first & (next_ref[i] >= 0))
    def _():
        for cp in weight_copies(next_ref[i]):
            cp.start(priority=1)

    @pl.when(nv > 0)
    def _():
        @pl.when(i >= 1)
        def _():
            y_copy(i - 1, 1 - slot).wait()
        y_copy(i, slot).start()

        @pl.when(i == total - 1)
        def _():
            y_copy(i, slot).wait()


def _experts(block_expert, block_rows, block_first, block_next, block_off, total, xs,
             w_gu, b_gu, w_dn, b_dn, layer):
    d = w_gu.shape[2]
    n_blocks = block_expert.shape[0]
    blk = EXPERT_TILE * d // LANES
    d_gu = w_gu.shape[3]
    d_e = w_dn.shape[2]
    e_map3 = lambda i, be, nv, bf, bn, off, tot: (layer * N_EXPERTS + be[i], 0, 0)
    w_gu = w_gu.reshape((-1,) + w_gu.shape[2:])
    w_dn = w_dn.reshape((-1,) + w_dn.shape[2:])
    b_gu = b_gu.reshape(-1, 1, d_gu)
    b_dn = b_dn.reshape(-1, 1, d)
    grid_spec = pltpu.PrefetchScalarGridSpec(
        num_scalar_prefetch=6,
        grid=(n_blocks,),
        in_specs=[
            pl.BlockSpec(memory_space=pl.ANY),
            pl.BlockSpec(memory_space=pl.ANY),
            pl.BlockSpec((1, 1, d_gu), e_map3),
            pl.BlockSpec(memory_space=pl.ANY),
            pl.BlockSpec((1, 1, d), e_map3),
        ],
        out_specs=pl.BlockSpec(memory_space=pl.ANY),
        scratch_shapes=[
            pltpu.VMEM((2, blk, LANES), F32),
            pltpu.VMEM((2, blk, LANES), F32),
            pltpu.VMEM((d, d_gu), F32),
            pltpu.VMEM((d_e, d), F32),
            pltpu.VMEM((d, d_gu), BF16),
            pltpu.VMEM((d_e, d), BF16),
            pltpu.VMEM((EXPERT_TILE, d_e), BF16),
            pltpu.SemaphoreType.DMA((2,)),
            pltpu.SemaphoreType.DMA((2,)),
            pltpu.SemaphoreType.DMA((2,)),
        ],
    )
    return pl.pallas_call(
        functools.partial(_expert_kernel, layer=layer),
        grid_spec=grid_spec,
        out_shape=jax.ShapeDtypeStruct(xs.shape, F32),
        compiler_params=_params(sem=("arbitrary",)),
        name="experts",
    )(block_expert, block_rows, block_first, block_next, block_off, total, xs, w_gu, b_gu, w_dn, b_dn)


def _combine_kernel(dest_hbm, x1_ref, g_ref, fw_ref, ys_hbm, o_ref,
                    idx_smem, ybuf, sem_idx, sem_rows, *, final_norm):
    i = pl.program_id(0)
    nt = pl.num_programs(0)
    tm, d = x1_ref.shape
    nsl = d // LANES
    slot = i % 2

    def idx_copy(t, s):
        n_idx = TOP_K * tm
        dst = idx_smem.at[pl.ds(pl.multiple_of(s * n_idx, n_idx), n_idx)]
        return pltpu.make_async_copy(dest_hbm.at[t], dst, sem_idx.at[s])

    def issue_rows(s):
        def body(j, carry):
            n0 = j * DMA_UNROLL
            i0 = s * (TOP_K * tm) + n0
            for u in range(DMA_UNROLL):
                for k in range(TOP_K):
                    src = ys_hbm.at[pl.ds(pl.multiple_of(idx_smem[i0 + (k * tm + u)], nsl), nsl)]
                    dst = ybuf.at[s, k, pl.ds(pl.multiple_of((n0 + u) * nsl, nsl), nsl)]
                    pltpu.make_async_copy(src, dst, sem_rows.at[s, k]).start(priority=k % 2)
            return carry
        lax.fori_loop(0, tm // DMA_UNROLL, body, 0)

    def wait_rows(s):
        for k in range(TOP_K):
            pltpu.make_async_copy(ys_hbm.at[pl.ds(0, tm * nsl)], ybuf.at[s, k], sem_rows.at[s, k]).wait()

    @pl.when(i == 0)
    def _():
        cp = idx_copy(0, 0)
        cp.start()
        cp.wait()
        issue_rows(0)

        @pl.when(nt > 1)
        def _():
            idx_copy(1, 1).start()

    @pl.when(i + 1 < nt)
    def _():
        idx_copy(i + 1, 1 - slot).wait()
        issue_rows(1 - slot)

        @pl.when(i + 2 < nt)
        def _():
            idx_copy(i + 2, slot).start()

    wait_rows(slot)
    for r0 in range(0, tm, COMBINE_ROWS):
        g = g_ref[r0:r0 + COMBINE_ROWS, :]
        gk = [jnp.broadcast_to(g[:, k:k + 1], (COMBINE_ROWS, LANES)) for k in range(TOP_K)]
        slabs = []
        for s in range(nsl):
            acc = x1_ref[r0:r0 + COMBINE_ROWS, s * LANES:(s + 1) * LANES]
            for k in range(TOP_K):
                acc = acc + gk[k] * _load_row_slab(ybuf.at[slot, k], s, COMBINE_ROWS, nsl, r0)
            slabs.append(acc)
        out = jnp.concatenate(slabs, axis=1)
        if final_norm:
            out = _rms(out, fw_ref[...])
        o_ref[r0:r0 + COMBINE_ROWS, :] = out


def _combine(x1, ys, gate, dest, final_w, final_norm):
    n, d = x1.shape
    tm = ROW_TILE
    nt = n // tm
    tok_tile = lambda i: (i, 0)
    return pl.pallas_call(
        functools.partial(_combine_kernel, final_norm=final_norm),
        grid=(nt,),
        in_specs=[
            pl.BlockSpec(memory_space=pl.ANY),
            pl.BlockSpec((tm, d), tok_tile),
            pl.BlockSpec((tm, LANES), tok_tile),
            pl.BlockSpec((1, d), lambda i: (0, 0)),
            pl.BlockSpec(memory_space=pl.ANY),
        ],
        out_specs=pl.BlockSpec((tm, d), tok_tile),
        out_shape=jax.ShapeDtypeStruct((n, d), F32),
        scratch_shapes=[
            pltpu.SMEM((2 * TOP_K * tm,), I32),
            pltpu.VMEM((2, TOP_K, tm * d // LANES, LANES), F32),
            pltpu.SemaphoreType.DMA((2,)),
            pltpu.SemaphoreType.DMA((2, TOP_K)),
        ],
        compiler_params=_params(sem=("arbitrary",)),
        name="combine",
    )(dest, x1, gate, final_w.reshape(1, d), ys)


def kernel(x, attn_norm_w, w_in, sgu_ln_w, sgu_ln_b, sgu_w, sgu_b, ssm_a_re, ssm_a_im, ssm_b_re, ssm_b_im, ssm_c_re, ssm_c_im, ssm_d, ssm_log_dt, ssm_glu_w, ssm_glu_b, out_norm_a, out_norm_b, w_out, ffn_norm_w, router_w, router_b, w_gate_up, b_gate_up, w_down, b_down, final_norm_w):
    b, l, d = x.shape
    n = b * l
    depth = w_in.shape[0]
    blocks_per_seq = l // SSM_T
    n_steps = max(1, (blocks_per_seq - 1).bit_length())
    n_rows = n * TOP_K + EXPERT_TILE
    x2 = x.reshape(n, d).astype(F32)
    k2, so3, si2, apw = _s5_tables(ssm_a_re, ssm_a_im, ssm_b_re, ssm_b_im, ssm_c_re, ssm_c_im, ssm_d,
                                   ssm_log_dt, n_steps)
    for layer in range(depth):
        ya, us3 = _front(x2, attn_norm_w[layer], w_in[layer], sgu_ln_w[layer], sgu_ln_b[layer],
                         sgu_w[layer], sgu_b[layer], out_norm_a[layer])
        ys3 = _s5(us3, k2, so3, si2, apw, blocks_per_seq, layer)
        x1, h2, gate, er, cnt = _back(x2, ya, ys3, ssm_glu_w[layer], ssm_glu_b[layer],
                                      out_norm_b[layer], w_out[layer], ffn_norm_w[layer],
                                      router_w[layer], router_b[layer])
        counts = cnt[0, :N_EXPERTS].astype(I32)
        dest, block_expert, block_rows, block_first, block_next, block_off, total = _routing_tables(
            er, counts, n, d // LANES)
        xs = _dispatch(dest, h2, n_rows, d // LANES)
        ys = _experts(block_expert, block_rows, block_first, block_next, block_off, total, xs,
                      w_gate_up, b_gate_up, w_down, b_down, layer)
        x2 = _combine(x1, ys, gate, dest, final_norm_w, final_norm=(layer == depth - 1))
    return x2.reshape(b, l, d).astype(x.dtype)
```

```python
import functools
import math

import jax
import jax.numpy as jnp
from jax import lax
from jax.experimental import pallas as pl
from jax.experimental.pallas import tpu as pltpu

F32 = jnp.float32
BF16 = jnp.bfloat16
I32 = jnp.int32

EPS = 1e-5
N_HEADS = 4
CHUNK = 128
SSM_GROUP = 16
SSM_STATE = 64
SSM_T = 16
N_EXPERTS = 32
TOP_K = 4
SWIGLU_LIMIT = 7.0
SWIGLU_ALPHA = 1.702
LANES = 128
SUBLANES = 8
GROUPS_PER_LANE_BLOCK = LANES // SSM_GROUP

ROW_TILE = 512
FRONT_TILE = 1024
S5_SEQS = 2
EXPERT_TILE = 1024
EXPERT_CHUNK = 256
DMA_UNROLL = 8
COMBINE_ROWS = 64
VMEM_LIMIT = 56 * 1024 * 1024


def _gelu(x):
    return 0.5 * x * (1.0 + jnp.tanh(math.sqrt(2.0 / math.pi) * (x + 0.044715 * (x * x * x))))


def _sigmoid(x):
    return 1.0 / (1.0 + jnp.exp(-x))


def _rms(x, w):
    return x * lax.rsqrt(jnp.mean(x * x, axis=-1, keepdims=True) + EPS) * w


def _store_rows(ref, val, row0=0):
    rows, d = val.shape
    nsl = d // LANES
    for s in range(nsl):
        ref[pl.ds(row0 * nsl + s, rows, stride=nsl), :] = val[:, s * LANES:(s + 1) * LANES]


def _load_row_slab(ref, s, rows, nsl, row0=0):
    return ref[pl.ds(row0 * nsl + s, rows, stride=nsl), :]


def _params(**kw):
    return pltpu.CompilerParams(dimension_semantics=kw.pop("sem"), vmem_limit_bytes=VMEM_LIMIT, **kw)


def _front_kernel(x_ref, nw_ref, win_ref, lnw_ref, lnb_ref, ws_ref, bst_ref, ona_ref,
                  ya_ref, us_ref, mixed_ref, us_scr):
    d_g = ya_ref.shape[1]
    hd = d_g // N_HEADS
    tm = x_ref.shape[0]
    x = x_ref[...]
    h = _rms(x, nw_ref[...]).astype(BF16)
    proj = jnp.dot(h, win_ref[...], preferred_element_type=F32)
    u = _gelu(proj[:, :d_g])
    v = _gelu(proj[:, d_g:2 * d_g])
    n_lb = us_scr.shape[0]
    for q in range(n_lb):
        us_scr[q] = proj[:, 2 * d_g + q * LANES:2 * d_g + (q + 1) * LANES]
    mu = jnp.mean(v, axis=-1, keepdims=True)
    vc = v - mu
    var = jnp.mean(vc * vc, axis=-1, keepdims=True)
    vb = (vc * lax.rsqrt(var + EPS) * lnw_ref[...] + lnb_ref[...]).astype(BF16)
    row = lax.broadcasted_iota(I32, (CHUNK, CHUNK), 0)
    col = lax.broadcasted_iota(I32, (CHUNK, CHUNK), 1)
    causal = row >= col
    for hh in range(N_HEADS):
        w = jnp.where(causal, ws_ref[hh], 0.0).astype(BF16)
        bias = bst_ref[:, hh:hh + 1]
        for c in range(tm // CHUNK):
            vv = vb[c * CHUNK:(c + 1) * CHUNK, hh * hd:(hh + 1) * hd]
            m = jnp.dot(w, vv, preferred_element_type=F32) + bias
            mixed_ref[c * CHUNK:(c + 1) * CHUNK, hh * hd:(hh + 1) * hd] = m
    ya = u * mixed_ref[...]
    ya_ref[...] = _rms(ya, ona_ref[...]).astype(BF16)
    for s in range(SSM_T):
        for q in range(n_lb):
            us_ref[s, :, q * LANES:(q + 1) * LANES] = (
                us_scr[q, pl.ds(s, tm // SSM_T, stride=SSM_T), :].astype(BF16))


def _front(x2, nw, w_in, ln_w, ln_b, w_s, b_s, on_a):
    n, d = x2.shape
    d_g = ln_w.shape[0]
    d_s = w_in.shape[1] - 2 * d_g
    tm = FRONT_TILE
    const2 = lambda i: (0, 0)
    return pl.pallas_call(
        _front_kernel,
        grid=(n // tm,),
        in_specs=[
            pl.BlockSpec((tm, d), lambda i: (i, 0)),
            pl.BlockSpec((1, d), const2),
            pl.BlockSpec(w_in.shape, const2),
            pl.BlockSpec((1, d_g), const2),
            pl.BlockSpec((1, d_g), const2),
            pl.BlockSpec(w_s.shape, lambda i: (0, 0, 0)),
            pl.BlockSpec((CHUNK, N_HEADS), const2),
            pl.BlockSpec((1, d_g), const2),
        ],
        out_specs=[
            pl.BlockSpec((tm, d_g), lambda i: (i, 0)),
            pl.BlockSpec((SSM_T, tm // SSM_T, d_s), lambda i: (0, i, 0)),
        ],
        out_shape=[
            jax.ShapeDtypeStruct((n, d_g), BF16),
            jax.ShapeDtypeStruct((SSM_T, n // SSM_T, d_s), BF16),
        ],
        scratch_shapes=[pltpu.VMEM((tm, d_g), F32), pltpu.VMEM((d_s // LANES, tm, LANES), F32)],
        compiler_params=_params(sem=("arbitrary",)),
        name="front",
    )(x2, nw.reshape(1, d), w_in.astype(BF16), ln_w.reshape(1, d_g), ln_b.reshape(1, d_g),
      w_s, b_s.T, on_a.reshape(1, d_g))


def _s5_tables(a_re, a_im, b_re, b_im, c_re, c_im, d, log_dt, n_steps):
    depth, g, p = a_re.shape
    hch = b_re.shape[-1]
    t = SSM_T
    gl = GROUPS_PER_LANE_BLOCK
    r = depth * (g // gl)
    a = lax.complex(a_re.astype(F32), a_im.astype(F32)).reshape(r, gl, p)
    dt = jnp.exp(log_dt.astype(F32)).reshape(r, gl, 1)
    dta = dt * a
    a_bar = jnp.exp(dta)
    b = lax.complex(b_re.astype(F32), b_im.astype(F32)).reshape(r, gl, p, hch)
    b_bar = ((a_bar - 1.0) / a)[..., None] * b
    c = lax.complex(c_re.astype(F32), c_im.astype(F32)).reshape(r, gl, hch, p)
    lags = jnp.arange(t + 1, dtype=F32)
    pw = jnp.exp(lags[None, :, None, None] * dta.transpose(0, 2, 1)[:, None])
    pwx = jnp.repeat(pw, hch, axis=-1)
    bb = b_bar.transpose(0, 2, 1, 3).reshape(r, p, gl * hch)
    cc = c.transpose(0, 3, 1, 2).reshape(r, p, gl * hch)
    q = pwx[:, :t] * bb[:, None]
    si = pwx[:, 1:] * cc[:, None]
    k2 = (jnp.einsum('rlpx,rpy->rlxy', q.real, cc.real, precision=lax.Precision.HIGHEST)
          - jnp.einsum('rlpx,rpy->rlxy', q.imag, cc.imag, precision=lax.Precision.HIGHEST))
    lane_g = jnp.arange(gl * hch) // hch
    k2 = jnp.where(lane_g[:, None] == lane_g[None, :], k2, 0.0)
    skip = jnp.eye(gl * hch, dtype=F32) * d.astype(F32).reshape(r, 1, gl * hch)
    k2 = k2.at[:, 0].add(skip)
    so3 = jnp.swapaxes(jnp.concatenate([q.real, q.imag], axis=2), 2, 3)
    si2 = jnp.concatenate([si.real, -si.imag], axis=2)
    steps = SSM_T * (2.0 ** jnp.arange(n_steps, dtype=F32))
    ap = jnp.exp(steps[None, :, None, None] * dta[:, None]).reshape(r, n_steps, gl * p)
    mul_same = jnp.concatenate([ap.real, ap.real], axis=-1)
    mul_swap = jnp.concatenate([-ap.imag, ap.imag], axis=-1)
    apw = jnp.stack([mul_same, mul_swap], axis=2)
    return k2.astype(BF16), so3.astype(BF16), si2.astype(BF16), apw


def _s5_kernel(u_ref, k2_ref, so3_ref, si2_ref, ap_ref, y_ref, wk_scr, wso_scr, wsi_scr, *, blocks_per_seq):
    t_blk = u_ref.shape[0]
    rows = u_ref.shape[1]
    n_state = wso_scr.shape[1]
    two_p = so3_ref.shape[3]
    p = two_p // 2
    half = n_state // 2

    @pl.when(pl.program_id(1) == 0)
    def _():
        for s in range(t_blk):
            if s > 0:
                wk_scr[s * LANES:(s + 1) * LANES, :s * LANES] = jnp.zeros((LANES, s * LANES), BF16)
            for t in range(s, t_blk):
                wk_scr[s * LANES:(s + 1) * LANES, t * LANES:(t + 1) * LANES] = k2_ref[0, t - s]
        r_e = lax.broadcasted_iota(I32, (two_p, n_state), 0)
        c_e = lax.broadcasted_iota(I32, (two_p, n_state), 1)
        spread = jnp.where((r_e // p == c_e // half) & (r_e % p == c_e % p), 1.0, 0.0).astype(BF16)
        r_g = lax.broadcasted_iota(I32, (LANES, n_state), 0) // SSM_GROUP
        c_g = (lax.broadcasted_iota(I32, (LANES, n_state), 1) % half) // p
        own = r_g == c_g
        for s in range(t_blk):
            full = jnp.dot(so3_ref[0, t_blk - 1 - s], spread, preferred_element_type=F32)
            wso_scr[s * LANES:(s + 1) * LANES, :] = jnp.where(own, full, 0.0).astype(BF16)
        r_e = lax.broadcasted_iota(I32, (n_state, two_p), 0)
        c_e = lax.broadcasted_iota(I32, (n_state, two_p), 1)
        gather = jnp.where((c_e // p == r_e // half) & (c_e % p == r_e % p), 1.0, 0.0).astype(BF16)
        r_g = (lax.broadcasted_iota(I32, (n_state, LANES), 0) % half) // p
        c_g = lax.broadcasted_iota(I32, (n_state, LANES), 1) // SSM_GROUP
        own = r_g == c_g
        for t in range(t_blk):
            full = jnp.dot(gather, si2_ref[0, t], preferred_element_type=F32)
            wsi_scr[:, t * LANES:(t + 1) * LANES] = jnp.where(own, full, 0.0).astype(BF16)

    xcat = jnp.concatenate([u_ref[s] for s in range(t_blk)], axis=1)
    x = jnp.dot(xcat, wso_scr[...], preferred_element_type=F32)
    pos = lax.broadcasted_iota(I32, (rows, n_state), 0) % blocks_per_seq
    for k in range(ap_ref.shape[1]):
        sh = 1 << k
        prev = pltpu.roll(x, sh, 0)
        prev_sw = pltpu.roll(prev, half, 1)
        upd = prev * ap_ref[0, k, 0:1, :] + prev_sw * ap_ref[0, k, 1:2, :]
        x = x + jnp.where(pos >= sh, upd, 0.0)
    xin = jnp.where(pos >= 1, pltpu.roll(x, 1, 0), 0.0).astype(BF16)
    for j in range(t_blk // 2):
        k_hi = (2 * j + 2) * LANES
        lo = 2 * j * LANES
        y = jnp.dot(xcat[:, :k_hi], wk_scr[:k_hi, lo:lo + 2 * LANES], preferred_element_type=F32)
        y = y + jnp.dot(xin, wsi_scr[:, lo:lo + 2 * LANES], preferred_element_type=F32)
        y_ref[2 * j] = y[:, :LANES].astype(BF16)
        y_ref[2 * j + 1] = y[:, LANES:].astype(BF16)


def _s5(us3, k2, so3, si2, apw, blocks_per_seq, layer):
    t_blk, n_blocks, d_s = us3.shape
    lb = d_s // LANES
    rows = S5_SEQS * blocks_per_seq
    n_steps = n_blocks // rows
    n_in = t_blk * LANES
    n_state = apw.shape[3]
    w_map = lambda a, b: (layer * lb + a, 0, 0, 0)
    io_spec = pl.BlockSpec((t_blk, rows, LANES), lambda a, b: (0, b, a))
    return pl.pallas_call(
        functools.partial(_s5_kernel, blocks_per_seq=blocks_per_seq),
        grid=(lb, n_steps),
        in_specs=[
            io_spec,
            pl.BlockSpec((1,) + k2.shape[1:], w_map),
            pl.BlockSpec((1,) + so3.shape[1:], w_map),
            pl.BlockSpec((1,) + si2.shape[1:], w_map),
            pl.BlockSpec((1,) + apw.shape[1:], w_map),
        ],
        out_specs=io_spec,
        out_shape=jax.ShapeDtypeStruct(us3.shape, BF16),
        scratch_shapes=[
            pltpu.VMEM((n_in, n_in), BF16),
            pltpu.VMEM((n_in, n_state), BF16),
            pltpu.VMEM((n_state, n_in), BF16),
        ],
        compiler_params=_params(sem=("arbitrary", "arbitrary")),
        name="s5",
    )(us3, k2, so3, si2, apw)


def _back_kernel(x_ref, ya_ref, ys_ref, gw_ref, gb_ref, onb_ref, wo_ref, fnw_ref,
                 rw_ref, rb_ref,
                 x1_ref, h2_ref, gate_ref, er_ref, cnt_ref, carry_ref, ys_scr):
    i = pl.program_id(0)
    tm = x_ref.shape[0]
    d_g = ya_ref.shape[1]

    @pl.when(i == 0)
    def _():
        carry_ref[...] = jnp.zeros_like(carry_ref)

    n_lb = ys_scr.shape[0]
    for t in range(SSM_T):
        for q in range(n_lb):
            ys_scr[q, pl.ds(t, tm // SSM_T, stride=SSM_T), :] = (
                ys_ref[t, :, q * LANES:(q + 1) * LANES].astype(F32))
    y = _gelu(jnp.concatenate([ys_scr[q] for q in range(n_lb)], axis=1))
    z = jnp.dot(y.astype(BF16), gw_ref[...], preferred_element_type=F32) + gb_ref[...]
    yb = y * _sigmoid(z)
    ybn = _rms(yb, onb_ref[...]).astype(BF16)
    x1 = (x_ref[...]
          + jnp.dot(ya_ref[...], wo_ref[:d_g, :], preferred_element_type=F32)
          + jnp.dot(ybn, wo_ref[d_g:, :], preferred_element_type=F32))
    x1_ref[...] = x1
    h2 = _rms(x1, fnw_ref[...])
    _store_rows(h2_ref, h2)

    hh = h2.astype(BF16)
    hl = (h2 - hh.astype(F32)).astype(BF16)
    p_hi = jnp.dot(hh, rw_ref[...], preferred_element_type=F32)
    p_lo = jnp.dot(hl, rw_ref[:, :LANES], preferred_element_type=F32)
    logits = p_hi[:, :LANES] + p_lo + p_hi[:, LANES:] + rb_ref[...]
    lane = lax.broadcasted_iota(I32, (tm, LANES), 1)
    lane_f = lane.astype(F32)
    neg = jnp.float32(-jnp.inf)
    work = jnp.where(lane < N_EXPERTS, logits, neg)
    vals, hots, idxs = [], [], []
    for k in range(TOP_K):
        m = jnp.max(work, axis=-1, keepdims=True)
        idx = jnp.min(jnp.where(work == m, lane_f, float(LANES)), axis=-1, keepdims=True)
        hot = lane_f == idx
        vals.append(m)
        hots.append(hot)
        idxs.append(idx.astype(I32))
        work = jnp.where(hot, neg, work)
    exps = [jnp.exp(v - vals[0]) for v in vals]
    denom = exps[0] + exps[1] + exps[2] + exps[3]
    gate = jnp.zeros((tm, LANES), F32)
    for k in range(TOP_K):
        gate = jnp.where(lane == k, exps[k] / denom, gate)
    gate_ref[...] = gate

    sel = (hots[0] | hots[1] | hots[2] | hots[3])
    sel_f = jnp.where(sel, 1.0, 0.0)
    r_i = lax.broadcasted_iota(I32, (tm, tm), 0)
    c_i = lax.broadcasted_iota(I32, (tm, tm), 1)
    lower = jnp.where(r_i > c_i, 1.0, 0.0).astype(BF16)
    cum = jnp.dot(lower, sel_f.astype(BF16), preferred_element_type=F32) + carry_ref[...]
    er = jnp.zeros((tm, LANES), I32)
    for k in range(TOP_K):
        rk = jnp.sum(jnp.where(hots[k], cum, 0.0), axis=-1, keepdims=True)
        er = jnp.where(lane == k, idxs[k], er)
        er = jnp.where(lane == TOP_K + k, rk.astype(I32), er)
    carry_ref[...] = carry_ref[...] + jnp.sum(sel_f, axis=0, keepdims=True)
    cnt_ref[...] = carry_ref[...]
    er_ref[0] = jnp.transpose(er)[:SUBLANES, :]


def _back(x2, ya, ys3, glu_w, glu_b, on_b, w_out, fn_w, router_w, router_b):
    n, d = x2.shape
    d_g = ya.shape[1]
    d_s = ys3.shape[2]
    tm = ROW_TILE
    nt = n // tm
    const2 = lambda i: (0, 0)
    rw = jnp.zeros((d, LANES), F32).at[:, :N_EXPERTS].set(router_w.astype(F32))
    rw_hi = rw.astype(BF16)
    rw_lo = (rw - rw_hi.astype(F32)).astype(BF16)
    rw_split = jnp.concatenate([rw_hi, rw_lo], axis=1)
    rb = jnp.zeros((1, LANES), F32).at[0, :N_EXPERTS].set(router_b.astype(F32))
    tok_tile = lambda i: (i, 0)
    return pl.pallas_call(
        _back_kernel,
        grid=(nt,),
        in_specs=[
            pl.BlockSpec((tm, d), tok_tile),
            pl.BlockSpec((tm, d_g), tok_tile),
            pl.BlockSpec((SSM_T, tm // SSM_T, d_s), lambda i: (0, i, 0)),
            pl.BlockSpec((d_s, d_s), const2),
            pl.BlockSpec((1, d_s), const2),
            pl.BlockSpec((1, d_s), const2),
            pl.BlockSpec((d_g + d_s, d), const2),
            pl.BlockSpec((1, d), const2),
            pl.BlockSpec((d, 2 * LANES), const2),
            pl.BlockSpec((1, LANES), const2),
        ],
        out_specs=[
            pl.BlockSpec((tm, d), tok_tile),
            pl.BlockSpec((tm * d // LANES, LANES), tok_tile),
            pl.BlockSpec((tm, LANES), tok_tile),
            pl.BlockSpec((1, SUBLANES, tm), lambda i: (i, 0, 0)),
            pl.BlockSpec((1, LANES), const2),
        ],
        out_shape=[
            jax.ShapeDtypeStruct((n, d), F32),
            jax.ShapeDtypeStruct((n * d // LANES, LANES), F32),
            jax.ShapeDtypeStruct((n, LANES), F32),
            jax.ShapeDtypeStruct((nt, SUBLANES, tm), I32),
            jax.ShapeDtypeStruct((1, LANES), F32),
        ],
        scratch_shapes=[pltpu.VMEM((1, LANES), F32), pltpu.VMEM((d_s // LANES, tm, LANES), F32)],
        compiler_params=_params(sem=("arbitrary",)),
        name="back",
    )(x2, ya, ys3, glu_w.astype(BF16), glu_b.reshape(1, d_s), on_b.reshape(1, d_s),
      w_out.astype(BF16), fn_w.reshape(1, d), rw_split, rb)


def _routing_tables(er, counts, n, row_sl):
    te = EXPERT_TILE
    n_blocks = n * TOP_K // te + N_EXPERTS
    start = jnp.cumsum(counts) - counts
    e_ids = jnp.arange(N_EXPERTS, dtype=I32)
    e_sel = er[:, :TOP_K, :, None] == e_ids
    dest = jnp.sum(jnp.where(e_sel, start, 0), axis=-1) + er[:, TOP_K:2 * TOP_K, :]
    dest = (dest * row_sl).reshape(dest.shape[0], -1)
    nb = (counts + te - 1) // te
    cum = jnp.cumsum(nb)
    total = cum[-1]
    j = jnp.arange(n_blocks, dtype=I32)
    e_j = jnp.sum((cum[None, :] <= jnp.minimum(j, total - 1)[:, None]).astype(I32), axis=1)
    valid = (j < total).astype(I32)
    own = e_ids[None, :] == e_j[:, None]
    first_j = jnp.sum(jnp.where(own, cum - nb, 0), axis=1)
    count_j = jnp.sum(jnp.where(own, counts, 0), axis=1)
    start_j = jnp.sum(jnp.where(own, start, 0), axis=1)
    first = valid * (j == first_j).astype(I32)
    later = (e_ids[None, :] > e_j[:, None]) & (nb > 0)[None, :]
    nxt = jnp.min(jnp.where(later, e_ids[None, :], N_EXPERTS), axis=1)
    nxt = jnp.where(nxt < N_EXPERTS, nxt, -1)
    rows = valid * jnp.clip(count_j - (j - first_j) * te, 0, te)
    row_off = (start_j + (j - first_j) * te) * row_sl
    return (dest.astype(I32), e_j.astype(I32), rows.astype(I32), first, nxt.astype(I32),
            row_off.astype(I32), total.reshape(1).astype(I32))


def _dispatch_kernel(dest_hbm, h_ref, xs_hbm, idx_smem, zbuf, sem_idx, sem_rows, sem_z, *, row_sl):
    i = pl.program_id(0)
    nt = pl.num_programs(0)
    tm = h_ref.shape[0] // row_sl
    blk = zbuf.shape[0]
    slot = i % 2

    def idx_copy(t, s):
        n_idx = TOP_K * tm
        dst = idx_smem.at[pl.ds(pl.multiple_of(s * n_idx, n_idx), n_idx)]
        return pltpu.make_async_copy(dest_hbm.at[t], dst, sem_idx.at[s])

    @pl.when(i == 0)
    def _():
        zbuf[...] = jnp.zeros_like(zbuf)
        cp = pltpu.make_async_copy(zbuf, xs_hbm.at[pl.ds(xs_hbm.shape[0] - blk, blk)], sem_z)
        cp.start()
        cp.wait()
        idx_copy(0, 0).start()

    @pl.when(i + 1 < nt)
    def _():
        idx_copy(i + 1, 1 - slot).start()

    idx_copy(i, slot).wait()

    def body(j, carry):
        n0 = j * DMA_UNROLL
        i0 = slot * (TOP_K * tm) + n0
        for u in range(DMA_UNROLL):
            src = h_ref.at[pl.ds(pl.multiple_of((n0 + u) * row_sl, row_sl), row_sl)]
            for k in range(TOP_K):
                dst = xs_hbm.at[pl.ds(pl.multiple_of(idx_smem[i0 + (k * tm + u)], row_sl), row_sl)]
                pltpu.make_async_copy(src, dst, sem_rows.at[k]).start(priority=k % 2)
        return carry
    lax.fori_loop(0, tm // DMA_UNROLL, body, 0)
    for k in range(TOP_K):
        pltpu.make_async_copy(h_ref, xs_hbm.at[pl.ds(0, tm * row_sl)], sem_rows.at[k]).wait()


def _dispatch(dest, h2t, n_rows, row_sl):
    tm = ROW_TILE
    nt = h2t.shape[0] // (tm * row_sl)
    return pl.pallas_call(
        functools.partial(_dispatch_kernel, row_sl=row_sl),
        grid=(nt,),
        in_specs=[
            pl.BlockSpec(memory_space=pl.ANY),
            pl.BlockSpec((tm * row_sl, LANES), lambda i: (i, 0)),
        ],
        out_specs=pl.BlockSpec(memory_space=pl.ANY),
        out_shape=jax.ShapeDtypeStruct((n_rows * row_sl, LANES), F32),
        scratch_shapes=[
            pltpu.SMEM((2 * TOP_K * tm,), I32),
            pltpu.VMEM((EXPERT_TILE * row_sl, LANES), F32),
            pltpu.SemaphoreType.DMA((2,)),
            pltpu.SemaphoreType.DMA((TOP_K,)),
            pltpu.SemaphoreType.DMA,
        ],
        compiler_params=_params(sem=("arbitrary",)),
        name="dispatch",
    )(dest, h2t)


def _expert_kernel(be_ref, nv_ref, first_ref, next_ref, off_ref, tot_ref,
                   xs_hbm, wgu_hbm, bgu_ref, wdn_hbm, bdn_ref, ys_hbm,
                   xbuf, ybuf, stage_gu, stage_dn, wgu_bf, wdn_bf, act_scr, sem_x, sem_y, sem_w, *, layer):
    i = pl.program_id(0)
    d, d_gu = wgu_bf.shape
    d_e = wdn_bf.shape[0]
    nsl = d // LANES
    blk = xbuf.shape[1]
    te = blk // nsl
    slot = i % 2
    total = tot_ref[0]

    def weight_copies(e):
        row = layer * N_EXPERTS + e
        return (pltpu.make_async_copy(wgu_hbm.at[row], stage_gu, sem_w.at[0]),
                pltpu.make_async_copy(wdn_hbm.at[row], stage_dn, sem_w.at[1]))

    def x_copy(j, s):
        src = xs_hbm.at[pl.ds(pl.multiple_of(off_ref[j], nsl), blk)]
        return pltpu.make_async_copy(src, xbuf.at[s], sem_x.at[s])

    def y_copy(j, s):
        dst = ys_hbm.at[pl.ds(pl.multiple_of(off_ref[j], nsl), blk)]
        return pltpu.make_async_copy(ybuf.at[s], dst, sem_y.at[s])

    @pl.when(i == 0)
    def _():
        for cp in weight_copies(be_ref[0]):
            cp.start(priority=1)
        x_copy(0, 0).start()
        ybuf[...] = jnp.zeros_like(ybuf)
        cp = pltpu.make_async_copy(ybuf.at[1], ys_hbm.at[pl.ds(ys_hbm.shape[0] - blk, blk)], sem_y.at[1])
        cp.start()
        cp.wait()

    @pl.when(first_ref[i] > 0)
    def _():
        for cp in weight_copies(be_ref[i]):
            cp.wait()
        rows = 128
        for r in range(0, d, rows):
            wgu_bf[r:r + rows, :] = stage_gu[r:r + rows, :].astype(BF16)
        for r in range(0, d_e, rows):
            wdn_bf[r:r + rows, :] = stage_dn[r:r + rows, :].astype(BF16)

        @pl.when(next_ref[i] >= 0)
        def _():
            for cp in weight_copies(next_ref[i]):
                cp.start(priority=1)

    def run_rows(m):
        x_ref = xbuf.at[slot]
        x = jnp.concatenate([_load_row_slab(x_ref, s, m, nsl).astype(BF16) for s in range(nsl)], axis=1)
        for c in range(d_e // EXPERT_CHUNK):
            lo = c * EXPERT_CHUNK
            hi = lo + EXPERT_CHUNK
            glu = jnp.dot(x, wgu_bf[:, lo:hi], preferred_element_type=F32) + bgu_ref[0, :, lo:hi]
            lin = (jnp.dot(x, wgu_bf[:, d_e + lo:d_e + hi], preferred_element_type=F32)
                   + bgu_ref[0, :, d_e + lo:d_e + hi])
            glu = jnp.minimum(glu, SWIGLU_LIMIT)
            lin = jnp.clip(lin, -SWIGLU_LIMIT, SWIGLU_LIMIT)
            act_scr[:m, lo:hi] = (glu * _sigmoid(SWIGLU_ALPHA * glu) * (lin + 1.0)).astype(BF16)
        y = jnp.dot(act_scr[:m, :], wdn_bf[...], preferred_element_type=F32) + bdn_ref[0]
        _store_rows(ybuf.at[slot], y)

    nv = nv_ref[i]

    @pl.when(nv > 0)
    def _():
        x_copy(i, slot).wait()

        @pl.when(i + 1 < total)
        def _():
            x_copy(i + 1, 1 - slot).start()

    sizes = [te // 4, te // 2, te]
    lower = 0
    for m in sizes:
        @pl.when((nv > lower) & (nv <= m))
        def _(m=m):
            run_rows(m)
        lower = m

    @pl.when(nv > 0)
    def _():
        @pl.when(i >= 1)
        def _():
            y_copy(i - 1, 1 - slot).wait()
        y_copy(i, slot).start()

        @pl.when(i == total - 1)
        def _():
            y_copy(i, slot).wait()


def _experts(block_expert, block_rows, block_first, block_next, block_off, total, xs,
             w_gu, b_gu, w_dn, b_dn, layer):
    d = w_gu.shape[2]
    n_blocks = block_expert.shape[0]
    blk = EXPERT_TILE * d // LANES
    d_gu = w_gu.shape[3]
    d_e = w_dn.shape[2]
    e_map3 = lambda i, be, nv, bf, bn, off, tot: (layer * N_EXPERTS + be[i], 0, 0)
    w_gu = w_gu.reshape((-1,) + w_gu.shape[2:])
    w_dn = w_dn.reshape((-1,) + w_dn.shape[2:])
    b_gu = b_gu.reshape(-1, 1, d_gu)
    b_dn = b_dn.reshape(-1, 1, d)
    grid_spec = pltpu.PrefetchScalarGridSpec(
        num_scalar_prefetch=6,
        grid=(n_blocks,),
        in_specs=[
            pl.BlockSpec(memory_space=pl.ANY),
            pl.BlockSpec(memory_space=pl.ANY),
            pl.BlockSpec((1, 1, d_gu), e_map3),
            pl.BlockSpec(memory_space=pl.ANY),
            pl.BlockSpec((1, 1, d), e_map3),
        ],
        out_specs=pl.BlockSpec(memory_space=pl.ANY),
        scratch_shapes=[
            pltpu.VMEM((2, blk, LANES), F32),
            pltpu.VMEM((2, blk, LANES), F32),
            pltpu.VMEM((d, d_gu), F32),
            pltpu.VMEM((d_e, d), F32),
            pltpu.VMEM((d, d_gu), BF16),
            pltpu.VMEM((d_e, d), BF16),
            pltpu.VMEM((EXPERT_TILE, d_e), BF16),
            pltpu.SemaphoreType.DMA((2,)),
            pltpu.SemaphoreType.DMA((2,)),
            pltpu.SemaphoreType.DMA((2,)),
        ],
    )
    return pl.pallas_call(
        functools.partial(_expert_kernel, layer=layer),
        grid_spec=grid_spec,
        out_shape=jax.ShapeDtypeStruct(xs.shape, F32),
        compiler_params=_params(sem=("arbitrary",)),
        name="experts",
    )(block_expert, block_rows, block_first, block_next, block_off, total, xs, w_gu, b_gu, w_dn, b_dn)


def _combine_kernel(dest_hbm, x1_ref, g_ref, fw_ref, ys_hbm, o_ref,
                    idx_smem, ybuf, sem_idx, sem_rows, *, final_norm):
    i = pl.program_id(0)
    nt = pl.num_programs(0)
    tm, d = x1_ref.shape
    nsl = d // LANES
    slot = i % 2

    def idx_copy(t, s):
        n_idx = TOP_K * tm
        dst = idx_smem.at[pl.ds(pl.multiple_of(s * n_idx, n_idx), n_idx)]
        return pltpu.make_async_copy(dest_hbm.at[t], dst, sem_idx.at[s])

    def issue_rows(s):
        def body(j, carry):
            n0 = j * DMA_UNROLL
            i0 = s * (TOP_K * tm) + n0
            for u in range(DMA_UNROLL):
                for k in range(TOP_K):
                    src = ys_hbm.at[pl.ds(pl.multiple_of(idx_smem[i0 + (k * tm + u)], nsl), nsl)]
                    dst = ybuf.at[s, k, pl.ds(pl.multiple_of((n0 + u) * nsl, nsl), nsl)]
                    pltpu.make_async_copy(src, dst, sem_rows.at[s, k]).start(priority=k % 2)
            return carry
        lax.fori_loop(0, tm // DMA_UNROLL, body, 0)

    def wait_rows(s):
        for k in range(TOP_K):
            pltpu.make_async_copy(ys_hbm.at[pl.ds(0, tm * nsl)], ybuf.at[s, k], sem_rows.at[s, k]).wait()

    @pl.when(i == 0)
    def _():
        cp = idx_copy(0, 0)
        cp.start()
        cp.wait()
        issue_rows(0)

        @pl.when(nt > 1)
        def _():
            idx_copy(1, 1).start()

    @pl.when(i + 1 < nt)
    def _():
        idx_copy(i + 1, 1 - slot).wait()
        issue_rows(1 - slot)

        @pl.when(i + 2 < nt)
        def _():
            idx_copy(i + 2, slot).start()

    wait_rows(slot)
    for r0 in range(0, tm, COMBINE_ROWS):
        g = g_ref[r0:r0 + COMBINE_ROWS, :]
        gk = [jnp.broadcast_to(g[:, k:k + 1], (COMBINE_ROWS, LANES)) for k in range(TOP_K)]
        slabs = []
        for s in range(nsl):
            acc = x1_ref[r0:r0 + COMBINE_ROWS, s * LANES:(s + 1) * LANES]
            for k in range(TOP_K):
                acc = acc + gk[k] * _load_row_slab(ybuf.at[slot, k], s, COMBINE_ROWS, nsl, r0)
            slabs.append(acc)
        out = jnp.concatenate(slabs, axis=1)
        if final_norm:
            out = _rms(out, fw_ref[...])
        o_ref[r0:r0 + COMBINE_ROWS, :] = out


def _combine(x1, ys, gate, dest, final_w, final_norm):
    n, d = x1.shape
    tm = ROW_TILE
    nt = n // tm
    tok_tile = lambda i: (i, 0)
    return pl.pallas_call(
        functools.partial(_combine_kernel, final_norm=final_norm),
        grid=(nt,),
        in_specs=[
            pl.BlockSpec(memory_space=pl.ANY),
            pl.BlockSpec((tm, d), tok_tile),
            pl.BlockSpec((tm, LANES), tok_tile),
            pl.BlockSpec((1, d), lambda i: (0, 0)),
            pl.BlockSpec(memory_space=pl.ANY),
        ],
        out_specs=pl.BlockSpec((tm, d), tok_tile),
        out_shape=jax.ShapeDtypeStruct((n, d), F32),
        scratch_shapes=[
            pltpu.SMEM((2 * TOP_K * tm,), I32),
            pltpu.VMEM((2, TOP_K, tm * d // LANES, LANES), F32),
            pltpu.SemaphoreType.DMA((2,)),
            pltpu.SemaphoreType.DMA((2, TOP_K)),
        ],
        compiler_params=_params(sem=("arbitrary",)),
        name="combine",
    )(dest, x1, gate, final_w.reshape(1, d), ys)


def kernel(x, attn_norm_w, w_in, sgu_ln_w, sgu_ln_b, sgu_w, sgu_b, ssm_a_re, ssm_a_im, ssm_b_re, ssm_b_im, ssm_c_re, ssm_c_im, ssm_d, ssm_log_dt, ssm_glu_w, ssm_glu_b, out_norm_a, out_norm_b, w_out, ffn_norm_w, router_w, router_b, w_gate_up, b_gate_up, w_down, b_down, final_norm_w):
    b, l, d = x.shape
    n = b * l
    depth = w_in.shape[0]
    blocks_per_seq = l // SSM_T
    n_steps = max(1, (blocks_per_seq - 1).bit_length())
    n_rows = n * TOP_K + EXPERT_TILE
    x2 = x.reshape(n, d).astype(F32)
    k2, so3, si2, apw = _s5_tables(ssm_a_re, ssm_a_im, ssm_b_re, ssm_b_im, ssm_c_re, ssm_c_im, ssm_d,
                                   ssm_log_dt, n_steps)
    for layer in range(depth):
        ya, us3 = _front(x2, attn_norm_w[layer], w_in[layer], sgu_ln_w[layer], sgu_ln_b[layer],
                         sgu_w[layer], sgu_b[layer], out_norm_a[layer])
        ys3 = _s5(us3, k2, so3, si2, apw, blocks_per_seq, layer)
        x1, h2, gate, er, cnt = _back(x2, ya, ys3, ssm_glu_w[layer], ssm_glu_b[layer],
                                      out_norm_b[layer], w_out[layer], ffn_norm_w[layer],
                                      router_w[layer], router_b[layer])
        counts = cnt[0, :N_EXPERTS].astype(I32)
        dest, block_expert, block_rows, block_first, block_next, block_off, total = _routing_tables(
            er, counts, n, d // LANES)
        xs = _dispatch(dest, h2, n_rows, d // LANES)
        ys = _experts(block_expert, block_rows, block_first, block_next, block_off, total, xs,
                      w_gate_up, b_gate_up, w_down, b_down, layer)
        x2 = _combine(x1, ys, gate, dest, final_norm_w, final_norm=(layer == depth - 1))
    return x2.reshape(b, l, d).astype(x.dtype)
```

```python
import functools
import math

import jax
import jax.numpy as jnp
from jax import lax
from jax.experimental import pallas as pl
from jax.experimental.pallas import tpu as pltpu

F32 = jnp.float32
BF16 = jnp.bfloat16
I32 = jnp.int32

EPS = 1e-5
N_HEADS = 4
CHUNK = 128
SSM_GROUP = 16
SSM_STATE = 64
SSM_T = 16
N_EXPERTS = 32
TOP_K = 4
SWIGLU_LIMIT = 7.0
SWIGLU_ALPHA = 1.702
LANES = 128
SUBLANES = 8
GROUPS_PER_LANE_BLOCK = LANES // SSM_GROUP

ROW_TILE = 512
FRONT_TILE = 1024
S5_SEQS = 2
EXPERT_TILE = 1024
EXPERT_CHUNK = 256
DMA_UNROLL = 8
COMBINE_ROWS = 8
VMEM_LIMIT = 56 * 1024 * 1024


def _gelu(x):
    return 0.5 * x * (1.0 + jnp.tanh(math.sqrt(2.0 / math.pi) * (x + 0.044715 * (x * x * x))))


def _sigmoid(x):
    return 1.0 / (1.0 + jnp.exp(-x))


def _rms(x, w):
    return x * lax.rsqrt(jnp.mean(x * x, axis=-1, keepdims=True) + EPS) * w


def _store_rows(ref, val, row0=0):
    rows, d = val.shape
    nsl = d // LANES
    for s in range(nsl):
        ref[pl.ds(row0 * nsl + s, rows, stride=nsl), :] = val[:, s * LANES:(s + 1) * LANES]


def _load_row_slab(ref, s, rows, nsl, row0=0):
    return ref[pl.ds(row0 * nsl + s, rows, stride=nsl), :]


def _params(**kw):
    return pltpu.CompilerParams(dimension_semantics=kw.pop("sem"), vmem_limit_bytes=VMEM_LIMIT, **kw)


def _front_kernel(x_ref, nw_ref, win_ref, lnw_ref, lnb_ref, ws_ref, bst_ref, ona_ref,
                  ya_ref, us_ref, mixed_ref, us_scr):
    d_g = ya_ref.shape[1]
    hd = d_g // N_HEADS
    tm = x_ref.shape[0]
    x = x_ref[...]
    h = _rms(x, nw_ref[...]).astype(BF16)
    proj = jnp.dot(h, win_ref[...], preferred_element_type=F32)
    u = _gelu(proj[:, :d_g])
    v = _gelu(proj[:, d_g:2 * d_g])
    n_lb = us_scr.shape[0]
    for q in range(n_lb):
        us_scr[q] = proj[:, 2 * d_g + q * LANES:2 * d_g + (q + 1) * LANES]
    mu = jnp.mean(v, axis=-1, keepdims=True)
    vc = v - mu
    var = jnp.mean(vc * vc, axis=-1, keepdims=True)
    vb = (vc * lax.rsqrt(var + EPS) * lnw_ref[...] + lnb_ref[...]).astype(BF16)
    row = lax.broadcasted_iota(I32, (CHUNK, CHUNK), 0)
    col = lax.broadcasted_iota(I32, (CHUNK, CHUNK), 1)
    causal = row >= col
    for hh in range(N_HEADS):
        w = jnp.where(causal, ws_ref[hh], 0.0).astype(BF16)
        bias = bst_ref[:, hh:hh + 1]
        for c in range(tm // CHUNK):
            vv = vb[c * CHUNK:(c + 1) * CHUNK, hh * hd:(hh + 1) * hd]
            m = jnp.dot(w, vv, preferred_element_type=F32) + bias
            mixed_ref[c * CHUNK:(c + 1) * CHUNK, hh * hd:(hh + 1) * hd] = m
    ya = u * mixed_ref[...]
    ya_ref[...] = _rms(ya, ona_ref[...]).astype(BF16)
    for s in range(SSM_T):
        for q in range(n_lb):
            us_ref[s, :, q * LANES:(q + 1) * LANES] = (
                us_scr[q, pl.ds(s, tm // SSM_T, stride=SSM_T), :].astype(BF16))


def _front(x2, nw, w_in, ln_w, ln_b, w_s, b_s, on_a):
    n, d = x2.shape
    d_g = ln_w.shape[0]
    d_s = w_in.shape[1] - 2 * d_g
    tm = FRONT_TILE
    const2 = lambda i: (0, 0)
    return pl.pallas_call(
        _front_kernel,
        grid=(n // tm,),
        in_specs=[
            pl.BlockSpec((tm, d), lambda i: (i, 0)),
            pl.BlockSpec((1, d), const2),
            pl.BlockSpec(w_in.shape, const2),
            pl.BlockSpec((1, d_g), const2),
            pl.BlockSpec((1, d_g), const2),
            pl.BlockSpec(w_s.shape, lambda i: (0, 0, 0)),
            pl.BlockSpec((CHUNK, N_HEADS), const2),
            pl.BlockSpec((1, d_g), const2),
        ],
        out_specs=[
            pl.BlockSpec((tm, d_g), lambda i: (i, 0)),
            pl.BlockSpec((SSM_T, tm // SSM_T, d_s), lambda i: (0, i, 0)),
        ],
        out_shape=[
            jax.ShapeDtypeStruct((n, d_g), BF16),
            jax.ShapeDtypeStruct((SSM_T, n // SSM_T, d_s), BF16),
        ],
        scratch_shapes=[pltpu.VMEM((tm, d_g), F32), pltpu.VMEM((d_s // LANES, tm, LANES), F32)],
        compiler_params=_params(sem=("arbitrary",)),
        name="front",
    )(x2, nw.reshape(1, d), w_in.astype(BF16), ln_w.reshape(1, d_g), ln_b.reshape(1, d_g),
      w_s, b_s.T, on_a.reshape(1, d_g))


def _s5_tables(a_re, a_im, b_re, b_im, c_re, c_im, d, log_dt, n_steps):
    depth, g, p = a_re.shape
    hch = b_re.shape[-1]
    t = SSM_T
    gl = GROUPS_PER_LANE_BLOCK
    r = depth * (g // gl)
    a = lax.complex(a_re.astype(F32), a_im.astype(F32)).reshape(r, gl, p)
    dt = jnp.exp(log_dt.astype(F32)).reshape(r, gl, 1)
    dta = dt * a
    a_bar = jnp.exp(dta)
    b = lax.complex(b_re.astype(F32), b_im.astype(F32)).reshape(r, gl, p, hch)
    b_bar = ((a_bar - 1.0) / a)[..., None] * b
    c = lax.complex(c_re.astype(F32), c_im.astype(F32)).reshape(r, gl, hch, p)
    lags = jnp.arange(t + 1, dtype=F32)
    pw = jnp.exp(lags[None, :, None, None] * dta.transpose(0, 2, 1)[:, None])
    pwx = jnp.repeat(pw, hch, axis=-1)
    bb = b_bar.transpose(0, 2, 1, 3).reshape(r, p, gl * hch)
    cc = c.transpose(0, 3, 1, 2).reshape(r, p, gl * hch)
    q = pwx[:, :t] * bb[:, None]
    si = pwx[:, 1:] * cc[:, None]
    k2 = (jnp.einsum('rlpx,rpy->rlxy', q.real, cc.real, precision=lax.Precision.HIGHEST)
          - jnp.einsum('rlpx,rpy->rlxy', q.imag, cc.imag, precision=lax.Precision.HIGHEST))
    lane_g = jnp.arange(gl * hch) // hch
    k2 = jnp.where(lane_g[:, None] == lane_g[None, :], k2, 0.0)
    skip = jnp.eye(gl * hch, dtype=F32) * d.astype(F32).reshape(r, 1, gl * hch)
    k2 = k2.at[:, 0].add(skip)
    so3 = jnp.swapaxes(jnp.concatenate([q.real, q.imag], axis=2), 2, 3)
    si2 = jnp.concatenate([si.real, -si.imag], axis=2)
    steps = SSM_T * (2.0 ** jnp.arange(n_steps, dtype=F32))
    ap = jnp.exp(steps[None, :, None, None] * dta[:, None]).reshape(r, n_steps, gl * p)
    mul_same = jnp.concatenate([ap.real, ap.real], axis=-1)
    mul_swap = jnp.concatenate([-ap.imag, ap.imag], axis=-1)
    apw = jnp.stack([mul_same, mul_swap], axis=2)
    return k2.astype(BF16), so3.astype(BF16), si2.astype(BF16), apw


def _s5_kernel(u_ref, k2_ref, so3_ref, si2_ref, ap_ref, y_ref, wk_scr, wso_scr, wsi_scr, *, blocks_per_seq):
    t_blk = u_ref.shape[0]
    rows = u_ref.shape[1]
    n_state = wso_scr.shape[1]
    two_p = so3_ref.shape[3]
    p = two_p // 2
    half = n_state // 2

    @pl.when(pl.program_id(1) == 0)
    def _():
        for s in range(t_blk):
            if s > 0:
                wk_scr[s * LANES:(s + 1) * LANES, :s * LANES] = jnp.zeros((LANES, s * LANES), BF16)
            for t in range(s, t_blk):
                wk_scr[s * LANES:(s + 1) * LANES, t * LANES:(t + 1) * LANES] = k2_ref[0, t - s]
        r_e = lax.broadcasted_iota(I32, (two_p, n_state), 0)
        c_e = lax.broadcasted_iota(I32, (two_p, n_state), 1)
        spread = jnp.where((r_e // p == c_e // half) & (r_e % p == c_e % p), 1.0, 0.0).astype(BF16)
        r_g = lax.broadcasted_iota(I32, (LANES, n_state), 0) // SSM_GROUP
        c_g = (lax.broadcasted_iota(I32, (LANES, n_state), 1) % half) // p
        own = r_g == c_g
        for s in range(t_blk):
            full = jnp.dot(so3_ref[0, t_blk - 1 - s], spread, preferred_element_type=F32)
            wso_scr[s * LANES:(s + 1) * LANES, :] = jnp.where(own, full, 0.0).astype(BF16)
        r_e = lax.broadcasted_iota(I32, (n_state, two_p), 0)
        c_e = lax.broadcasted_iota(I32, (n_state, two_p), 1)
        gather = jnp.where((c_e // p == r_e // half) & (c_e % p == r_e % p), 1.0, 0.0).astype(BF16)
        r_g = (lax.broadcasted_iota(I32, (n_state, LANES), 0) % half) // p
        c_g = lax.broadcasted_iota(I32, (n_state, LANES), 1) // SSM_GROUP
        own = r_g == c_g
        for t in range(t_blk):
            full = jnp.dot(gather, si2_ref[0, t], preferred_element_type=F32)
            wsi_scr[:, t * LANES:(t + 1) * LANES] = jnp.where(own, full, 0.0).astype(BF16)

    xcat = jnp.concatenate([u_ref[s] for s in range(t_blk)], axis=1)
    x = jnp.dot(xcat, wso_scr[...], preferred_element_type=F32)
    pos = lax.broadcasted_iota(I32, (rows, n_state), 0) % blocks_per_seq
    for k in range(ap_ref.shape[1]):
        sh = 1 << k
        prev = pltpu.roll(x, sh, 0)
        prev_sw = pltpu.roll(prev, half, 1)
        upd = prev * ap_ref[0, k, 0:1, :] + prev_sw * ap_ref[0, k, 1:2, :]
        x = x + jnp.where(pos >= sh, upd, 0.0)
    xin = jnp.where(pos >= 1, pltpu.roll(x, 1, 0), 0.0).astype(BF16)
    for j in range(t_blk // 2):
        k_hi = (2 * j + 2) * LANES
        lo = 2 * j * LANES
        y = jnp.dot(xcat[:, :k_hi], wk_scr[:k_hi, lo:lo + 2 * LANES], preferred_element_type=F32)
        y = y + jnp.dot(xin, wsi_scr[:, lo:lo + 2 * LANES], preferred_element_type=F32)
        y_ref[2 * j] = y[:, :LANES].astype(BF16)
        y_ref[2 * j + 1] = y[:, LANES:].astype(BF16)


def _s5(us3, k2, so3, si2, apw, blocks_per_seq, layer):
    t_blk, n_blocks, d_s = us3.shape
    lb = d_s // LANES
    rows = S5_SEQS * blocks_per_seq
    n_steps = n_blocks // rows
    n_in = t_blk * LANES
    n_state = apw.shape[3]
    w_map = lambda a, b: (layer * lb + a, 0, 0, 0)
    io_spec = pl.BlockSpec((t_blk, rows, LANES), lambda a, b: (0, b, a))
    return pl.pallas_call(
        functools.partial(_s5_kernel, blocks_per_seq=blocks_per_seq),
        grid=(lb, n_steps),
        in_specs=[
            io_spec,
            pl.BlockSpec((1,) + k2.shape[1:], w_map),
            pl.BlockSpec((1,) + so3.shape[1:], w_map),
            pl.BlockSpec((1,) + si2.shape[1:], w_map),
            pl.BlockSpec((1,) + apw.shape[1:], w_map),
        ],
        out_specs=io_spec,
        out_shape=jax.ShapeDtypeStruct(us3.shape, BF16),
        scratch_shapes=[
            pltpu.VMEM((n_in, n_in), BF16),
            pltpu.VMEM((n_in, n_state), BF16),
            pltpu.VMEM((n_state, n_in), BF16),
        ],
        compiler_params=_params(sem=("arbitrary", "arbitrary")),
        name="s5",
    )(us3, k2, so3, si2, apw)


def _back_kernel(x_ref, ya_ref, ys_ref, gw_ref, gb_ref, onb_ref, wo_ref, fnw_ref,
                 rw_ref, rb_ref,
                 x1_ref, h2_ref, gate_ref, er_ref, cnt_ref, carry_ref, ys_scr):
    i = pl.program_id(0)
    tm = x_ref.shape[0]
    d_g = ya_ref.shape[1]

    @pl.when(i == 0)
    def _():
        carry_ref[...] = jnp.zeros_like(carry_ref)

    n_lb = ys_scr.shape[0]
    for t in range(SSM_T):
        for q in range(n_lb):
            ys_scr[q, pl.ds(t, tm // SSM_T, stride=SSM_T), :] = (
                ys_ref[t, :, q * LANES:(q + 1) * LANES].astype(F32))
    y = _gelu(jnp.concatenate([ys_scr[q] for q in range(n_lb)], axis=1))
    z = jnp.dot(y.astype(BF16), gw_ref[...], preferred_element_type=F32) + gb_ref[...]
    yb = y * _sigmoid(z)
    ybn = _rms(yb, onb_ref[...]).astype(BF16)
    x1 = (x_ref[...]
          + jnp.dot(ya_ref[...], wo_ref[:d_g, :], preferred_element_type=F32)
          + jnp.dot(ybn, wo_ref[d_g:, :], preferred_element_type=F32))
    x1_ref[...] = x1
    h2 = _rms(x1, fnw_ref[...])
    _store_rows(h2_ref, h2)

    hh = h2.astype(BF16)
    hl = (h2 - hh.astype(F32)).astype(BF16)
    p_hi = jnp.dot(hh, rw_ref[...], preferred_element_type=F32)
    p_lo = jnp.dot(hl, rw_ref[:, :LANES], preferred_element_type=F32)
    logits = p_hi[:, :LANES] + p_lo + p_hi[:, LANES:] + rb_ref[...]
    lane = lax.broadcasted_iota(I32, (tm, LANES), 1)
    lane_f = lane.astype(F32)
    neg = jnp.float32(-jnp.inf)
    work = jnp.where(lane < N_EXPERTS, logits, neg)
    vals, hots, idxs = [], [], []
    for k in range(TOP_K):
        m = jnp.max(work, axis=-1, keepdims=True)
        idx = jnp.min(jnp.where(work == m, lane_f, float(LANES)), axis=-1, keepdims=True)
        hot = lane_f == idx
        vals.append(m)
        hots.append(hot)
        idxs.append(idx.astype(I32))
        work = jnp.where(hot, neg, work)
    exps = [jnp.exp(v - vals[0]) for v in vals]
    denom = exps[0] + exps[1] + exps[2] + exps[3]
    gate = jnp.zeros((tm, LANES), F32)
    for k in range(TOP_K):
        gate = jnp.where(lane == k, exps[k] / denom, gate)
    gate_ref[...] = gate

    sel = (hots[0] | hots[1] | hots[2] | hots[3])
    sel_f = jnp.where(sel, 1.0, 0.0)
    r_i = lax.broadcasted_iota(I32, (tm, tm), 0)
    c_i = lax.broadcasted_iota(I32, (tm, tm), 1)
    lower = jnp.where(r_i > c_i, 1.0, 0.0).astype(BF16)
    cum = jnp.dot(lower, sel_f.astype(BF16), preferred_element_type=F32) + carry_ref[...]
    er = jnp.zeros((tm, LANES), I32)
    for k in range(TOP_K):
        rk = jnp.sum(jnp.where(hots[k], cum, 0.0), axis=-1, keepdims=True)
        er = jnp.where(lane == k, idxs[k], er)
        er = jnp.where(lane == TOP_K + k, rk.astype(I32), er)
    carry_ref[...] = carry_ref[...] + jnp.sum(sel_f, axis=0, keepdims=True)
    cnt_ref[...] = carry_ref[...]
    er_ref[0] = jnp.transpose(er)[:SUBLANES, :]


def _back(x2, ya, ys3, glu_w, glu_b, on_b, w_out, fn_w, router_w, router_b):
    n, d = x2.shape
    d_g = ya.shape[1]
    d_s = ys3.shape[2]
    tm = ROW_TILE
    nt = n // tm
    const2 = lambda i: (0, 0)
    rw = jnp.zeros((d, LANES), F32).at[:, :N_EXPERTS].set(router_w.astype(F32))
    rw_hi = rw.astype(BF16)
    rw_lo = (rw - rw_hi.astype(F32)).astype(BF16)
    rw_split = jnp.concatenate([rw_hi, rw_lo], axis=1)
    rb = jnp.zeros((1, LANES), F32).at[0, :N_EXPERTS].set(router_b.astype(F32))
    tok_tile = lambda i: (i, 0)
    return pl.pallas_call(
        _back_kernel,
        grid=(nt,),
        in_specs=[
            pl.BlockSpec((tm, d), tok_tile),
            pl.BlockSpec((tm, d_g), tok_tile),
            pl.BlockSpec((SSM_T, tm // SSM_T, d_s), lambda i: (0, i, 0)),
            pl.BlockSpec((d_s, d_s), const2),
            pl.BlockSpec((1, d_s), const2),
            pl.BlockSpec((1, d_s), const2),
            pl.BlockSpec((d_g + d_s, d), const2),
            pl.BlockSpec((1, d), const2),
            pl.BlockSpec((d, 2 * LANES), const2),
            pl.BlockSpec((1, LANES), const2),
        ],
        out_specs=[
            pl.BlockSpec((tm, d), tok_tile),
            pl.BlockSpec((tm * d // LANES, LANES), tok_tile),
            pl.BlockSpec((tm, LANES), tok_tile),
            pl.BlockSpec((1, SUBLANES, tm), lambda i: (i, 0, 0)),
            pl.BlockSpec((1, LANES), const2),
        ],
        out_shape=[
            jax.ShapeDtypeStruct((n, d), F32),
            jax.ShapeDtypeStruct((n * d // LANES, LANES), F32),
            jax.ShapeDtypeStruct((n, LANES), F32),
            jax.ShapeDtypeStruct((nt, SUBLANES, tm), I32),
            jax.ShapeDtypeStruct((1, LANES), F32),
        ],
        scratch_shapes=[pltpu.VMEM((1, LANES), F32), pltpu.VMEM((d_s // LANES, tm, LANES), F32)],
        compiler_params=_params(sem=("arbitrary",)),
        name="back",
    )(x2, ya, ys3, glu_w.astype(BF16), glu_b.reshape(1, d_s), on_b.reshape(1, d_s),
      w_out.astype(BF16), fn_w.reshape(1, d), rw_split, rb)


def _routing_tables(er, counts, n, row_sl):
    te = EXPERT_TILE
    n_blocks = n * TOP_K // te + N_EXPERTS
    start = jnp.cumsum(counts) - counts
    e_ids = jnp.arange(N_EXPERTS, dtype=I32)
    e_sel = er[:, :TOP_K, :, None] == e_ids
    dest = jnp.sum(jnp.where(e_sel, start, 0), axis=-1) + er[:, TOP_K:2 * TOP_K, :]
    dest = (dest * row_sl).reshape(dest.shape[0], -1)
    nb = (counts + te - 1) // te
    cum = jnp.cumsum(nb)
    total = cum[-1]
    j = jnp.arange(n_blocks, dtype=I32)
    e_j = jnp.sum((cum[None, :] <= jnp.minimum(j, total - 1)[:, None]).astype(I32), axis=1)
    valid = (j < total).astype(I32)
    own = e_ids[None, :] == e_j[:, None]
    first_j = jnp.sum(jnp.where(own, cum - nb, 0), axis=1)
    count_j = jnp.sum(jnp.where(own, counts, 0), axis=1)
    start_j = jnp.sum(jnp.where(own, start, 0), axis=1)
    first = valid * (j == first_j).astype(I32)
    later = (e_ids[None, :] > e_j[:, None]) & (nb > 0)[None, :]
    nxt = jnp.min(jnp.where(later, e_ids[None, :], N_EXPERTS), axis=1)
    nxt = jnp.where(nxt < N_EXPERTS, nxt, -1)
    rows = valid * jnp.clip(count_j - (j - first_j) * te, 0, te)
    row_off = (start_j + (j - first_j) * te) * row_sl
    return (dest.astype(I32), e_j.astype(I32), rows.astype(I32), first, nxt.astype(I32),
            row_off.astype(I32), total.reshape(1).astype(I32))


def _dispatch_kernel(dest_hbm, h_ref, xs_hbm, idx_smem, zbuf, sem_idx, sem_rows, sem_z, *, row_sl):
    i = pl.program_id(0)
    nt = pl.num_programs(0)
    tm = h_ref.shape[0] // row_sl
    blk = zbuf.shape[0]
    slot = i % 2

    def idx_copy(t, s):
        n_idx = TOP_K * tm
        dst = idx_smem.at[pl.ds(pl.multiple_of(s * n_idx, n_idx), n_idx)]
        return pltpu.make_async_copy(dest_hbm.at[t], dst, sem_idx.at[s])

    @pl.when(i == 0)
    def _():
        zbuf[...] = jnp.zeros_like(zbuf)
        cp = pltpu.make_async_copy(zbuf, xs_hbm.at[pl.ds(xs_hbm.shape[0] - blk, blk)], sem_z)
        cp.start()
        cp.wait()
        idx_copy(0, 0).start()

    @pl.when(i + 1 < nt)
    def _():
        idx_copy(i + 1, 1 - slot).start()

    idx_copy(i, slot).wait()

    def body(j, carry):
        n0 = j * DMA_UNROLL
        i0 = slot * (TOP_K * tm) + n0
        for u in range(DMA_UNROLL):
            src = h_ref.at[pl.ds(pl.multiple_of((n0 + u) * row_sl, row_sl), row_sl)]
            for k in range(TOP_K):
                dst = xs_hbm.at[pl.ds(pl.multiple_of(idx_smem[i0 + (k * tm + u)], row_sl), row_sl)]
                pltpu.make_async_copy(src, dst, sem_rows.at[k]).start(priority=k % 2)
        return carry
    lax.fori_loop(0, tm // DMA_UNROLL, body, 0)
    for k in range(TOP_K):
        pltpu.make_async_copy(h_ref, xs_hbm.at[pl.ds(0, tm * row_sl)], sem_rows.at[k]).wait()


def _dispatch(dest, h2t, n_rows, row_sl):
    tm = ROW_TILE
    nt = h2t.shape[0] // (tm * row_sl)
    return pl.pallas_call(
        functools.partial(_dispatch_kernel, row_sl=row_sl),
        grid=(nt,),
        in_specs=[
            pl.BlockSpec(memory_space=pl.ANY),
            pl.BlockSpec((tm * row_sl, LANES), lambda i: (i, 0)),
        ],
        out_specs=pl.BlockSpec(memory_space=pl.ANY),
        out_shape=jax.ShapeDtypeStruct((n_rows * row_sl, LANES), F32),
        scratch_shapes=[
            pltpu.SMEM((2 * TOP_K * tm,), I32),
            pltpu.VMEM((EXPERT_TILE * row_sl, LANES), F32),
            pltpu.SemaphoreType.DMA((2,)),
            pltpu.SemaphoreType.DMA((TOP_K,)),
            pltpu.SemaphoreType.DMA,
        ],
        compiler_params=_params(sem=("arbitrary",)),
        name="dispatch",
    )(dest, h2t)


def _expert_kernel(be_ref, nv_ref, first_ref, next_ref, off_ref, tot_ref,
                   xs_hbm, wgu_hbm, bgu_ref, wdn_hbm, bdn_ref, ys_hbm,
                   xbuf, ybuf, stage_gu, stage_dn, wgu_bf, wdn_bf, act_scr, sem_x, sem_y, sem_w, *, layer):
    i = pl.program_id(0)
    d, d_gu = wgu_bf.shape
    d_e = wdn_bf.shape[0]
    nsl = d // LANES
    blk = xbuf.shape[1]
    te = blk // nsl
    slot = i % 2
    total = tot_ref[0]

    def weight_copies(e):
        row = layer * N_EXPERTS + e
        return (pltpu.make_async_copy(wgu_hbm.at[row], stage_gu, sem_w.at[0]),
                pltpu.make_async_copy(wdn_hbm.at[row], stage_dn, sem_w.at[1]))

    def x_copy(j, s):
        src = xs_hbm.at[pl.ds(pl.multiple_of(off_ref[j], nsl), blk)]
        return pltpu.make_async_copy(src, xbuf.at[s], sem_x.at[s])

    def y_copy(j, s):
        dst = ys_hbm.at[pl.ds(pl.multiple_of(off_ref[j], nsl), blk)]
        return pltpu.make_async_copy(ybuf.at[s], dst, sem_y.at[s])

    @pl.when(i == 0)
    def _():
        for cp in weight_copies(be_ref[0]):
            cp.start(priority=1)
        x_copy(0, 0).start()
        ybuf[...] = jnp.zeros_like(ybuf)
        cp = pltpu.make_async_copy(ybuf.at[1], ys_hbm.at[pl.ds(ys_hbm.shape[0] - blk, blk)], sem_y.at[1])
        cp.start()
        cp.wait()

    @pl.when(first_ref[i] > 0)
    def _():
        for cp in weight_copies(be_ref[i]):
            cp.wait()
        rows = 128
        for r in range(0, d, rows):
            wgu_bf[r:r + rows, :] = stage_gu[r:r + rows, :].astype(BF16)
        for r in range(0, d_e, rows):
            wdn_bf[r:r + rows, :] = stage_dn[r:r + rows, :].astype(BF16)

        @pl.when(next_ref[i] >= 0)
        def _():
            for cp in weight_copies(next_ref[i]):
                cp.start(priority=1)

    def run_rows(m):
        x_ref = xbuf.at[slot]
        x = jnp.concatenate([_load_row_slab(x_ref, s, m, nsl).astype(BF16) for s in range(nsl)], axis=1)
        for c in range(d_e // EXPERT_CHUNK):
            lo = c * EXPERT_CHUNK
            hi = lo + EXPERT_CHUNK
            glu = jnp.dot(x, wgu_bf[:, lo:hi], preferred_element_type=F32) + bgu_ref[0, :, lo:hi]
            lin = (jnp.dot(x, wgu_bf[:, d_e + lo:d_e + hi], preferred_element_type=F32)
                   + bgu_ref[0, :, d_e + lo:d_e + hi])
            glu = jnp.minimum(glu, SWIGLU_LIMIT)
            lin = jnp.clip(lin, -SWIGLU_LIMIT, SWIGLU_LIMIT)
            act_scr[:m, lo:hi] = (glu * _sigmoid(SWIGLU_ALPHA * glu) * (lin + 1.0)).astype(BF16)
        y = jnp.dot(act_scr[:m, :], wdn_bf[...], preferred_element_type=F32) + bdn_ref[0]
        _store_rows(ybuf.at[slot], y)

    nv = nv_ref[i]

    @pl.when(nv > 0)
    def _():
        x_copy(i, slot).wait()

        @pl.when(i + 1 < total)
        def _():
            x_copy(i + 1, 1 - slot).start()

    sizes = [te // 4, te // 2, te]
    lower = 0
    for m in sizes:
        @pl.when((nv > lower) & (nv <= m))
        def _(m=m):
            run_rows(m)
        lower = m

    @pl.when(nv > 0)
    def _():
        @pl.when(i >= 1)
        def _():
            y_copy(i - 1, 1 - slot).wait()
        y_copy(i, slot).start()

        @pl.when(i == total - 1)
        def _():
            y_copy(i, slot).wait()


def _experts(block_expert, block_rows, block_first, block_next, block_off, total, xs,
             w_gu, b_gu, w_dn, b_dn, layer):
    d = w_gu.shape[2]
    n_blocks = block_expert.shape[0]
    blk = EXPERT_TILE * d // LANES
    d_gu = w_gu.shape[3]
    d_e = w_dn.shape[2]
    e_map3 = lambda i, be, nv, bf, bn, off, tot: (layer * N_EXPERTS + be[i], 0, 0)
    w_gu = w_gu.reshape((-1,) + w_gu.shape[2:])
    w_dn = w_dn.reshape((-1,) + w_dn.shape[2:])
    b_gu = b_gu.reshape(-1, 1, d_gu)
    b_dn = b_dn.reshape(-1, 1, d)
    grid_spec = pltpu.PrefetchScalarGridSpec(
        num_scalar_prefetch=6,
        grid=(n_blocks,),
        in_specs=[
            pl.BlockSpec(memory_space=pl.ANY),
            pl.BlockSpec(memory_space=pl.ANY),
            pl.BlockSpec((1, 1, d_gu), e_map3),
            pl.BlockSpec(memory_space=pl.ANY),
            pl.BlockSpec((1, 1, d), e_map3),
        ],
        out_specs=pl.BlockSpec(memory_space=pl.ANY),
        scratch_shapes=[
            pltpu.VMEM((2, blk, LANES), F32),
            pltpu.VMEM((2, blk, LANES), F32),
            pltpu.VMEM((d, d_gu), F32),
            pltpu.VMEM((d_e, d), F32),
            pltpu.VMEM((d, d_gu), BF16),
            pltpu.VMEM((d_e, d), BF16),
            pltpu.VMEM((EXPERT_TILE, d_e), BF16),
            pltpu.SemaphoreType.DMA((2,)),
            pltpu.SemaphoreType.DMA((2,)),
            pltpu.SemaphoreType.DMA((2,)),
        ],
    )
    return pl.pallas_call(
        functools.partial(_expert_kernel, layer=layer),
        grid_spec=grid_spec,
        out_shape=jax.ShapeDtypeStruct(xs.shape, F32),
        compiler_params=_params(sem=("arbitrary",)),
        name="experts",
    )(block_expert, block_rows, block_first, block_next, block_off, total, xs, w_gu, b_gu, w_dn, b_dn)


def _combine_kernel(dest_hbm, x1_ref, g_ref, fw_ref, ys_hbm, o_ref,
                    idx_smem, ybuf, sem_idx, sem_rows, *, final_norm):
    i = pl.program_id(0)
    nt = pl.num_programs(0)
    tm, d = x1_ref.shape
    nsl = d // LANES
    slot = i % 2

    def idx_copy(t, s):
        n_idx = TOP_K * tm
        dst = idx_smem.at[pl.ds(pl.multiple_of(s * n_idx, n_idx), n_idx)]
        return pltpu.make_async_copy(dest_hbm.at[t], dst, sem_idx.at[s])

    def issue_rows(s):
        def body(j, carry):
            n0 = j * DMA_UNROLL
            i0 = s * (TOP_K * tm) + n0
            for u in range(DMA_UNROLL):
                for k in range(TOP_K):
                    src = ys_hbm.at[pl.ds(pl.multiple_of(idx_smem[i0 + (k * tm + u)], nsl), nsl)]
                    dst = ybuf.at[s, k, pl.ds(pl.multiple_of((n0 + u) * nsl, nsl), nsl)]
                    pltpu.make_async_copy(src, dst, sem_rows.at[s, k]).start(priority=k % 2)
            return carry
        lax.fori_loop(0, tm // DMA_UNROLL, body, 0)

    def wait_rows(s):
        for k in range(TOP_K):
            pltpu.make_async_copy(ys_hbm.at[pl.ds(0, tm * nsl)], ybuf.at[s, k], sem_rows.at[s, k]).wait()

    @pl.when(i == 0)
    def _():
        cp = idx_copy(0, 0)
        cp.start()
        cp.wait()
        issue_rows(0)

        @pl.when(nt > 1)
        def _():
            idx_copy(1, 1).start()

    @pl.when(i + 1 < nt)
    def _():
        idx_copy(i + 1, 1 - slot).wait()
        issue_rows(1 - slot)

        @pl.when(i + 2 < nt)
        def _():
            idx_copy(i + 2, slot).start()

    wait_rows(slot)
    for r0 in range(0, tm, COMBINE_ROWS):
        g = g_ref[r0:r0 + COMBINE_ROWS, :]
        gk = [jnp.broadcast_to(g[:, k:k + 1], (COMBINE_ROWS, LANES)) for k in range(TOP_K)]
        slabs = []
        for s in range(nsl):
            acc = x1_ref[r0:r0 + COMBINE_ROWS, s * LANES:(s + 1) * LANES]
            for k in range(TOP_K):
                acc = acc + gk[k] * _load_row_slab(ybuf.at[slot, k], s, COMBINE_ROWS, nsl, r0)
            slabs.append(acc)
        out = jnp.concatenate(slabs, axis=1)
        if final_norm:
            out = _rms(out, fw_ref[...])
        o_ref[r0:r0 + COMBINE_ROWS, :] = out


def _combine(x1, ys, gate, dest, final_w, final_norm):
    n, d = x1.shape
    tm = ROW_TILE
    nt = n // tm
    tok_tile = lambda i: (i, 0)
    return pl.pallas_call(
        functools.partial(_combine_kernel, final_norm=final_norm),
        grid=(nt,),
        in_specs=[
            pl.BlockSpec(memory_space=pl.ANY),
            pl.BlockSpec((tm, d), tok_tile),
            pl.BlockSpec((tm, LANES), tok_tile),
            pl.BlockSpec((1, d), lambda i: (0, 0)),
            pl.BlockSpec(memory_space=pl.ANY),
        ],
        out_specs=pl.BlockSpec((tm, d), tok_tile),
        out_shape=jax.ShapeDtypeStruct((n, d), F32),
        scratch_shapes=[
            pltpu.SMEM((2 * TOP_K * tm,), I32),
            pltpu.VMEM((2, TOP_K, tm * d // LANES, LANES), F32),
            pltpu.SemaphoreType.DMA((2,)),
            pltpu.SemaphoreType.DMA((2, TOP_K)),
        ],
        compiler_params=_params(sem=("arbitrary",)),
        name="combine",
    )(dest, x1, gate, final_w.reshape(1, d), ys)


def kernel(x, attn_norm_w, w_in, sgu_ln_w, sgu_ln_b, sgu_w, sgu_b, ssm_a_re, ssm_a_im, ssm_b_re, ssm_b_im, ssm_c_re, ssm_c_im, ssm_d, ssm_log_dt, ssm_glu_w, ssm_glu_b, out_norm_a, out_norm_b, w_out, ffn_norm_w, router_w, router_b, w_gate_up, b_gate_up, w_down, b_down, final_norm_w):
    b, l, d = x.shape
    n = b * l
    depth = w_in.shape[0]
    blocks_per_seq = l // SSM_T
    n_steps = max(1, (blocks_per_seq - 1).bit_length())
    n_rows = n * TOP_K + EXPERT_TILE
    x2 = x.reshape(n, d).astype(F32)
    k2, so3, si2, apw = _s5_tables(ssm_a_re, ssm_a_im, ssm_b_re, ssm_b_im, ssm_c_re, ssm_c_im, ssm_d,
                                   ssm_log_dt, n_steps)
    for layer in range(depth):
        ya, us3 = _front(x2, attn_norm_w[layer], w_in[layer], sgu_ln_w[layer], sgu_ln_b[layer],
                         sgu_w[layer], sgu_b[layer], out_norm_a[layer])
        ys3 = _s5(us3, k2, so3, si2, apw, blocks_per_seq, layer)
        x1, h2, gate, er, cnt = _back(x2, ya, ys3, ssm_glu_w[layer], ssm_glu_b[layer],
                                      out_norm_b[layer], w_out[layer], ffn_norm_w[layer],
                                      router_w[layer], router_b[layer])
        counts = cnt[0, :N_EXPERTS].astype(I32)
        dest, block_expert, block_rows, block_first, block_next, block_off, total = _routing_tables(
            er, counts, n, d // LANES)
        xs = _dispatch(dest, h2, n_rows, d // LANES)
        ys = _experts(block_expert, block_rows, block_first, block_next, block_off, total, xs,
                      w_gate_up, b_gate_up, w_down, b_down, layer)
        x2 = _combine(x1, ys, gate, dest, final_norm_w, final_norm=(layer == depth - 1))
    return x2.reshape(b, l, d).astype(x.dtype)
```

```python
import functools
import math

import jax
import jax.numpy as jnp
from jax import lax
from jax.experimental import pallas as pl
from jax.experimental.pallas import tpu as pltpu

F32 = jnp.float32
BF16 = jnp.bfloat16
I32 = jnp.int32

EPS = 1e-5
N_HEADS = 4
CHUNK = 128
SSM_GROUP = 16
SSM_STATE = 64
SSM_T = 16
N_EXPERTS = 32
TOP_K = 4
SWIGLU_LIMIT = 7.0
SWIGLU_ALPHA = 1.702
LANES = 128
SUBLANES = 8
GROUPS_PER_LANE_BLOCK = LANES // SSM_GROUP

ROW_TILE = 512
FRONT_TILE = 1024
S5_SEQS = 4
EXPERT_TILE = 1024
EXPERT_CHUNK = 256
DMA_UNROLL = 8
COMBINE_ROWS = 8
COMBINE_ROWS_NORM = 64
VMEM_LIMIT = 56 * 1024 * 1024


def _gelu(x):
    return 0.5 * x * (1.0 + jnp.tanh(math.sqrt(2.0 / math.pi) * (x + 0.044715 * (x * x * x))))


def _sigmoid(x):
    return 1.0 / (1.0 + jnp.exp(-x))


def _rms(x, w):
    return x * lax.rsqrt(jnp.mean(x * x, axis=-1, keepdims=True) + EPS) * w


def _store_rows(ref, val, row0=0):
    rows, d = val.shape
    nsl = d // LANES
    for s in range(nsl):
        ref[pl.ds(row0 * nsl + s, rows, stride=nsl), :] = val[:, s * LANES:(s + 1) * LANES]


def _load_row_slab(ref, s, rows, nsl, row0=0):
    return ref[pl.ds(row0 * nsl + s, rows, stride=nsl), :]


def _params(**kw):
    return pltpu.CompilerParams(dimension_semantics=kw.pop("sem"), vmem_limit_bytes=VMEM_LIMIT, **kw)


def _front_kernel(x_ref, nw_ref, win_ref, lnw_ref, lnb_ref, ws_ref, bst_ref, ona_ref,
                  ya_ref, us_ref, mixed_ref, us_scr):
    d_g = ya_ref.shape[1]
    hd = d_g // N_HEADS
    tm = x_ref.shape[0]
    x = x_ref[...]
    h = _rms(x, nw_ref[...]).astype(BF16)
    proj = jnp.dot(h, win_ref[...], preferred_element_type=F32)
    u = _gelu(proj[:, :d_g])
    v = _gelu(proj[:, d_g:2 * d_g])
    n_lb = us_scr.shape[0]
    for q in range(n_lb):
        us_scr[q] = proj[:, 2 * d_g + q * LANES:2 * d_g + (q + 1) * LANES]
    mu = jnp.mean(v, axis=-1, keepdims=True)
    vc = v - mu
    var = jnp.mean(vc * vc, axis=-1, keepdims=True)
    vb = (vc * lax.rsqrt(var + EPS) * lnw_ref[...] + lnb_ref[...]).astype(BF16)
    row = lax.broadcasted_iota(I32, (CHUNK, CHUNK), 0)
    col = lax.broadcasted_iota(I32, (CHUNK, CHUNK), 1)
    causal = row >= col
    for hh in range(N_HEADS):
        w = jnp.where(causal, ws_ref[hh], 0.0).astype(BF16)
        bias = bst_ref[:, hh:hh + 1]
        for c in range(tm // CHUNK):
            vv = vb[c * CHUNK:(c + 1) * CHUNK, hh * hd:(hh + 1) * hd]
            m = jnp.dot(w, vv, preferred_element_type=F32) + bias
            mixed_ref[c * CHUNK:(c + 1) * CHUNK, hh * hd:(hh + 1) * hd] = m
    ya = u * mixed_ref[...]
    ya_ref[...] = _rms(ya, ona_ref[...]).astype(BF16)
    for s in range(SSM_T):
        for q in range(n_lb):
            us_ref[s, :, q * LANES:(q + 1) * LANES] = (
                us_scr[q, pl.ds(s, tm // SSM_T, stride=SSM_T), :].astype(BF16))


def _front(x2, nw, w_in, ln_w, ln_b, w_s, b_s, on_a):
    n, d = x2.shape
    d_g = ln_w.shape[0]
    d_s = w_in.shape[1] - 2 * d_g
    tm = FRONT_TILE
    const2 = lambda i: (0, 0)
    return pl.pallas_call(
        _front_kernel,
        grid=(n // tm,),
        in_specs=[
            pl.BlockSpec((tm, d), lambda i: (i, 0)),
            pl.BlockSpec((1, d), const2),
            pl.BlockSpec(w_in.shape, const2),
            pl.BlockSpec((1, d_g), const2),
            pl.BlockSpec((1, d_g), const2),
            pl.BlockSpec(w_s.shape, lambda i: (0, 0, 0)),
            pl.BlockSpec((CHUNK, N_HEADS), const2),
            pl.BlockSpec((1, d_g), const2),
        ],
        out_specs=[
            pl.BlockSpec((tm, d_g), lambda i: (i, 0)),
            pl.BlockSpec((SSM_T, tm // SSM_T, d_s), lambda i: (0, i, 0)),
        ],
        out_shape=[
            jax.ShapeDtypeStruct((n, d_g), BF16),
            jax.ShapeDtypeStruct((SSM_T, n // SSM_T, d_s), BF16),
        ],
        scratch_shapes=[pltpu.VMEM((tm, d_g), F32), pltpu.VMEM((d_s // LANES, tm, LANES), F32)],
        compiler_params=_params(sem=("arbitrary",)),
        name="front",
    )(x2, nw.reshape(1, d), w_in.astype(BF16), ln_w.reshape(1, d_g), ln_b.reshape(1, d_g),
      w_s, b_s.T, on_a.reshape(1, d_g))


def _s5_tables(a_re, a_im, b_re, b_im, c_re, c_im, d, log_dt, n_steps):
    depth, g, p = a_re.shape
    hch = b_re.shape[-1]
    t = SSM_T
    gl = GROUPS_PER_LANE_BLOCK
    r = depth * (g // gl)
    a = lax.complex(a_re.astype(F32), a_im.astype(F32)).reshape(r, gl, p)
    dt = jnp.exp(log_dt.astype(F32)).reshape(r, gl, 1)
    dta = dt * a
    a_bar = jnp.exp(dta)
    b = lax.complex(b_re.astype(F32), b_im.astype(F32)).reshape(r, gl, p, hch)
    b_bar = ((a_bar - 1.0) / a)[..., None] * b
    c = lax.complex(c_re.astype(F32), c_im.astype(F32)).reshape(r, gl, hch, p)
    lags = jnp.arange(t + 1, dtype=F32)
    pw = jnp.exp(lags[None, :, None, None] * dta.transpose(0, 2, 1)[:, None])
    pwx = jnp.repeat(pw, hch, axis=-1)
    bb = b_bar.transpose(0, 2, 1, 3).reshape(r, p, gl * hch)
    cc = c.transpose(0, 3, 1, 2).reshape(r, p, gl * hch)
    q = pwx[:, :t] * bb[:, None]
    si = pwx[:, 1:] * cc[:, None]
    k2 = (jnp.einsum('rlpx,rpy->rlxy', q.real, cc.real, precision=lax.Precision.HIGHEST)
          - jnp.einsum('rlpx,rpy->rlxy', q.imag, cc.imag, precision=lax.Precision.HIGHEST))
    lane_g = jnp.arange(gl * hch) // hch
    k2 = jnp.where(lane_g[:, None] == lane_g[None, :], k2, 0.0)
    skip = jnp.eye(gl * hch, dtype=F32) * d.astype(F32).reshape(r, 1, gl * hch)
    k2 = k2.at[:, 0].add(skip)
    so3 = jnp.swapaxes(jnp.concatenate([q.real, q.imag], axis=2), 2, 3)
    si2 = jnp.concatenate([si.real, -si.imag], axis=2)
    steps = SSM_T * (2.0 ** jnp.arange(n_steps, dtype=F32))
    ap = jnp.exp(steps[None, :, None, None] * dta[:, None]).reshape(r, n_steps, gl * p)
    mul_same = jnp.concatenate([ap.real, ap.real], axis=-1)
    mul_swap = jnp.concatenate([-ap.imag, ap.imag], axis=-1)
    apw = jnp.stack([mul_same, mul_swap], axis=2)
    return k2.astype(BF16), so3.astype(BF16), si2.astype(BF16), apw


def _s5_kernel(u_ref, k2_ref, so3_ref, si2_ref, ap_ref, y_ref, wk_scr, wso_scr, wsi_scr, *, blocks_per_seq):
    t_blk = u_ref.shape[0]
    rows = u_ref.shape[1]
    n_state = wso_scr.shape[1]
    two_p = so3_ref.shape[3]
    p = two_p // 2
    half = n_state // 2

    @pl.when(pl.program_id(1) == 0)
    def _():
        for s in range(t_blk):
            if s > 0:
                wk_scr[s * LANES:(s + 1) * LANES, :s * LANES] = jnp.zeros((LANES, s * LANES), BF16)
            for t in range(s, t_blk):
                wk_scr[s * LANES:(s + 1) * LANES, t * LANES:(t + 1) * LANES] = k2_ref[0, t - s]
        r_e = lax.broadcasted_iota(I32, (two_p, n_state), 0)
        c_e = lax.broadcasted_iota(I32, (two_p, n_state), 1)
        spread = jnp.where((r_e // p == c_e // half) & (r_e % p == c_e % p), 1.0, 0.0).astype(BF16)
        r_g = lax.broadcasted_iota(I32, (LANES, n_state), 0) // SSM_GROUP
        c_g = (lax.broadcasted_iota(I32, (LANES, n_state), 1) % half) // p
        own = r_g == c_g
        for s in range(t_blk):
            full = jnp.dot(so3_ref[0, t_blk - 1 - s], spread, preferred_element_type=F32)
            wso_scr[s * LANES:(s + 1) * LANES, :] = jnp.where(own, full, 0.0).astype(BF16)
        r_e = lax.broadcasted_iota(I32, (n_state, two_p), 0)
        c_e = lax.broadcasted_iota(I32, (n_state, two_p), 1)
        gather = jnp.where((c_e // p == r_e // half) & (c_e % p == r_e % p), 1.0, 0.0).astype(BF16)
        r_g = (lax.broadcasted_iota(I32, (n_state, LANES), 0) % half) // p
        c_g = lax.broadcasted_iota(I32, (n_state, LANES), 1) // SSM_GROUP
        own = r_g == c_g
        for t in range(t_blk):
            full = jnp.dot(gather, si2_ref[0, t], preferred_element_type=F32)
            wsi_scr[:, t * LANES:(t + 1) * LANES] = jnp.where(own, full, 0.0).astype(BF16)

    xcat = jnp.concatenate([u_ref[s] for s in range(t_blk)], axis=1)
    x = jnp.dot(xcat, wso_scr[...], preferred_element_type=F32)
    pos = lax.broadcasted_iota(I32, (rows, n_state), 0) % blocks_per_seq
    for k in range(ap_ref.shape[1]):
        sh = 1 << k
        prev = pltpu.roll(x, sh, 0)
        prev_sw = pltpu.roll(prev, half, 1)
        upd = prev * ap_ref[0, k, 0:1, :] + prev_sw * ap_ref[0, k, 1:2, :]
        x = x + jnp.where(pos >= sh, upd, 0.0)
    xin = jnp.where(pos >= 1, pltpu.roll(x, 1, 0), 0.0).astype(BF16)
    for j in range(t_blk // 2):
        k_hi = (2 * j + 2) * LANES
        lo = 2 * j * LANES
        y = jnp.dot(xcat[:, :k_hi], wk_scr[:k_hi, lo:lo + 2 * LANES], preferred_element_type=F32)
        y = y + jnp.dot(xin, wsi_scr[:, lo:lo + 2 * LANES], preferred_element_type=F32)
        y_ref[2 * j] = y[:, :LANES].astype(BF16)
        y_ref[2 * j + 1] = y[:, LANES:].astype(BF16)


def _s5(us3, k2, so3, si2, apw, blocks_per_seq, layer):
    t_blk, n_blocks, d_s = us3.shape
    lb = d_s // LANES
    rows = S5_SEQS * blocks_per_seq
    n_steps = n_blocks // rows
    n_in = t_blk * LANES
    n_state = apw.shape[3]
    w_map = lambda a, b: (layer * lb + a, 0, 0, 0)
    io_spec = pl.BlockSpec((t_blk, rows, LANES), lambda a, b: (0, b, a))
    return pl.pallas_call(
        functools.partial(_s5_kernel, blocks_per_seq=blocks_per_seq),
        grid=(lb, n_steps),
        in_specs=[
            io_spec,
            pl.BlockSpec((1,) + k2.shape[1:], w_map),
            pl.BlockSpec((1,) + so3.shape[1:], w_map),
            pl.BlockSpec((1,) + si2.shape[1:], w_map),
            pl.BlockSpec((1,) + apw.shape[1:], w_map),
        ],
        out_specs=io_spec,
        out_shape=jax.ShapeDtypeStruct(us3.shape, BF16),
        scratch_shapes=[
            pltpu.VMEM((n_in, n_in), BF16),
            pltpu.VMEM((n_in, n_state), BF16),
            pltpu.VMEM((n_state, n_in), BF16),
        ],
        compiler_params=_params(sem=("arbitrary", "arbitrary")),
        name="s5",
    )(us3, k2, so3, si2, apw)


def _back_kernel(x_ref, ya_ref, ys_ref, gw_ref, gb_ref, onb_ref, wo_ref, fnw_ref,
                 rw_ref, rb_ref,
                 x1_ref, h2_ref, gate_ref, er_ref, cnt_ref, carry_ref, ys_scr):
    i = pl.program_id(0)
    tm = x_ref.shape[0]
    d_g = ya_ref.shape[1]

    @pl.when(i == 0)
    def _():
        carry_ref[...] = jnp.zeros_like(carry_ref)

    n_lb = ys_scr.shape[0]
    for t in range(SSM_T):
        for q in range(n_lb):
            ys_scr[q, pl.ds(t, tm // SSM_T, stride=SSM_T), :] = (
                ys_ref[t, :, q * LANES:(q + 1) * LANES].astype(F32))
    y = _gelu(jnp.concatenate([ys_scr[q] for q in range(n_lb)], axis=1))
    z = jnp.dot(y.astype(BF16), gw_ref[...], preferred_element_type=F32) + gb_ref[...]
    yb = y * _sigmoid(z)
    ybn = _rms(yb, onb_ref[...]).astype(BF16)
    x1 = (x_ref[...]
          + jnp.dot(ya_ref[...], wo_ref[:d_g, :], preferred_element_type=F32)
          + jnp.dot(ybn, wo_ref[d_g:, :], preferred_element_type=F32))
    x1_ref[...] = x1
    h2 = _rms(x1, fnw_ref[...])
    _store_rows(h2_ref, h2)

    hh = h2.astype(BF16)
    hl = (h2 - hh.astype(F32)).astype(BF16)
    p_hi = jnp.dot(hh, rw_ref[...], preferred_element_type=F32)
    p_lo = jnp.dot(hl, rw_ref[:, :LANES], preferred_element_type=F32)
    logits = p_hi[:, :LANES] + p_lo + p_hi[:, LANES:] + rb_ref[...]
    lane = lax.broadcasted_iota(I32, (tm, LANES), 1)
    lane_f = lane.astype(F32)
    neg = jnp.float32(-jnp.inf)
    work = jnp.where(lane < N_EXPERTS, logits, neg)
    vals, hots, idxs = [], [], []
    for k in range(TOP_K):
        m = jnp.max(work, axis=-1, keepdims=True)
        idx = jnp.min(jnp.where(work == m, lane_f, float(LANES)), axis=-1, keepdims=True)
        hot = lane_f == idx
        vals.append(m)
        hots.append(hot)
        idxs.append(idx.astype(I32))
        work = jnp.where(hot, neg, work)
    exps = [jnp.exp(v - vals[0]) for v in vals]
    denom = exps[0] + exps[1] + exps[2] + exps[3]
    gate = jnp.zeros((tm, LANES), F32)
    for k in range(TOP_K):
        gate = jnp.where(lane == k, exps[k] / denom, gate)
    gate_ref[...] = gate

    sel = (hots[0] | hots[1] | hots[2] | hots[3])
    sel_f = jnp.where(sel, 1.0, 0.0)
    r_i = lax.broadcasted_iota(I32, (tm, tm), 0)
    c_i = lax.broadcasted_iota(I32, (tm, tm), 1)
    lower = jnp.where(r_i > c_i, 1.0, 0.0).astype(BF16)
    cum = jnp.dot(lower, sel_f.astype(BF16), preferred_element_type=F32) + carry_ref[...]
    er = jnp.zeros((tm, LANES), I32)
    for k in range(TOP_K):
        rk = jnp.sum(jnp.where(hots[k], cum, 0.0), axis=-1, keepdims=True)
        er = jnp.where(lane == k, idxs[k], er)
        er = jnp.where(lane == TOP_K + k, rk.astype(I32), er)
    carry_ref[...] = carry_ref[...] + jnp.sum(sel_f, axis=0, keepdims=True)
    cnt_ref[...] = carry_ref[...]
    er_ref[0] = jnp.transpose(er)[:SUBLANES, :]


def _back(x2, ya, ys3, glu_w, glu_b, on_b, w_out, fn_w, router_w, router_b):
    n, d = x2.shape
    d_g = ya.shape[1]
    d_s = ys3.shape[2]
    tm = ROW_TILE
    nt = n // tm
    const2 = lambda i: (0, 0)
    rw = jnp.zeros((d, LANES), F32).at[:, :N_EXPERTS].set(router_w.astype(F32))
    rw_hi = rw.astype(BF16)
    rw_lo = (rw - rw_hi.astype(F32)).astype(BF16)
    rw_split = jnp.concatenate([rw_hi, rw_lo], axis=1)
    rb = jnp.zeros((1, LANES), F32).at[0, :N_EXPERTS].set(router_b.astype(F32))
    tok_tile = lambda i: (i, 0)
    return pl.pallas_call(
        _back_kernel,
        grid=(nt,),
        in_specs=[
            pl.BlockSpec((tm, d), tok_tile),
            pl.BlockSpec((tm, d_g), tok_tile),
            pl.BlockSpec((SSM_T, tm // SSM_T, d_s), lambda i: (0, i, 0)),
            pl.BlockSpec((d_s, d_s), const2),
            pl.BlockSpec((1, d_s), const2),
            pl.BlockSpec((1, d_s), const2),
            pl.BlockSpec((d_g + d_s, d), const2),
            pl.BlockSpec((1, d), const2),
            pl.BlockSpec((d, 2 * LANES), const2),
            pl.BlockSpec((1, LANES), const2),
        ],
        out_specs=[
            pl.BlockSpec((tm, d), tok_tile),
            pl.BlockSpec((tm * d // LANES, LANES), tok_tile),
            pl.BlockSpec((tm, LANES), tok_tile),
            pl.BlockSpec((1, SUBLANES, tm), lambda i: (i, 0, 0)),
            pl.BlockSpec((1, LANES), const2),
        ],
        out_shape=[
            jax.ShapeDtypeStruct((n, d), F32),
            jax.ShapeDtypeStruct((n * d // LANES, LANES), F32),
            jax.ShapeDtypeStruct((n, LANES), F32),
            jax.ShapeDtypeStruct((nt, SUBLANES, tm), I32),
            jax.ShapeDtypeStruct((1, LANES), F32),
        ],
        scratch_shapes=[pltpu.VMEM((1, LANES), F32), pltpu.VMEM((d_s // LANES, tm, LANES), F32)],
        compiler_params=_params(sem=("arbitrary",)),
        name="back",
    )(x2, ya, ys3, glu_w.astype(BF16), glu_b.reshape(1, d_s), on_b.reshape(1, d_s),
      w_out.astype(BF16), fn_w.reshape(1, d), rw_split, rb)


def _routing_tables(er, counts, n, row_sl):
    te = EXPERT_TILE
    n_blocks = n * TOP_K // te + N_EXPERTS
    start = jnp.cumsum(counts) - counts
    e_ids = jnp.arange(N_EXPERTS, dtype=I32)
    e_sel = er[:, :TOP_K, :, None] == e_ids
    dest = jnp.sum(jnp.where(e_sel, start, 0), axis=-1) + er[:, TOP_K:2 * TOP_K, :]
    dest = (dest * row_sl).reshape(dest.shape[0], -1)
    nb = (counts + te - 1) // te
    cum = jnp.cumsum(nb)
    total = cum[-1]
    j = jnp.arange(n_blocks, dtype=I32)
    e_j = jnp.sum((cum[None, :] <= jnp.minimum(j, total - 1)[:, None]).astype(I32), axis=1)
    valid = (j < total).astype(I32)
    own = e_ids[None, :] == e_j[:, None]
    first_j = jnp.sum(jnp.where(own, cum - nb, 0), axis=1)
    count_j = jnp.sum(jnp.where(own, counts, 0), axis=1)
    start_j = jnp.sum(jnp.where(own, start, 0), axis=1)
    first = valid * (j == first_j).astype(I32)
    later = (e_ids[None, :] > e_j[:, None]) & (nb > 0)[None, :]
    nxt = jnp.min(jnp.where(later, e_ids[None, :], N_EXPERTS), axis=1)
    nxt = jnp.where(nxt < N_EXPERTS, nxt, -1)
    rows = valid * jnp.clip(count_j - (j - first_j) * te, 0, te)
    row_off = (start_j + (j - first_j) * te) * row_sl
    return (dest.astype(I32), e_j.astype(I32), rows.astype(I32), first, nxt.astype(I32),
            row_off.astype(I32), total.reshape(1).astype(I32))


def _dispatch_kernel(dest_hbm, h_ref, xs_hbm, idx_smem, zbuf, sem_idx, sem_rows, sem_z, *, row_sl):
    i = pl.program_id(0)
    nt = pl.num_programs(0)
    tm = h_ref.shape[0] // row_sl
    blk = zbuf.shape[0]
    slot = i % 2

    def idx_copy(t, s):
        n_idx = TOP_K * tm
        dst = idx_smem.at[pl.ds(pl.multiple_of(s * n_idx, n_idx), n_idx)]
        return pltpu.make_async_copy(dest_hbm.at[t], dst, sem_idx.at[s])

    @pl.when(i == 0)
    def _():
        zbuf[...] = jnp.zeros_like(zbuf)
        cp = pltpu.make_async_copy(zbuf, xs_hbm.at[pl.ds(xs_hbm.shape[0] - blk, blk)], sem_z)
        cp.start()
        cp.wait()
        idx_copy(0, 0).start()

    @pl.when(i + 1 < nt)
    def _():
        idx_copy(i + 1, 1 - slot).start()

    idx_copy(i, slot).wait()

    def body(j, carry):
        n0 = j * DMA_UNROLL
        i0 = slot * (TOP_K * tm) + n0
        for u in range(DMA_UNROLL):
            src = h_ref.at[pl.ds(pl.multiple_of((n0 + u) * row_sl, row_sl), row_sl)]
            for k in range(TOP_K):
                dst = xs_hbm.at[pl.ds(pl.multiple_of(idx_smem[i0 + (k * tm + u)], row_sl), row_sl)]
                pltpu.make_async_copy(src, dst, sem_rows.at[k]).start(priority=k % 2)
        return carry
    lax.fori_loop(0, tm // DMA_UNROLL, body, 0)
    for k in range(TOP_K):
        pltpu.make_async_copy(h_ref, xs_hbm.at[pl.ds(0, tm * row_sl)], sem_rows.at[k]).wait()


def _dispatch(dest, h2t, n_rows, row_sl):
    tm = ROW_TILE
    nt = h2t.shape[0] // (tm * row_sl)
    return pl.pallas_call(
        functools.partial(_dispatch_kernel, row_sl=row_sl),
        grid=(nt,),
        in_specs=[
            pl.BlockSpec(memory_space=pl.ANY),
            pl.BlockSpec((tm * row_sl, LANES), lambda i: (i, 0)),
        ],
        out_specs=pl.BlockSpec(memory_space=pl.ANY),
        out_shape=jax.ShapeDtypeStruct((n_rows * row_sl, LANES), F32),
        scratch_shapes=[
            pltpu.SMEM((2 * TOP_K * tm,), I32),
            pltpu.VMEM((EXPERT_TILE * row_sl, LANES), F32),
            pltpu.SemaphoreType.DMA((2,)),
            pltpu.SemaphoreType.DMA((TOP_K,)),
            pltpu.SemaphoreType.DMA,
        ],
        compiler_params=_params(sem=("arbitrary",)),
        name="dispatch",
    )(dest, h2t)


def _expert_kernel(be_ref, nv_ref, first_ref, next_ref, off_ref, tot_ref,
                   xs_hbm, wgu_hbm, bgu_ref, wdn_hbm, bdn_ref, ys_hbm,
                   xbuf, ybuf, stage_gu, stage_dn, wgu_bf, wdn_bf, act_scr, sem_x, sem_y, sem_w, *, layer):
    i = pl.program_id(0)
    d, d_gu = wgu_bf.shape
    d_e = wdn_bf.shape[0]
    nsl = d // LANES
    blk = xbuf.shape[1]
    te = blk // nsl
    slot = i % 2
    total = tot_ref[0]

    def weight_copies(e):
        row = layer * N_EXPERTS + e
        return (pltpu.make_async_copy(wgu_hbm.at[row], stage_gu, sem_w.at[0]),
                pltpu.make_async_copy(wdn_hbm.at[row], stage_dn, sem_w.at[1]))

    def x_copy(j, s):
        src = xs_hbm.at[pl.ds(pl.multiple_of(off_ref[j], nsl), blk)]
        return pltpu.make_async_copy(src, xbuf.at[s], sem_x.at[s])

    def y_copy(j, s):
        dst = ys_hbm.at[pl.ds(pl.multiple_of(off_ref[j], nsl), blk)]
        return pltpu.make_async_copy(ybuf.at[s], dst, sem_y.at[s])

    @pl.when(i == 0)
    def _():
        for cp in weight_copies(be_ref[0]):
            cp.start(priority=1)
        x_copy(0, 0).start()
        ybuf[...] = jnp.zeros_like(ybuf)
        cp = pltpu.make_async_copy(ybuf.at[1], ys_hbm.at[pl.ds(ys_hbm.shape[0] - blk, blk)], sem_y.at[1])
        cp.start()
        cp.wait()

    @pl.when(first_ref[i] > 0)
    def _():
        for cp in weight_copies(be_ref[i]):
            cp.wait()
        rows = 128
        for r in range(0, d, rows):
            wgu_bf[r:r + rows, :] = stage_gu[r:r + rows, :].astype(BF16)
        for r in range(0, d_e, rows):
            wdn_bf[r:r + rows, :] = stage_dn[r:r + rows, :].astype(BF16)

        @pl.when(next_ref[i] >= 0)
        def _():
            for cp in weight_copies(next_ref[i]):
                cp.start(priority=1)

    def run_rows(m):
        x_ref = xbuf.at[slot]
        x = jnp.concatenate([_load_row_slab(x_ref, s, m, nsl).astype(BF16) for s in range(nsl)], axis=1)
        for c in range(d_e // EXPERT_CHUNK):
            lo = c * EXPERT_CHUNK
            hi = lo + EXPERT_CHUNK
            glu = jnp.dot(x, wgu_bf[:, lo:hi], preferred_element_type=F32) + bgu_ref[0, :, lo:hi]
            lin = (jnp.dot(x, wgu_bf[:, d_e + lo:d_e + hi], preferred_element_type=F32)
                   + bgu_ref[0, :, d_e + lo:d_e + hi])
            glu = jnp.minimum(glu, SWIGLU_LIMIT)
            lin = jnp.clip(lin, -SWIGLU_LIMIT, SWIGLU_LIMIT)
            act_scr[:m, lo:hi] = (glu * _sigmoid(SWIGLU_ALPHA * glu) * (lin + 1.0)).astype(BF16)
        y = jnp.dot(act_scr[:m, :], wdn_bf[...], preferred_element_type=F32) + bdn_ref[0]
        _store_rows(ybuf.at[slot], y)

    nv = nv_ref[i]

    @pl.when(nv > 0)
    def _():
        x_copy(i, slot).wait()

        @pl.when(i + 1 < total)
        def _():
            x_copy(i + 1, 1 - slot).start()

    sizes = [te // 4, te // 2, te]
    lower = 0
    for m in sizes:
        @pl.when((nv > lower) & (nv <= m))
        def _(m=m):
            run_rows(m)
        lower = m

    @pl.when(nv > 0)
    def _():
        @pl.when(i >= 1)
        def _():
            y_copy(i - 1, 1 - slot).wait()
        y_copy(i, slot).start()

        @pl.when(i == total - 1)
        def _():
            y_copy(i, slot).wait()


def _experts(block_expert, block_rows, block_first, block_next, block_off, total, xs,
             w_gu, b_gu, w_dn, b_dn, layer):
    d = w_gu.shape[2]
    n_blocks = block_expert.shape[0]
    blk = EXPERT_TILE * d // LANES
    d_gu = w_gu.shape[3]
    d_e = w_dn.shape[2]
    e_map3 = lambda i, be, nv, bf, bn, off, tot: (layer * N_EXPERTS + be[i], 0, 0)
    w_gu = w_gu.reshape((-1,) + w_gu.shape[2:])
    w_dn = w_dn.reshape((-1,) + w_dn.shape[2:])
    b_gu = b_gu.reshape(-1, 1, d_gu)
    b_dn = b_dn.reshape(-1, 1, d)
    grid_spec = pltpu.PrefetchScalarGridSpec(
        num_scalar_prefetch=6,
        grid=(n_blocks,),
        in_specs=[
            pl.BlockSpec(memory_space=pl.ANY),
            pl.BlockSpec(memory_space=pl.ANY),
            pl.BlockSpec((1, 1, d_gu), e_map3),
            pl.BlockSpec(memory_space=pl.ANY),
            pl.BlockSpec((1, 1, d), e_map3),
        ],
        out_specs=pl.BlockSpec(memory_space=pl.ANY),
        scratch_shapes=[
            pltpu.VMEM((2, blk, LANES), F32),
            pltpu.VMEM((2, blk, LANES), F32),
            pltpu.VMEM((d, d_gu), F32),
            pltpu.VMEM((d_e, d), F32),
            pltpu.VMEM((d, d_gu), BF16),
            pltpu.VMEM((d_e, d), BF16),
            pltpu.VMEM((EXPERT_TILE, d_e), BF16),
            pltpu.SemaphoreType.DMA((2,)),
            pltpu.SemaphoreType.DMA((2,)),
            pltpu.SemaphoreType.DMA((2,)),
        ],
    )
    return pl.pallas_call(
        functools.partial(_expert_kernel, layer=layer),
        grid_spec=grid_spec,
        out_shape=jax.ShapeDtypeStruct(xs.shape, F32),
        compiler_params=_params(sem=("arbitrary",)),
        name="experts",
    )(block_expert, block_rows, block_first, block_next, block_off, total, xs, w_gu, b_gu, w_dn, b_dn)


def _combine_kernel(dest_hbm, x1_ref, g_ref, fw_ref, ys_hbm, o_ref,
                    idx_smem, ybuf, sem_idx, sem_rows, *, final_norm):
    i = pl.program_id(0)
    nt = pl.num_programs(0)
    tm, d = x1_ref.shape
    nsl = d // LANES
    slot = i % 2

    def idx_copy(t, s):
        n_idx = TOP_K * tm
        dst = idx_smem.at[pl.ds(pl.multiple_of(s * n_idx, n_idx), n_idx)]
        return pltpu.make_async_copy(dest_hbm.at[t], dst, sem_idx.at[s])

    def issue_rows(s):
        def body(j, carry):
            n0 = j * DMA_UNROLL
            i0 = s * (TOP_K * tm) + n0
            for u in range(DMA_UNROLL):
                for k in range(TOP_K):
                    src = ys_hbm.at[pl.ds(pl.multiple_of(idx_smem[i0 + (k * tm + u)], nsl), nsl)]
                    dst = ybuf.at[s, k, pl.ds(pl.multiple_of((n0 + u) * nsl, nsl), nsl)]
                    pltpu.make_async_copy(src, dst, sem_rows.at[s, k]).start(priority=k % 2)
            return carry
        lax.fori_loop(0, tm // DMA_UNROLL, body, 0)

    def wait_rows(s):
        for k in range(TOP_K):
            pltpu.make_async_copy(ys_hbm.at[pl.ds(0, tm * nsl)], ybuf.at[s, k], sem_rows.at[s, k]).wait()

    @pl.when(i == 0)
    def _():
        cp = idx_copy(0, 0)
        cp.start()
        cp.wait()
        issue_rows(0)

        @pl.when(nt > 1)
        def _():
            idx_copy(1, 1).start()

    @pl.when(i + 1 < nt)
    def _():
        idx_copy(i + 1, 1 - slot).wait()
        issue_rows(1 - slot)

        @pl.when(i + 2 < nt)
        def _():
            idx_copy(i + 2, slot).start()

    wait_rows(slot)
    rows = COMBINE_ROWS_NORM if final_norm else COMBINE_ROWS
    for r0 in range(0, tm, rows):
        g = g_ref[r0:r0 + rows, :]
        gk = [jnp.broadcast_to(g[:, k:k + 1], (rows, LANES)) for k in range(TOP_K)]
        slabs = []
        for s in range(nsl):
            acc = x1_ref[r0:r0 + rows, s * LANES:(s + 1) * LANES]
            for k in range(TOP_K):
                acc = acc + gk[k] * _load_row_slab(ybuf.at[slot, k], s, rows, nsl, r0)
            slabs.append(acc)
        out = jnp.concatenate(slabs, axis=1)
        if final_norm:
            out = _rms(out, fw_ref[...])
        o_ref[r0:r0 + rows, :] = out


def _combine(x1, ys, gate, dest, final_w, final_norm):
    n, d = x1.shape
    tm = ROW_TILE
    nt = n // tm
    tok_tile = lambda i: (i, 0)
    return pl.pallas_call(
        functools.partial(_combine_kernel, final_norm=final_norm),
        grid=(nt,),
        in_specs=[
            pl.BlockSpec(memory_space=pl.ANY),
            pl.BlockSpec((tm, d), tok_tile),
            pl.BlockSpec((tm, LANES), tok_tile),
            pl.BlockSpec((1, d), lambda i: (0, 0)),
            pl.BlockSpec(memory_space=pl.ANY),
        ],
        out_specs=pl.BlockSpec((tm, d), tok_tile),
        out_shape=jax.ShapeDtypeStruct((n, d), F32),
        scratch_shapes=[
            pltpu.SMEM((2 * TOP_K * tm,), I32),
            pltpu.VMEM((2, TOP_K, tm * d // LANES, LANES), F32),
            pltpu.SemaphoreType.DMA((2,)),
            pltpu.SemaphoreType.DMA((2, TOP_K)),
        ],
        compiler_params=_params(sem=("arbitrary",)),
        name="combine",
    )(dest, x1, gate, final_w.reshape(1, d), ys)


def kernel(x, attn_norm_w, w_in, sgu_ln_w, sgu_ln_b, sgu_w, sgu_b, ssm_a_re, ssm_a_im, ssm_b_re, ssm_b_im, ssm_c_re, ssm_c_im, ssm_d, ssm_log_dt, ssm_glu_w, ssm_glu_b, out_norm_a, out_norm_b, w_out, ffn_norm_w, router_w, router_b, w_gate_up, b_gate_up, w_down, b_down, final_norm_w):
    b, l, d = x.shape
    n = b * l
    depth = w_in.shape[0]
    blocks_per_seq = l // SSM_T
    n_steps = max(1, (blocks_per_seq - 1).bit_length())
    n_rows = n * TOP_K + EXPERT_TILE
    x2 = x.reshape(n, d).astype(F32)
    k2, so3, si2, apw = _s5_tables(ssm_a_re, ssm_a_im, ssm_b_re, ssm_b_im, ssm_c_re, ssm_c_im, ssm_d,
                                   ssm_log_dt, n_steps)
    for layer in range(depth):
        ya, us3 = _front(x2, attn_norm_w[layer], w_in[layer], sgu_ln_w[layer], sgu_ln_b[layer],
                         sgu_w[layer], sgu_b[layer], out_norm_a[layer])
        ys3 = _s5(us3, k2, so3, si2, apw, blocks_per_seq, layer)
        x1, h2, gate, er, cnt = _back(x2, ya, ys3, ssm_glu_w[layer], ssm_glu_b[layer],
                                      out_norm_b[layer], w_out[layer], ffn_norm_w[layer],
                                      router_w[layer], router_b[layer])
        counts = cnt[0, :N_EXPERTS].astype(I32)
        dest, block_expert, block_rows, block_first, block_next, block_off, total = _routing_tables(
            er, counts, n, d // LANES)
        xs = _dispatch(dest, h2, n_rows, d // LANES)
        ys = _experts(block_expert, block_rows, block_first, block_next, block_off, total, xs,
                      w_gate_up, b_gate_up, w_down, b_down, layer)
        x2 = _combine(x1, ys, gate, dest, final_norm_w, final_norm=(layer == depth - 1))
    return x2.reshape(b, l, d).astype(x.dtype)
```

```python
import functools
import math

import jax
import jax.numpy as jnp
from jax import lax
from jax.experimental import pallas as pl
from jax.experimental.pallas import tpu as pltpu

F32 = jnp.float32
BF16 = jnp.bfloat16
I32 = jnp.int32

EPS = 1e-5
N_HEADS = 4
CHUNK = 128
SSM_GROUP = 16
SSM_STATE = 64
SSM_T = 16
N_EXPERTS = 32
TOP_K = 4
SWIGLU_LIMIT = 7.0
SWIGLU_ALPHA = 1.702
LANES = 128
SUBLANES = 8
GROUPS_PER_LANE_BLOCK = LANES // SSM_GROUP

ROW_TILE = 512
FRONT_TILE = 1024
DISPATCH_TILE = 1024
S5_SEQS = 4
EXPERT_TILE = 1024
EXPERT_CHUNK = 256
DMA_UNROLL = 8
COMBINE_ROWS = 8
COMBINE_ROWS_NORM = 64
VMEM_LIMIT = 56 * 1024 * 1024


def _gelu(x):
    return 0.5 * x * (1.0 + jnp.tanh(math.sqrt(2.0 / math.pi) * (x + 0.044715 * (x * x * x))))


def _sigmoid(x):
    return 1.0 / (1.0 + jnp.exp(-x))


def _rms(x, w):
    return x * lax.rsqrt(jnp.mean(x * x, axis=-1, keepdims=True) + EPS) * w


def _store_rows(ref, val, row0=0):
    rows, d = val.shape
    nsl = d // LANES
    for s in range(nsl):
        ref[pl.ds(row0 * nsl + s, rows, stride=nsl), :] = val[:, s * LANES:(s + 1) * LANES]


def _load_row_slab(ref, s, rows, nsl, row0=0):
    return ref[pl.ds(row0 * nsl + s, rows, stride=nsl), :]


def _params(**kw):
    return pltpu.CompilerParams(dimension_semantics=kw.pop("sem"), vmem_limit_bytes=VMEM_LIMIT, **kw)


def _front_kernel(x_ref, nw_ref, win_ref, lnw_ref, lnb_ref, ws_ref, bst_ref, ona_ref,
                  ya_ref, us_ref, mixed_ref, us_scr):
    d_g = ya_ref.shape[1]
    hd = d_g // N_HEADS
    tm = x_ref.shape[0]
    x = x_ref[...]
    h = _rms(x, nw_ref[...]).astype(BF16)
    proj = jnp.dot(h, win_ref[...], preferred_element_type=F32)
    u = _gelu(proj[:, :d_g])
    v = _gelu(proj[:, d_g:2 * d_g])
    n_lb = us_scr.shape[0]
    for q in range(n_lb):
        us_scr[q] = proj[:, 2 * d_g + q * LANES:2 * d_g + (q + 1) * LANES]
    mu = jnp.mean(v, axis=-1, keepdims=True)
    vc = v - mu
    var = jnp.mean(vc * vc, axis=-1, keepdims=True)
    vb = (vc * lax.rsqrt(var + EPS) * lnw_ref[...] + lnb_ref[...]).astype(BF16)
    row = lax.broadcasted_iota(I32, (CHUNK, CHUNK), 0)
    col = lax.broadcasted_iota(I32, (CHUNK, CHUNK), 1)
    causal = row >= col
    for hh in range(N_HEADS):
        w = jnp.where(causal, ws_ref[hh], 0.0).astype(BF16)
        bias = bst_ref[:, hh:hh + 1]
        for c in range(tm // CHUNK):
            vv = vb[c * CHUNK:(c + 1) * CHUNK, hh * hd:(hh + 1) * hd]
            m = jnp.dot(w, vv, preferred_element_type=F32) + bias
            mixed_ref[c * CHUNK:(c + 1) * CHUNK, hh * hd:(hh + 1) * hd] = m
    ya = u * mixed_ref[...]
    ya_ref[...] = _rms(ya, ona_ref[...]).astype(BF16)
    for s in range(SSM_T):
        for q in range(n_lb):
            us_ref[s, :, q * LANES:(q + 1) * LANES] = (
                us_scr[q, pl.ds(s, tm // SSM_T, stride=SSM_T), :].astype(BF16))


def _front(x2, nw, w_in, ln_w, ln_b, w_s, b_s, on_a):
    n, d = x2.shape
    d_g = ln_w.shape[0]
    d_s = w_in.shape[1] - 2 * d_g
    tm = FRONT_TILE
    const2 = lambda i: (0, 0)
    return pl.pallas_call(
        _front_kernel,
        grid=(n // tm,),
        in_specs=[
            pl.BlockSpec((tm, d), lambda i: (i, 0)),
            pl.BlockSpec((1, d), const2),
            pl.BlockSpec(w_in.shape, const2),
            pl.BlockSpec((1, d_g), const2),
            pl.BlockSpec((1, d_g), const2),
            pl.BlockSpec(w_s.shape, lambda i: (0, 0, 0)),
            pl.BlockSpec((CHUNK, N_HEADS), const2),
            pl.BlockSpec((1, d_g), const2),
        ],
        out_specs=[
            pl.BlockSpec((tm, d_g), lambda i: (i, 0)),
            pl.BlockSpec((SSM_T, tm // SSM_T, d_s), lambda i: (0, i, 0)),
        ],
        out_shape=[
            jax.ShapeDtypeStruct((n, d_g), BF16),
            jax.ShapeDtypeStruct((SSM_T, n // SSM_T, d_s), BF16),
        ],
        scratch_shapes=[pltpu.VMEM((tm, d_g), F32), pltpu.VMEM((d_s // LANES, tm, LANES), F32)],
        compiler_params=_params(sem=("arbitrary",)),
        name="front",
    )(x2, nw.reshape(1, d), w_in.astype(BF16), ln_w.reshape(1, d_g), ln_b.reshape(1, d_g),
      w_s, b_s.T, on_a.reshape(1, d_g))


def _s5_tables(a_re, a_im, b_re, b_im, c_re, c_im, d, log_dt, n_steps):
    depth, g, p = a_re.shape
    hch = b_re.shape[-1]
    t = SSM_T
    gl = GROUPS_PER_LANE_BLOCK
    r = depth * (g // gl)
    a = lax.complex(a_re.astype(F32), a_im.astype(F32)).reshape(r, gl, p)
    dt = jnp.exp(log_dt.astype(F32)).reshape(r, gl, 1)
    dta = dt * a
    a_bar = jnp.exp(dta)
    b = lax.complex(b_re.astype(F32), b_im.astype(F32)).reshape(r, gl, p, hch)
    b_bar = ((a_bar - 1.0) / a)[..., None] * b
    c = lax.complex(c_re.astype(F32), c_im.astype(F32)).reshape(r, gl, hch, p)
    lags = jnp.arange(t + 1, dtype=F32)
    pw = jnp.exp(lags[None, :, None, None] * dta.transpose(0, 2, 1)[:, None])
    pwx = jnp.repeat(pw, hch, axis=-1)
    bb = b_bar.transpose(0, 2, 1, 3).reshape(r, p, gl * hch)
    cc = c.transpose(0, 3, 1, 2).reshape(r, p, gl * hch)
    q = pwx[:, :t] * bb[:, None]
    si = pwx[:, 1:] * cc[:, None]
    k2 = (jnp.einsum('rlpx,rpy->rlxy', q.real, cc.real, precision=lax.Precision.HIGHEST)
          - jnp.einsum('rlpx,rpy->rlxy', q.imag, cc.imag, precision=lax.Precision.HIGHEST))
    lane_g = jnp.arange(gl * hch) // hch
    k2 = jnp.where(lane_g[:, None] == lane_g[None, :], k2, 0.0)
    skip = jnp.eye(gl * hch, dtype=F32) * d.astype(F32).reshape(r, 1, gl * hch)
    k2 = k2.at[:, 0].add(skip)
    so3 = jnp.swapaxes(jnp.concatenate([q.real, q.imag], axis=2), 2, 3)
    si2 = jnp.concatenate([si.real, -si.imag], axis=2)
    steps = SSM_T * (2.0 ** jnp.arange(n_steps, dtype=F32))
    ap = jnp.exp(steps[None, :, None, None] * dta[:, None]).reshape(r, n_steps, gl * p)
    mul_same = jnp.concatenate([ap.real, ap.real], axis=-1)
    mul_swap = jnp.concatenate([-ap.imag, ap.imag], axis=-1)
    apw = jnp.stack([mul_same, mul_swap], axis=2)
    return k2.astype(BF16), so3.astype(BF16), si2.astype(BF16), apw


def _s5_kernel(u_ref, k2_ref, so3_ref, si2_ref, ap_ref, y_ref, wk_scr, wso_scr, wsi_scr, *, blocks_per_seq):
    t_blk = u_ref.shape[0]
    rows = u_ref.shape[1]
    n_state = wso_scr.shape[1]
    two_p = so3_ref.shape[3]
    p = two_p // 2
    half = n_state // 2

    @pl.when(pl.program_id(1) == 0)
    def _():
        for s in range(t_blk):
            if s > 0:
                wk_scr[s * LANES:(s + 1) * LANES, :s * LANES] = jnp.zeros((LANES, s * LANES), BF16)
            for t in range(s, t_blk):
                wk_scr[s * LANES:(s + 1) * LANES, t * LANES:(t + 1) * LANES] = k2_ref[0, t - s]
        r_e = lax.broadcasted_iota(I32, (two_p, n_state), 0)
        c_e = lax.broadcasted_iota(I32, (two_p, n_state), 1)
        spread = jnp.where((r_e // p == c_e // half) & (r_e % p == c_e % p), 1.0, 0.0).astype(BF16)
        r_g = lax.broadcasted_iota(I32, (LANES, n_state), 0) // SSM_GROUP
        c_g = (lax.broadcasted_iota(I32, (LANES, n_state), 1) % half) // p
        own = r_g == c_g
        for s in range(t_blk):
            full = jnp.dot(so3_ref[0, t_blk - 1 - s], spread, preferred_element_type=F32)
            wso_scr[s * LANES:(s + 1) * LANES, :] = jnp.where(own, full, 0.0).astype(BF16)
        r_e = lax.broadcasted_iota(I32, (n_state, two_p), 0)
        c_e = lax.broadcasted_iota(I32, (n_state, two_p), 1)
        gather = jnp.where((c_e // p == r_e // half) & (c_e % p == r_e % p), 1.0, 0.0).astype(BF16)
        r_g = (lax.broadcasted_iota(I32, (n_state, LANES), 0) % half) // p
        c_g = lax.broadcasted_iota(I32, (n_state, LANES), 1) // SSM_GROUP
        own = r_g == c_g
        for t in range(t_blk):
            full = jnp.dot(gather, si2_ref[0, t], preferred_element_type=F32)
            wsi_scr[:, t * LANES:(t + 1) * LANES] = jnp.where(own, full, 0.0).astype(BF16)

    xcat = jnp.concatenate([u_ref[s] for s in range(t_blk)], axis=1)
    x = jnp.dot(xcat, wso_scr[...], preferred_element_type=F32)
    pos = lax.broadcasted_iota(I32, (rows, n_state), 0) % blocks_per_seq
    for k in range(ap_ref.shape[1]):
        sh = 1 << k
        prev = pltpu.roll(x, sh, 0)
        prev_sw = pltpu.roll(prev, half, 1)
        upd = prev * ap_ref[0, k, 0:1, :] + prev_sw * ap_ref[0, k, 1:2, :]
        x = x + jnp.where(pos >= sh, upd, 0.0)
    xin = jnp.where(pos >= 1, pltpu.roll(x, 1, 0), 0.0).astype(BF16)
    for j in range(t_blk // 2):
        k_hi = (2 * j + 2) * LANES
        lo = 2 * j * LANES
        y = jnp.dot(xcat[:, :k_hi], wk_scr[:k_hi, lo:lo + 2 * LANES], preferred_element_type=F32)
        y = y + jnp.dot(xin, wsi_scr[:, lo:lo + 2 * LANES], preferred_element_type=F32)
        y_ref[2 * j] = y[:, :LANES].astype(BF16)
        y_ref[2 * j + 1] = y[:, LANES:].astype(BF16)


def _s5(us3, k2, so3, si2, apw, blocks_per_seq, layer):
    t_blk, n_blocks, d_s = us3.shape
    lb = d_s // LANES
    rows = S5_SEQS * blocks_per_seq
    n_steps = n_blocks // rows
    n_in = t_blk * LANES
    n_state = apw.shape[3]
    w_map = lambda a, b: (layer * lb + a, 0, 0, 0)
    io_spec = pl.BlockSpec((t_blk, rows, LANES), lambda a, b: (0, b, a))
    return pl.pallas_call(
        functools.partial(_s5_kernel, blocks_per_seq=blocks_per_seq),
        grid=(lb, n_steps),
        in_specs=[
            io_spec,
            pl.BlockSpec((1,) + k2.shape[1:], w_map),
            pl.BlockSpec((1,) + so3.shape[1:], w_map),
            pl.BlockSpec((1,) + si2.shape[1:], w_map),
            pl.BlockSpec((1,) + apw.shape[1:], w_map),
        ],
        out_specs=io_spec,
        out_shape=jax.ShapeDtypeStruct(us3.shape, BF16),
        scratch_shapes=[
            pltpu.VMEM((n_in, n_in), BF16),
            pltpu.VMEM((n_in, n_state), BF16),
            pltpu.VMEM((n_state, n_in), BF16),
        ],
        compiler_params=_params(sem=("arbitrary", "arbitrary")),
        name="s5",
    )(us3, k2, so3, si2, apw)


def _back_kernel(x_ref, ya_ref, ys_ref, gw_ref, gb_ref, onb_ref, wo_ref, fnw_ref,
                 rw_ref, rb_ref,
                 x1_ref, h2_ref, gate_ref, er_ref, cnt_ref, carry_ref, ys_scr):
    i = pl.program_id(0)
    tm = x_ref.shape[0]
    d_g = ya_ref.shape[1]

    @pl.when(i == 0)
    def _():
        carry_ref[...] = jnp.zeros_like(carry_ref)

    n_lb = ys_scr.shape[0]
    for t in range(SSM_T):
        for q in range(n_lb):
            ys_scr[q, pl.ds(t, tm // SSM_T, stride=SSM_T), :] = (
                ys_ref[t, :, q * LANES:(q + 1) * LANES].astype(F32))
    y = _gelu(jnp.concatenate([ys_scr[q] for q in range(n_lb)], axis=1))
    z = jnp.dot(y.astype(BF16), gw_ref[...], preferred_element_type=F32) + gb_ref[...]
    yb = y * _sigmoid(z)
    ybn = _rms(yb, onb_ref[...]).astype(BF16)
    x1 = (x_ref[...]
          + jnp.dot(ya_ref[...], wo_ref[:d_g, :], preferred_element_type=F32)
          + jnp.dot(ybn, wo_ref[d_g:, :], preferred_element_type=F32))
    x1_ref[...] = x1
    h2 = _rms(x1, fnw_ref[...])
    _store_rows(h2_ref, h2)

    hh = h2.astype(BF16)
    hl = (h2 - hh.astype(F32)).astype(BF16)
    p_hi = jnp.dot(hh, rw_ref[...], preferred_element_type=F32)
    p_lo = jnp.dot(hl, rw_ref[:, :LANES], preferred_element_type=F32)
    logits = p_hi[:, :LANES] + p_lo + p_hi[:, LANES:] + rb_ref[...]
    lane = lax.broadcasted_iota(I32, (tm, LANES), 1)
    lane_f = lane.astype(F32)
    neg = jnp.float32(-jnp.inf)
    work = jnp.where(lane < N_EXPERTS, logits, neg)
    vals, hots, idxs = [], [], []
    for k in range(TOP_K):
        m = jnp.max(work, axis=-1, keepdims=True)
        idx = jnp.min(jnp.where(work == m, lane_f, float(LANES)), axis=-1, keepdims=True)
        hot = lane_f == idx
        vals.append(m)
        hots.append(hot)
        idxs.append(idx.astype(I32))
        work = jnp.where(hot, neg, work)
    exps = [jnp.exp(v - vals[0]) for v in vals]
    denom = exps[0] + exps[1] + exps[2] + exps[3]
    gate = jnp.zeros((tm, LANES), F32)
    for k in range(TOP_K):
        gate = jnp.where(lane == k, exps[k] / denom, gate)
    gate_ref[...] = gate

    sel = (hots[0] | hots[1] | hots[2] | hots[3])
    sel_f = jnp.where(sel, 1.0, 0.0)
    r_i = lax.broadcasted_iota(I32, (tm, tm), 0)
    c_i = lax.broadcasted_iota(I32, (tm, tm), 1)
    lower = jnp.where(r_i > c_i, 1.0, 0.0).astype(BF16)
    cum = jnp.dot(lower, sel_f.astype(BF16), preferred_element_type=F32) + carry_ref[...]
    er = jnp.zeros((tm, LANES), I32)
    for k in range(TOP_K):
        rk = jnp.sum(jnp.where(hots[k], cum, 0.0), axis=-1, keepdims=True)
        er = jnp.where(lane == k, idxs[k], er)
        er = jnp.where(lane == TOP_K + k, rk.astype(I32), er)
    carry_ref[...] = carry_ref[...] + jnp.sum(sel_f, axis=0, keepdims=True)
    cnt_ref[...] = carry_ref[...]
    er_ref[0] = jnp.transpose(er)[:SUBLANES, :]


def _back(x2, ya, ys3, glu_w, glu_b, on_b, w_out, fn_w, router_w, router_b):
    n, d = x2.shape
    d_g = ya.shape[1]
    d_s = ys3.shape[2]
    tm = ROW_TILE
    nt = n // tm
    const2 = lambda i: (0, 0)
    rw = jnp.zeros((d, LANES), F32).at[:, :N_EXPERTS].set(router_w.astype(F32))
    rw_hi = rw.astype(BF16)
    rw_lo = (rw - rw_hi.astype(F32)).astype(BF16)
    rw_split = jnp.concatenate([rw_hi, rw_lo], axis=1)
    rb = jnp.zeros((1, LANES), F32).at[0, :N_EXPERTS].set(router_b.astype(F32))
    tok_tile = lambda i: (i, 0)
    return pl.pallas_call(
        _back_kernel,
        grid=(nt,),
        in_specs=[
            pl.BlockSpec((tm, d), tok_tile),
            pl.BlockSpec((tm, d_g), tok_tile),
            pl.BlockSpec((SSM_T, tm // SSM_T, d_s), lambda i: (0, i, 0)),
            pl.BlockSpec((d_s, d_s), const2),
            pl.BlockSpec((1, d_s), const2),
            pl.BlockSpec((1, d_s), const2),
            pl.BlockSpec((d_g + d_s, d), const2),
            pl.BlockSpec((1, d), const2),
            pl.BlockSpec((d, 2 * LANES), const2),
            pl.BlockSpec((1, LANES), const2),
        ],
        out_specs=[
            pl.BlockSpec((tm, d), tok_tile),
            pl.BlockSpec((tm * d // LANES, LANES), tok_tile),
            pl.BlockSpec((tm, LANES), tok_tile),
            pl.BlockSpec((1, SUBLANES, tm), lambda i: (i, 0, 0)),
            pl.BlockSpec((1, LANES), const2),
        ],
        out_shape=[
            jax.ShapeDtypeStruct((n, d), F32),
            jax.ShapeDtypeStruct((n * d // LANES, LANES), F32),
            jax.ShapeDtypeStruct((n, LANES), F32),
            jax.ShapeDtypeStruct((nt, SUBLANES, tm), I32),
            jax.ShapeDtypeStruct((1, LANES), F32),
        ],
        scratch_shapes=[pltpu.VMEM((1, LANES), F32), pltpu.VMEM((d_s // LANES, tm, LANES), F32)],
        compiler_params=_params(sem=("arbitrary",)),
        name="back",
    )(x2, ya, ys3, glu_w.astype(BF16), glu_b.reshape(1, d_s), on_b.reshape(1, d_s),
      w_out.astype(BF16), fn_w.reshape(1, d), rw_split, rb)


def _routing_tables(er, counts, n, row_sl):
    te = EXPERT_TILE
    n_blocks = n * TOP_K // te + N_EXPERTS
    start = jnp.cumsum(counts) - counts
    e_ids = jnp.arange(N_EXPERTS, dtype=I32)
    e_sel = er[:, :TOP_K, :, None] == e_ids
    dest = jnp.sum(jnp.where(e_sel, start, 0), axis=-1) + er[:, TOP_K:2 * TOP_K, :]
    dest = (dest * row_sl).reshape(dest.shape[0], -1)
    nb = (counts + te - 1) // te
    cum = jnp.cumsum(nb)
    total = cum[-1]
    j = jnp.arange(n_blocks, dtype=I32)
    e_j = jnp.sum((cum[None, :] <= jnp.minimum(j, total - 1)[:, None]).astype(I32), axis=1)
    valid = (j < total).astype(I32)
    own = e_ids[None, :] == e_j[:, None]
    first_j = jnp.sum(jnp.where(own, cum - nb, 0), axis=1)
    count_j = jnp.sum(jnp.where(own, counts, 0), axis=1)
    start_j = jnp.sum(jnp.where(own, start, 0), axis=1)
    first = valid * (j == first_j).astype(I32)
    later = (e_ids[None, :] > e_j[:, None]) & (nb > 0)[None, :]
    nxt = jnp.min(jnp.where(later, e_ids[None, :], N_EXPERTS), axis=1)
    nxt = jnp.where(nxt < N_EXPERTS, nxt, -1)
    rows = valid * jnp.clip(count_j - (j - first_j) * te, 0, te)
    row_off = (start_j + (j - first_j) * te) * row_sl
    return (dest.astype(I32), e_j.astype(I32), rows.astype(I32), first, nxt.astype(I32),
            row_off.astype(I32), total.reshape(1).astype(I32))


def _dispatch_kernel(dest_hbm, h_ref, xs_hbm, idx_smem, zbuf, sem_idx, sem_rows, sem_z, *, row_sl):
    i = pl.program_id(0)
    nt = pl.num_programs(0)
    tm = h_ref.shape[0] // row_sl
    blk = zbuf.shape[0]
    slot = i % 2

    def idx_copy(t, s):
        n_idx = TOP_K * tm
        dst = idx_smem.at[pl.ds(pl.multiple_of(s * n_idx, n_idx), n_idx)]
        return pltpu.make_async_copy(dest_hbm.at[t], dst, sem_idx.at[s])

    @pl.when(i == 0)
    def _():
        zbuf[...] = jnp.zeros_like(zbuf)
        cp = pltpu.make_async_copy(zbuf, xs_hbm.at[pl.ds(xs_hbm.shape[0] - blk, blk)], sem_z)
        cp.start()
        cp.wait()
        idx_copy(0, 0).start()

    @pl.when(i + 1 < nt)
    def _():
        idx_copy(i + 1, 1 - slot).start()

    idx_copy(i, slot).wait()

    def body(j, carry):
        n0 = j * DMA_UNROLL
        i0 = slot * (TOP_K * tm) + n0
        for u in range(DMA_UNROLL):
            src = h_ref.at[pl.ds(pl.multiple_of((n0 + u) * row_sl, row_sl), row_sl)]
            for k in range(TOP_K):
                dst = xs_hbm.at[pl.ds(pl.multiple_of(idx_smem[i0 + (k * tm + u)], row_sl), row_sl)]
                pltpu.make_async_copy(src, dst, sem_rows.at[k]).start(priority=k % 2)
        return carry
    lax.fori_loop(0, tm // DMA_UNROLL, body, 0)
    for k in range(TOP_K):
        pltpu.make_async_copy(h_ref, xs_hbm.at[pl.ds(0, tm * row_sl)], sem_rows.at[k]).wait()


def _dispatch(dest, h2t, n_rows, row_sl):
    tm = DISPATCH_TILE
    nt = h2t.shape[0] // (tm * row_sl)
    g = tm // ROW_TILE
    dest = dest.reshape(nt, g, TOP_K, ROW_TILE).transpose(0, 2, 1, 3).reshape(nt, TOP_K * tm)
    return pl.pallas_call(
        functools.partial(_dispatch_kernel, row_sl=row_sl),
        grid=(nt,),
        in_specs=[
            pl.BlockSpec(memory_space=pl.ANY),
            pl.BlockSpec((tm * row_sl, LANES), lambda i: (i, 0)),
        ],
        out_specs=pl.BlockSpec(memory_space=pl.ANY),
        out_shape=jax.ShapeDtypeStruct((n_rows * row_sl, LANES), F32),
        scratch_shapes=[
            pltpu.SMEM((2 * TOP_K * tm,), I32),
            pltpu.VMEM((EXPERT_TILE * row_sl, LANES), F32),
            pltpu.SemaphoreType.DMA((2,)),
            pltpu.SemaphoreType.DMA((TOP_K,)),
            pltpu.SemaphoreType.DMA,
        ],
        compiler_params=_params(sem=("arbitrary",)),
        name="dispatch",
    )(dest, h2t)


def _expert_kernel(be_ref, nv_ref, first_ref, next_ref, off_ref, tot_ref,
                   xs_hbm, wgu_hbm, bgu_ref, wdn_hbm, bdn_ref, ys_hbm,
                   xbuf, ybuf, stage_gu, stage_dn, wgu_bf, wdn_bf, act_scr, sem_x, sem_y, sem_w, *, layer):
    i = pl.program_id(0)
    d, d_gu = wgu_bf.shape
    d_e = wdn_bf.shape[0]
    nsl = d // LANES
    blk = xbuf.shape[1]
    te = blk // nsl
    slot = i % 2
    total = tot_ref[0]

    def weight_copies(e):
        row = layer * N_EXPERTS + e
        return (pltpu.make_async_copy(wgu_hbm.at[row], stage_gu, sem_w.at[0]),
                pltpu.make_async_copy(wdn_hbm.at[row], stage_dn, sem_w.at[1]))

    def x_copy(j, s):
        src = xs_hbm.at[pl.ds(pl.multiple_of(off_ref[j], nsl), blk)]
        return pltpu.make_async_copy(src, xbuf.at[s], sem_x.at[s])

    def y_copy(j, s):
        dst = ys_hbm.at[pl.ds(pl.multiple_of(off_ref[j], nsl), blk)]
        return pltpu.make_async_copy(ybuf.at[s], dst, sem_y.at[s])

    @pl.when(i == 0)
    def _():
        for cp in weight_copies(be_ref[0]):
            cp.start(priority=1)
        x_copy(0, 0).start()
        ybuf[...] = jnp.zeros_like(ybuf)
        cp = pltpu.make_async_copy(ybuf.at[1], ys_hbm.at[pl.ds(ys_hbm.shape[0] - blk, blk)], sem_y.at[1])
        cp.start()
        cp.wait()

    @pl.when(first_ref[i] > 0)
    def _():
        for cp in weight_copies(be_ref[i]):
            cp.wait()
        rows = 128
        for r in range(0, d, rows):
            wgu_bf[r:r + rows, :] = stage_gu[r:r + rows, :].astype(BF16)
        for r in range(0, d_e, rows):
            wdn_bf[r:r + rows, :] = stage_dn[r:r + rows, :].astype(BF16)

        @pl.when(next_ref[i] >= 0)
        def _():
            for cp in weight_copies(next_ref[i]):
                cp.start(priority=1)

    def run_rows(m):
        x_ref = xbuf.at[slot]
        x = jnp.concatenate([_load_row_slab(x_ref, s, m, nsl).astype(BF16) for s in range(nsl)], axis=1)
        for c in range(d_e // EXPERT_CHUNK):
            lo = c * EXPERT_CHUNK
            hi = lo + EXPERT_CHUNK
            glu = jnp.dot(x, wgu_bf[:, lo:hi], preferred_element_type=F32) + bgu_ref[0, :, lo:hi]
            lin = (jnp.dot(x, wgu_bf[:, d_e + lo:d_e + hi], preferred_element_type=F32)
                   + bgu_ref[0, :, d_e + lo:d_e + hi])
            glu = jnp.minimum(glu, SWIGLU_LIMIT)
            lin = jnp.clip(lin, -SWIGLU_LIMIT, SWIGLU_LIMIT)
            act_scr[:m, lo:hi] = (glu * _sigmoid(SWIGLU_ALPHA * glu) * (lin + 1.0)).astype(BF16)
        y = jnp.dot(act_scr[:m, :], wdn_bf[...], preferred_element_type=F32) + bdn_ref[0]
        _store_rows(ybuf.at[slot], y)

    nv = nv_ref[i]

    @pl.when(nv > 0)
    def _():
        x_copy(i, slot).wait()

        @pl.when(i + 1 < total)
        def _():
            x_copy(i + 1, 1 - slot).start()

    sizes = [te // 4, te // 2, te]
    lower = 0
    for m in sizes:
        @pl.when((nv > lower) & (nv <= m))
        def _(m=m):
            run_rows(m)
        lower = m

    @pl.when(nv > 0)
    def _():
        @pl.when(i >= 1)
        def _():
            y_copy(i - 1, 1 - slot).wait()
        y_copy(i, slot).start()

        @pl.when(i == total - 1)
        def _():
            y_copy(i, slot).wait()


def _experts(block_expert, block_rows, block_first, block_next, block_off, total, xs,
             w_gu, b_gu, w_dn, b_dn, layer):
    d = w_gu.shape[2]
    n_blocks = block_expert.shape[0]
    blk = EXPERT_TILE * d // LANES
    d_gu = w_gu.shape[3]
    d_e = w_dn.shape[2]
    e_map3 = lambda i, be, nv, bf, bn, off, tot: (layer * N_EXPERTS + be[i], 0, 0)
    w_gu = w_gu.reshape((-1,) + w_gu.shape[2:])
    w_dn = w_dn.reshape((-1,) + w_dn.shape[2:])
    b_gu = b_gu.reshape(-1, 1, d_gu)
    b_dn = b_dn.reshape(-1, 1, d)
    grid_spec = pltpu.PrefetchScalarGridSpec(
        num_scalar_prefetch=6,
        grid=(n_blocks,),
        in_specs=[
            pl.BlockSpec(memory_space=pl.ANY),
            pl.BlockSpec(memory_space=pl.ANY),
            pl.BlockSpec((1, 1, d_gu), e_map3),
            pl.BlockSpec(memory_space=pl.ANY),
            pl.BlockSpec((1, 1, d), e_map3),
        ],
        out_specs=pl.BlockSpec(memory_space=pl.ANY),
        scratch_shapes=[
            pltpu.VMEM((2, blk, LANES), F32),
            pltpu.VMEM((2, blk, LANES), F32),
            pltpu.VMEM((d, d_gu), F32),
            pltpu.VMEM((d_e, d), F32),
            pltpu.VMEM((d, d_gu), BF16),
            pltpu.VMEM((d_e, d), BF16),
            pltpu.VMEM((EXPERT_TILE, d_e), BF16),
            pltpu.SemaphoreType.DMA((2,)),
            pltpu.SemaphoreType.DMA((2,)),
            pltpu.SemaphoreType.DMA((2,)),
        ],
    )
    return pl.pallas_call(
        functools.partial(_expert_kernel, layer=layer),
        grid_spec=grid_spec,
        out_shape=jax.ShapeDtypeStruct(xs.shape, F32),
        compiler_params=_params(sem=("arbitrary",)),
        name="experts",
    )(block_expert, block_rows, block_first, block_next, block_off, total, xs, w_gu, b_gu, w_dn, b_dn)


def _combine_kernel(dest_hbm, x1_ref, g_ref, fw_ref, ys_hbm, o_ref,
                    idx_smem, ybuf, sem_idx, sem_rows, *, final_norm):
    i = pl.program_id(0)
    nt = pl.num_programs(0)
    tm, d = x1_ref.shape
    nsl = d // LANES
    slot = i % 2

    def idx_copy(t, s):
        n_idx = TOP_K * tm
        dst = idx_smem.at[pl.ds(pl.multiple_of(s * n_idx, n_idx), n_idx)]
        return pltpu.make_async_copy(dest_hbm.at[t], dst, sem_idx.at[s])

    def issue_rows(s):
        def body(j, carry):
            n0 = j * DMA_UNROLL
            i0 = s * (TOP_K * tm) + n0
            for u in range(DMA_UNROLL):
                for k in range(TOP_K):
                    src = ys_hbm.at[pl.ds(pl.multiple_of(idx_smem[i0 + (k * tm + u)], nsl), nsl)]
                    dst = ybuf.at[s, k, pl.ds(pl.multiple_of((n0 + u) * nsl, nsl), nsl)]
                    pltpu.make_async_copy(src, dst, sem_rows.at[s, k]).start(priority=k % 2)
            return carry
        lax.fori_loop(0, tm // DMA_UNROLL, body, 0)

    def wait_rows(s):
        for k in range(TOP_K):
            pltpu.make_async_copy(ys_hbm.at[pl.ds(0, tm * nsl)], ybuf.at[s, k], sem_rows.at[s, k]).wait()

    @pl.when(i == 0)
    def _():
        cp = idx_copy(0, 0)
        cp.start()
        cp.wait()
        issue_rows(0)

        @pl.when(nt > 1)
        def _():
            idx_copy(1, 1).start()

    @pl.when(i + 1 < nt)
    def _():
        idx_copy(i + 1, 1 - slot).wait()
        issue_rows(1 - slot)

        @pl.when(i + 2 < nt)
        def _():
            idx_copy(i + 2, slot).start()

    wait_rows(slot)
    rows = COMBINE_ROWS_NORM if final_norm else COMBINE_ROWS
    for r0 in range(0, tm, rows):
        g = g_ref[r0:r0 + rows, :]
        gk = [jnp.broadcast_to(g[:, k:k + 1], (rows, LANES)) for k in range(TOP_K)]
        slabs = []
        for s in range(nsl):
            acc = x1_ref[r0:r0 + rows, s * LANES:(s + 1) * LANES]
            for k in range(TOP_K):
                acc = acc + gk[k] * _load_row_slab(ybuf.at[slot, k], s, rows, nsl, r0)
            slabs.append(acc)
        out = jnp.concatenate(slabs, axis=1)
        if final_norm:
            out = _rms(out, fw_ref[...])
        o_ref[r0:r0 + rows, :] = out


def _combine(x1, ys, gate, dest, final_w, final_norm):
    n, d = x1.shape
    tm = ROW_TILE
    nt = n // tm
    tok_tile = lambda i: (i, 0)
    return pl.pallas_call(
        functools.partial(_combine_kernel, final_norm=final_norm),
        grid=(nt,),
        in_specs=[
            pl.BlockSpec(memory_space=pl.ANY),
            pl.BlockSpec((tm, d), tok_tile),
            pl.BlockSpec((tm, LANES), tok_tile),
            pl.BlockSpec((1, d), lambda i: (0, 0)),
            pl.BlockSpec(memory_space=pl.ANY),
        ],
        out_specs=pl.BlockSpec((tm, d), tok_tile),
        out_shape=jax.ShapeDtypeStruct((n, d), F32),
        scratch_shapes=[
            pltpu.SMEM((2 * TOP_K * tm,), I32),
            pltpu.VMEM((2, TOP_K, tm * d // LANES, LANES), F32),
            pltpu.SemaphoreType.DMA((2,)),
            pltpu.SemaphoreType.DMA((2, TOP_K)),
        ],
        compiler_params=_params(sem=("arbitrary",)),
        name="combine",
    )(dest, x1, gate, final_w.reshape(1, d), ys)


def kernel(x, attn_norm_w, w_in, sgu_ln_w, sgu_ln_b, sgu_w, sgu_b, ssm_a_re, ssm_a_im, ssm_b_re, ssm_b_im, ssm_c_re, ssm_c_im, ssm_d, ssm_log_dt, ssm_glu_w, ssm_glu_b, out_norm_a, out_norm_b, w_out, ffn_norm_w, router_w, router_b, w_gate_up, b_gate_up, w_down, b_down, final_norm_w):
    b, l, d = x.shape
    n = b * l
    depth = w_in.shape[0]
    blocks_per_seq = l // SSM_T
    n_steps = max(1, (blocks_per_seq - 1).bit_length())
    n_rows = n * TOP_K + EXPERT_TILE
    x2 = x.reshape(n, d).astype(F32)
    k2, so3, si2, apw = _s5_tables(ssm_a_re, ssm_a_im, ssm_b_re, ssm_b_im, ssm_c_re, ssm_c_im, ssm_d,
                                   ssm_log_dt, n_steps)
    for layer in range(depth):
        ya, us3 = _front(x2, attn_norm_w[layer], w_in[layer], sgu_ln_w[layer], sgu_ln_b[layer],
                         sgu_w[layer], sgu_b[layer], out_norm_a[layer])
        ys3 = _s5(us3, k2, so3, si2, apw, blocks_per_seq, layer)
        x1, h2, gate, er, cnt = _back(x2, ya, ys3, ssm_glu_w[layer], ssm_glu_b[layer],
                                      out_norm_b[layer], w_out[layer], ffn_norm_w[layer],
                                      router_w[layer], router_b[layer])
        counts = cnt[0, :N_EXPERTS].astype(I32)
        dest, block_expert, block_rows, block_first, block_next, block_off, total = _routing_tables(
            er, counts, n, d // LANES)
        xs = _dispatch(dest, h2, n_rows, d // LANES)
        ys = _experts(block_expert, block_rows, block_first, block_next, block_off, total, xs,
                      w_gate_up, b_gate_up, w_down, b_down, layer)
        x2 = _combine(x1, ys, gate, dest, final_norm_w, final_norm=(layer == depth - 1))
    return x2.reshape(b, l, d).astype(x.dtype)
```

```python
import functools
import math

import jax
import jax.numpy as jnp
from jax import lax
from jax.experimental import pallas as pl
from jax.experimental.pallas import tpu as pltpu

F32 = jnp.float32
BF16 = jnp.bfloat16
I32 = jnp.int32

EPS = 1e-5
N_HEADS = 4
CHUNK = 128
SSM_GROUP = 16
SSM_STATE = 64
SSM_T = 16
N_EXPERTS = 32
TOP_K = 4
SWIGLU_LIMIT = 7.0
SWIGLU_ALPHA = 1.702
LANES = 128
SUBLANES = 8
GROUPS_PER_LANE_BLOCK = LANES // SSM_GROUP

ROW_TILE = 512
FRONT_TILE = 1024
DISPATCH_TILE = 2048
S5_SEQS = 4
EXPERT_TILE = 1024
EXPERT_CHUNK = 256
DMA_UNROLL = 8
COMBINE_ROWS = 8
COMBINE_ROWS_NORM = 64
VMEM_LIMIT = 56 * 1024 * 1024


def _gelu(x):
    return 0.5 * x * (1.0 + jnp.tanh(math.sqrt(2.0 / math.pi) * (x + 0.044715 * (x * x * x))))


def _sigmoid(x):
    return 1.0 / (1.0 + jnp.exp(-x))


def _rms(x, w):
    return x * lax.rsqrt(jnp.mean(x * x, axis=-1, keepdims=True) + EPS) * w


def _store_rows(ref, val, row0=0):
    rows, d = val.shape
    nsl = d // LANES
    for s in range(nsl):
        ref[pl.ds(row0 * nsl + s, rows, stride=nsl), :] = val[:, s * LANES:(s + 1) * LANES]


def _load_row_slab(ref, s, rows, nsl, row0=0):
    return ref[pl.ds(row0 * nsl + s, rows, stride=nsl), :]


def _params(**kw):
    return pltpu.CompilerParams(dimension_semantics=kw.pop("sem"), vmem_limit_bytes=VMEM_LIMIT, **kw)


def _front_kernel(x_ref, nw_ref, win_ref, lnw_ref, lnb_ref, ws_ref, bst_ref, ona_ref,
                  ya_ref, us_ref, mixed_ref, us_scr):
    d_g = ya_ref.shape[1]
    hd = d_g // N_HEADS
    tm = x_ref.shape[0]
    x = x_ref[...]
    h = _rms(x, nw_ref[...]).astype(BF16)
    proj = jnp.dot(h, win_ref[...], preferred_element_type=F32)
    u = _gelu(proj[:, :d_g])
    v = _gelu(proj[:, d_g:2 * d_g])
    n_lb = us_scr.shape[0]
    for q in range(n_lb):
        us_scr[q] = proj[:, 2 * d_g + q * LANES:2 * d_g + (q + 1) * LANES]
    mu = jnp.mean(v, axis=-1, keepdims=True)
    vc = v - mu
    var = jnp.mean(vc * vc, axis=-1, keepdims=True)
    vb = (vc * lax.rsqrt(var + EPS) * lnw_ref[...] + lnb_ref[...]).astype(BF16)
    row = lax.broadcasted_iota(I32, (CHUNK, CHUNK), 0)
    col = lax.broadcasted_iota(I32, (CHUNK, CHUNK), 1)
    causal = row >= col
    for hh in range(N_HEADS):
        w = jnp.where(causal, ws_ref[hh], 0.0).astype(BF16)
        bias = bst_ref[:, hh:hh + 1]
        for c in range(tm // CHUNK):
            vv = vb[c * CHUNK:(c + 1) * CHUNK, hh * hd:(hh + 1) * hd]
            m = jnp.dot(w, vv, preferred_element_type=F32) + bias
            mixed_ref[c * CHUNK:(c + 1) * CHUNK, hh * hd:(hh + 1) * hd] = m
    ya = u * mixed_ref[...]
    ya_ref[...] = _rms(ya, ona_ref[...]).astype(BF16)
    for s in range(SSM_T):
        for q in range(n_lb):
            us_ref[s, :, q * LANES:(q + 1) * LANES] = (
                us_scr[q, pl.ds(s, tm // SSM_T, stride=SSM_T), :].astype(BF16))


def _front(x2, nw, w_in, ln_w, ln_b, w_s, b_s, on_a):
    n, d = x2.shape
    d_g = ln_w.shape[0]
    d_s = w_in.shape[1] - 2 * d_g
    tm = FRONT_TILE
    const2 = lambda i: (0, 0)
    return pl.pallas_call(
        _front_kernel,
        grid=(n // tm,),
        in_specs=[
            pl.BlockSpec((tm, d), lambda i: (i, 0)),
            pl.BlockSpec((1, d), const2),
            pl.BlockSpec(w_in.shape, const2),
            pl.BlockSpec((1, d_g), const2),
            pl.BlockSpec((1, d_g), const2),
            pl.BlockSpec(w_s.shape, lambda i: (0, 0, 0)),
            pl.BlockSpec((CHUNK, N_HEADS), const2),
            pl.BlockSpec((1, d_g), const2),
        ],
        out_specs=[
            pl.BlockSpec((tm, d_g), lambda i: (i, 0)),
            pl.BlockSpec((SSM_T, tm // SSM_T, d_s), lambda i: (0, i, 0)),
        ],
        out_shape=[
            jax.ShapeDtypeStruct((n, d_g), BF16),
            jax.ShapeDtypeStruct((SSM_T, n // SSM_T, d_s), BF16),
        ],
        scratch_shapes=[pltpu.VMEM((tm, d_g), F32), pltpu.VMEM((d_s // LANES, tm, LANES), F32)],
        compiler_params=_params(sem=("arbitrary",)),
        name="front",
    )(x2, nw.reshape(1, d), w_in.astype(BF16), ln_w.reshape(1, d_g), ln_b.reshape(1, d_g),
      w_s, b_s.T, on_a.reshape(1, d_g))


def _s5_tables(a_re, a_im, b_re, b_im, c_re, c_im, d, log_dt, n_steps):
    depth, g, p = a_re.shape
    hch = b_re.shape[-1]
    t = SSM_T
    gl = GROUPS_PER_LANE_BLOCK
    r = depth * (g // gl)
    a = lax.complex(a_re.astype(F32), a_im.astype(F32)).reshape(r, gl, p)
    dt = jnp.exp(log_dt.astype(F32)).reshape(r, gl, 1)
    dta = dt * a
    a_bar = jnp.exp(dta)
    b = lax.complex(b_re.astype(F32), b_im.astype(F32)).reshape(r, gl, p, hch)
    b_bar = ((a_bar - 1.0) / a)[..., None] * b
    c = lax.complex(c_re.astype(F32), c_im.astype(F32)).reshape(r, gl, hch, p)
    lags = jnp.arange(t + 1, dtype=F32)
    pw = jnp.exp(lags[None, :, None, None] * dta.transpose(0, 2, 1)[:, None])
    pwx = jnp.repeat(pw, hch, axis=-1)
    bb = b_bar.transpose(0, 2, 1, 3).reshape(r, p, gl * hch)
    cc = c.transpose(0, 3, 1, 2).reshape(r, p, gl * hch)
    q = pwx[:, :t] * bb[:, None]
    si = pwx[:, 1:] * cc[:, None]
    k2 = (jnp.einsum('rlpx,rpy->rlxy', q.real, cc.real, precision=lax.Precision.HIGHEST)
          - jnp.einsum('rlpx,rpy->rlxy', q.imag, cc.imag, precision=lax.Precision.HIGHEST))
    lane_g = jnp.arange(gl * hch) // hch
    k2 = jnp.where(lane_g[:, None] == lane_g[None, :], k2, 0.0)
    skip = jnp.eye(gl * hch, dtype=F32) * d.astype(F32).reshape(r, 1, gl * hch)
    k2 = k2.at[:, 0].add(skip)
    so3 = jnp.swapaxes(jnp.concatenate([q.real, q.imag], axis=2), 2, 3)
    si2 = jnp.concatenate([si.real, -si.imag], axis=2)
    steps = SSM_T * (2.0 ** jnp.arange(n_steps, dtype=F32))
    ap = jnp.exp(steps[None, :, None, None] * dta[:, None]).reshape(r, n_steps, gl * p)
    mul_same = jnp.concatenate([ap.real, ap.real], axis=-1)
    mul_swap = jnp.concatenate([-ap.imag, ap.imag], axis=-1)
    apw = jnp.stack([mul_same, mul_swap], axis=2)
    return k2.astype(BF16), so3.astype(BF16), si2.astype(BF16), apw


def _s5_kernel(u_ref, k2_ref, so3_ref, si2_ref, ap_ref, y_ref, wk_scr, wso_scr, wsi_scr, *, blocks_per_seq):
    t_blk = u_ref.shape[0]
    rows = u_ref.shape[1]
    n_state = wso_scr.shape[1]
    two_p = so3_ref.shape[3]
    p = two_p // 2
    half = n_state // 2

    @pl.when(pl.program_id(1) == 0)
    def _():
        for s in range(t_blk):
            if s > 0:
                wk_scr[s * LANES:(s + 1) * LANES, :s * LANES] = jnp.zeros((LANES, s * LANES), BF16)
            for t in range(s, t_blk):
                wk_scr[s * LANES:(s + 1) * LANES, t * LANES:(t + 1) * LANES] = k2_ref[0, t - s]
        r_e = lax.broadcasted_iota(I32, (two_p, n_state), 0)
        c_e = lax.broadcasted_iota(I32, (two_p, n_state), 1)
        spread = jnp.where((r_e // p == c_e // half) & (r_e % p == c_e % p), 1.0, 0.0).astype(BF16)
        r_g = lax.broadcasted_iota(I32, (LANES, n_state), 0) // SSM_GROUP
        c_g = (lax.broadcasted_iota(I32, (LANES, n_state), 1) % half) // p
        own = r_g == c_g
        for s in range(t_blk):
            full = jnp.dot(so3_ref[0, t_blk - 1 - s], spread, preferred_element_type=F32)
            wso_scr[s * LANES:(s + 1) * LANES, :] = jnp.where(own, full, 0.0).astype(BF16)
        r_e = lax.broadcasted_iota(I32, (n_state, two_p), 0)
        c_e = lax.broadcasted_iota(I32, (n_state, two_p), 1)
        gather = jnp.where((c_e // p == r_e // half) & (c_e % p == r_e % p), 1.0, 0.0).astype(BF16)
        r_g = (lax.broadcasted_iota(I32, (n_state, LANES), 0) % half) // p
        c_g = lax.broadcasted_iota(I32, (n_state, LANES), 1) // SSM_GROUP
        own = r_g == c_g
        for t in range(t_blk):
            full = jnp.dot(gather, si2_ref[0, t], preferred_element_type=F32)
            wsi_scr[:, t * LANES:(t + 1) * LANES] = jnp.where(own, full, 0.0).astype(BF16)

    xcat = jnp.concatenate([u_ref[s] for s in range(t_blk)], axis=1)
    x = jnp.dot(xcat, wso_scr[...], preferred_element_type=F32)
    pos = lax.broadcasted_iota(I32, (rows, n_state), 0) % blocks_per_seq
    for k in range(ap_ref.shape[1]):
        sh = 1 << k
        prev = pltpu.roll(x, sh, 0)
        prev_sw = pltpu.roll(prev, half, 1)
        upd = prev * ap_ref[0, k, 0:1, :] + prev_sw * ap_ref[0, k, 1:2, :]
        x = x + jnp.where(pos >= sh, upd, 0.0)
    xin = jnp.where(pos >= 1, pltpu.roll(x, 1, 0), 0.0).astype(BF16)
    for j in range(t_blk // 2):
        k_hi = (2 * j + 2) * LANES
        lo = 2 * j * LANES
        y = jnp.dot(xcat[:, :k_hi], wk_scr[:k_hi, lo:lo + 2 * LANES], preferred_element_type=F32)
        y = y + jnp.dot(xin, wsi_scr[:, lo:lo + 2 * LANES], preferred_element_type=F32)
        y_ref[2 * j] = y[:, :LANES].astype(BF16)
        y_ref[2 * j + 1] = y[:, LANES:].astype(BF16)


def _s5(us3, k2, so3, si2, apw, blocks_per_seq, layer):
    t_blk, n_blocks, d_s = us3.shape
    lb = d_s // LANES
    rows = S5_SEQS * blocks_per_seq
    n_steps = n_blocks // rows
    n_in = t_blk * LANES
    n_state = apw.shape[3]
    w_map = lambda a, b: (layer * lb + a, 0, 0, 0)
    io_spec = pl.BlockSpec((t_blk, rows, LANES), lambda a, b: (0, b, a))
    return pl.pallas_call(
        functools.partial(_s5_kernel, blocks_per_seq=blocks_per_seq),
        grid=(lb, n_steps),
        in_specs=[
            io_spec,
            pl.BlockSpec((1,) + k2.shape[1:], w_map),
            pl.BlockSpec((1,) + so3.shape[1:], w_map),
            pl.BlockSpec((1,) + si2.shape[1:], w_map),
            pl.BlockSpec((1,) + apw.shape[1:], w_map),
        ],
        out_specs=io_spec,
        out_shape=jax.ShapeDtypeStruct(us3.shape, BF16),
        scratch_shapes=[
            pltpu.VMEM((n_in, n_in), BF16),
            pltpu.VMEM((n_in, n_state), BF16),
            pltpu.VMEM((n_state, n_in), BF16),
        ],
        compiler_params=_params(sem=("arbitrary", "arbitrary")),
        name="s5",
    )(us3, k2, so3, si2, apw)


def _back_kernel(x_ref, ya_ref, ys_ref, gw_ref, gb_ref, onb_ref, wo_ref, fnw_ref,
                 rw_ref, rb_ref,
                 x1_ref, h2_ref, gate_ref, er_ref, cnt_ref, carry_ref, ys_scr):
    i = pl.program_id(0)
    tm = x_ref.shape[0]
    d_g = ya_ref.shape[1]

    @pl.when(i == 0)
    def _():
        carry_ref[...] = jnp.zeros_like(carry_ref)

    n_lb = ys_scr.shape[0]
    for t in range(SSM_T):
        for q in range(n_lb):
            ys_scr[q, pl.ds(t, tm // SSM_T, stride=SSM_T), :] = (
                ys_ref[t, :, q * LANES:(q + 1) * LANES].astype(F32))
    y = _gelu(jnp.concatenate([ys_scr[q] for q in range(n_lb)], axis=1))
    z = jnp.dot(y.astype(BF16), gw_ref[...], preferred_element_type=F32) + gb_ref[...]
    yb = y * _sigmoid(z)
    ybn = _rms(yb, onb_ref[...]).astype(BF16)
    x1 = (x_ref[...]
          + jnp.dot(ya_ref[...], wo_ref[:d_g, :], preferred_element_type=F32)
          + jnp.dot(ybn, wo_ref[d_g:, :], preferred_element_type=F32))
    x1_ref[...] = x1
    h2 = _rms(x1, fnw_ref[...])
    _store_rows(h2_ref, h2)

    hh = h2.astype(BF16)
    hl = (h2 - hh.astype(F32)).astype(BF16)
    p_hi = jnp.dot(hh, rw_ref[...], preferred_element_type=F32)
    p_lo = jnp.dot(hl, rw_ref[:, :LANES], preferred_element_type=F32)
    logits = p_hi[:, :LANES] + p_lo + p_hi[:, LANES:] + rb_ref[...]
    lane = lax.broadcasted_iota(I32, (tm, LANES), 1)
    lane_f = lane.astype(F32)
    neg = jnp.float32(-jnp.inf)
    work = jnp.where(lane < N_EXPERTS, logits, neg)
    vals, hots, idxs = [], [], []
    for k in range(TOP_K):
        m = jnp.max(work, axis=-1, keepdims=True)
        idx = jnp.min(jnp.where(work == m, lane_f, float(LANES)), axis=-1, keepdims=True)
        hot = lane_f == idx
        vals.append(m)
        hots.append(hot)
        idxs.append(idx.astype(I32))
        work = jnp.where(hot, neg, work)
    exps = [jnp.exp(v - vals[0]) for v in vals]
    denom = exps[0] + exps[1] + exps[2] + exps[3]
    gate = jnp.zeros((tm, LANES), F32)
    for k in range(TOP_K):
        gate = jnp.where(lane == k, exps[k] / denom, gate)
    gate_ref[...] = gate

    sel = (hots[0] | hots[1] | hots[2] | hots[3])
    sel_f = jnp.where(sel, 1.0, 0.0)
    r_i = lax.broadcasted_iota(I32, (tm, tm), 0)
    c_i = lax.broadcasted_iota(I32, (tm, tm), 1)
    lower = jnp.where(r_i > c_i, 1.0, 0.0).astype(BF16)
    cum = jnp.dot(lower, sel_f.astype(BF16), preferred_element_type=F32) + carry_ref[...]
    er = jnp.zeros((tm, LANES), I32)
    for k in range(TOP_K):
        rk = jnp.sum(jnp.where(hots[k], cum, 0.0), axis=-1, keepdims=True)
        er = jnp.where(lane == k, idxs[k], er)
        er = jnp.where(lane == TOP_K + k, rk.astype(I32), er)
    carry_ref[...] = carry_ref[...] + jnp.sum(sel_f, axis=0, keepdims=True)
    cnt_ref[...] = carry_ref[...]
    er_ref[0] = jnp.transpose(er)[:SUBLANES, :]


def _back(x2, ya, ys3, glu_w, glu_b, on_b, w_out, fn_w, router_w, router_b):
    n, d = x2.shape
    d_g = ya.shape[1]
    d_s = ys3.shape[2]
    tm = ROW_TILE
    nt = n // tm
    const2 = lambda i: (0, 0)
    rw = jnp.zeros((d, LANES), F32).at[:, :N_EXPERTS].set(router_w.astype(F32))
    rw_hi = rw.astype(BF16)
    rw_lo = (rw - rw_hi.astype(F32)).astype(BF16)
    rw_split = jnp.concatenate([rw_hi, rw_lo], axis=1)
    rb = jnp.zeros((1, LANES), F32).at[0, :N_EXPERTS].set(router_b.astype(F32))
    tok_tile = lambda i: (i, 0)
    return pl.pallas_call(
        _back_kernel,
        grid=(nt,),
        in_specs=[
            pl.BlockSpec((tm, d), tok_tile),
            pl.BlockSpec((tm, d_g), tok_tile),
            pl.BlockSpec((SSM_T, tm // SSM_T, d_s), lambda i: (0, i, 0)),
            pl.BlockSpec((d_s, d_s), const2),
            pl.BlockSpec((1, d_s), const2),
            pl.BlockSpec((1, d_s), const2),
            pl.BlockSpec((d_g + d_s, d), const2),
            pl.BlockSpec((1, d), const2),
            pl.BlockSpec((d, 2 * LANES), const2),
            pl.BlockSpec((1, LANES), const2),
        ],
        out_specs=[
            pl.BlockSpec((tm, d), tok_tile),
            pl.BlockSpec((tm * d // LANES, LANES), tok_tile),
            pl.BlockSpec((tm, LANES), tok_tile),
            pl.BlockSpec((1, SUBLANES, tm), lambda i: (i, 0, 0)),
            pl.BlockSpec((1, LANES), const2),
        ],
        out_shape=[
            jax.ShapeDtypeStruct((n, d), F32),
            jax.ShapeDtypeStruct((n * d // LANES, LANES), F32),
            jax.ShapeDtypeStruct((n, LANES), F32),
            jax.ShapeDtypeStruct((nt, SUBLANES, tm), I32),
            jax.ShapeDtypeStruct((1, LANES), F32),
        ],
        scratch_shapes=[pltpu.VMEM((1, LANES), F32), pltpu.VMEM((d_s // LANES, tm, LANES), F32)],
        compiler_params=_params(sem=("arbitrary",)),
        name="back",
    )(x2, ya, ys3, glu_w.astype(BF16), glu_b.reshape(1, d_s), on_b.reshape(1, d_s),
      w_out.astype(BF16), fn_w.reshape(1, d), rw_split, rb)


def _routing_tables(er, counts, n, row_sl):
    te = EXPERT_TILE
    n_blocks = n * TOP_K // te + N_EXPERTS
    start = jnp.cumsum(counts) - counts
    e_ids = jnp.arange(N_EXPERTS, dtype=I32)
    e_sel = er[:, :TOP_K, :, None] == e_ids
    dest = jnp.sum(jnp.where(e_sel, start, 0), axis=-1) + er[:, TOP_K:2 * TOP_K, :]
    dest = (dest * row_sl).reshape(dest.shape[0], -1)
    nb = (counts + te - 1) // te
    cum = jnp.cumsum(nb)
    total = cum[-1]
    j = jnp.arange(n_blocks, dtype=I32)
    e_j = jnp.sum((cum[None, :] <= jnp.minimum(j, total - 1)[:, None]).astype(I32), axis=1)
    valid = (j < total).astype(I32)
    own = e_ids[None, :] == e_j[:, None]
    first_j = jnp.sum(jnp.where(own, cum - nb, 0), axis=1)
    count_j = jnp.sum(jnp.where(own, counts, 0), axis=1)
    start_j = jnp.sum(jnp.where(own, start, 0), axis=1)
    first = valid * (j == first_j).astype(I32)
    later = (e_ids[None, :] > e_j[:, None]) & (nb > 0)[None, :]
    nxt = jnp.min(jnp.where(later, e_ids[None, :], N_EXPERTS), axis=1)
    nxt = jnp.where(nxt < N_EXPERTS, nxt, -1)
    rows = valid * jnp.clip(count_j - (j - first_j) * te, 0, te)
    row_off = (start_j + (j - first_j) * te) * row_sl
    return (dest.astype(I32), e_j.astype(I32), rows.astype(I32), first, nxt.astype(I32),
            row_off.astype(I32), total.reshape(1).astype(I32))


def _dispatch_kernel(dest_hbm, h_ref, xs_hbm, idx_smem, zbuf, sem_idx, sem_rows, sem_z, *, row_sl):
    i = pl.program_id(0)
    nt = pl.num_programs(0)
    tm = h_ref.shape[0] // row_sl
    blk = zbuf.shape[0]
    slot = i % 2

    def idx_copy(t, s):
        n_idx = TOP_K * tm
        dst = idx_smem.at[pl.ds(pl.multiple_of(s * n_idx, n_idx), n_idx)]
        return pltpu.make_async_copy(dest_hbm.at[t], dst, sem_idx.at[s])

    @pl.when(i == 0)
    def _():
        zbuf[...] = jnp.zeros_like(zbuf)
        cp = pltpu.make_async_copy(zbuf, xs_hbm.at[pl.ds(xs_hbm.shape[0] - blk, blk)], sem_z)
        cp.start()
        cp.wait()
        idx_copy(0, 0).start()

    @pl.when(i + 1 < nt)
    def _():
        idx_copy(i + 1, 1 - slot).start()

    idx_copy(i, slot).wait()

    def body(j, carry):
        n0 = j * DMA_UNROLL
        i0 = slot * (TOP_K * tm) + n0
        for u in range(DMA_UNROLL):
            src = h_ref.at[pl.ds(pl.multiple_of((n0 + u) * row_sl, row_sl), row_sl)]
            for k in range(TOP_K):
                dst = xs_hbm.at[pl.ds(pl.multiple_of(idx_smem[i0 + (k * tm + u)], row_sl), row_sl)]
                pltpu.make_async_copy(src, dst, sem_rows.at[k]).start(priority=k % 2)
        return carry
    lax.fori_loop(0, tm // DMA_UNROLL, body, 0)
    for k in range(TOP_K):
        pltpu.make_async_copy(h_ref, xs_hbm.at[pl.ds(0, tm * row_sl)], sem_rows.at[k]).wait()


def _dispatch(dest, h2t, n_rows, row_sl):
    tm = DISPATCH_TILE
    nt = h2t.shape[0] // (tm * row_sl)
    g = tm // ROW_TILE
    dest = dest.reshape(nt, g, TOP_K, ROW_TILE).transpose(0, 2, 1, 3).reshape(nt, TOP_K * tm)
    return pl.pallas_call(
        functools.partial(_dispatch_kernel, row_sl=row_sl),
        grid=(nt,),
        in_specs=[
            pl.BlockSpec(memory_space=pl.ANY),
            pl.BlockSpec((tm * row_sl, LANES), lambda i: (i, 0)),
        ],
        out_specs=pl.BlockSpec(memory_space=pl.ANY),
        out_shape=jax.ShapeDtypeStruct((n_rows * row_sl, LANES), F32),
        scratch_shapes=[
            pltpu.SMEM((2 * TOP_K * tm,), I32),
            pltpu.VMEM((EXPERT_TILE * row_sl, LANES), F32),
            pltpu.SemaphoreType.DMA((2,)),
            pltpu.SemaphoreType.DMA((TOP_K,)),
            pltpu.SemaphoreType.DMA,
        ],
        compiler_params=_params(sem=("arbitrary",)),
        name="dispatch",
    )(dest, h2t)


def _expert_kernel(be_ref, nv_ref, first_ref, next_ref, off_ref, tot_ref,
                   xs_hbm, wgu_hbm, bgu_ref, wdn_hbm, bdn_ref, ys_hbm,
                   xbuf, ybuf, stage_gu, stage_dn, wgu_bf, wdn_bf, act_scr, sem_x, sem_y, sem_w, *, layer):
    i = pl.program_id(0)
    d, d_gu = wgu_bf.shape
    d_e = wdn_bf.shape[0]
    nsl = d // LANES
    blk = xbuf.shape[1]
    te = blk // nsl
    slot = i % 2
    total = tot_ref[0]

    def weight_copies(e):
        row = layer * N_EXPERTS + e
        return (pltpu.make_async_copy(wgu_hbm.at[row], stage_gu, sem_w.at[0]),
                pltpu.make_async_copy(wdn_hbm.at[row], stage_dn, sem_w.at[1]))

    def x_copy(j, s):
        src = xs_hbm.at[pl.ds(pl.multiple_of(off_ref[j], nsl), blk)]
        return pltpu.make_async_copy(src, xbuf.at[s], sem_x.at[s])

    def y_copy(j, s):
        dst = ys_hbm.at[pl.ds(pl.multiple_of(off_ref[j], nsl), blk)]
        return pltpu.make_async_copy(ybuf.at[s], dst, sem_y.at[s])

    @pl.when(i == 0)
    def _():
        for cp in weight_copies(be_ref[0]):
            cp.start(priority=1)
        x_copy(0, 0).start()
        ybuf[...] = jnp.zeros_like(ybuf)
        cp = pltpu.make_async_copy(ybuf.at[1], ys_hbm.at[pl.ds(ys_hbm.shape[0] - blk, blk)], sem_y.at[1])
        cp.start()
        cp.wait()

    @pl.when(first_ref[i] > 0)
    def _():
        for cp in weight_copies(be_ref[i]):
            cp.wait()
        rows = 128
        for r in range(0, d, rows):
            wgu_bf[r:r + rows, :] = stage_gu[r:r + rows, :].astype(BF16)
        for r in range(0, d_e, rows):
            wdn_bf[r:r + rows, :] = stage_dn[r:r + rows, :].astype(BF16)

        @pl.when(next_ref[i] >= 0)
        def _():
            for cp in weight_copies(next_ref[i]):
                cp.start(priority=1)

    def run_rows(m):
        x_ref = xbuf.at[slot]
        x = jnp.concatenate([_load_row_slab(x_ref, s, m, nsl).astype(BF16) for s in range(nsl)], axis=1)
        for c in range(d_e // EXPERT_CHUNK):
            lo = c * EXPERT_CHUNK
            hi = lo + EXPERT_CHUNK
            glu = jnp.dot(x, wgu_bf[:, lo:hi], preferred_element_type=F32) + bgu_ref[0, :, lo:hi]
            lin = (jnp.dot(x, wgu_bf[:, d_e + lo:d_e + hi], preferred_element_type=F32)
                   + bgu_ref[0, :, d_e + lo:d_e + hi])
            glu = jnp.minimum(glu, SWIGLU_LIMIT)
            lin = jnp.clip(lin, -SWIGLU_LIMIT, SWIGLU_LIMIT)
            act_scr[:m, lo:hi] = (glu * _sigmoid(SWIGLU_ALPHA * glu) * (lin + 1.0)).astype(BF16)
        y = jnp.dot(act_scr[:m, :], wdn_bf[...], preferred_element_type=F32) + bdn_ref[0]
        _store_rows(ybuf.at[slot], y)

    nv = nv_ref[i]

    @pl.when(nv > 0)
    def _():
        x_copy(i, slot).wait()

        @pl.when(i + 1 < total)
        def _():
            x_copy(i + 1, 1 - slot).start()

    sizes = [te // 4, te // 2, te]
    lower = 0
    for m in sizes:
        @pl.when((nv > lower) & (nv <= m))
        def _(m=m):
            run_rows(m)
        lower = m

    @pl.when(nv > 0)
    def _():
        @pl.when(i >= 1)
        def _():
            y_copy(i - 1, 1 - slot).wait()
        y_copy(i, slot).start()

        @pl.when(i == total - 1)
        def _():
            y_copy(i, slot).wait()


def _experts(block_expert, block_rows, block_first, block_next, block_off, total, xs,
             w_gu, b_gu, w_dn, b_dn, layer):
    d = w_gu.shape[2]
    n_blocks = block_expert.shape[0]
    blk = EXPERT_TILE * d // LANES
    d_gu = w_gu.shape[3]
    d_e = w_dn.shape[2]
    e_map3 = lambda i, be, nv, bf, bn, off, tot: (layer * N_EXPERTS + be[i], 0, 0)
    w_gu = w_gu.reshape((-1,) + w_gu.shape[2:])
    w_dn = w_dn.reshape((-1,) + w_dn.shape[2:])
    b_gu = b_gu.reshape(-1, 1, d_gu)
    b_dn = b_dn.reshape(-1, 1, d)
    grid_spec = pltpu.PrefetchScalarGridSpec(
        num_scalar_prefetch=6,
        grid=(n_blocks,),
        in_specs=[
            pl.BlockSpec(memory_space=pl.ANY),
            pl.BlockSpec(memory_space=pl.ANY),
            pl.BlockSpec((1, 1, d_gu), e_map3),
            pl.BlockSpec(memory_space=pl.ANY),
            pl.BlockSpec((1, 1, d), e_map3),
        ],
        out_specs=pl.BlockSpec(memory_space=pl.ANY),
        scratch_shapes=[
            pltpu.VMEM((2, blk, LANES), F32),
            pltpu.VMEM((2, blk, LANES), F32),
            pltpu.VMEM((d, d_gu), F32),
            pltpu.VMEM((d_e, d), F32),
            pltpu.VMEM((d, d_gu), BF16),
            pltpu.VMEM((d_e, d), BF16),
            pltpu.VMEM((EXPERT_TILE, d_e), BF16),
            pltpu.SemaphoreType.DMA((2,)),
            pltpu.SemaphoreType.DMA((2,)),
            pltpu.SemaphoreType.DMA((2,)),
        ],
    )
    return pl.pallas_call(
        functools.partial(_expert_kernel, layer=layer),
        grid_spec=grid_spec,
        out_shape=jax.ShapeDtypeStruct(xs.shape, F32),
        compiler_params=_params(sem=("arbitrary",)),
        name="experts",
    )(block_expert, block_rows, block_first, block_next, block_off, total, xs, w_gu, b_gu, w_dn, b_dn)


def _combine_kernel(dest_hbm, x1_ref, g_ref, fw_ref, ys_hbm, o_ref,
                    idx_smem, ybuf, sem_idx, sem_rows, *, final_norm):
    i = pl.program_id(0)
    nt = pl.num_programs(0)
    tm, d = x1_ref.shape
    nsl = d // LANES
    slot = i % 2

    def idx_copy(t, s):
        n_idx = TOP_K * tm
        dst = idx_smem.at[pl.ds(pl.multiple_of(s * n_idx, n_idx), n_idx)]
        return pltpu.make_async_copy(dest_hbm.at[t], dst, sem_idx.at[s])

    def issue_rows(s):
        def body(j, carry):
            n0 = j * DMA_UNROLL
            i0 = s * (TOP_K * tm) + n0
            for u in range(DMA_UNROLL):
                for k in range(TOP_K):
                    src = ys_hbm.at[pl.ds(pl.multiple_of(idx_smem[i0 + (k * tm + u)], nsl), nsl)]
                    dst = ybuf.at[s, k, pl.ds(pl.multiple_of((n0 + u) * nsl, nsl), nsl)]
                    pltpu.make_async_copy(src, dst, sem_rows.at[s, k]).start(priority=k % 2)
            return carry
        lax.fori_loop(0, tm // DMA_UNROLL, body, 0)

    def wait_rows(s):
        for k in range(TOP_K):
            pltpu.make_async_copy(ys_hbm.at[pl.ds(0, tm * nsl)], ybuf.at[s, k], sem_rows.at[s, k]).wait()

    @pl.when(i == 0)
    def _():
        cp = idx_copy(0, 0)
        cp.start()
        cp.wait()
        issue_rows(0)

        @pl.when(nt > 1)
        def _():
            idx_copy(1, 1).start()

    @pl.when(i + 1 < nt)
    def _():
        idx_copy(i + 1, 1 - slot).wait()
        issue_rows(1 - slot)

        @pl.when(i + 2 < nt)
        def _():
            idx_copy(i + 2, slot).start()

    wait_rows(slot)
    rows = COMBINE_ROWS_NORM if final_norm else COMBINE_ROWS
    for r0 in range(0, tm, rows):
        g = g_ref[r0:r0 + rows, :]
        gk = [jnp.broadcast_to(g[:, k:k + 1], (rows, LANES)) for k in range(TOP_K)]
        slabs = []
        for s in range(nsl):
            acc = x1_ref[r0:r0 + rows, s * LANES:(s + 1) * LANES]
            for k in range(TOP_K):
                acc = acc + gk[k] * _load_row_slab(ybuf.at[slot, k], s, rows, nsl, r0)
            slabs.append(acc)
        out = jnp.concatenate(slabs, axis=1)
        if final_norm:
            out = _rms(out, fw_ref[...])
        o_ref[r0:r0 + rows, :] = out


def _combine(x1, ys, gate, dest, final_w, final_norm):
    n, d = x1.shape
    tm = ROW_TILE
    nt = n // tm
    tok_tile = lambda i: (i, 0)
    return pl.pallas_call(
        functools.partial(_combine_kernel, final_norm=final_norm),
        grid=(nt,),
        in_specs=[
            pl.BlockSpec(memory_space=pl.ANY),
            pl.BlockSpec((tm, d), tok_tile),
            pl.BlockSpec((tm, LANES), tok_tile),
            pl.BlockSpec((1, d), lambda i: (0, 0)),
            pl.BlockSpec(memory_space=pl.ANY),
        ],
        out_specs=pl.BlockSpec((tm, d), tok_tile),
        out_shape=jax.ShapeDtypeStruct((n, d), F32),
        scratch_shapes=[
            pltpu.SMEM((2 * TOP_K * tm,), I32),
            pltpu.VMEM((2, TOP_K, tm * d // LANES, LANES), F32),
            pltpu.SemaphoreType.DMA((2,)),
            pltpu.SemaphoreType.DMA((2, TOP_K)),
        ],
        compiler_params=_params(sem=("arbitrary",)),
        name="combine",
    )(dest, x1, gate, final_w.reshape(1, d), ys)


def kernel(x, attn_norm_w, w_in, sgu_ln_w, sgu_ln_b, sgu_w, sgu_b, ssm_a_re, ssm_a_im, ssm_b_re, ssm_b_im, ssm_c_re, ssm_c_im, ssm_d, ssm_log_dt, ssm_glu_w, ssm_glu_b, out_norm_a, out_norm_b, w_out, ffn_norm_w, router_w, router_b, w_gate_up, b_gate_up, w_down, b_down, final_norm_w):
    b, l, d = x.shape
    n = b * l
    depth = w_in.shape[0]
    blocks_per_seq = l // SSM_T
    n_steps = max(1, (blocks_per_seq - 1).bit_length())
    n_rows = n * TOP_K + EXPERT_TILE
    x2 = x.reshape(n, d).astype(F32)
    k2, so3, si2, apw = _s5_tables(ssm_a_re, ssm_a_im, ssm_b_re, ssm_b_im, ssm_c_re, ssm_c_im, ssm_d,
                                   ssm_log_dt, n_steps)
    for layer in range(depth):
        ya, us3 = _front(x2, attn_norm_w[layer], w_in[layer], sgu_ln_w[layer], sgu_ln_b[layer],
                         sgu_w[layer], sgu_b[layer], out_norm_a[layer])
        ys3 = _s5(us3, k2, so3, si2, apw, blocks_per_seq, layer)
        x1, h2, gate, er, cnt = _back(x2, ya, ys3, ssm_glu_w[layer], ssm_glu_b[layer],
                                      out_norm_b[layer], w_out[layer], ffn_norm_w[layer],
                                      router_w[layer], router_b[layer])
        counts = cnt[0, :N_EXPERTS].astype(I32)
        dest, block_expert, block_rows, block_first, block_next, block_off, total = _routing_tables(
            er, counts, n, d // LANES)
        xs = _dispatch(dest, h2, n_rows, d // LANES)
        ys = _experts(block_expert, block_rows, block_first, block_next, block_off, total, xs,
                      w_gate_up, b_gate_up, w_down, b_down, layer)
        x2 = _combine(x1, ys, gate, dest, final_norm_w, final_norm=(layer == depth - 1))
    return x2.reshape(b, l, d).astype(x.dtype)
```

```python
import functools
import math

import jax
import jax.numpy as jnp
from jax import lax
from jax.experimental import pallas as pl
from jax.experimental.pallas import tpu as pltpu

F32 = jnp.float32
BF16 = jnp.bfloat16
I32 = jnp.int32

EPS = 1e-5
N_HEADS = 4
CHUNK = 128
SSM_GROUP = 16
SSM_STATE = 64
SSM_T = 16
N_EXPERTS = 32
TOP_K = 4
SWIGLU_LIMIT = 7.0
SWIGLU_ALPHA = 1.702
LANES = 128
SUBLANES = 8
GROUPS_PER_LANE_BLOCK = LANES // SSM_GROUP

ROW_TILE = 512
FRONT_TILE = 1024
DISPATCH_TILE = 2048
S5_SEQS = 4
EXPERT_TILE = 1024
EXPERT_CHUNK = 256
DMA_UNROLL = 8
COMBINE_ROWS = 8
COMBINE_ROWS_NORM = 32
VMEM_LIMIT = 56 * 1024 * 1024


def _gelu(x):
    return 0.5 * x * (1.0 + jnp.tanh(math.sqrt(2.0 / math.pi) * (x + 0.044715 * (x * x * x))))


def _sigmoid(x):
    return 1.0 / (1.0 + jnp.exp(-x))


def _rms(x, w):
    return x * lax.rsqrt(jnp.mean(x * x, axis=-1, keepdims=True) + EPS) * w


def _store_rows(ref, val, row0=0):
    rows, d = val.shape
    nsl = d // LANES
    for s in range(nsl):
        ref[pl.ds(row0 * nsl + s, rows, stride=nsl), :] = val[:, s * LANES:(s + 1) * LANES]


def _load_row_slab(ref, s, rows, nsl, row0=0):
    return ref[pl.ds(row0 * nsl + s, rows, stride=nsl), :]


def _params(**kw):
    return pltpu.CompilerParams(dimension_semantics=kw.pop("sem"), vmem_limit_bytes=VMEM_LIMIT, **kw)


def _front_kernel(x_ref, nw_ref, win_ref, lnw_ref, lnb_ref, ws_ref, bst_ref, ona_ref,
                  ya_ref, us_ref, mixed_ref, us_scr):
    d_g = ya_ref.shape[1]
    hd = d_g // N_HEADS
    tm = x_ref.shape[0]
    x = x_ref[...]
    h = _rms(x, nw_ref[...]).astype(BF16)
    proj = jnp.dot(h, win_ref[...], preferred_element_type=F32)
    u = _gelu(proj[:, :d_g])
    v = _gelu(proj[:, d_g:2 * d_g])
    n_lb = us_scr.shape[0]
    for q in range(n_lb):
        us_scr[q] = proj[:, 2 * d_g + q * LANES:2 * d_g + (q + 1) * LANES]
    mu = jnp.mean(v, axis=-1, keepdims=True)
    vc = v - mu
    var = jnp.mean(vc * vc, axis=-1, keepdims=True)
    vb = (vc * lax.rsqrt(var + EPS) * lnw_ref[...] + lnb_ref[...]).astype(BF16)
    row = lax.broadcasted_iota(I32, (CHUNK, CHUNK), 0)
    col = lax.broadcasted_iota(I32, (CHUNK, CHUNK), 1)
    causal = row >= col
    for hh in range(N_HEADS):
        w = jnp.where(causal, ws_ref[hh], 0.0).astype(BF16)
        bias = bst_ref[:, hh:hh + 1]
        for c in range(tm // CHUNK):
            vv = vb[c * CHUNK:(c + 1) * CHUNK, hh * hd:(hh + 1) * hd]
            m = jnp.dot(w, vv, preferred_element_type=F32) + bias
            mixed_ref[c * CHUNK:(c + 1) * CHUNK, hh * hd:(hh + 1) * hd] = m
    ya = u * mixed_ref[...]
    ya_ref[...] = _rms(ya, ona_ref[...]).astype(BF16)
    for s in range(SSM_T):
        for q in range(n_lb):
            us_ref[s, :, q * LANES:(q + 1) * LANES] = (
                us_scr[q, pl.ds(s, tm // SSM_T, stride=SSM_T), :].astype(BF16))


def _front(x2, nw, w_in, ln_w, ln_b, w_s, b_s, on_a):
    n, d = x2.shape
    d_g = ln_w.shape[0]
    d_s = w_in.shape[1] - 2 * d_g
    tm = FRONT_TILE
    const2 = lambda i: (0, 0)
    return pl.pallas_call(
        _front_kernel,
        grid=(n // tm,),
        in_specs=[
            pl.BlockSpec((tm, d), lambda i: (i, 0)),
            pl.BlockSpec((1, d), const2),
            pl.BlockSpec(w_in.shape, const2),
            pl.BlockSpec((1, d_g), const2),
            pl.BlockSpec((1, d_g), const2),
            pl.BlockSpec(w_s.shape, lambda i: (0, 0, 0)),
            pl.BlockSpec((CHUNK, N_HEADS), const2),
            pl.BlockSpec((1, d_g), const2),
        ],
        out_specs=[
            pl.BlockSpec((tm, d_g), lambda i: (i, 0)),
            pl.BlockSpec((SSM_T, tm // SSM_T, d_s), lambda i: (0, i, 0)),
        ],
        out_shape=[
            jax.ShapeDtypeStruct((n, d_g), BF16),
            jax.ShapeDtypeStruct((SSM_T, n // SSM_T, d_s), BF16),
        ],
        scratch_shapes=[pltpu.VMEM((tm, d_g), F32), pltpu.VMEM((d_s // LANES, tm, LANES), F32)],
        compiler_params=_params(sem=("arbitrary",)),
        name="front",
    )(x2, nw.reshape(1, d), w_in.astype(BF16), ln_w.reshape(1, d_g), ln_b.reshape(1, d_g),
      w_s, b_s.T, on_a.reshape(1, d_g))


def _s5_tables(a_re, a_im, b_re, b_im, c_re, c_im, d, log_dt, n_steps):
    depth, g, p = a_re.shape
    hch = b_re.shape[-1]
    t = SSM_T
    gl = GROUPS_PER_LANE_BLOCK
    r = depth * (g // gl)
    a = lax.complex(a_re.astype(F32), a_im.astype(F32)).reshape(r, gl, p)
    dt = jnp.exp(log_dt.astype(F32)).reshape(r, gl, 1)
    dta = dt * a
    a_bar = jnp.exp(dta)
    b = lax.complex(b_re.astype(F32), b_im.astype(F32)).reshape(r, gl, p, hch)
    b_bar = ((a_bar - 1.0) / a)[..., None] * b
    c = lax.complex(c_re.astype(F32), c_im.astype(F32)).reshape(r, gl, hch, p)
    lags = jnp.arange(t + 1, dtype=F32)
    pw = jnp.exp(lags[None, :, None, None] * dta.transpose(0, 2, 1)[:, None])
    pwx = jnp.repeat(pw, hch, axis=-1)
    bb = b_bar.transpose(0, 2, 1, 3).reshape(r, p, gl * hch)
    cc = c.transpose(0, 3, 1, 2).reshape(r, p, gl * hch)
    q = pwx[:, :t] * bb[:, None]
    si = pwx[:, 1:] * cc[:, None]
    k2 = (jnp.einsum('rlpx,rpy->rlxy', q.real, cc.real, precision=lax.Precision.HIGHEST)
          - jnp.einsum('rlpx,rpy->rlxy', q.imag, cc.imag, precision=lax.Precision.HIGHEST))
    lane_g = jnp.arange(gl * hch) // hch
    k2 = jnp.where(lane_g[:, None] == lane_g[None, :], k2, 0.0)
    skip = jnp.eye(gl * hch, dtype=F32) * d.astype(F32).reshape(r, 1, gl * hch)
    k2 = k2.at[:, 0].add(skip)
    so3 = jnp.swapaxes(jnp.concatenate([q.real, q.imag], axis=2), 2, 3)
    si2 = jnp.concatenate([si.real, -si.imag], axis=2)
    steps = SSM_T * (2.0 ** jnp.arange(n_steps, dtype=F32))
    ap = jnp.exp(steps[None, :, None, None] * dta[:, None]).reshape(r, n_steps, gl * p)
    mul_same = jnp.concatenate([ap.real, ap.real], axis=-1)
    mul_swap = jnp.concatenate([-ap.imag, ap.imag], axis=-1)
    apw = jnp.stack([mul_same, mul_swap], axis=2)
    return k2.astype(BF16), so3.astype(BF16), si2.astype(BF16), apw


def _s5_kernel(u_ref, k2_ref, so3_ref, si2_ref, ap_ref, y_ref, wk_scr, wso_scr, wsi_scr, *, blocks_per_seq):
    t_blk = u_ref.shape[0]
    rows = u_ref.shape[1]
    n_state = wso_scr.shape[1]
    two_p = so3_ref.shape[3]
    p = two_p // 2
    half = n_state // 2

    @pl.when(pl.program_id(1) == 0)
    def _():
        for s in range(t_blk):
            if s > 0:
                wk_scr[s * LANES:(s + 1) * LANES, :s * LANES] = jnp.zeros((LANES, s * LANES), BF16)
            for t in range(s, t_blk):
                wk_scr[s * LANES:(s + 1) * LANES, t * LANES:(t + 1) * LANES] = k2_ref[0, t - s]
        r_e = lax.broadcasted_iota(I32, (two_p, n_state), 0)
        c_e = lax.broadcasted_iota(I32, (two_p, n_state), 1)
        spread = jnp.where((r_e // p == c_e // half) & (r_e % p == c_e % p), 1.0, 0.0).astype(BF16)
        r_g = lax.broadcasted_iota(I32, (LANES, n_state), 0) // SSM_GROUP
        c_g = (lax.broadcasted_iota(I32, (LANES, n_state), 1) % half) // p
        own = r_g == c_g
        for s in range(t_blk):
            full = jnp.dot(so3_ref[0, t_blk - 1 - s], spread, preferred_element_type=F32)
            wso_scr[s * LANES:(s + 1) * LANES, :] = jnp.where(own, full, 0.0).astype(BF16)
        r_e = lax.broadcasted_iota(I32, (n_state, two_p), 0)
        c_e = lax.broadcasted_iota(I32, (n_state, two_p), 1)
        gather = jnp.where((c_e // p == r_e // half) & (c_e % p == r_e % p), 1.0, 0.0).astype(BF16)
        r_g = (lax.broadcasted_iota(I32, (n_state, LANES), 0) % half) // p
        c_g = lax.broadcasted_iota(I32, (n_state, LANES), 1) // SSM_GROUP
        own = r_g == c_g
        for t in range(t_blk):
            full = jnp.dot(gather, si2_ref[0, t], preferred_element_type=F32)
            wsi_scr[:, t * LANES:(t + 1) * LANES] = jnp.where(own, full, 0.0).astype(BF16)

    xcat = jnp.concatenate([u_ref[s] for s in range(t_blk)], axis=1)
    x = jnp.dot(xcat, wso_scr[...], preferred_element_type=F32)
    pos = lax.broadcasted_iota(I32, (rows, n_state), 0) % blocks_per_seq
    for k in range(ap_ref.shape[1]):
        sh = 1 << k
        prev = pltpu.roll(x, sh, 0)
        prev_sw = pltpu.roll(prev, half, 1)
        upd = prev * ap_ref[0, k, 0:1, :] + prev_sw * ap_ref[0, k, 1:2, :]
        x = x + jnp.where(pos >= sh, upd, 0.0)
    xin = jnp.where(pos >= 1, pltpu.roll(x, 1, 0), 0.0).astype(BF16)
    for j in range(t_blk // 2):
        k_hi = (2 * j + 2) * LANES
        lo = 2 * j * LANES
        y = jnp.dot(xcat[:, :k_hi], wk_scr[:k_hi, lo:lo + 2 * LANES], preferred_element_type=F32)
        y = y + jnp.dot(xin, wsi_scr[:, lo:lo + 2 * LANES], preferred_element_type=F32)
        y_ref[2 * j] = y[:, :LANES].astype(BF16)
        y_ref[2 * j + 1] = y[:, LANES:].astype(BF16)


def _s5(us3, k2, so3, si2, apw, blocks_per_seq, layer):
    t_blk, n_blocks, d_s = us3.shape
    lb = d_s // LANES
    rows = S5_SEQS * blocks_per_seq
    n_steps = n_blocks // rows
    n_in = t_blk * LANES
    n_state = apw.shape[3]
    w_map = lambda a, b: (layer * lb + a, 0, 0, 0)
    io_spec = pl.BlockSpec((t_blk, rows, LANES), lambda a, b: (0, b, a))
    return pl.pallas_call(
        functools.partial(_s5_kernel, blocks_per_seq=blocks_per_seq),
        grid=(lb, n_steps),
        in_specs=[
            io_spec,
            pl.BlockSpec((1,) + k2.shape[1:], w_map),
            pl.BlockSpec((1,) + so3.shape[1:], w_map),
            pl.BlockSpec((1,) + si2.shape[1:], w_map),
            pl.BlockSpec((1,) + apw.shape[1:], w_map),
        ],
        out_specs=io_spec,
        out_shape=jax.ShapeDtypeStruct(us3.shape, BF16),
        scratch_shapes=[
            pltpu.VMEM((n_in, n_in), BF16),
            pltpu.VMEM((n_in, n_state), BF16),
            pltpu.VMEM((n_state, n_in), BF16),
        ],
        compiler_params=_params(sem=("arbitrary", "arbitrary")),
        name="s5",
    )(us3, k2, so3, si2, apw)


def _back_kernel(x_ref, ya_ref, ys_ref, gw_ref, gb_ref, onb_ref, wo_ref, fnw_ref,
                 rw_ref, rb_ref,
                 x1_ref, h2_ref, gate_ref, er_ref, cnt_ref, carry_ref, ys_scr):
    i = pl.program_id(0)
    tm = x_ref.shape[0]
    d_g = ya_ref.shape[1]

    @pl.when(i == 0)
    def _():
        carry_ref[...] = jnp.zeros_like(carry_ref)

    n_lb = ys_scr.shape[0]
    for t in range(SSM_T):
        for q in range(n_lb):
            ys_scr[q, pl.ds(t, tm // SSM_T, stride=SSM_T), :] = (
                ys_ref[t, :, q * LANES:(q + 1) * LANES].astype(F32))
    y = _gelu(jnp.concatenate([ys_scr[q] for q in range(n_lb)], axis=1))
    z = jnp.dot(y.astype(BF16), gw_ref[...], preferred_element_type=F32) + gb_ref[...]
    yb = y * _sigmoid(z)
    ybn = _rms(yb, onb_ref[...]).astype(BF16)
    x1 = (x_ref[...]
          + jnp.dot(ya_ref[...], wo_ref[:d_g, :], preferred_element_type=F32)
          + jnp.dot(ybn, wo_ref[d_g:, :], preferred_element_type=F32))
    x1_ref[...] = x1
    h2 = _rms(x1, fnw_ref[...])
    _store_rows(h2_ref, h2)

    hh = h2.astype(BF16)
    hl = (h2 - hh.astype(F32)).astype(BF16)
    p_hi = jnp.dot(hh, rw_ref[...], preferred_element_type=F32)
    p_lo = jnp.dot(hl, rw_ref[:, :LANES], preferred_element_type=F32)
    logits = p_hi[:, :LANES] + p_lo + p_hi[:, LANES:] + rb_ref[...]
    lane = lax.broadcasted_iota(I32, (tm, LANES), 1)
    lane_f = lane.astype(F32)
    neg = jnp.float32(-jnp.inf)
    work = jnp.where(lane < N_EXPERTS, logits, neg)
    vals, hots, idxs = [], [], []
    for k in range(TOP_K):
        m = jnp.max(work, axis=-1, keepdims=True)
        idx = jnp.min(jnp.where(work == m, lane_f, float(LANES)), axis=-1, keepdims=True)
        hot = lane_f == idx
        vals.append(m)
        hots.append(hot)
        idxs.append(idx.astype(I32))
        work = jnp.where(hot, neg, work)
    exps = [jnp.exp(v - vals[0]) for v in vals]
    denom = exps[0] + exps[1] + exps[2] + exps[3]
    gate = jnp.zeros((tm, LANES), F32)
    for k in range(TOP_K):
        gate = jnp.where(lane == k, exps[k] / denom, gate)
    gate_ref[...] = gate

    sel = (hots[0] | hots[1] | hots[2] | hots[3])
    sel_f = jnp.where(sel, 1.0, 0.0)
    r_i = lax.broadcasted_iota(I32, (tm, tm), 0)
    c_i = lax.broadcasted_iota(I32, (tm, tm), 1)
    lower = jnp.where(r_i > c_i, 1.0, 0.0).astype(BF16)
    cum = jnp.dot(lower, sel_f.astype(BF16), preferred_element_type=F32) + carry_ref[...]
    er = jnp.zeros((tm, LANES), I32)
    for k in range(TOP_K):
        rk = jnp.sum(jnp.where(hots[k], cum, 0.0), axis=-1, keepdims=True)
        er = jnp.where(lane == k, idxs[k], er)
        er = jnp.where(lane == TOP_K + k, rk.astype(I32), er)
    carry_ref[...] = carry_ref[...] + jnp.sum(sel_f, axis=0, keepdims=True)
    cnt_ref[...] = carry_ref[...]
    er_ref[0] = jnp.transpose(er)[:SUBLANES, :]


def _back(x2, ya, ys3, glu_w, glu_b, on_b, w_out, fn_w, router_w, router_b):
    n, d = x2.shape
    d_g = ya.shape[1]
    d_s = ys3.shape[2]
    tm = ROW_TILE
    nt = n // tm
    const2 = lambda i: (0, 0)
    rw = jnp.zeros((d, LANES), F32).at[:, :N_EXPERTS].set(router_w.astype(F32))
    rw_hi = rw.astype(BF16)
    rw_lo = (rw - rw_hi.astype(F32)).astype(BF16)
    rw_split = jnp.concatenate([rw_hi, rw_lo], axis=1)
    rb = jnp.zeros((1, LANES), F32).at[0, :N_EXPERTS].set(router_b.astype(F32))
    tok_tile = lambda i: (i, 0)
    return pl.pallas_call(
        _back_kernel,
        grid=(nt,),
        in_specs=[
            pl.BlockSpec((tm, d), tok_tile),
            pl.BlockSpec((tm, d_g), tok_tile),
            pl.BlockSpec((SSM_T, tm // SSM_T, d_s), lambda i: (0, i, 0)),
            pl.BlockSpec((d_s, d_s), const2),
            pl.BlockSpec((1, d_s), const2),
            pl.BlockSpec((1, d_s), const2),
            pl.BlockSpec((d_g + d_s, d), const2),
            pl.BlockSpec((1, d), const2),
            pl.BlockSpec((d, 2 * LANES), const2),
            pl.BlockSpec((1, LANES), const2),
        ],
        out_specs=[
            pl.BlockSpec((tm, d), tok_tile),
            pl.BlockSpec((tm * d // LANES, LANES), tok_tile),
            pl.BlockSpec((tm, LANES), tok_tile),
            pl.BlockSpec((1, SUBLANES, tm), lambda i: (i, 0, 0)),
            pl.BlockSpec((1, LANES), const2),
        ],
        out_shape=[
            jax.ShapeDtypeStruct((n, d), F32),
            jax.ShapeDtypeStruct((n * d // LANES, LANES), F32),
            jax.ShapeDtypeStruct((n, LANES), F32),
            jax.ShapeDtypeStruct((nt, SUBLANES, tm), I32),
            jax.ShapeDtypeStruct((1, LANES), F32),
        ],
        scratch_shapes=[pltpu.VMEM((1, LANES), F32), pltpu.VMEM((d_s // LANES, tm, LANES), F32)],
        compiler_params=_params(sem=("arbitrary",)),
        name="back",
    )(x2, ya, ys3, glu_w.astype(BF16), glu_b.reshape(1, d_s), on_b.reshape(1, d_s),
      w_out.astype(BF16), fn_w.reshape(1, d), rw_split, rb)


def _routing_tables(er, counts, n, row_sl):
    te = EXPERT_TILE
    n_blocks = n * TOP_K // te + N_EXPERTS
    start = jnp.cumsum(counts) - counts
    e_ids = jnp.arange(N_EXPERTS, dtype=I32)
    e_sel = er[:, :TOP_K, :, None] == e_ids
    dest = jnp.sum(jnp.where(e_sel, start, 0), axis=-1) + er[:, TOP_K:2 * TOP_K, :]
    dest = (dest * row_sl).reshape(dest.shape[0], -1)
    nb = (counts + te - 1) // te
    cum = jnp.cumsum(nb)
    total = cum[-1]
    j = jnp.arange(n_blocks, dtype=I32)
    e_j = jnp.sum((cum[None, :] <= jnp.minimum(j, total - 1)[:, None]).astype(I32), axis=1)
    valid = (j < total).astype(I32)
    own = e_ids[None, :] == e_j[:, None]
    first_j = jnp.sum(jnp.where(own, cum - nb, 0), axis=1)
    count_j = jnp.sum(jnp.where(own, counts, 0), axis=1)
    start_j = jnp.sum(jnp.where(own, start, 0), axis=1)
    first = valid * (j == first_j).astype(I32)
    later = (e_ids[None, :] > e_j[:, None]) & (nb > 0)[None, :]
    nxt = jnp.min(jnp.where(later, e_ids[None, :], N_EXPERTS), axis=1)
    nxt = jnp.where(nxt < N_EXPERTS, nxt, -1)
    rows = valid * jnp.clip(count_j - (j - first_j) * te, 0, te)
    row_off = (start_j + (j - first_j) * te) * row_sl
    return (dest.astype(I32), e_j.astype(I32), rows.astype(I32), first, nxt.astype(I32),
            row_off.astype(I32), total.reshape(1).astype(I32))


def _dispatch_kernel(dest_hbm, h_ref, xs_hbm, idx_smem, zbuf, sem_idx, sem_rows, sem_z, *, row_sl):
    i = pl.program_id(0)
    nt = pl.num_programs(0)
    tm = h_ref.shape[0] // row_sl
    blk = zbuf.shape[0]
    slot = i % 2

    def idx_copy(t, s):
        n_idx = TOP_K * tm
        dst = idx_smem.at[pl.ds(pl.multiple_of(s * n_idx, n_idx), n_idx)]
        return pltpu.make_async_copy(dest_hbm.at[t], dst, sem_idx.at[s])

    @pl.when(i == 0)
    def _():
        zbuf[...] = jnp.zeros_like(zbuf)
        cp = pltpu.make_async_copy(zbuf, xs_hbm.at[pl.ds(xs_hbm.shape[0] - blk, blk)], sem_z)
        cp.start()
        cp.wait()
        idx_copy(0, 0).start()

    @pl.when(i + 1 < nt)
    def _():
        idx_copy(i + 1, 1 - slot).start()

    idx_copy(i, slot).wait()

    def body(j, carry):
        n0 = j * DMA_UNROLL
        i0 = slot * (TOP_K * tm) + n0
        for u in range(DMA_UNROLL):
            src = h_ref.at[pl.ds(pl.multiple_of((n0 + u) * row_sl, row_sl), row_sl)]
            for k in range(TOP_K):
                dst = xs_hbm.at[pl.ds(pl.multiple_of(idx_smem[i0 + (k * tm + u)], row_sl), row_sl)]
                pltpu.make_async_copy(src, dst, sem_rows.at[k]).start(priority=k % 2)
        return carry
    lax.fori_loop(0, tm // DMA_UNROLL, body, 0)
    for k in range(TOP_K):
        pltpu.make_async_copy(h_ref, xs_hbm.at[pl.ds(0, tm * row_sl)], sem_rows.at[k]).wait()


def _dispatch(dest, h2t, n_rows, row_sl):
    tm = DISPATCH_TILE
    nt = h2t.shape[0] // (tm * row_sl)
    g = tm // ROW_TILE
    dest = dest.reshape(nt, g, TOP_K, ROW_TILE).transpose(0, 2, 1, 3).reshape(nt, TOP_K * tm)
    return pl.pallas_call(
        functools.partial(_dispatch_kernel, row_sl=row_sl),
        grid=(nt,),
        in_specs=[
            pl.BlockSpec(memory_space=pl.ANY),
            pl.BlockSpec((tm * row_sl, LANES), lambda i: (i, 0)),
        ],
        out_specs=pl.BlockSpec(memory_space=pl.ANY),
        out_shape=jax.ShapeDtypeStruct((n_rows * row_sl, LANES), F32),
        scratch_shapes=[
            pltpu.SMEM((2 * TOP_K * tm,), I32),
            pltpu.VMEM((EXPERT_TILE * row_sl, LANES), F32),
            pltpu.SemaphoreType.DMA((2,)),
            pltpu.SemaphoreType.DMA((TOP_K,)),
            pltpu.SemaphoreType.DMA,
        ],
        compiler_params=_params(sem=("arbitrary",)),
        name="dispatch",
    )(dest, h2t)


def _expert_kernel(be_ref, nv_ref, first_ref, next_ref, off_ref, tot_ref,
                   xs_hbm, wgu_hbm, bgu_ref, wdn_hbm, bdn_ref, ys_hbm,
                   xbuf, ybuf, stage_gu, stage_dn, wgu_bf, wdn_bf, act_scr, sem_x, sem_y, sem_w, *, layer):
    i = pl.program_id(0)
    d, d_gu = wgu_bf.shape
    d_e = wdn_bf.shape[0]
    nsl = d // LANES
    blk = xbuf.shape[1]
    te = blk // nsl
    slot = i % 2
    total = tot_ref[0]

    def weight_copies(e):
        row = layer * N_EXPERTS + e
        return (pltpu.make_async_copy(wgu_hbm.at[row], stage_gu, sem_w.at[0]),
                pltpu.make_async_copy(wdn_hbm.at[row], stage_dn, sem_w.at[1]))

    def x_copy(j, s):
        src = xs_hbm.at[pl.ds(pl.multiple_of(off_ref[j], nsl), blk)]
        return pltpu.make_async_copy(src, xbuf.at[s], sem_x.at[s])

    def y_copy(j, s):
        dst = ys_hbm.at[pl.ds(pl.multiple_of(off_ref[j], nsl), blk)]
        return pltpu.make_async_copy(ybuf.at[s], dst, sem_y.at[s])

    @pl.when(i == 0)
    def _():
        for cp in weight_copies(be_ref[0]):
            cp.start(priority=1)
        x_copy(0, 0).start()
        ybuf[...] = jnp.zeros_like(ybuf)
        cp = pltpu.make_async_copy(ybuf.at[1], ys_hbm.at[pl.ds(ys_hbm.shape[0] - blk, blk)], sem_y.at[1])
        cp.start()
        cp.wait()

    @pl.when(first_ref[i] > 0)
    def _():
        for cp in weight_copies(be_ref[i]):
            cp.wait()
        rows = 128
        for r in range(0, d, rows):
            wgu_bf[r:r + rows, :] = stage_gu[r:r + rows, :].astype(BF16)
        for r in range(0, d_e, rows):
            wdn_bf[r:r + rows, :] = stage_dn[r:r + rows, :].astype(BF16)

        @pl.when(next_ref[i] >= 0)
        def _():
            for cp in weight_copies(next_ref[i]):
                cp.start(priority=1)

    def run_rows(m):
        x_ref = xbuf.at[slot]
        x = jnp.concatenate([_load_row_slab(x_ref, s, m, nsl).astype(BF16) for s in range(nsl)], axis=1)
        for c in range(d_e // EXPERT_CHUNK):
            lo = c * EXPERT_CHUNK
            hi = lo + EXPERT_CHUNK
            glu = jnp.dot(x, wgu_bf[:, lo:hi], preferred_element_type=F32) + bgu_ref[0, :, lo:hi]
            lin = (jnp.dot(x, wgu_bf[:, d_e + lo:d_e + hi], preferred_element_type=F32)
                   + bgu_ref[0, :, d_e + lo:d_e + hi])
            glu = jnp.minimum(glu, SWIGLU_LIMIT)
            lin = jnp.clip(lin, -SWIGLU_LIMIT, SWIGLU_LIMIT)
            act_scr[:m, lo:hi] = (glu * _sigmoid(SWIGLU_ALPHA * glu) * (lin + 1.0)).astype(BF16)
        y = jnp.dot(act_scr[:m, :], wdn_bf[...], preferred_element_type=F32) + bdn_ref[0]
        _store_rows(ybuf.at[slot], y)

    nv = nv_ref[i]

    @pl.when(nv > 0)
    def _():
        x_copy(i, slot).wait()

        @pl.when(i + 1 < total)
        def _():
            x_copy(i + 1, 1 - slot).start()

    sizes = [te // 4, te // 2, te]
    lower = 0
    for m in sizes:
        @pl.when((nv > lower) & (nv <= m))
        def _(m=m):
            run_rows(m)
        lower = m

    @pl.when(nv > 0)
    def _():
        @pl.when(i >= 1)
        def _():
            y_copy(i - 1, 1 - slot).wait()
        y_copy(i, slot).start()

        @pl.when(i == total - 1)
        def _():
            y_copy(i, slot).wait()


def _experts(block_expert, block_rows, block_first, block_next, block_off, total, xs,
             w_gu, b_gu, w_dn, b_dn, layer):
    d = w_gu.shape[2]
    n_blocks = block_expert.shape[0]
    blk = EXPERT_TILE * d // LANES
    d_gu = w_gu.shape[3]
    d_e = w_dn.shape[2]
    e_map3 = lambda i, be, nv, bf, bn, off, tot: (layer * N_EXPERTS + be[i], 0, 0)
    w_gu = w_gu.reshape((-1,) + w_gu.shape[2:])
    w_dn = w_dn.reshape((-1,) + w_dn.shape[2:])
    b_gu = b_gu.reshape(-1, 1, d_gu)
    b_dn = b_dn.reshape(-1, 1, d)
    grid_spec = pltpu.PrefetchScalarGridSpec(
        num_scalar_prefetch=6,
        grid=(n_blocks,),
        in_specs=[
            pl.BlockSpec(memory_space=pl.ANY),
            pl.BlockSpec(memory_space=pl.ANY),
            pl.BlockSpec((1, 1, d_gu), e_map3),
            pl.BlockSpec(memory_space=pl.ANY),
            pl.BlockSpec((1, 1, d), e_map3),
        ],
        out_specs=pl.BlockSpec(memory_space=pl.ANY),
        scratch_shapes=[
            pltpu.VMEM((2, blk, LANES), F32),
            pltpu.VMEM((2, blk, LANES), F32),
            pltpu.VMEM((d, d_gu), F32),
            pltpu.VMEM((d_e, d), F32),
            pltpu.VMEM((d, d_gu), BF16),
            pltpu.VMEM((d_e, d), BF16),
            pltpu.VMEM((EXPERT_TILE, d_e), BF16),
            pltpu.SemaphoreType.DMA((2,)),
            pltpu.SemaphoreType.DMA((2,)),
            pltpu.SemaphoreType.DMA((2,)),
        ],
    )
    return pl.pallas_call(
        functools.partial(_expert_kernel, layer=layer),
        grid_spec=grid_spec,
        out_shape=jax.ShapeDtypeStruct(xs.shape, F32),
        compiler_params=_params(sem=("arbitrary",)),
        name="experts",
    )(block_expert, block_rows, block_first, block_next, block_off, total, xs, w_gu, b_gu, w_dn, b_dn)


def _combine_kernel(dest_hbm, x1_ref, g_ref, fw_ref, ys_hbm, o_ref,
                    idx_smem, ybuf, sem_idx, sem_rows, *, final_norm):
    i = pl.program_id(0)
    nt = pl.num_programs(0)
    tm, d = x1_ref.shape
    nsl = d // LANES
    slot = i % 2

    def idx_copy(t, s):
        n_idx = TOP_K * tm
        dst = idx_smem.at[pl.ds(pl.multiple_of(s * n_idx, n_idx), n_idx)]
        return pltpu.make_async_copy(dest_hbm.at[t], dst, sem_idx.at[s])

    def issue_rows(s):
        def body(j, carry):
            n0 = j * DMA_UNROLL
            i0 = s * (TOP_K * tm) + n0
            for u in range(DMA_UNROLL):
                for k in range(TOP_K):
                    src = ys_hbm.at[pl.ds(pl.multiple_of(idx_smem[i0 + (k * tm + u)], nsl), nsl)]
                    dst = ybuf.at[s, k, pl.ds(pl.multiple_of((n0 + u) * nsl, nsl), nsl)]
                    pltpu.make_async_copy(src, dst, sem_rows.at[s, k]).start(priority=k % 2)
            return carry
        lax.fori_loop(0, tm // DMA_UNROLL, body, 0)

    def wait_rows(s):
        for k in range(TOP_K):
            pltpu.make_async_copy(ys_hbm.at[pl.ds(0, tm * nsl)], ybuf.at[s, k], sem_rows.at[s, k]).wait()

    @pl.when(i == 0)
    def _():
        cp = idx_copy(0, 0)
        cp.start()
        cp.wait()
        issue_rows(0)

        @pl.when(nt > 1)
        def _():
            idx_copy(1, 1).start()

    @pl.when(i + 1 < nt)
    def _():
        idx_copy(i + 1, 1 - slot).wait()
        issue_rows(1 - slot)

        @pl.when(i + 2 < nt)
        def _():
            idx_copy(i + 2, slot).start()

    wait_rows(slot)
    rows = COMBINE_ROWS_NORM if final_norm else COMBINE_ROWS
    for r0 in range(0, tm, rows):
        g = g_ref[r0:r0 + rows, :]
        gk = [jnp.broadcast_to(g[:, k:k + 1], (rows, LANES)) for k in range(TOP_K)]
        slabs = []
        for s in range(nsl):
            acc = x1_ref[r0:r0 + rows, s * LANES:(s + 1) * LANES]
            for k in range(TOP_K):
                acc = acc + gk[k] * _load_row_slab(ybuf.at[slot, k], s, rows, nsl, r0)
            slabs.append(acc)
        out = jnp.concatenate(slabs, axis=1)
        if final_norm:
            out = _rms(out, fw_ref[...])
        o_ref[r0:r0 + rows, :] = out


def _combine(x1, ys, gate, dest, final_w, final_norm):
    n, d = x1.shape
    tm = ROW_TILE
    nt = n // tm
    tok_tile = lambda i: (i, 0)
    return pl.pallas_call(
        functools.partial(_combine_kernel, final_norm=final_norm),
        grid=(nt,),
        in_specs=[
            pl.BlockSpec(memory_space=pl.ANY),
            pl.BlockSpec((tm, d), tok_tile),
            pl.BlockSpec((tm, LANES), tok_tile),
            pl.BlockSpec((1, d), lambda i: (0, 0)),
            pl.BlockSpec(memory_space=pl.ANY),
        ],
        out_specs=pl.BlockSpec((tm, d), tok_tile),
        out_shape=jax.ShapeDtypeStruct((n, d), F32),
        scratch_shapes=[
            pltpu.SMEM((2 * TOP_K * tm,), I32),
            pltpu.VMEM((2, TOP_K, tm * d // LANES, LANES), F32),
            pltpu.SemaphoreType.DMA((2,)),
            pltpu.SemaphoreType.DMA((2, TOP_K)),
        ],
        compiler_params=_params(sem=("arbitrary",)),
        name="combine",
    )(dest, x1, gate, final_w.reshape(1, d), ys)


def kernel(x, attn_norm_w, w_in, sgu_ln_w, sgu_ln_b, sgu_w, sgu_b, ssm_a_re, ssm_a_im, ssm_b_re, ssm_b_im, ssm_c_re, ssm_c_im, ssm_d, ssm_log_dt, ssm_glu_w, ssm_glu_b, out_norm_a, out_norm_b, w_out, ffn_norm_w, router_w, router_b, w_gate_up, b_gate_up, w_down, b_down, final_norm_w):
    b, l, d = x.shape
    n = b * l
    depth = w_in.shape[0]
    assert l % CHUNK == 0 and l % SSM_T == 0 and b % S5_SEQS == 0
    assert n % FRONT_TILE == 0 and n % DISPATCH_TILE == 0 and DISPATCH_TILE % ROW_TILE == 0
    assert (n * TOP_K) % EXPERT_TILE == 0 and d % LANES == 0
    assert router_w.shape[-1] == N_EXPERTS and ssm_a_re.shape[1] % GROUPS_PER_LANE_BLOCK == 0
    blocks_per_seq = l // SSM_T
    n_steps = max(1, (blocks_per_seq - 1).bit_length())
    n_rows = n * TOP_K + EXPERT_TILE
    x2 = x.reshape(n, d).astype(F32)
    k2, so3, si2, apw = _s5_tables(ssm_a_re, ssm_a_im, ssm_b_re, ssm_b_im, ssm_c_re, ssm_c_im, ssm_d,
                                   ssm_log_dt, n_steps)
    for layer in range(depth):
        ya, us3 = _front(x2, attn_norm_w[layer], w_in[layer], sgu_ln_w[layer], sgu_ln_b[layer],
                         sgu_w[layer], sgu_b[layer], out_norm_a[layer])
        ys3 = _s5(us3, k2, so3, si2, apw, blocks_per_seq, layer)
        x1, h2, gate, er, cnt = _back(x2, ya, ys3, ssm_glu_w[layer], ssm_glu_b[layer],
                                      out_norm_b[layer], w_out[layer], ffn_norm_w[layer],
                                      router_w[layer], router_b[layer])
        counts = cnt[0, :N_EXPERTS].astype(I32)
        dest, block_expert, block_rows, block_first, block_next, block_off, total = _routing_tables(
            er, counts, n, d // LANES)
        xs = _dispatch(dest, h2, n_rows, d // LANES)
        ys = _experts(block_expert, block_rows, block_first, block_next, block_off, total, xs,
                      w_gate_up, b_gate_up, w_down, b_down, layer)
        x2 = _combine(x1, ys, gate, dest, final_norm_w, final_norm=(layer == depth - 1))
    return x2.reshape(b, l, d).astype(x.dtype)
```

```python
import functools
import math

import jax
import jax.numpy as jnp
from jax import lax
from jax.experimental import pallas as pl
from jax.experimental.pallas import tpu as pltpu

F32 = jnp.float32
BF16 = jnp.bfloat16
I32 = jnp.int32

EPS = 1e-5
N_HEADS = 4
CHUNK = 128
SSM_GROUP = 16
SSM_STATE = 64
SSM_T = 16
N_EXPERTS = 32
TOP_K = 4
SWIGLU_LIMIT = 7.0
SWIGLU_ALPHA = 1.702
LANES = 128
SUBLANES = 8
GROUPS_PER_LANE_BLOCK = LANES // SSM_GROUP

ROW_TILE = 512
FRONT_TILE = 1024
DISPATCH_TILE = 2048
S5_SEQS = 4
EXPERT_TILE = 1024
EXPERT_CHUNK = 256
DMA_UNROLL = 8
COMBINE_ROWS = 8
COMBINE_ROWS_NORM = 256
VMEM_LIMIT = 56 * 1024 * 1024


def _gelu(x):
    return 0.5 * x * (1.0 + jnp.tanh(math.sqrt(2.0 / math.pi) * (x + 0.044715 * (x * x * x))))


def _sigmoid(x):
    return 1.0 / (1.0 + jnp.exp(-x))


def _rms(x, w):
    return x * lax.rsqrt(jnp.mean(x * x, axis=-1, keepdims=True) + EPS) * w


def _store_rows(ref, val, row0=0):
    rows, d = val.shape
    nsl = d // LANES
    for s in range(nsl):
        ref[pl.ds(row0 * nsl + s, rows, stride=nsl), :] = val[:, s * LANES:(s + 1) * LANES]


def _load_row_slab(ref, s, rows, nsl, row0=0):
    return ref[pl.ds(row0 * nsl + s, rows, stride=nsl), :]


def _params(**kw):
    return pltpu.CompilerParams(dimension_semantics=kw.pop("sem"), vmem_limit_bytes=VMEM_LIMIT, **kw)


def _front_kernel(x_ref, nw_ref, win_ref, lnw_ref, lnb_ref, ws_ref, bst_ref, ona_ref,
                  ya_ref, us_ref, mixed_ref, us_scr):
    d_g = ya_ref.shape[1]
    hd = d_g // N_HEADS
    tm = x_ref.shape[0]
    x = x_ref[...]
    h = _rms(x, nw_ref[...]).astype(BF16)
    proj = jnp.dot(h, win_ref[...], preferred_element_type=F32)
    u = _gelu(proj[:, :d_g])
    v = _gelu(proj[:, d_g:2 * d_g])
    n_lb = us_scr.shape[0]
    for q in range(n_lb):
        us_scr[q] = proj[:, 2 * d_g + q * LANES:2 * d_g + (q + 1) * LANES]
    mu = jnp.mean(v, axis=-1, keepdims=True)
    vc = v - mu
    var = jnp.mean(vc * vc, axis=-1, keepdims=True)
    vb = (vc * lax.rsqrt(var + EPS) * lnw_ref[...] + lnb_ref[...]).astype(BF16)
    row = lax.broadcasted_iota(I32, (CHUNK, CHUNK), 0)
    col = lax.broadcasted_iota(I32, (CHUNK, CHUNK), 1)
    causal = row >= col
    for hh in range(N_HEADS):
        w = jnp.where(causal, ws_ref[hh], 0.0).astype(BF16)
        bias = bst_ref[:, hh:hh + 1]
        for c in range(tm // CHUNK):
            vv = vb[c * CHUNK:(c + 1) * CHUNK, hh * hd:(hh + 1) * hd]
            m = jnp.dot(w, vv, preferred_element_type=F32) + bias
            mixed_ref[c * CHUNK:(c + 1) * CHUNK, hh * hd:(hh + 1) * hd] = m
    ya = u * mixed_ref[...]
    ya_ref[...] = _rms(ya, ona_ref[...]).astype(BF16)
    for s in range(SSM_T):
        for q in range(n_lb):
            us_ref[s, :, q * LANES:(q + 1) * LANES] = (
                us_scr[q, pl.ds(s, tm // SSM_T, stride=SSM_T), :].astype(BF16))


def _front(x2, nw, w_in, ln_w, ln_b, w_s, b_s, on_a):
    n, d = x2.shape
    d_g = ln_w.shape[0]
    d_s = w_in.shape[1] - 2 * d_g
    tm = FRONT_TILE
    const2 = lambda i: (0, 0)
    return pl.pallas_call(
        _front_kernel,
        grid=(n // tm,),
        in_specs=[
            pl.BlockSpec((tm, d), lambda i: (i, 0)),
            pl.BlockSpec((1, d), const2),
            pl.BlockSpec(w_in.shape, const2),
            pl.BlockSpec((1, d_g), const2),
            pl.BlockSpec((1, d_g), const2),
            pl.BlockSpec(w_s.shape, lambda i: (0, 0, 0)),
            pl.BlockSpec((CHUNK, N_HEADS), const2),
            pl.BlockSpec((1, d_g), const2),
        ],
        out_specs=[
            pl.BlockSpec((tm, d_g), lambda i: (i, 0)),
            pl.BlockSpec((SSM_T, tm // SSM_T, d_s), lambda i: (0, i, 0)),
        ],
        out_shape=[
            jax.ShapeDtypeStruct((n, d_g), BF16),
            jax.ShapeDtypeStruct((SSM_T, n // SSM_T, d_s), BF16),
        ],
        scratch_shapes=[pltpu.VMEM((tm, d_g), F32), pltpu.VMEM((d_s // LANES, tm, LANES), F32)],
        compiler_params=_params(sem=("arbitrary",)),
        name="front",
    )(x2, nw.reshape(1, d), w_in.astype(BF16), ln_w.reshape(1, d_g), ln_b.reshape(1, d_g),
      w_s, b_s.T, on_a.reshape(1, d_g))


def _s5_tables(a_re, a_im, b_re, b_im, c_re, c_im, d, log_dt, n_steps):
    depth, g, p = a_re.shape
    hch = b_re.shape[-1]
    t = SSM_T
    gl = GROUPS_PER_LANE_BLOCK
    r = depth * (g // gl)
    a = lax.complex(a_re.astype(F32), a_im.astype(F32)).reshape(r, gl, p)
    dt = jnp.exp(log_dt.astype(F32)).reshape(r, gl, 1)
    dta = dt * a
    a_bar = jnp.exp(dta)
    b = lax.complex(b_re.astype(F32), b_im.astype(F32)).reshape(r, gl, p, hch)
    b_bar = ((a_bar - 1.0) / a)[..., None] * b
    c = lax.complex(c_re.astype(F32), c_im.astype(F32)).reshape(r, gl, hch, p)
    lags = jnp.arange(t + 1, dtype=F32)
    pw = jnp.exp(lags[None, :, None, None] * dta.transpose(0, 2, 1)[:, None])
    pwx = jnp.repeat(pw, hch, axis=-1)
    bb = b_bar.transpose(0, 2, 1, 3).reshape(r, p, gl * hch)
    cc = c.transpose(0, 3, 1, 2).reshape(r, p, gl * hch)
    q = pwx[:, :t] * bb[:, None]
    si = pwx[:, 1:] * cc[:, None]
    k2 = (jnp.einsum('rlpx,rpy->rlxy', q.real, cc.real, precision=lax.Precision.HIGHEST)
          - jnp.einsum('rlpx,rpy->rlxy', q.imag, cc.imag, precision=lax.Precision.HIGHEST))
    lane_g = jnp.arange(gl * hch) // hch
    k2 = jnp.where(lane_g[:, None] == lane_g[None, :], k2, 0.0)
    skip = jnp.eye(gl * hch, dtype=F32) * d.astype(F32).reshape(r, 1, gl * hch)
    k2 = k2.at[:, 0].add(skip)
    so3 = jnp.swapaxes(jnp.concatenate([q.real, q.imag], axis=2), 2, 3)
    si2 = jnp.concatenate([si.real, -si.imag], axis=2)
    steps = SSM_T * (2.0 ** jnp.arange(n_steps, dtype=F32))
    ap = jnp.exp(steps[None, :, None, None] * dta[:, None]).reshape(r, n_steps, gl * p)
    mul_same = jnp.concatenate([ap.real, ap.real], axis=-1)
    mul_swap = jnp.concatenate([-ap.imag, ap.imag], axis=-1)
    apw = jnp.stack([mul_same, mul_swap], axis=2)
    return k2.astype(BF16), so3.astype(BF16), si2.astype(BF16), apw


def _s5_kernel(u_ref, k2_ref, so3_ref, si2_ref, ap_ref, y_ref, wk_scr, wso_scr, wsi_scr, *, blocks_per_seq):
    t_blk = u_ref.shape[0]
    rows = u_ref.shape[1]
    n_state = wso_scr.shape[1]
    two_p = so3_ref.shape[3]
    p = two_p // 2
    half = n_state // 2

    @pl.when(pl.program_id(1) == 0)
    def _():
        for s in range(t_blk):
            if s > 0:
                wk_scr[s * LANES:(s + 1) * LANES, :s * LANES] = jnp.zeros((LANES, s * LANES), BF16)
            for t in range(s, t_blk):
                wk_scr[s * LANES:(s + 1) * LANES, t * LANES:(t + 1) * LANES] = k2_ref[0, t - s]
        r_e = lax.broadcasted_iota(I32, (two_p, n_state), 0)
        c_e = lax.broadcasted_iota(I32, (two_p, n_state), 1)
        spread = jnp.where((r_e // p == c_e // half) & (r_e % p == c_e % p), 1.0, 0.0).astype(BF16)
        r_g = lax.broadcasted_iota(I32, (LANES, n_state), 0) // SSM_GROUP
        c_g = (lax.broadcasted_iota(I32, (LANES, n_state), 1) % half) // p
        own = r_g == c_g
        for s in range(t_blk):
            full = jnp.dot(so3_ref[0, t_blk - 1 - s], spread, preferred_element_type=F32)
            wso_scr[s * LANES:(s + 1) * LANES, :] = jnp.where(own, full, 0.0).astype(BF16)
        r_e = lax.broadcasted_iota(I32, (n_state, two_p), 0)
        c_e = lax.broadcasted_iota(I32, (n_state, two_p), 1)
        gather = jnp.where((c_e // p == r_e // half) & (c_e % p == r_e % p), 1.0, 0.0).astype(BF16)
        r_g = (lax.broadcasted_iota(I32, (n_state, LANES), 0) % half) // p
        c_g = lax.broadcasted_iota(I32, (n_state, LANES), 1) // SSM_GROUP
        own = r_g == c_g
        for t in range(t_blk):
            full = jnp.dot(gather, si2_ref[0, t], preferred_element_type=F32)
            wsi_scr[:, t * LANES:(t + 1) * LANES] = jnp.where(own, full, 0.0).astype(BF16)

    xcat = jnp.concatenate([u_ref[s] for s in range(t_blk)], axis=1)
    x = jnp.dot(xcat, wso_scr[...], preferred_element_type=F32)
    pos = lax.broadcasted_iota(I32, (rows, n_state), 0) % blocks_per_seq
    for k in range(ap_ref.shape[1]):
        sh = 1 << k
        prev = pltpu.roll(x, sh, 0)
        prev_sw = pltpu.roll(prev, half, 1)
        upd = prev * ap_ref[0, k, 0:1, :] + prev_sw * ap_ref[0, k, 1:2, :]
        x = x + jnp.where(pos >= sh, upd, 0.0)
    xin = jnp.where(pos >= 1, pltpu.roll(x, 1, 0), 0.0).astype(BF16)
    for j in range(t_blk // 2):
        k_hi = (2 * j + 2) * LANES
        lo = 2 * j * LANES
        y = jnp.dot(xcat[:, :k_hi], wk_scr[:k_hi, lo:lo + 2 * LANES], preferred_element_type=F32)
        y = y + jnp.dot(xin, wsi_scr[:, lo:lo + 2 * LANES], preferred_element_type=F32)
        y_ref[2 * j] = y[:, :LANES].astype(BF16)
        y_ref[2 * j + 1] = y[:, LANES:].astype(BF16)


def _s5(us3, k2, so3, si2, apw, blocks_per_seq, layer):
    t_blk, n_blocks, d_s = us3.shape
    lb = d_s // LANES
    rows = S5_SEQS * blocks_per_seq
    n_steps = n_blocks // rows
    n_in = t_blk * LANES
    n_state = apw.shape[3]
    w_map = lambda a, b: (layer * lb + a, 0, 0, 0)
    io_spec = pl.BlockSpec((t_blk, rows, LANES), lambda a, b: (0, b, a))
    return pl.pallas_call(
        functools.partial(_s5_kernel, blocks_per_seq=blocks_per_seq),
        grid=(lb, n_steps),
        in_specs=[
            io_spec,
            pl.BlockSpec((1,) + k2.shape[1:], w_map),
            pl.BlockSpec((1,) + so3.shape[1:], w_map),
            pl.BlockSpec((1,) + si2.shape[1:], w_map),
            pl.BlockSpec((1,) + apw.shape[1:], w_map),
        ],
        out_specs=io_spec,
        out_shape=jax.ShapeDtypeStruct(us3.shape, BF16),
        scratch_shapes=[
            pltpu.VMEM((n_in, n_in), BF16),
            pltpu.VMEM((n_in, n_state), BF16),
            pltpu.VMEM((n_state, n_in), BF16),
        ],
        compiler_params=_params(sem=("arbitrary", "arbitrary")),
        name="s5",
    )(us3, k2, so3, si2, apw)


def _back_kernel(x_ref, ya_ref, ys_ref, gw_ref, gb_ref, onb_ref, wo_ref, fnw_ref,
                 rw_ref, rb_ref,
                 x1_ref, h2_ref, gate_ref, er_ref, cnt_ref, carry_ref, ys_scr):
    i = pl.program_id(0)
    tm = x_ref.shape[0]
    d_g = ya_ref.shape[1]

    @pl.when(i == 0)
    def _():
        carry_ref[...] = jnp.zeros_like(carry_ref)

    n_lb = ys_scr.shape[0]
    for t in range(SSM_T):
        for q in range(n_lb):
            ys_scr[q, pl.ds(t, tm // SSM_T, stride=SSM_T), :] = (
                ys_ref[t, :, q * LANES:(q + 1) * LANES].astype(F32))
    y = _gelu(jnp.concatenate([ys_scr[q] for q in range(n_lb)], axis=1))
    z = jnp.dot(y.astype(BF16), gw_ref[...], preferred_element_type=F32) + gb_ref[...]
    yb = y * _sigmoid(z)
    ybn = _rms(yb, onb_ref[...]).astype(BF16)
    x1 = (x_ref[...]
          + jnp.dot(ya_ref[...], wo_ref[:d_g, :], preferred_element_type=F32)
          + jnp.dot(ybn, wo_ref[d_g:, :], preferred_element_type=F32))
    x1_ref[...] = x1
    h2 = _rms(x1, fnw_ref[...])
    _store_rows(h2_ref, h2)

    hh = h2.astype(BF16)
    hl = (h2 - hh.astype(F32)).astype(BF16)
    p_hi = jnp.dot(hh, rw_ref[...], preferred_element_type=F32)
    p_lo = jnp.dot(hl, rw_ref[:, :LANES], preferred_element_type=F32)
    logits = p_hi[:, :LANES] + p_lo + p_hi[:, LANES:] + rb_ref[...]
    lane = lax.broadcasted_iota(I32, (tm, LANES), 1)
    lane_f = lane.astype(F32)
    neg = jnp.float32(-jnp.inf)
    work = jnp.where(lane < N_EXPERTS, logits, neg)
    vals, hots, idxs = [], [], []
    for k in range(TOP_K):
        m = jnp.max(work, axis=-1, keepdims=True)
        idx = jnp.min(jnp.where(work == m, lane_f, float(LANES)), axis=-1, keepdims=True)
        hot = lane_f == idx
        vals.append(m)
        hots.append(hot)
        idxs.append(idx.astype(I32))
        work = jnp.where(hot, neg, work)
    exps = [jnp.exp(v - vals[0]) for v in vals]
    denom = exps[0] + exps[1] + exps[2] + exps[3]
    gate = jnp.zeros((tm, LANES), F32)
    for k in range(TOP_K):
        gate = jnp.where(lane == k, exps[k] / denom, gate)
    gate_ref[...] = gate

    sel = (hots[0] | hots[1] | hots[2] | hots[3])
    sel_f = jnp.where(sel, 1.0, 0.0)
    r_i = lax.broadcasted_iota(I32, (tm, tm), 0)
    c_i = lax.broadcasted_iota(I32, (tm, tm), 1)
    lower = jnp.where(r_i > c_i, 1.0, 0.0).astype(BF16)
    cum = jnp.dot(lower, sel_f.astype(BF16), preferred_element_type=F32) + carry_ref[...]
    er = jnp.zeros((tm, LANES), I32)
    for k in range(TOP_K):
        rk = jnp.sum(jnp.where(hots[k], cum, 0.0), axis=-1, keepdims=True)
        er = jnp.where(lane == k, idxs[k], er)
        er = jnp.where(lane == TOP_K + k, rk.astype(I32), er)
    carry_ref[...] = carry_ref[...] + jnp.sum(sel_f, axis=0, keepdims=True)
    cnt_ref[...] = carry_ref[...]
    er_ref[0] = jnp.transpose(er)[:SUBLANES, :]


def _back(x2, ya, ys3, glu_w, glu_b, on_b, w_out, fn_w, router_w, router_b):
    n, d = x2.shape
    d_g = ya.shape[1]
    d_s = ys3.shape[2]
    tm = ROW_TILE
    nt = n // tm
    const2 = lambda i: (0, 0)
    rw = jnp.zeros((d, LANES), F32).at[:, :N_EXPERTS].set(router_w.astype(F32))
    rw_hi = rw.astype(BF16)
    rw_lo = (rw - rw_hi.astype(F32)).astype(BF16)
    rw_split = jnp.concatenate([rw_hi, rw_lo], axis=1)
    rb = jnp.zeros((1, LANES), F32).at[0, :N_EXPERTS].set(router_b.astype(F32))
    tok_tile = lambda i: (i, 0)
    return pl.pallas_call(
        _back_kernel,
        grid=(nt,),
        in_specs=[
            pl.BlockSpec((tm, d), tok_tile),
            pl.BlockSpec((tm, d_g), tok_tile),
            pl.BlockSpec((SSM_T, tm // SSM_T, d_s), lambda i: (0, i, 0)),
            pl.BlockSpec((d_s, d_s), const2),
            pl.BlockSpec((1, d_s), const2),
            pl.BlockSpec((1, d_s), const2),
            pl.BlockSpec((d_g + d_s, d), const2),
            pl.BlockSpec((1, d), const2),
            pl.BlockSpec((d, 2 * LANES), const2),
            pl.BlockSpec((1, LANES), const2),
        ],
        out_specs=[
            pl.BlockSpec((tm, d), tok_tile),
            pl.BlockSpec((tm * d // LANES, LANES), tok_tile),
            pl.BlockSpec((tm, LANES), tok_tile),
            pl.BlockSpec((1, SUBLANES, tm), lambda i: (i, 0, 0)),
            pl.BlockSpec((1, LANES), const2),
        ],
        out_shape=[
            jax.ShapeDtypeStruct((n, d), F32),
            jax.ShapeDtypeStruct((n * d // LANES, LANES), F32),
            jax.ShapeDtypeStruct((n, LANES), F32),
            jax.ShapeDtypeStruct((nt, SUBLANES, tm), I32),
            jax.ShapeDtypeStruct((1, LANES), F32),
        ],
        scratch_shapes=[pltpu.VMEM((1, LANES), F32), pltpu.VMEM((d_s // LANES, tm, LANES), F32)],
        compiler_params=_params(sem=("arbitrary",)),
        name="back",
    )(x2, ya, ys3, glu_w.astype(BF16), glu_b.reshape(1, d_s), on_b.reshape(1, d_s),
      w_out.astype(BF16), fn_w.reshape(1, d), rw_split, rb)


def _routing_tables(er, counts, n, row_sl):
    te = EXPERT_TILE
    n_blocks = n * TOP_K // te + N_EXPERTS
    start = jnp.cumsum(counts) - counts
    e_ids = jnp.arange(N_EXPERTS, dtype=I32)
    e_sel = er[:, :TOP_K, :, None] == e_ids
    dest = jnp.sum(jnp.where(e_sel, start, 0), axis=-1) + er[:, TOP_K:2 * TOP_K, :]
    dest = (dest * row_sl).reshape(dest.shape[0], -1)
    nb = (counts + te - 1) // te
    cum = jnp.cumsum(nb)
    total = cum[-1]
    j = jnp.arange(n_blocks, dtype=I32)
    e_j = jnp.sum((cum[None, :] <= jnp.minimum(j, total - 1)[:, None]).astype(I32), axis=1)
    valid = (j < total).astype(I32)
    own = e_ids[None, :] == e_j[:, None]
    first_j = jnp.sum(jnp.where(own, cum - nb, 0), axis=1)
    count_j = jnp.sum(jnp.where(own, counts, 0), axis=1)
    start_j = jnp.sum(jnp.where(own, start, 0), axis=1)
    first = valid * (j == first_j).astype(I32)
    later = (e_ids[None, :] > e_j[:, None]) & (nb > 0)[None, :]
    nxt = jnp.min(jnp.where(later, e_ids[None, :], N_EXPERTS), axis=1)
    nxt = jnp.where(nxt < N_EXPERTS, nxt, -1)
    rows = valid * jnp.clip(count_j - (j - first_j) * te, 0, te)
    row_off = (start_j + (j - first_j) * te) * row_sl
    return (dest.astype(I32), e_j.astype(I32), rows.astype(I32), first, nxt.astype(I32),
            row_off.astype(I32), total.reshape(1).astype(I32))


def _dispatch_kernel(dest_hbm, h_ref, xs_hbm, idx_smem, zbuf, sem_idx, sem_rows, sem_z, *, row_sl):
    i = pl.program_id(0)
    nt = pl.num_programs(0)
    tm = h_ref.shape[0] // row_sl
    blk = zbuf.shape[0]
    slot = i % 2

    def idx_copy(t, s):
        n_idx = TOP_K * tm
        dst = idx_smem.at[pl.ds(pl.multiple_of(s * n_idx, n_idx), n_idx)]
        return pltpu.make_async_copy(dest_hbm.at[t], dst, sem_idx.at[s])

    @pl.when(i == 0)
    def _():
        zbuf[...] = jnp.zeros_like(zbuf)
        cp = pltpu.make_async_copy(zbuf, xs_hbm.at[pl.ds(xs_hbm.shape[0] - blk, blk)], sem_z)
        cp.start()
        cp.wait()
        idx_copy(0, 0).start()

    @pl.when(i + 1 < nt)
    def _():
        idx_copy(i + 1, 1 - slot).start()

    idx_copy(i, slot).wait()

    def body(j, carry):
        n0 = j * DMA_UNROLL
        i0 = slot * (TOP_K * tm) + n0
        for u in range(DMA_UNROLL):
            src = h_ref.at[pl.ds(pl.multiple_of((n0 + u) * row_sl, row_sl), row_sl)]
            for k in range(TOP_K):
                dst = xs_hbm.at[pl.ds(pl.multiple_of(idx_smem[i0 + (k * tm + u)], row_sl), row_sl)]
                pltpu.make_async_copy(src, dst, sem_rows.at[k]).start(priority=k % 2)
        return carry
    lax.fori_loop(0, tm // DMA_UNROLL, body, 0)
    for k in range(TOP_K):
        pltpu.make_async_copy(h_ref, xs_hbm.at[pl.ds(0, tm * row_sl)], sem_rows.at[k]).wait()


def _dispatch(dest, h2t, n_rows, row_sl):
    tm = DISPATCH_TILE
    nt = h2t.shape[0] // (tm * row_sl)
    g = tm // ROW_TILE
    dest = dest.reshape(nt, g, TOP_K, ROW_TILE).transpose(0, 2, 1, 3).reshape(nt, TOP_K * tm)
    return pl.pallas_call(
        functools.partial(_dispatch_kernel, row_sl=row_sl),
        grid=(nt,),
        in_specs=[
            pl.BlockSpec(memory_space=pl.ANY),
            pl.BlockSpec((tm * row_sl, LANES), lambda i: (i, 0)),
        ],
        out_specs=pl.BlockSpec(memory_space=pl.ANY),
        out_shape=jax.ShapeDtypeStruct((n_rows * row_sl, LANES), F32),
        scratch_shapes=[
            pltpu.SMEM((2 * TOP_K * tm,), I32),
            pltpu.VMEM((EXPERT_TILE * row_sl, LANES), F32),
            pltpu.SemaphoreType.DMA((2,)),
            pltpu.SemaphoreType.DMA((TOP_K,)),
            pltpu.SemaphoreType.DMA,
        ],
        compiler_params=_params(sem=("arbitrary",)),
        name="dispatch",
    )(dest, h2t)


def _expert_kernel(be_ref, nv_ref, first_ref, next_ref, off_ref, tot_ref,
                   xs_hbm, wgu_hbm, bgu_ref, wdn_hbm, bdn_ref, ys_hbm,
                   xbuf, ybuf, stage_gu, stage_dn, wgu_bf, wdn_bf, act_scr, sem_x, sem_y, sem_w, *, layer):
    i = pl.program_id(0)
    d, d_gu = wgu_bf.shape
    d_e = wdn_bf.shape[0]
    nsl = d // LANES
    blk = xbuf.shape[1]
    te = blk // nsl
    slot = i % 2
    total = tot_ref[0]

    def weight_copies(e):
        row = layer * N_EXPERTS + e
        return (pltpu.make_async_copy(wgu_hbm.at[row], stage_gu, sem_w.at[0]),
                pltpu.make_async_copy(wdn_hbm.at[row], stage_dn, sem_w.at[1]))

    def x_copy(j, s):
        src = xs_hbm.at[pl.ds(pl.multiple_of(off_ref[j], nsl), blk)]
        return pltpu.make_async_copy(src, xbuf.at[s], sem_x.at[s])

    def y_copy(j, s):
        dst = ys_hbm.at[pl.ds(pl.multiple_of(off_ref[j], nsl), blk)]
        return pltpu.make_async_copy(ybuf.at[s], dst, sem_y.at[s])

    @pl.when(i == 0)
    def _():
        for cp in weight_copies(be_ref[0]):
            cp.start(priority=1)
        x_copy(0, 0).start()
        ybuf[...] = jnp.zeros_like(ybuf)
        cp = pltpu.make_async_copy(ybuf.at[1], ys_hbm.at[pl.ds(ys_hbm.shape[0] - blk, blk)], sem_y.at[1])
        cp.start()
        cp.wait()

    @pl.when(first_ref[i] > 0)
    def _():
        for cp in weight_copies(be_ref[i]):
            cp.wait()
        rows = 128
        for r in range(0, d, rows):
            wgu_bf[r:r + rows, :] = stage_gu[r:r + rows, :].astype(BF16)
        for r in range(0, d_e, rows):
            wdn_bf[r:r + rows, :] = stage_dn[r:r + rows, :].astype(BF16)

        @pl.when(next_ref[i] >= 0)
        def _():
            for cp in weight_copies(next_ref[i]):
                cp.start(priority=1)

    def run_rows(m):
        x_ref = xbuf.at[slot]
        x = jnp.concatenate([_load_row_slab(x_ref, s, m, nsl).astype(BF16) for s in range(nsl)], axis=1)
        for c in range(d_e // EXPERT_CHUNK):
            lo = c * EXPERT_CHUNK
            hi = lo + EXPERT_CHUNK
            glu = jnp.dot(x, wgu_bf[:, lo:hi], preferred_element_type=F32) + bgu_ref[0, :, lo:hi]
            lin = (jnp.dot(x, wgu_bf[:, d_e + lo:d_e + hi], preferred_element_type=F32)
                   + bgu_ref[0, :, d_e + lo:d_e + hi])
            glu = jnp.minimum(glu, SWIGLU_LIMIT)
            lin = jnp.clip(lin, -SWIGLU_LIMIT, SWIGLU_LIMIT)
            act_scr[:m, lo:hi] = (glu * _sigmoid(SWIGLU_ALPHA * glu) * (lin + 1.0)).astype(BF16)
        y = jnp.dot(act_scr[:m, :], wdn_bf[...], preferred_element_type=F32) + bdn_ref[0]
        _store_rows(ybuf.at[slot], y)

    nv = nv_ref[i]

    @pl.when(nv > 0)
    def _():
        x_copy(i, slot).wait()

        @pl.when(i + 1 < total)
        def _():
            x_copy(i + 1, 1 - slot).start()

    sizes = [te // 4, te // 2, te]
    lower = 0
    for m in sizes:
        @pl.when((nv > lower) & (nv <= m))
        def _(m=m):
            run_rows(m)
        lower = m

    @pl.when(nv > 0)
    def _():
        @pl.when(i >= 1)
        def _():
            y_copy(i - 1, 1 - slot).wait()
        y_copy(i, slot).start()

        @pl.when(i == total - 1)
        def _():
            y_copy(i, slot).wait()


def _experts(block_expert, block_rows, block_first, block_next, block_off, total, xs,
             w_gu, b_gu, w_dn, b_dn, layer):
    d = w_gu.shape[2]
    n_blocks = block_expert.shape[0]
    blk = EXPERT_TILE * d // LANES
    d_gu = w_gu.shape[3]
    d_e = w_dn.shape[2]
    e_map3 = lambda i, be, nv, bf, bn, off, tot: (layer * N_EXPERTS + be[i], 0, 0)
    w_gu = w_gu.reshape((-1,) + w_gu.shape[2:])
    w_dn = w_dn.reshape((-1,) + w_dn.shape[2:])
    b_gu = b_gu.reshape(-1, 1, d_gu)
    b_dn = b_dn.reshape(-1, 1, d)
    grid_spec = pltpu.PrefetchScalarGridSpec(
        num_scalar_prefetch=6,
        grid=(n_blocks,),
        in_specs=[
            pl.BlockSpec(memory_space=pl.ANY),
            pl.BlockSpec(memory_space=pl.ANY),
            pl.BlockSpec((1, 1, d_gu), e_map3),
            pl.BlockSpec(memory_space=pl.ANY),
            pl.BlockSpec((1, 1, d), e_map3),
        ],
        out_specs=pl.BlockSpec(memory_space=pl.ANY),
        scratch_shapes=[
            pltpu.VMEM((2, blk, LANES), F32),
            pltpu.VMEM((2, blk, LANES), F32),
            pltpu.VMEM((d, d_gu), F32),
            pltpu.VMEM((d_e, d), F32),
            pltpu.VMEM((d, d_gu), BF16),
            pltpu.VMEM((d_e, d), BF16),
            pltpu.VMEM((EXPERT_TILE, d_e), BF16),
            pltpu.SemaphoreType.DMA((2,)),
            pltpu.SemaphoreType.DMA((2,)),
            pltpu.SemaphoreType.DMA((2,)),
        ],
    )
    return pl.pallas_call(
        functools.partial(_expert_kernel, layer=layer),
        grid_spec=grid_spec,
        out_shape=jax.ShapeDtypeStruct(xs.shape, F32),
        compiler_params=_params(sem=("arbitrary",)),
        name="experts",
    )(block_expert, block_rows, block_first, block_next, block_off, total, xs, w_gu, b_gu, w_dn, b_dn)


def _combine_kernel(dest_hbm, x1_ref, g_ref, fw_ref, ys_hbm, o_ref,
                    idx_smem, ybuf, sem_idx, sem_rows, *, final_norm):
    i = pl.program_id(0)
    nt = pl.num_programs(0)
    tm, d = x1_ref.shape
    nsl = d // LANES
    slot = i % 2

    def idx_copy(t, s):
        n_idx = TOP_K * tm
        dst = idx_smem.at[pl.ds(pl.multiple_of(s * n_idx, n_idx), n_idx)]
        return pltpu.make_async_copy(dest_hbm.at[t], dst, sem_idx.at[s])

    def issue_rows(s):
        def body(j, carry):
            n0 = j * DMA_UNROLL
            i0 = s * (TOP_K * tm) + n0
            for u in range(DMA_UNROLL):
                for k in range(TOP_K):
                    src = ys_hbm.at[pl.ds(pl.multiple_of(idx_smem[i0 + (k * tm + u)], nsl), nsl)]
                    dst = ybuf.at[s, k, pl.ds(pl.multiple_of((n0 + u) * nsl, nsl), nsl)]
                    pltpu.make_async_copy(src, dst, sem_rows.at[s, k]).start(priority=k % 2)
            return carry
        lax.fori_loop(0, tm // DMA_UNROLL, body, 0)

    def wait_rows(s):
        for k in range(TOP_K):
            pltpu.make_async_copy(ys_hbm.at[pl.ds(0, tm * nsl)], ybuf.at[s, k], sem_rows.at[s, k]).wait()

    @pl.when(i == 0)
    def _():
        cp = idx_copy(0, 0)
        cp.start()
        cp.wait()
        issue_rows(0)

        @pl.when(nt > 1)
        def _():
            idx_copy(1, 1).start()

    @pl.when(i + 1 < nt)
    def _():
        idx_copy(i + 1, 1 - slot).wait()
        issue_rows(1 - slot)

        @pl.when(i + 2 < nt)
        def _():
            idx_copy(i + 2, slot).start()

    wait_rows(slot)
    rows = COMBINE_ROWS_NORM if final_norm else COMBINE_ROWS
    for r0 in range(0, tm, rows):
        g = g_ref[r0:r0 + rows, :]
        gk = [jnp.broadcast_to(g[:, k:k + 1], (rows, LANES)) for k in range(TOP_K)]
        slabs = []
        for s in range(nsl):
            acc = x1_ref[r0:r0 + rows, s * LANES:(s + 1) * LANES]
            for k in range(TOP_K):
                acc = acc + gk[k] * _load_row_slab(ybuf.at[slot, k], s, rows, nsl, r0)
            slabs.append(acc)
        out = jnp.concatenate(slabs, axis=1)
        if final_norm:
            out = _rms(out, fw_ref[...])
        o_ref[r0:r0 + rows, :] = out


def _combine(x1, ys, gate, dest, final_w, final_norm):
    n, d = x1.shape
    tm = ROW_TILE
    nt = n // tm
    tok_tile = lambda i: (i, 0)
    return pl.pallas_call(
        functools.partial(_combine_kernel, final_norm=final_norm),
        grid=(nt,),
        in_specs=[
            pl.BlockSpec(memory_space=pl.ANY),
            pl.BlockSpec((tm, d), tok_tile),
            pl.BlockSpec((tm, LANES), tok_tile),
            pl.BlockSpec((1, d), lambda i: (0, 0)),
            pl.BlockSpec(memory_space=pl.ANY),
        ],
        out_specs=pl.BlockSpec((tm, d), tok_tile),
        out_shape=jax.ShapeDtypeStruct((n, d), F32),
        scratch_shapes=[
            pltpu.SMEM((2 * TOP_K * tm,), I32),
            pltpu.VMEM((2, TOP_K, tm * d // LANES, LANES), F32),
            pltpu.SemaphoreType.DMA((2,)),
            pltpu.SemaphoreType.DMA((2, TOP_K)),
        ],
        compiler_params=_params(sem=("arbitrary",)),
        name="combine",
    )(dest, x1, gate, final_w.reshape(1, d), ys)


def kernel(x, attn_norm_w, w_in, sgu_ln_w, sgu_ln_b, sgu_w, sgu_b, ssm_a_re, ssm_a_im, ssm_b_re, ssm_b_im, ssm_c_re, ssm_c_im, ssm_d, ssm_log_dt, ssm_glu_w, ssm_glu_b, out_norm_a, out_norm_b, w_out, ffn_norm_w, router_w, router_b, w_gate_up, b_gate_up, w_down, b_down, final_norm_w):
    b, l, d = x.shape
    n = b * l
    depth = w_in.shape[0]
    assert l % CHUNK == 0 and l % SSM_T == 0 and b % S5_SEQS == 0
    assert n % FRONT_TILE == 0 and n % DISPATCH_TILE == 0 and DISPATCH_TILE % ROW_TILE == 0
    assert (n * TOP_K) % EXPERT_TILE == 0 and d % LANES == 0
    assert router_w.shape[-1] == N_EXPERTS and ssm_a_re.shape[1] % GROUPS_PER_LANE_BLOCK == 0
    blocks_per_seq = l // SSM_T
    n_steps = max(1, (blocks_per_seq - 1).bit_length())
    n_rows = n * TOP_K + EXPERT_TILE
    x2 = x.reshape(n, d).astype(F32)
    k2, so3, si2, apw = _s5_tables(ssm_a_re, ssm_a_im, ssm_b_re, ssm_b_im, ssm_c_re, ssm_c_im, ssm_d,
                                   ssm_log_dt, n_steps)
    for layer in range(depth):
        ya, us3 = _front(x2, attn_norm_w[layer], w_in[layer], sgu_ln_w[layer], sgu_ln_b[layer],
                         sgu_w[layer], sgu_b[layer], out_norm_a[layer])
        ys3 = _s5(us3, k2, so3, si2, apw, blocks_per_seq, layer)
        x1, h2, gate, er, cnt = _back(x2, ya, ys3, ssm_glu_w[layer], ssm_glu_b[layer],
                                      out_norm_b[layer], w_out[layer], ffn_norm_w[layer],
                                      router_w[layer], router_b[layer])
        counts = cnt[0, :N_EXPERTS].astype(I32)
        dest, block_expert, block_rows, block_first, block_next, block_off, total = _routing_tables(
            er, counts, n, d // LANES)
        xs = _dispatch(dest, h2, n_rows, d // LANES)
        ys = _experts(block_expert, block_rows, block_first, block_next, block_off, total, xs,
                      w_gate_up, b_gate_up, w_down, b_down, layer)
        x2 = _combine(x1, ys, gate, dest, final_norm_w, final_norm=(layer == depth - 1))
    return x2.reshape(b, l, d).astype(x.dtype)
```

```python
import functools
import math

import jax
import jax.numpy as jnp
from jax import lax
from jax.experimental import pallas as pl
from jax.experimental.pallas import tpu as pltpu

F32 = jnp.float32
BF16 = jnp.bfloat16
I32 = jnp.int32

EPS = 1e-5
N_HEADS = 4
CHUNK = 128
SSM_GROUP = 16
SSM_STATE = 64
SSM_T = 16
N_EXPERTS = 32
TOP_K = 4
SWIGLU_LIMIT = 7.0
SWIGLU_ALPHA = 1.702
LANES = 128
SUBLANES = 8
GROUPS_PER_LANE_BLOCK = LANES // SSM_GROUP

ROW_TILE = 512
FRONT_TILE = 1024
DISPATCH_TILE = 4096
S5_SEQS = 4
EXPERT_TILE = 1024
EXPERT_CHUNK = 256
DMA_UNROLL = 8
COMBINE_ROWS = 8
COMBINE_ROWS_NORM = 256
VMEM_LIMIT = 56 * 1024 * 1024


def _gelu(x):
    return 0.5 * x * (1.0 + jnp.tanh(math.sqrt(2.0 / math.pi) * (x + 0.044715 * (x * x * x))))


def _sigmoid(x):
    return 1.0 / (1.0 + jnp.exp(-x))


def _rms(x, w):
    return x * lax.rsqrt(jnp.mean(x * x, axis=-1, keepdims=True) + EPS) * w


def _store_rows(ref, val, row0=0):
    rows, d = val.shape
    nsl = d // LANES
    for s in range(nsl):
        ref[pl.ds(row0 * nsl + s, rows, stride=nsl), :] = val[:, s * LANES:(s + 1) * LANES]


def _load_row_slab(ref, s, rows, nsl, row0=0):
    return ref[pl.ds(row0 * nsl + s, rows, stride=nsl), :]


def _params(**kw):
    return pltpu.CompilerParams(dimension_semantics=kw.pop("sem"), vmem_limit_bytes=VMEM_LIMIT, **kw)


def _front_kernel(x_ref, nw_ref, win_ref, lnw_ref, lnb_ref, ws_ref, bst_ref, ona_ref,
                  ya_ref, us_ref, mixed_ref, us_scr):
    d_g = ya_ref.shape[1]
    hd = d_g // N_HEADS
    tm = x_ref.shape[0]
    x = x_ref[...]
    h = _rms(x, nw_ref[...]).astype(BF16)
    proj = jnp.dot(h, win_ref[...], preferred_element_type=F32)
    u = _gelu(proj[:, :d_g])
    v = _gelu(proj[:, d_g:2 * d_g])
    n_lb = us_scr.shape[0]
    for q in range(n_lb):
        us_scr[q] = proj[:, 2 * d_g + q * LANES:2 * d_g + (q + 1) * LANES]
    mu = jnp.mean(v, axis=-1, keepdims=True)
    vc = v - mu
    var = jnp.mean(vc * vc, axis=-1, keepdims=True)
    vb = (vc * lax.rsqrt(var + EPS) * lnw_ref[...] + lnb_ref[...]).astype(BF16)
    row = lax.broadcasted_iota(I32, (CHUNK, CHUNK), 0)
    col = lax.broadcasted_iota(I32, (CHUNK, CHUNK), 1)
    causal = row >= col
    for hh in range(N_HEADS):
        w = jnp.where(causal, ws_ref[hh], 0.0).astype(BF16)
        bias = bst_ref[:, hh:hh + 1]
        for c in range(tm // CHUNK):
            vv = vb[c * CHUNK:(c + 1) * CHUNK, hh * hd:(hh + 1) * hd]
            m = jnp.dot(w, vv, preferred_element_type=F32) + bias
            mixed_ref[c * CHUNK:(c + 1) * CHUNK, hh * hd:(hh + 1) * hd] = m
    ya = u * mixed_ref[...]
    ya_ref[...] = _rms(ya, ona_ref[...]).astype(BF16)
    for s in range(SSM_T):
        for q in range(n_lb):
            us_ref[s, :, q * LANES:(q + 1) * LANES] = (
                us_scr[q, pl.ds(s, tm // SSM_T, stride=SSM_T), :].astype(BF16))


def _front(x2, nw, w_in, ln_w, ln_b, w_s, b_s, on_a):
    n, d = x2.shape
    d_g = ln_w.shape[0]
    d_s = w_in.shape[1] - 2 * d_g
    tm = FRONT_TILE
    const2 = lambda i: (0, 0)
    return pl.pallas_call(
        _front_kernel,
        grid=(n // tm,),
        in_specs=[
            pl.BlockSpec((tm, d), lambda i: (i, 0)),
            pl.BlockSpec((1, d), const2),
            pl.BlockSpec(w_in.shape, const2),
            pl.BlockSpec((1, d_g), const2),
            pl.BlockSpec((1, d_g), const2),
            pl.BlockSpec(w_s.shape, lambda i: (0, 0, 0)),
            pl.BlockSpec((CHUNK, N_HEADS), const2),
            pl.BlockSpec((1, d_g), const2),
        ],
        out_specs=[
            pl.BlockSpec((tm, d_g), lambda i: (i, 0)),
            pl.BlockSpec((SSM_T, tm // SSM_T, d_s), lambda i: (0, i, 0)),
        ],
        out_shape=[
            jax.ShapeDtypeStruct((n, d_g), BF16),
            jax.ShapeDtypeStruct((SSM_T, n // SSM_T, d_s), BF16),
        ],
        scratch_shapes=[pltpu.VMEM((tm, d_g), F32), pltpu.VMEM((d_s // LANES, tm, LANES), F32)],
        compiler_params=_params(sem=("arbitrary",)),
        name="front",
    )(x2, nw.reshape(1, d), w_in.astype(BF16), ln_w.reshape(1, d_g), ln_b.reshape(1, d_g),
      w_s, b_s.T, on_a.reshape(1, d_g))


def _s5_tables(a_re, a_im, b_re, b_im, c_re, c_im, d, log_dt, n_steps):
    depth, g, p = a_re.shape
    hch = b_re.shape[-1]
    t = SSM_T
    gl = GROUPS_PER_LANE_BLOCK
    r = depth * (g // gl)
    a = lax.complex(a_re.astype(F32), a_im.astype(F32)).reshape(r, gl, p)
    dt = jnp.exp(log_dt.astype(F32)).reshape(r, gl, 1)
    dta = dt * a
    a_bar = jnp.exp(dta)
    b = lax.complex(b_re.astype(F32), b_im.astype(F32)).reshape(r, gl, p, hch)
    b_bar = ((a_bar - 1.0) / a)[..., None] * b
    c = lax.complex(c_re.astype(F32), c_im.astype(F32)).reshape(r, gl, hch, p)
    lags = jnp.arange(t + 1, dtype=F32)
    pw = jnp.exp(lags[None, :, None, None] * dta.transpose(0, 2, 1)[:, None])
    pwx = jnp.repeat(pw, hch, axis=-1)
    bb = b_bar.transpose(0, 2, 1, 3).reshape(r, p, gl * hch)
    cc = c.transpose(0, 3, 1, 2).reshape(r, p, gl * hch)
    q = pwx[:, :t] * bb[:, None]
    si = pwx[:, 1:] * cc[:, None]
    k2 = (jnp.einsum('rlpx,rpy->rlxy', q.real, cc.real, precision=lax.Precision.HIGHEST)
          - jnp.einsum('rlpx,rpy->rlxy', q.imag, cc.imag, precision=lax.Precision.HIGHEST))
    lane_g = jnp.arange(gl * hch) // hch
    k2 = jnp.where(lane_g[:, None] == lane_g[None, :], k2, 0.0)
    skip = jnp.eye(gl * hch, dtype=F32) * d.astype(F32).reshape(r, 1, gl * hch)
    k2 = k2.at[:, 0].add(skip)
    so3 = jnp.swapaxes(jnp.concatenate([q.real, q.imag], axis=2), 2, 3)
    si2 = jnp.concatenate([si.real, -si.imag], axis=2)
    steps = SSM_T * (2.0 ** jnp.arange(n_steps, dtype=F32))
    ap = jnp.exp(steps[None, :, None, None] * dta[:, None]).reshape(r, n_steps, gl * p)
    mul_same = jnp.concatenate([ap.real, ap.real], axis=-1)
    mul_swap = jnp.concatenate([-ap.imag, ap.imag], axis=-1)
    apw = jnp.stack([mul_same, mul_swap], axis=2)
    return k2.astype(BF16), so3.astype(BF16), si2.astype(BF16), apw


def _s5_kernel(u_ref, k2_ref, so3_ref, si2_ref, ap_ref, y_ref, wk_scr, wso_scr, wsi_scr, *, blocks_per_seq):
    t_blk = u_ref.shape[0]
    rows = u_ref.shape[1]
    n_state = wso_scr.shape[1]
    two_p = so3_ref.shape[3]
    p = two_p // 2
    half = n_state // 2

    @pl.when(pl.program_id(1) == 0)
    def _():
        for s in range(t_blk):
            if s > 0:
                wk_scr[s * LANES:(s + 1) * LANES, :s * LANES] = jnp.zeros((LANES, s * LANES), BF16)
            for t in range(s, t_blk):
                wk_scr[s * LANES:(s + 1) * LANES, t * LANES:(t + 1) * LANES] = k2_ref[0, t - s]
        r_e = lax.broadcasted_iota(I32, (two_p, n_state), 0)
        c_e = lax.broadcasted_iota(I32, (two_p, n_state), 1)
        spread = jnp.where((r_e // p == c_e // half) & (r_e % p == c_e % p), 1.0, 0.0).astype(BF16)
        r_g = lax.broadcasted_iota(I32, (LANES, n_state), 0) // SSM_GROUP
        c_g = (lax.broadcasted_iota(I32, (LANES, n_state), 1) % half) // p
        own = r_g == c_g
        for s in range(t_blk):
            full = jnp.dot(so3_ref[0, t_blk - 1 - s], spread, preferred_element_type=F32)
            wso_scr[s * LANES:(s + 1) * LANES, :] = jnp.where(own, full, 0.0).astype(BF16)
        r_e = lax.broadcasted_iota(I32, (n_state, two_p), 0)
        c_e = lax.broadcasted_iota(I32, (n_state, two_p), 1)
        gather = jnp.where((c_e // p == r_e // half) & (c_e % p == r_e % p), 1.0, 0.0).astype(BF16)
        r_g = (lax.broadcasted_iota(I32, (n_state, LANES), 0) % half) // p
        c_g = lax.broadcasted_iota(I32, (n_state, LANES), 1) // SSM_GROUP
        own = r_g == c_g
        for t in range(t_blk):
            full = jnp.dot(gather, si2_ref[0, t], preferred_element_type=F32)
            wsi_scr[:, t * LANES:(t + 1) * LANES] = jnp.where(own, full, 0.0).astype(BF16)

    xcat = jnp.concatenate([u_ref[s] for s in range(t_blk)], axis=1)
    x = jnp.dot(xcat, wso_scr[...], preferred_element_type=F32)
    pos = lax.broadcasted_iota(I32, (rows, n_state), 0) % blocks_per_seq
    for k in range(ap_ref.shape[1]):
        sh = 1 << k
        prev = pltpu.roll(x, sh, 0)
        prev_sw = pltpu.roll(prev, half, 1)
        upd = prev * ap_ref[0, k, 0:1, :] + prev_sw * ap_ref[0, k, 1:2, :]
        x = x + jnp.where(pos >= sh, upd, 0.0)
    xin = jnp.where(pos >= 1, pltpu.roll(x, 1, 0), 0.0).astype(BF16)
    for j in range(t_blk // 2):
        k_hi = (2 * j + 2) * LANES
        lo = 2 * j * LANES
        y = jnp.dot(xcat[:, :k_hi], wk_scr[:k_hi, lo:lo + 2 * LANES], preferred_element_type=F32)
        y = y + jnp.dot(xin, wsi_scr[:, lo:lo + 2 * LANES], preferred_element_type=F32)
        y_ref[2 * j] = y[:, :LANES].astype(BF16)
        y_ref[2 * j + 1] = y[:, LANES:].astype(BF16)


def _s5(us3, k2, so3, si2, apw, blocks_per_seq, layer):
    t_blk, n_blocks, d_s = us3.shape
    lb = d_s // LANES
    rows = S5_SEQS * blocks_per_seq
    n_steps = n_blocks // rows
    n_in = t_blk * LANES
    n_state = apw.shape[3]
    w_map = lambda a, b: (layer * lb + a, 0, 0, 0)
    io_spec = pl.BlockSpec((t_blk, rows, LANES), lambda a, b: (0, b, a))
    return pl.pallas_call(
        functools.partial(_s5_kernel, blocks_per_seq=blocks_per_seq),
        grid=(lb, n_steps),
        in_specs=[
            io_spec,
            pl.BlockSpec((1,) + k2.shape[1:], w_map),
            pl.BlockSpec((1,) + so3.shape[1:], w_map),
            pl.BlockSpec((1,) + si2.shape[1:], w_map),
            pl.BlockSpec((1,) + apw.shape[1:], w_map),
        ],
        out_specs=io_spec,
        out_shape=jax.ShapeDtypeStruct(us3.shape, BF16),
        scratch_shapes=[
            pltpu.VMEM((n_in, n_in), BF16),
            pltpu.VMEM((n_in, n_state), BF16),
            pltpu.VMEM((n_state, n_in), BF16),
        ],
        compiler_params=_params(sem=("arbitrary", "arbitrary")),
        name="s5",
    )(us3, k2, so3, si2, apw)


def _back_kernel(x_ref, ya_ref, ys_ref, gw_ref, gb_ref, onb_ref, wo_ref, fnw_ref,
                 rw_ref, rb_ref,
                 x1_ref, h2_ref, gate_ref, er_ref, cnt_ref, carry_ref, ys_scr):
    i = pl.program_id(0)
    tm = x_ref.shape[0]
    d_g = ya_ref.shape[1]

    @pl.when(i == 0)
    def _():
        carry_ref[...] = jnp.zeros_like(carry_ref)

    n_lb = ys_scr.shape[0]
    for t in range(SSM_T):
        for q in range(n_lb):
            ys_scr[q, pl.ds(t, tm // SSM_T, stride=SSM_T), :] = (
                ys_ref[t, :, q * LANES:(q + 1) * LANES].astype(F32))
    y = _gelu(jnp.concatenate([ys_scr[q] for q in range(n_lb)], axis=1))
    z = jnp.dot(y.astype(BF16), gw_ref[...], preferred_element_type=F32) + gb_ref[...]
    yb = y * _sigmoid(z)
    ybn = _rms(yb, onb_ref[...]).astype(BF16)
    x1 = (x_ref[...]
          + jnp.dot(ya_ref[...], wo_ref[:d_g, :], preferred_element_type=F32)
          + jnp.dot(ybn, wo_ref[d_g:, :], preferred_element_type=F32))
    x1_ref[...] = x1
    h2 = _rms(x1, fnw_ref[...])
    _store_rows(h2_ref, h2)

    hh = h2.astype(BF16)
    hl = (h2 - hh.astype(F32)).astype(BF16)
    p_hi = jnp.dot(hh, rw_ref[...], preferred_element_type=F32)
    p_lo = jnp.dot(hl, rw_ref[:, :LANES], preferred_element_type=F32)
    logits = p_hi[:, :LANES] + p_lo + p_hi[:, LANES:] + rb_ref[...]
    lane = lax.broadcasted_iota(I32, (tm, LANES), 1)
    lane_f = lane.astype(F32)
    neg = jnp.float32(-jnp.inf)
    work = jnp.where(lane < N_EXPERTS, logits, neg)
    vals, hots, idxs = [], [], []
    for k in range(TOP_K):
        m = jnp.max(work, axis=-1, keepdims=True)
        idx = jnp.min(jnp.where(work == m, lane_f, float(LANES)), axis=-1, keepdims=True)
        hot = lane_f == idx
        vals.append(m)
        hots.append(hot)
        idxs.append(idx.astype(I32))
        work = jnp.where(hot, neg, work)
    exps = [jnp.exp(v - vals[0]) for v in vals]
    denom = exps[0] + exps[1] + exps[2] + exps[3]
    gate = jnp.zeros((tm, LANES), F32)
    for k in range(TOP_K):
        gate = jnp.where(lane == k, exps[k] / denom, gate)
    gate_ref[...] = gate

    sel = (hots[0] | hots[1] | hots[2] | hots[3])
    sel_f = jnp.where(sel, 1.0, 0.0)
    r_i = lax.broadcasted_iota(I32, (tm, tm), 0)
    c_i = lax.broadcasted_iota(I32, (tm, tm), 1)
    lower = jnp.where(r_i > c_i, 1.0, 0.0).astype(BF16)
    cum = jnp.dot(lower, sel_f.astype(BF16), preferred_element_type=F32) + carry_ref[...]
    er = jnp.zeros((tm, LANES), I32)
    for k in range(TOP_K):
        rk = jnp.sum(jnp.where(hots[k], cum, 0.0), axis=-1, keepdims=True)
        er = jnp.where(lane == k, idxs[k], er)
        er = jnp.where(lane == TOP_K + k, rk.astype(I32), er)
    carry_ref[...] = carry_ref[...] + jnp.sum(sel_f, axis=0, keepdims=True)
    cnt_ref[...] = carry_ref[...]
    er_ref[0] = jnp.transpose(er)[:SUBLANES, :]


def _back(x2, ya, ys3, glu_w, glu_b, on_b, w_out, fn_w, router_w, router_b):
    n, d = x2.shape
    d_g = ya.shape[1]
    d_s = ys3.shape[2]
    tm = ROW_TILE
    nt = n // tm
    const2 = lambda i: (0, 0)
    rw = jnp.zeros((d, LANES), F32).at[:, :N_EXPERTS].set(router_w.astype(F32))
    rw_hi = rw.astype(BF16)
    rw_lo = (rw - rw_hi.astype(F32)).astype(BF16)
    rw_split = jnp.concatenate([rw_hi, rw_lo], axis=1)
    rb = jnp.zeros((1, LANES), F32).at[0, :N_EXPERTS].set(router_b.astype(F32))
    tok_tile = lambda i: (i, 0)
    return pl.pallas_call(
        _back_kernel,
        grid=(nt,),
        in_specs=[
            pl.BlockSpec((tm, d), tok_tile),
            pl.BlockSpec((tm, d_g), tok_tile),
            pl.BlockSpec((SSM_T, tm // SSM_T, d_s), lambda i: (0, i, 0)),
            pl.BlockSpec((d_s, d_s), const2),
            pl.BlockSpec((1, d_s), const2),
            pl.BlockSpec((1, d_s), const2),
            pl.BlockSpec((d_g + d_s, d), const2),
            pl.BlockSpec((1, d), const2),
            pl.BlockSpec((d, 2 * LANES), const2),
            pl.BlockSpec((1, LANES), const2),
        ],
        out_specs=[
            pl.BlockSpec((tm, d), tok_tile),
            pl.BlockSpec((tm * d // LANES, LANES), tok_tile),
            pl.BlockSpec((tm, LANES), tok_tile),
            pl.BlockSpec((1, SUBLANES, tm), lambda i: (i, 0, 0)),
            pl.BlockSpec((1, LANES), const2),
        ],
        out_shape=[
            jax.ShapeDtypeStruct((n, d), F32),
            jax.ShapeDtypeStruct((n * d // LANES, LANES), F32),
            jax.ShapeDtypeStruct((n, LANES), F32),
            jax.ShapeDtypeStruct((nt, SUBLANES, tm), I32),
            jax.ShapeDtypeStruct((1, LANES), F32),
        ],
        scratch_shapes=[pltpu.VMEM((1, LANES), F32), pltpu.VMEM((d_s // LANES, tm, LANES), F32)],
        compiler_params=_params(sem=("arbitrary",)),
        name="back",
    )(x2, ya, ys3, glu_w.astype(BF16), glu_b.reshape(1, d_s), on_b.reshape(1, d_s),
      w_out.astype(BF16), fn_w.reshape(1, d), rw_split, rb)


def _routing_tables(er, counts, n, row_sl):
    te = EXPERT_TILE
    n_blocks = n * TOP_K // te + N_EXPERTS
    start = jnp.cumsum(counts) - counts
    e_ids = jnp.arange(N_EXPERTS, dtype=I32)
    e_sel = er[:, :TOP_K, :, None] == e_ids
    dest = jnp.sum(jnp.where(e_sel, start, 0), axis=-1) + er[:, TOP_K:2 * TOP_K, :]
    dest = (dest * row_sl).reshape(dest.shape[0], -1)
    nb = (counts + te - 1) // te
    cum = jnp.cumsum(nb)
    total = cum[-1]
    j = jnp.arange(n_blocks, dtype=I32)
    e_j = jnp.sum((cum[None, :] <= jnp.minimum(j, total - 1)[:, None]).astype(I32), axis=1)
    valid = (j < total).astype(I32)
    own = e_ids[None, :] == e_j[:, None]
    first_j = jnp.sum(jnp.where(own, cum - nb, 0), axis=1)
    count_j = jnp.sum(jnp.where(own, counts, 0), axis=1)
    start_j = jnp.sum(jnp.where(own, start, 0), axis=1)
    first = valid * (j == first_j).astype(I32)
    later = (e_ids[None, :] > e_j[:, None]) & (nb > 0)[None, :]
    nxt = jnp.min(jnp.where(later, e_ids[None, :], N_EXPERTS), axis=1)
    nxt = jnp.where(nxt < N_EXPERTS, nxt, -1)
    rows = valid * jnp.clip(count_j - (j - first_j) * te, 0, te)
    row_off = (start_j + (j - first_j) * te) * row_sl
    return (dest.astype(I32), e_j.astype(I32), rows.astype(I32), first, nxt.astype(I32),
            row_off.astype(I32), total.reshape(1).astype(I32))


def _dispatch_kernel(dest_hbm, h_ref, xs_hbm, idx_smem, zbuf, sem_idx, sem_rows, sem_z, *, row_sl):
    i = pl.program_id(0)
    nt = pl.num_programs(0)
    tm = h_ref.shape[0] // row_sl
    blk = zbuf.shape[0]
    slot = i % 2

    def idx_copy(t, s):
        n_idx = TOP_K * tm
        dst = idx_smem.at[pl.ds(pl.multiple_of(s * n_idx, n_idx), n_idx)]
        return pltpu.make_async_copy(dest_hbm.at[t], dst, sem_idx.at[s])

    @pl.when(i == 0)
    def _():
        zbuf[...] = jnp.zeros_like(zbuf)
        cp = pltpu.make_async_copy(zbuf, xs_hbm.at[pl.ds(xs_hbm.shape[0] - blk, blk)], sem_z)
        cp.start()
        cp.wait()
        idx_copy(0, 0).start()

    @pl.when(i + 1 < nt)
    def _():
        idx_copy(i + 1, 1 - slot).start()

    idx_copy(i, slot).wait()

    def body(j, carry):
        n0 = j * DMA_UNROLL
        i0 = slot * (TOP_K * tm) + n0
        for u in range(DMA_UNROLL):
            src = h_ref.at[pl.ds(pl.multiple_of((n0 + u) * row_sl, row_sl), row_sl)]
            for k in range(TOP_K):
                dst = xs_hbm.at[pl.ds(pl.multiple_of(idx_smem[i0 + (k * tm + u)], row_sl), row_sl)]
                pltpu.make_async_copy(src, dst, sem_rows.at[k]).start(priority=k % 2)
        return carry
    lax.fori_loop(0, tm // DMA_UNROLL, body, 0)
    for k in range(TOP_K):
        pltpu.make_async_copy(h_ref, xs_hbm.at[pl.ds(0, tm * row_sl)], sem_rows.at[k]).wait()


def _dispatch(dest, h2t, n_rows, row_sl):
    tm = DISPATCH_TILE
    nt = h2t.shape[0] // (tm * row_sl)
    g = tm // ROW_TILE
    dest = dest.reshape(nt, g, TOP_K, ROW_TILE).transpose(0, 2, 1, 3).reshape(nt, TOP_K * tm)
    return pl.pallas_call(
        functools.partial(_dispatch_kernel, row_sl=row_sl),
        grid=(nt,),
        in_specs=[
            pl.BlockSpec(memory_space=pl.ANY),
            pl.BlockSpec((tm * row_sl, LANES), lambda i: (i, 0)),
        ],
        out_specs=pl.BlockSpec(memory_space=pl.ANY),
        out_shape=jax.ShapeDtypeStruct((n_rows * row_sl, LANES), F32),
        scratch_shapes=[
            pltpu.SMEM((2 * TOP_K * tm,), I32),
            pltpu.VMEM((EXPERT_TILE * row_sl, LANES), F32),
            pltpu.SemaphoreType.DMA((2,)),
            pltpu.SemaphoreType.DMA((TOP_K,)),
            pltpu.SemaphoreType.DMA,
        ],
        compiler_params=_params(sem=("arbitrary",)),
        name="dispatch",
    )(dest, h2t)


def _expert_kernel(be_ref, nv_ref, first_ref, next_ref, off_ref, tot_ref,
                   xs_hbm, wgu_hbm, bgu_ref, wdn_hbm, bdn_ref, ys_hbm,
                   xbuf, ybuf, stage_gu, stage_dn, wgu_bf, wdn_bf, act_scr, sem_x, sem_y, sem_w, *, layer):
    i = pl.program_id(0)
    d, d_gu = wgu_bf.shape
    d_e = wdn_bf.shape[0]
    nsl = d // LANES
    blk = xbuf.shape[1]
    te = blk // nsl
    slot = i % 2
    total = tot_ref[0]

    def weight_copies(e):
        row = layer * N_EXPERTS + e
        return (pltpu.make_async_copy(wgu_hbm.at[row], stage_gu, sem_w.at[0]),
                pltpu.make_async_copy(wdn_hbm.at[row], stage_dn, sem_w.at[1]))

    def x_copy(j, s):
        src = xs_hbm.at[pl.ds(pl.multiple_of(off_ref[j], nsl), blk)]
        return pltpu.make_async_copy(src, xbuf.at[s], sem_x.at[s])

    def y_copy(j, s):
        dst = ys_hbm.at[pl.ds(pl.multiple_of(off_ref[j], nsl), blk)]
        return pltpu.make_async_copy(ybuf.at[s], dst, sem_y.at[s])

    @pl.when(i == 0)
    def _():
        for cp in weight_copies(be_ref[0]):
            cp.start(priority=1)
        x_copy(0, 0).start()
        ybuf[...] = jnp.zeros_like(ybuf)
        cp = pltpu.make_async_copy(ybuf.at[1], ys_hbm.at[pl.ds(ys_hbm.shape[0] - blk, blk)], sem_y.at[1])
        cp.start()
        cp.wait()

    @pl.when(first_ref[i] > 0)
    def _():
        for cp in weight_copies(be_ref[i]):
            cp.wait()
        rows = 128
        for r in range(0, d, rows):
            wgu_bf[r:r + rows, :] = stage_gu[r:r + rows, :].astype(BF16)
        for r in range(0, d_e, rows):
            wdn_bf[r:r + rows, :] = stage_dn[r:r + rows, :].astype(BF16)

        @pl.when(next_ref[i] >= 0)
        def _():
            for cp in weight_copies(next_ref[i]):
                cp.start(priority=1)

    def run_rows(m):
        x_ref = xbuf.at[slot]
        x = jnp.concatenate([_load_row_slab(x_ref, s, m, nsl).astype(BF16) for s in range(nsl)], axis=1)
        for c in range(d_e // EXPERT_CHUNK):
            lo = c * EXPERT_CHUNK
            hi = lo + EXPERT_CHUNK
            glu = jnp.dot(x, wgu_bf[:, lo:hi], preferred_element_type=F32) + bgu_ref[0, :, lo:hi]
            lin = (jnp.dot(x, wgu_bf[:, d_e + lo:d_e + hi], preferred_element_type=F32)
                   + bgu_ref[0, :, d_e + lo:d_e + hi])
            glu = jnp.minimum(glu, SWIGLU_LIMIT)
            lin = jnp.clip(lin, -SWIGLU_LIMIT, SWIGLU_LIMIT)
            act_scr[:m, lo:hi] = (glu * _sigmoid(SWIGLU_ALPHA * glu) * (lin + 1.0)).astype(BF16)
        y = jnp.dot(act_scr[:m, :], wdn_bf[...], preferred_element_type=F32) + bdn_ref[0]
        _store_rows(ybuf.at[slot], y)

    nv = nv_ref[i]

    @pl.when(nv > 0)
    def _():
        x_copy(i, slot).wait()

        @pl.when(i + 1 < total)
        def _():
            x_copy(i + 1, 1 - slot).start()

    sizes = [te // 4, te // 2, te]
    lower = 0
    for m in sizes:
        @pl.when((nv > lower) & (nv <= m))
        def _(m=m):
            run_rows(m)
        lower = m

    @pl.when(nv > 0)
    def _():
        @pl.when(i >= 1)
        def _():
            y_copy(i - 1, 1 - slot).wait()
        y_copy(i, slot).start()

        @pl.when(i == total - 1)
        def _():
            y_copy(i, slot).wait()


def _experts(block_expert, block_rows, block_first, block_next, block_off, total, xs,
             w_gu, b_gu, w_dn, b_dn, layer):
    d = w_gu.shape[2]
    n_blocks = block_expert.shape[0]
    blk = EXPERT_TILE * d // LANES
    d_gu = w_gu.shape[3]
    d_e = w_dn.shape[2]
    e_map3 = lambda i, be, nv, bf, bn, off, tot: (layer * N_EXPERTS + be[i], 0, 0)
    w_gu = w_gu.reshape((-1,) + w_gu.shape[2:])
    w_dn = w_dn.reshape((-1,) + w_dn.shape[2:])
    b_gu = b_gu.reshape(-1, 1, d_gu)
    b_dn = b_dn.reshape(-1, 1, d)
    grid_spec = pltpu.PrefetchScalarGridSpec(
        num_scalar_prefetch=6,
        grid=(n_blocks,),
        in_specs=[
            pl.BlockSpec(memory_space=pl.ANY),
            pl.BlockSpec(memory_space=pl.ANY),
            pl.BlockSpec((1, 1, d_gu), e_map3),
            pl.BlockSpec(memory_space=pl.ANY),
            pl.BlockSpec((1, 1, d), e_map3),
        ],
        out_specs=pl.BlockSpec(memory_space=pl.ANY),
        scratch_shapes=[
            pltpu.VMEM((2, blk, LANES), F32),
            pltpu.VMEM((2, blk, LANES), F32),
            pltpu.VMEM((d, d_gu), F32),
            pltpu.VMEM((d_e, d), F32),
            pltpu.VMEM((d, d_gu), BF16),
            pltpu.VMEM((d_e, d), BF16),
            pltpu.VMEM((EXPERT_TILE, d_e), BF16),
            pltpu.SemaphoreType.DMA((2,)),
            pltpu.SemaphoreType.DMA((2,)),
            pltpu.SemaphoreType.DMA((2,)),
        ],
    )
    return pl.pallas_call(
        functools.partial(_expert_kernel, layer=layer),
        grid_spec=grid_spec,
        out_shape=jax.ShapeDtypeStruct(xs.shape, F32),
        compiler_params=_params(sem=("arbitrary",)),
        name="experts",
    )(block_expert, block_rows, block_first, block_next, block_off, total, xs, w_gu, b_gu, w_dn, b_dn)


def _combine_kernel(dest_hbm, x1_ref, g_ref, fw_ref, ys_hbm, o_ref,
                    idx_smem, ybuf, sem_idx, sem_rows, *, final_norm):
    i = pl.program_id(0)
    nt = pl.num_programs(0)
    tm, d = x1_ref.shape
    nsl = d // LANES
    slot = i % 2

    def idx_copy(t, s):
        n_idx = TOP_K * tm
        dst = idx_smem.at[pl.ds(pl.multiple_of(s * n_idx, n_idx), n_idx)]
        return pltpu.make_async_copy(dest_hbm.at[t], dst, sem_idx.at[s])

    def issue_rows(s):
        def body(j, carry):
            n0 = j * DMA_UNROLL
            i0 = s * (TOP_K * tm) + n0
            for u in range(DMA_UNROLL):
                for k in range(TOP_K):
                    src = ys_hbm.at[pl.ds(pl.multiple_of(idx_smem[i0 + (k * tm + u)], nsl), nsl)]
                    dst = ybuf.at[s, k, pl.ds(pl.multiple_of((n0 + u) * nsl, nsl), nsl)]
                    pltpu.make_async_copy(src, dst, sem_rows.at[s, k]).start(priority=k % 2)
            return carry
        lax.fori_loop(0, tm // DMA_UNROLL, body, 0)

    def wait_rows(s):
        for k in range(TOP_K):
            pltpu.make_async_copy(ys_hbm.at[pl.ds(0, tm * nsl)], ybuf.at[s, k], sem_rows.at[s, k]).wait()

    @pl.when(i == 0)
    def _():
        cp = idx_copy(0, 0)
        cp.start()
        cp.wait()
        issue_rows(0)

        @pl.when(nt > 1)
        def _():
            idx_copy(1, 1).start()

    @pl.when(i + 1 < nt)
    def _():
        idx_copy(i + 1, 1 - slot).wait()
        issue_rows(1 - slot)

        @pl.when(i + 2 < nt)
        def _():
            idx_copy(i + 2, slot).start()

    wait_rows(slot)
    rows = COMBINE_ROWS_NORM if final_norm else COMBINE_ROWS
    for r0 in range(0, tm, rows):
        g = g_ref[r0:r0 + rows, :]
        gk = [jnp.broadcast_to(g[:, k:k + 1], (rows, LANES)) for k in range(TOP_K)]
        slabs = []
        for s in range(nsl):
            acc = x1_ref[r0:r0 + rows, s * LANES:(s + 1) * LANES]
            for k in range(TOP_K):
                acc = acc + gk[k] * _load_row_slab(ybuf.at[slot, k], s, rows, nsl, r0)
            slabs.append(acc)
        out = jnp.concatenate(slabs, axis=1)
        if final_norm:
            out = _rms(out, fw_ref[...])
        o_ref[r0:r0 + rows, :] = out


def _combine(x1, ys, gate, dest, final_w, final_norm):
    n, d = x1.shape
    tm = ROW_TILE
    nt = n // tm
    tok_tile = lambda i: (i, 0)
    return pl.pallas_call(
        functools.partial(_combine_kernel, final_norm=final_norm),
        grid=(nt,),
        in_specs=[
            pl.BlockSpec(memory_space=pl.ANY),
            pl.BlockSpec((tm, d), tok_tile),
            pl.BlockSpec((tm, LANES), tok_tile),
            pl.BlockSpec((1, d), lambda i: (0, 0)),
            pl.BlockSpec(memory_space=pl.ANY),
        ],
        out_specs=pl.BlockSpec((tm, d), tok_tile),
        out_shape=jax.ShapeDtypeStruct((n, d), F32),
        scratch_shapes=[
            pltpu.SMEM((2 * TOP_K * tm,), I32),
            pltpu.VMEM((2, TOP_K, tm * d // LANES, LANES), F32),
            pltpu.SemaphoreType.DMA((2,)),
            pltpu.SemaphoreType.DMA((2, TOP_K)),
        ],
        compiler_params=_params(sem=("arbitrary",)),
        name="combine",
    )(dest, x1, gate, final_w.reshape(1, d), ys)


def kernel(x, attn_norm_w, w_in, sgu_ln_w, sgu_ln_b, sgu_w, sgu_b, ssm_a_re, ssm_a_im, ssm_b_re, ssm_b_im, ssm_c_re, ssm_c_im, ssm_d, ssm_log_dt, ssm_glu_w, ssm_glu_b, out_norm_a, out_norm_b, w_out, ffn_norm_w, router_w, router_b, w_gate_up, b_gate_up, w_down, b_down, final_norm_w):
    b, l, d = x.shape
    n = b * l
    depth = w_in.shape[0]
    assert l % CHUNK == 0 and l % SSM_T == 0 and b % S5_SEQS == 0
    assert n % FRONT_TILE == 0 and n % DISPATCH_TILE == 0 and DISPATCH_TILE % ROW_TILE == 0
    assert (n * TOP_K) % EXPERT_TILE == 0 and d % LANES == 0
    assert router_w.shape[-1] == N_EXPERTS and ssm_a_re.shape[1] % GROUPS_PER_LANE_BLOCK == 0
    blocks_per_seq = l // SSM_T
    n_steps = max(1, (blocks_per_seq - 1).bit_length())
    n_rows = n * TOP_K + EXPERT_TILE
    x2 = x.reshape(n, d).astype(F32)
    k2, so3, si2, apw = _s5_tables(ssm_a_re, ssm_a_im, ssm_b_re, ssm_b_im, ssm_c_re, ssm_c_im, ssm_d,
                                   ssm_log_dt, n_steps)
    for layer in range(depth):
        ya, us3 = _front(x2, attn_norm_w[layer], w_in[layer], sgu_ln_w[layer], sgu_ln_b[layer],
                         sgu_w[layer], sgu_b[layer], out_norm_a[layer])
        ys3 = _s5(us3, k2, so3, si2, apw, blocks_per_seq, layer)
        x1, h2, gate, er, cnt = _back(x2, ya, ys3, ssm_glu_w[layer], ssm_glu_b[layer],
                                      out_norm_b[layer], w_out[layer], ffn_norm_w[layer],
                                      router_w[layer], router_b[layer])
        counts = cnt[0, :N_EXPERTS].astype(I32)
        dest, block_expert, block_rows, block_first, block_next, block_off, total = _routing_tables(
            er, counts, n, d // LANES)
        xs = _dispatch(dest, h2, n_rows, d // LANES)
        ys = _experts(block_expert, block_rows, block_first, block_next, block_off, total, xs,
                      w_gate_up, b_gate_up, w_down, b_down, layer)
        x2 = _combine(x1, ys, gate, dest, final_norm_w, final_norm=(layer == depth - 1))
    return x2.reshape(b, l, d).astype(x.dtype)
```
